```python
import math
import jax, jax.numpy as jnp
from jax import lax
import numpy as np

D_MODEL = 2048
BATCH = 4
SEQ = 4096
DEPTH = 1

D_MIX = D_MODEL
SSM_WIDTH = D_MIX // 2
SSM_GROUP = 16
SSM_GROUPS = SSM_WIDTH // SSM_GROUP
SSM_STATE = 64
NSA_WIDTH = D_MIX - SSM_WIDTH
HEAD_DIM = 64
NSA_HEADS = NSA_WIDTH // HEAD_DIM
KV_HEADS = 4
Q_PER_KV = NSA_HEADS // KV_HEADS
KV_WIDTH = KV_HEADS * HEAD_DIM
CMP_BLOCK = 32
CMP_STRIDE = 16
CMP_HIDDEN = 256
SLC_BLOCK = 64
SLC_TOPK = 16
WINDOW = 512
Q_BLOCK = 128
SLC_Q_BLOCK = 64
REL_BUCKETS = 32
REL_MAX_DIST = 128
D_FF = 4 * D_MODEL
EPS = 1e-6
NEG = -1e30
FORCED_SCORE = 1e4
D_IN = SSM_WIDTH + NSA_WIDTH + 6 * KV_WIDTH + 3 * NSA_HEADS

kernel_name = "hymba_s5_nsa_hybrid_block"


def rmsnorm(x, w):
    xf = x.astype(jnp.float32)
    y = xf * lax.rsqrt(jnp.mean(xf * xf, axis=-1, keepdims=True) + EPS)
    return (y * w.astype(jnp.float32)).astype(x.dtype)


def masked_softmax(logits, mask):
    p = jax.nn.softmax(jnp.where(mask, logits, NEG), axis=-1)
    return p * jnp.any(mask, axis=-1, keepdims=True)


def rel_bucket(dist):
    dist = jnp.maximum(dist, 0)
    max_exact = REL_BUCKETS // 2
    large = max_exact + (jnp.log(jnp.maximum(dist, 1).astype(jnp.float32) / max_exact)
                         / math.log(REL_MAX_DIST / max_exact) * (REL_BUCKETS - max_exact)).astype(jnp.int32)
    large = jnp.minimum(large, REL_BUCKETS - 1)
    return jnp.where(dist < max_exact, dist, large)


def s5_mixer(u, a_re, a_im, log_dt, b_re, b_im, c_re, c_im, d, w_glu, b_glu):
    bsz, L, _ = u.shape
    ug = u.reshape(bsz, L, SSM_GROUPS, SSM_GROUP)
    dt = jnp.exp(log_dt)[:, None]
    mag = jnp.exp(dt * a_re)
    abar_re = mag * jnp.cos(dt * a_im)
    abar_im = mag * jnp.sin(dt * a_im)
    den = a_re * a_re + a_im * a_im
    zr, zi = abar_re - 1.0, abar_im
    coef_re = (zr * a_re + zi * a_im) / den
    coef_im = (zi * a_re - zr * a_im) / den
    bu_re = jnp.einsum('blgh,gph->blgp', ug, b_re)
    bu_im = jnp.einsum('blgh,gph->blgp', ug, b_im)
    xb_re = coef_re * bu_re - coef_im * bu_im
    xb_im = coef_re * bu_im + coef_im * bu_re
    ar = jnp.broadcast_to(abar_re, xb_re.shape)
    ai = jnp.broadcast_to(abar_im, xb_re.shape)

    def combine(e1, e2):
        a1r, a1i, b1r, b1i = e1
        a2r, a2i, b2r, b2i = e2
        return (a1r * a2r - a1i * a2i, a1r * a2i + a1i * a2r,
                a2r * b1r - a2i * b1i + b2r, a2r * b1i + a2i * b1r + b2i)

    _, _, s_re, s_im = lax.associative_scan(combine, (ar, ai, xb_re, xb_im), axis=1)
    y = (jnp.einsum('blgp,ghp->blgh', s_re, c_re) - jnp.einsum('blgp,ghp->blgh', s_im, c_im)
         + d * ug).reshape(bsz, L, SSM_WIDTH)
    z = jax.nn.gelu(y)
    return z * jax.nn.sigmoid(z @ w_glu + b_glu)


def compress(k, idx, pe, w1, w2):
    bsz = k.shape[0]
    n_cmp = idx.shape[0]
    blocks = k[:, idx] + pe[None, None, :, None, :]
    flat = blocks.transpose(0, 1, 3, 2, 4).reshape(bsz, n_cmp, KV_HEADS, CMP_BLOCK * HEAD_DIM)
    return jax.nn.gelu(flat @ w1) @ w2


def nsa_mixer(q, kc, vc, ks, vs, kw, vw, gate_logits, pe_ck, w_ck1, w_ck2, pe_cv, w_cv1, w_cv2, rel_table):
    bsz, L, _ = q.shape
    q = q.reshape(bsz, L, KV_HEADS, Q_PER_KV, HEAD_DIM) * (HEAD_DIM ** -0.5)
    kc, vc, ks, vs, kw, vw = [t.reshape(bsz, L, KV_HEADS, HEAD_DIM) for t in (kc, vc, ks, vs, kw, vw)]
    pos = jnp.arange(L)
    table_g = rel_table.reshape(REL_BUCKETS, KV_HEADS, Q_PER_KV).transpose(1, 0, 2)
    g_ix = jnp.arange(KV_HEADS)[None, :, None, None]
    b_ix = jnp.arange(bsz)[:, None, None, None]

    n_cmp = (L - CMP_BLOCK) // CMP_STRIDE + 1
    cmp_idx = np.arange(n_cmp)[:, None] * CMP_STRIDE + np.arange(CMP_BLOCK)[None, :]
    k_cmp = compress(kc, cmp_idx, pe_ck, w_ck1, w_ck2)
    v_cmp = compress(vc, cmp_idx, pe_cv, w_cv1, w_cv2)
    dist_c = pos[:, None] - jnp.asarray(cmp_idx[:, -1])[None, :]
    bias_c = table_g[:, rel_bucket(dist_c)].transpose(0, 3, 1, 2)
    logits_c = jnp.einsum('blgrd,bngd->bgrln', q, k_cmp).astype(jnp.float32) + bias_c
    p_c = masked_softmax(logits_c, dist_c >= 0)
    o_cmp = jnp.einsum('bgrln,bngd->blgrd', p_c.astype(v_cmp.dtype), v_cmp)

    n_slc = L // SLC_BLOCK
    overlap = ((cmp_idx[:, :, None] // SLC_BLOCK) == np.arange(n_slc)[None, None, :]).sum(1) / CMP_BLOCK
    imp = jnp.einsum('bgrln,ns->bgls', p_c, jnp.asarray(overlap, dtype=jnp.float32))
    cur = pos // SLC_BLOCK
    blk = jnp.arange(n_slc)
    forced = (blk[None, :] == 0) | (blk[None, :] == cur[:, None]) | (blk[None, :] == cur[:, None] - 1)
    valid = blk[None, :] <= cur[:, None]
    score = jnp.where(forced, FORCED_SCORE, jnp.where(valid, imp, -1.0))
    n_top = min(SLC_TOPK, n_slc)
    _, sel = lax.top_k(score, n_top)

    ks_g = ks.reshape(bsz, n_slc, SLC_BLOCK, KV_HEADS, HEAD_DIM).transpose(0, 3, 1, 2, 4)
    vs_g = vs.reshape(bsz, n_slc, SLC_BLOCK, KV_HEADS, HEAD_DIM).transpose(0, 3, 1, 2, 4)
    n_qb = L // SLC_Q_BLOCK
    q_blocks = q.reshape(bsz, n_qb, SLC_Q_BLOCK, KV_HEADS, Q_PER_KV, HEAD_DIM).transpose(1, 0, 2, 3, 4, 5)
    sel_blocks = sel.reshape(bsz, KV_HEADS, n_qb, SLC_Q_BLOCK, n_top).transpose(2, 0, 1, 3, 4)
    pos_blocks = pos.reshape(n_qb, SLC_Q_BLOCK)
    n_tok = n_top * SLC_BLOCK

    def slc_block(args):
        qb_, selb, posb = args
        kg = ks_g[b_ix, g_ix, selb].reshape(bsz, KV_HEADS, SLC_Q_BLOCK, n_tok, HEAD_DIM)
        vg = vs_g[b_ix, g_ix, selb].reshape(bsz, KV_HEADS, SLC_Q_BLOCK, n_tok, HEAD_DIM)
        tok = (selb[..., None] * SLC_BLOCK + jnp.arange(SLC_BLOCK)).reshape(bsz, KV_HEADS, SLC_Q_BLOCK, n_tok)
        dist = posb[None, None, :, None] - tok
        bias = table_g[g_ix, rel_bucket(dist)].transpose(0, 1, 4, 2, 3)
        logits = jnp.einsum('bqgrd,bgqtd->bgrqt', qb_, kg).astype(jnp.float32) + bias
        p = masked_softmax(logits, (dist >= 0)[:, :, None])
        return jnp.einsum('bgrqt,bgqtd->bqgrd', p.astype(vg.dtype), vg)

    o_slc = lax.map(slc_block, (q_blocks, sel_blocks, pos_blocks))
    o_slc = o_slc.transpose(1, 0, 2, 3, 4, 5).reshape(bsz, L, KV_HEADS, Q_PER_KV, HEAD_DIM)

    n_wb = L // Q_BLOCK
    span = WINDOW + Q_BLOCK
    key_idx = np.arange(n_wb)[:, None] * Q_BLOCK + np.arange(span)[None, :]
    kw_pad = jnp.pad(kw, ((0, 0), (WINDOW, 0), (0, 0), (0, 0)))
    vw_pad = jnp.pad(vw, ((0, 0), (WINDOW, 0), (0, 0), (0, 0)))
    kwin = kw_pad[:, key_idx]
    vwin = vw_pad[:, key_idx]
    qw = q.reshape(bsz, n_wb, Q_BLOCK, KV_HEADS, Q_PER_KV, HEAD_DIM)
    kpos = jnp.asarray(key_idx - WINDOW)
    qpos = pos.reshape(n_wb, Q_BLOCK)
    dist_w = qpos[:, :, None] - kpos[:, None, :]
    mask_w = (dist_w >= 0) & (dist_w < WINDOW) & (kpos[:, None, :] >= 0)
    bias_w = table_g[:, rel_bucket(dist_w)].transpose(1, 0, 4, 2, 3)
    logits_w = jnp.einsum('bnqgrd,bnkgd->bngrqk', qw, kwin).astype(jnp.float32) + bias_w
    p_w = masked_softmax(logits_w, mask_w[:, None, None])
    o_win = jnp.einsum('bngrqk,bnkgd->bnqgrd', p_w.astype(vwin.dtype), vwin)
    o_win = o_win.reshape(bsz, L, KV_HEADS, Q_PER_KV, HEAD_DIM)

    g = jax.nn.sigmoid(gate_logits.reshape(bsz, L, KV_HEADS, Q_PER_KV, 3))
    o = g[..., 0:1] * o_cmp + g[..., 1:2] * o_slc + g[..., 2:3] * o_win
    return o.reshape(bsz, L, NSA_WIDTH)


def setup_inputs(seed: int = 0) -> dict:
    key = jax.random.key(seed)
    ks = jax.random.split(key, 26)
    nrm = jax.random.normal
    f32 = jnp.float32
    x = nrm(ks[0], (BATCH, SEQ, D_MODEL), f32)
    norm1_w = 1.0 + 0.01 * nrm(ks[1], (DEPTH, D_MODEL), f32)
    w_in = nrm(ks[2], (DEPTH, D_MODEL, D_IN), f32) * D_MODEL ** -0.5
    ssm_a_re = -0.5 * jnp.exp(0.02 * nrm(ks[3], (DEPTH, SSM_GROUPS, SSM_STATE), f32))
    ssm_a_im = (math.pi * jnp.arange(SSM_STATE, dtype=f32))[None, None, :] + 0.01 * nrm(ks[4], (DEPTH, SSM_GROUPS, SSM_STATE), f32)
    ssm_log_dt = jax.random.uniform(ks[5], (DEPTH, SSM_GROUPS), f32, math.log(1e-3), math.log(1e-1))
    ssm_b_re = nrm(ks[6], (DEPTH, SSM_GROUPS, SSM_STATE, SSM_GROUP), f32) * (2 * SSM_GROUP) ** -0.5
    ssm_b_im = nrm(ks[7], (DEPTH, SSM_GROUPS, SSM_STATE, SSM_GROUP), f32) * (2 * SSM_GROUP) ** -0.5
    ssm_c_re = nrm(ks[8], (DEPTH, SSM_GROUPS, SSM_GROUP, SSM_STATE), f32) * (2 * SSM_STATE) ** -0.5
    ssm_c_im = nrm(ks[9], (DEPTH, SSM_GROUPS, SSM_GROUP, SSM_STATE), f32) * (2 * SSM_STATE) ** -0.5
    ssm_d = nrm(ks[10], (DEPTH, SSM_GROUPS, SSM_GROUP), f32)
    w_glu = nrm(ks[11], (DEPTH, SSM_WIDTH, SSM_WIDTH), f32) * SSM_WIDTH ** -0.5
    b_glu = 0.01 * nrm(ks[12], (DEPTH, SSM_WIDTH), f32)
    pe_ck = 0.1 * nrm(ks[13], (DEPTH, CMP_BLOCK, HEAD_DIM), f32)
    w_ck1 = nrm(ks[14], (DEPTH, CMP_BLOCK * HEAD_DIM, CMP_HIDDEN), f32) * (CMP_BLOCK * HEAD_DIM) ** -0.5
    w_ck2 = nrm(ks[15], (DEPTH, CMP_HIDDEN, HEAD_DIM), f32) * CMP_HIDDEN ** -0.5
    pe_cv = 0.1 * nrm(ks[16], (DEPTH, CMP_BLOCK, HEAD_DIM), f32)
    w_cv1 = nrm(ks[17], (DEPTH, CMP_BLOCK * HEAD_DIM, CMP_HIDDEN), f32) * (CMP_BLOCK * HEAD_DIM) ** -0.5
    w_cv2 = nrm(ks[18], (DEPTH, CMP_HIDDEN, HEAD_DIM), f32) * CMP_HIDDEN ** -0.5
    w_out = nrm(ks[19], (DEPTH, D_MIX, D_MODEL), f32) * D_MIX ** -0.5
    norm2_w = 1.0 + 0.01 * nrm(ks[20], (DEPTH, D_MODEL), f32)
    w_up = nrm(ks[21], (DEPTH, D_MODEL, D_FF), f32) * D_MODEL ** -0.5
    w_down = nrm(ks[22], (DEPTH, D_FF, D_MODEL), f32) * D_FF ** -0.5
    rel_table = 0.5 * nrm(ks[23], (REL_BUCKETS, NSA_HEADS), f32)
    norm_f_w = 1.0 + 0.01 * nrm(ks[24], (D_MODEL,), f32)
    return {"x": x, "norm1_w": norm1_w, "w_in": w_in, "ssm_a_re": ssm_a_re, "ssm_a_im": ssm_a_im,
            "ssm_log_dt": ssm_log_dt, "ssm_b_re": ssm_b_re, "ssm_b_im": ssm_b_im, "ssm_c_re": ssm_c_re,
            "ssm_c_im": ssm_c_im, "ssm_d": ssm_d, "w_glu": w_glu, "b_glu": b_glu, "pe_ck": pe_ck,
            "w_ck1": w_ck1, "w_ck2": w_ck2, "pe_cv": pe_cv, "w_cv1": w_cv1, "w_cv2": w_cv2,
            "w_out": w_out, "norm2_w": norm2_w, "w_up": w_up, "w_down": w_down,
            "rel_table": rel_table, "norm_f_w": norm_f_w}


def reference(x, norm1_w, w_in, ssm_a_re, ssm_a_im, ssm_log_dt, ssm_b_re, ssm_b_im, ssm_c_re, ssm_c_im,
              ssm_d, w_glu, b_glu, pe_ck, w_ck1, w_ck2, pe_cv, w_cv1, w_cv2, w_out, norm2_w, w_up, w_down,
              rel_table, norm_f_w):
    sizes = [SSM_WIDTH, NSA_WIDTH] + [KV_WIDTH] * 6
    offsets = np.cumsum(sizes).tolist()
    for l in range(DEPTH):
        h = rmsnorm(x, norm1_w[l])
        proj = h @ w_in[l]
        u, q, kc, vc, ks_, vs_, kw, vw, gl = jnp.split(proj, offsets, axis=-1)
        y_ssm = s5_mixer(u, ssm_a_re[l], ssm_a_im[l], ssm_log_dt[l], ssm_b_re[l], ssm_b_im[l],
                         ssm_c_re[l], ssm_c_im[l], ssm_d[l], w_glu[l], b_glu[l])
        y_nsa = nsa_mixer(q, kc, vc, ks_, vs_, kw, vw, gl, pe_ck[l], w_ck1[l], w_ck2[l],
                          pe_cv[l], w_cv1[l], w_cv2[l], rel_table)
        x = x + jnp.concatenate([y_ssm, y_nsa], axis=-1) @ w_out[l]
        h = rmsnorm(x, norm2_w[l])
        x = x + jnp.square(jax.nn.relu(h @ w_up[l])) @ w_down[l]
    return rmsnorm(x, norm_f_w)
```

```python
import functools
import math

import numpy as np
import jax
import jax.numpy as jnp
from jax import lax
from jax.experimental import pallas as pl
from jax.experimental.pallas import tpu as pltpu

F32 = jnp.float32
BF16 = jnp.bfloat16

D_MODEL = 2048
SSM_WIDTH = 1024
SSM_GROUP = 16
SSM_GROUPS = 64
SSM_STATE = 64
NSA_WIDTH = 1024
HEAD_DIM = 64
NSA_HEADS = 16
KV_HEADS = 4
Q_PER_KV = 4
KV_WIDTH = 256
CMP_BLOCK = 32
CMP_STRIDE = 16
CMP_HIDDEN = 256
SLC_BLOCK = 64
SLC_TOPK = 16
WINDOW = 512
REL_BUCKETS = 32
REL_MAX_DIST = 128
D_FF = 8192
EPS = 1e-6
NEG = -1e30
FORCED_SCORE = 1e4
D_IN = SSM_WIDTH + NSA_WIDTH + 6 * KV_WIDTH + 3 * NSA_HEADS
D_IN_PAD = 3840
GATE_COL = SSM_WIDTH + NSA_WIDTH + 6 * KV_WIDTH

S5_CHUNK = 16
QT = 128
N_SLC_PAD = 64
UNSELECTED = -1e9
VMEM_LIMIT = 56 * 1024 * 1024
HI = lax.Precision.HIGHEST


def _cparams(*sem):
    return pltpu.CompilerParams(dimension_semantics=sem, vmem_limit_bytes=VMEM_LIMIT)


def _rms(x, w):
    ms = jnp.mean(x * x, axis=-1, keepdims=True)
    return x * lax.rsqrt(ms + EPS) * w


def _norm_matmul_kernel(x_ref, nw_ref, w_ref, o_ref, h_ref):
    @pl.when(pl.program_id(1) == 0)
    def _():
        h_ref[...] = _rms(x_ref[...], nw_ref[...]).astype(BF16)

    o_ref[...] = jnp.dot(h_ref[...], w_ref[...], preferred_element_type=F32)


def _norm_matmul(x2, nw, w, tm=512, tn=256):
    t, d = x2.shape
    n = w.shape[1]
    return pl.pallas_call(
        _norm_matmul_kernel,
        grid=(t // tm, n // tn),
        in_specs=[pl.BlockSpec((tm, d), lambda i, j: (i, 0)),
                  pl.BlockSpec((1, d), lambda i, j: (0, 0)),
                  pl.BlockSpec((d, tn), lambda i, j: (0, j))],
        out_specs=pl.BlockSpec((tm, tn), lambda i, j: (i, j)),
        out_shape=jax.ShapeDtypeStruct((t, n), F32),
        scratch_shapes=[pltpu.VMEM((tm, d), BF16)],
        compiler_params=_cparams("parallel", "arbitrary"),
        name="norm_in_proj",
    )(x2, nw, w)


def _s5_operators(a_re, a_im, log_dt, b_re, b_im, c_re, c_im, d):
    q = S5_CHUNK
    g, p = a_re.shape
    dt = jnp.exp(log_dt)[:, None]
    lam_re, lam_im = dt * a_re, dt * a_im
    mag1 = jnp.exp(lam_re)
    abar_re, abar_im = mag1 * jnp.cos(lam_im), mag1 * jnp.sin(lam_im)
    den = a_re * a_re + a_im * a_im
    zr, zi = abar_re - 1.0, abar_im
    coef_re = (zr * a_re + zi * a_im) / den
    coef_im = (zi * a_re - zr * a_im) / den
    bb_re = coef_re[..., None] * b_re - coef_im[..., None] * b_im
    bb_im = coef_re[..., None] * b_im + coef_im[..., None] * b_re
    k = jnp.arange(q + 1, dtype=F32)[:, None, None]
    mag = jnp.exp(k * lam_re)
    pw_re, pw_im = mag * jnp.cos(k * lam_im), mag * jnp.sin(k * lam_im)
    m_re = pw_re[..., None] * bb_re - pw_im[..., None] * bb_im
    m_im = pw_re[..., None] * bb_im + pw_im[..., None] * bb_re
    kern = (jnp.einsum('ghp,kgpi->gkhi', c_re, m_re[:q], precision=HI)
            - jnp.einsum('ghp,kgpi->gkhi', c_im, m_im[:q], precision=HI))
    kern = kern.at[:, 0].add(jax.vmap(jnp.diag)(d))
    s_ix = np.arange(q)[:, None]
    t_ix = np.arange(q)[None, :]
    lag = np.clip(t_ix - s_ix, 0, q - 1)
    t_intra = jnp.where((t_ix >= s_ix)[None, :, :, None, None], kern[:, lag], 0.0)
    t_intra = t_intra.transpose(0, 1, 4, 2, 3).reshape(g, q * SSM_GROUP, q * SSM_GROUP)
    rev = np.arange(q - 1, -1, -1)
    s_re = m_re[rev].transpose(1, 0, 3, 2).reshape(g, q * SSM_GROUP, p)
    s_im = m_im[rev].transpose(1, 0, 3, 2).reshape(g, q * SSM_GROUP, p)
    s_end = jnp.concatenate([s_re, s_im], axis=-1)
    pr, pi = pw_re[1:], pw_im[1:]
    o_re = c_re[None] * pr[:, :, None, :] - c_im[None] * pi[:, :, None, :]
    o_im = -(c_re[None] * pi[:, :, None, :] + c_im[None] * pr[:, :, None, :])
    o_re = o_re.transpose(1, 3, 0, 2).reshape(g, p, q * SSM_GROUP)
    o_im = o_im.transpose(1, 3, 0, 2).reshape(g, p, q * SSM_GROUP)
    o_carry = jnp.concatenate([o_re, o_im], axis=1)
    a1 = jnp.concatenate([pw_re[q], pw_re[q]], axis=-1)
    a2 = jnp.concatenate([-pw_im[q], pw_im[q]], axis=-1)
    return t_intra, s_end, o_carry, a1, a2


def _s5_state_kernel(u_ref, s_ref, o_ref):
    o_ref[0] = jnp.dot(u_ref[0], s_ref[0], precision=HI, preferred_element_type=F32)


def _s5_scan_kernel(h_ref, a1_ref, a2_ref, o_ref):
    a1 = a1_ref[...]
    a2 = a2_ref[...]

    def body(c, h):
        o_ref[c] = h
        return a1 * h + a2 * pltpu.roll(h, SSM_STATE, axis=1) + h_ref[c]

    lax.fori_loop(0, h_ref.shape[0], body, jnp.zeros(a1.shape, F32))


def _s5_out_kernel(u_ref, t_ref, h_ref, oc_ref, y_ref):
    y_ref[0] = (jnp.dot(u_ref[0], t_ref[0], precision=HI, preferred_element_type=F32)
                + jnp.dot(h_ref[0], oc_ref[0], precision=HI, preferred_element_type=F32))


def _glu_kernel(y_ref, w_ref, b_ref, o_ref):
    z = jax.nn.gelu(y_ref[...])
    gate = jnp.dot(z.astype(BF16), w_ref[...], preferred_element_type=F32) + b_ref[...]
    o_ref[...] = z * jax.nn.sigmoid(gate)


def _s5_mixer(u, ops, w_glu, b_glu):
    t_intra, s_end, o_carry, a1, a2 = ops
    bsz, L, _ = u.shape
    q = S5_CHUNK
    nc = L // q
    g = SSM_GROUPS
    lanes = q * SSM_GROUP
    rows = bsz * nc
    ur = u.reshape(bsz, nc, q, g, SSM_GROUP).transpose(3, 0, 1, 2, 4).reshape(g, rows, lanes)
    hend = pl.pallas_call(
        _s5_state_kernel,
        grid=(g,),
        in_specs=[pl.BlockSpec((1, rows, lanes), lambda i: (i, 0, 0)),
                  pl.BlockSpec((1, lanes, 2 * SSM_STATE), lambda i: (i, 0, 0))],
        out_specs=pl.BlockSpec((1, rows, 2 * SSM_STATE), lambda i: (i, 0, 0)),
        out_shape=jax.ShapeDtypeStruct((g, rows, 2 * SSM_STATE), F32),
        compiler_params=_cparams("parallel"),
        name="s5_chunk_state",
    )(ur, s_end)
    hend_t = hend.reshape(g, bsz, nc, 2 * SSM_STATE).transpose(2, 0, 1, 3).reshape(nc, g * bsz, 2 * SSM_STATE)
    a1r = jnp.repeat(a1, bsz, axis=0)
    a2r = jnp.repeat(a2, bsz, axis=0)
    rb = 32
    hprev_t = pl.pallas_call(
        _s5_scan_kernel,
        grid=(g * bsz // rb,),
        in_specs=[pl.BlockSpec((nc, rb, 2 * SSM_STATE), lambda i: (0, i, 0)),
                  pl.BlockSpec((rb, 2 * SSM_STATE), lambda i: (i, 0)),
                  pl.BlockSpec((rb, 2 * SSM_STATE), lambda i: (i, 0))],
        out_specs=pl.BlockSpec((nc, rb, 2 * SSM_STATE), lambda i: (0, i, 0)),
        out_shape=jax.ShapeDtypeStruct((nc, g * bsz, 2 * SSM_STATE), F32),
        compiler_params=_cparams("parallel"),
        name="s5_chunk_scan",
    )(hend_t, a1r, a2r)
    hprev = hprev_t.reshape(nc, g, bsz, 2 * SSM_STATE).transpose(1, 2, 0, 3).reshape(g, rows, 2 * SSM_STATE)
    yr = pl.pallas_call(
        _s5_out_kernel,
        grid=(g,),
        in_specs=[pl.BlockSpec((1, rows, lanes), lambda i: (i, 0, 0)),
                  pl.BlockSpec((1, lanes, lanes), lambda i: (i, 0, 0)),
                  pl.BlockSpec((1, rows, 2 * SSM_STATE), lambda i: (i, 0, 0)),
                  pl.BlockSpec((1, 2 * SSM_STATE, lanes), lambda i: (i, 0, 0))],
        out_specs=pl.BlockSpec((1, rows, lanes), lambda i: (i, 0, 0)),
        out_shape=jax.ShapeDtypeStruct((g, rows, lanes), F32),
        compiler_params=_cparams("parallel"),
        name="s5_chunk_out",
    )(ur, t_intra, hprev, o_carry)
    y = yr.reshape(g, bsz, nc, q, SSM_GROUP).transpose(1, 2, 3, 0, 4).reshape(bsz * L, SSM_WIDTH)
    tm = 512
    return pl.pallas_call(
        _glu_kernel,
        grid=(bsz * L // tm,),
        in_specs=[pl.BlockSpec((tm, SSM_WIDTH), lambda i: (i, 0)),
                  pl.BlockSpec((SSM_WIDTH, SSM_WIDTH), lambda i: (0, 0)),
                  pl.BlockSpec((1, SSM_WIDTH), lambda i: (0, 0))],
        out_specs=pl.BlockSpec((tm, SSM_WIDTH), lambda i: (i, 0)),
        out_shape=jax.ShapeDtypeStruct((bsz * L, SSM_WIDTH), F32),
        compiler_params=_cparams("parallel"),
        name="s5_gelu_glu",
    )(y, w_glu.astype(BF16), b_glu.reshape(1, SSM_WIDTH))


def _compress_kernel(cf_ref, w1_ref, pe_ref, w1full_ref, w2_ref, o_ref):
    prod = jnp.dot(cf_ref[0, 0], w1_ref[...], precision=HI, preferred_element_type=F32)
    nc = prod.shape[0]
    pe_h = jnp.dot(jnp.broadcast_to(pe_ref[...], (8, pe_ref.shape[1])), w1full_ref[...],
                   precision=HI, preferred_element_type=F32)[0:1]
    hid = prod[:, :CMP_HIDDEN] + pltpu.roll(prod[:, CMP_HIDDEN:], nc - 1, axis=0) + pe_h
    o_ref[0, 0] = jnp.dot(jax.nn.gelu(hid), w2_ref[...], precision=HI, preferred_element_type=F32)


def _compress(k, pe, w1, w2):
    bsz, L, g, dh = k.shape
    nc = L // CMP_STRIDE
    half = CMP_STRIDE * dh
    cf = k.reshape(bsz, nc, CMP_STRIDE, g, dh).transpose(0, 3, 1, 2, 4).reshape(bsz, g, nc, half)
    w1cat = jnp.concatenate([w1[:half], w1[half:]], axis=1)
    return pl.pallas_call(
        _compress_kernel,
        grid=(bsz, g),
        in_specs=[pl.BlockSpec((1, 1, nc, half), lambda b, h: (b, h, 0, 0)),
                  pl.BlockSpec((half, 2 * CMP_HIDDEN), lambda b, h: (0, 0)),
                  pl.BlockSpec((1, 2 * half), lambda b, h: (0, 0)),
                  pl.BlockSpec((2 * half, CMP_HIDDEN), lambda b, h: (0, 0)),
                  pl.BlockSpec((CMP_HIDDEN, dh), lambda b, h: (0, 0))],
        out_specs=pl.BlockSpec((1, 1, nc, dh), lambda b, h: (b, h, 0, 0)),
        out_shape=jax.ShapeDtypeStruct((bsz, g, nc, dh), F32),
        compiler_params=_cparams("parallel", "parallel"),
        name="nsa_compress",
    )(cf, w1cat, pe.reshape(1, 2 * half), w1, w2)


def _rel_bucket(dist):
    dist = jnp.maximum(dist, 0)
    max_exact = REL_BUCKETS // 2
    large = max_exact + (jnp.log(jnp.maximum(dist, 1).astype(F32) / max_exact)
                         / math.log(REL_MAX_DIST / max_exact) * (REL_BUCKETS - max_exact)).astype(jnp.int32)
    large = jnp.minimum(large, REL_BUCKETS - 1)
    return jnp.where(dist < max_exact, dist, large)


def _lane_cat(parts):
    return jnp.concatenate(parts, axis=1)


def _split3(x):
    hi = x.astype(BF16)
    r1 = x - hi.astype(F32)
    mid = r1.astype(BF16)
    lo = (r1 - mid.astype(F32)).astype(BF16)
    return hi, mid, lo


def _nsa_kernel(qT_ref, kc_ref, vcT_ref, bc_ref, ovT_ref, ks_ref, vsT_ref, kw_ref, vwT_ref, wb_ref, crow_ref,
                gl_ref, o_ref, *, n_slc):
    i = pl.program_id(2)
    R = Q_PER_KV
    qT = qT_ref[0, 0] * 0.125
    q_heads = [qT[r * HEAD_DIM:(r + 1) * HEAD_DIM] for r in range(R)]
    qcat = _lane_cat(q_heads)
    t_pos = lax.broadcasted_iota(jnp.int32, (1, QT), 1) + i * QT
    t_pos_r = (lax.broadcasted_iota(jnp.int32, (1, R * QT), 1) & (QT - 1)) + i * QT

    ncp = kc_ref.shape[2]
    sc = jnp.dot(kc_ref[0, 0], qcat, preferred_element_type=F32)
    sc = sc + _lane_cat([bc_ref[0, r, 0] for r in range(R)])
    n_end = lax.broadcasted_iota(jnp.int32, (ncp, 1), 0) * CMP_STRIDE + (CMP_BLOCK - 1)
    mask_c = n_end <= t_pos_r
    sm = jnp.where(mask_c, sc, NEG)
    m_c = jnp.max(sm, axis=0, keepdims=True)
    e_c = jnp.where(mask_c, jnp.exp(sm - m_c), 0.0)
    l_c = jnp.sum(e_c, axis=0, keepdims=True)
    inv_c = jnp.where(l_c > 0.0, 1.0 / jnp.where(l_c > 0.0, l_c, 1.0), 0.0)
    o_cmp = jnp.dot(vcT_ref[0, 0], e_c.astype(BF16), preferred_element_type=F32) * inv_c
    p_c = e_c * inv_c
    p_sum = p_c[:, 0:QT]
    for r in range(1, R):
        p_sum = p_sum + p_c[:, r * QT:(r + 1) * QT]

    ov = ovT_ref[...]
    imp = None
    for part in _split3(p_sum):
        term = jnp.dot(ov, part, preferred_element_type=F32)
        imp = term if imp is None else imp + term
    s_ix = lax.broadcasted_iota(jnp.int32, (N_SLC_PAD, QT), 0)
    cur = t_pos >> int(math.log2(SLC_BLOCK))
    forced = (s_ix == 0) | (s_ix == cur) | (s_ix == cur - 1)
    score = jnp.where(forced, FORCED_SCORE, jnp.where(s_ix <= cur, imp, -1.0))
    score = jnp.where(s_ix < n_slc, score, -2.0)
    rank = jnp.zeros((N_SLC_PAD, QT), F32)
    for sp in range(n_slc):
        row = score[sp:sp + 1, :]
        beats = (row > score) | ((row == score) & (s_ix > sp))
        rank = rank + jnp.where(beats, 1.0, 0.0)
    sel_neg = jnp.where(rank < float(SLC_TOPK), 0.0, UNSELECTED).astype(BF16)

    slab = WINDOW + QT
    k_rel = lax.broadcasted_iota(jnp.int32, (slab, 1), 0) + (i - WINDOW // QT) * QT
    wb = _lane_cat([wb_ref[0, r] for r in range(R)])
    wb = jnp.where(k_rel >= 0, wb, NEG)

    qsel = _lane_cat([jnp.concatenate([q_heads[r], sel_neg], axis=0) for r in range(R)])
    c_row = crow_ref[0]
    pad_tiles = WINDOW // QT

    def far_tile(j, carry):
        m, l, acc = carry
        s = jnp.dot(ks_ref[0, 0, j + pad_tiles], qsel, preferred_element_type=F32) + c_row
        m_new = jnp.maximum(m, jnp.max(s, axis=0, keepdims=True))
        alpha = jnp.exp(m - m_new)
        p = jnp.exp(s - m_new)
        l = alpha * l + jnp.sum(p, axis=0, keepdims=True)
        acc = alpha * acc + jnp.dot(vsT_ref[0, 0, j + pad_tiles], p.astype(BF16), preferred_element_type=F32)
        return m_new, l, acc

    init = (jnp.full((1, R * QT), NEG, F32), jnp.zeros((1, R * QT), F32), jnp.zeros((HEAD_DIM, R * QT), F32))
    m, l, acc = lax.fori_loop(0, jnp.maximum(i - 1, 0), far_tile, init)
    k_near = jnp.concatenate([ks_ref[0, 0, i + pad_tiles - 1], ks_ref[0, 0, i + pad_tiles]], axis=0)
    s = jnp.dot(k_near, qsel, preferred_element_type=F32) + wb[slab - 2 * QT:]
    m_new = jnp.maximum(m, jnp.max(s, axis=0, keepdims=True))
    alpha = jnp.exp(m - m_new)
    p = jnp.exp(s - m_new)
    l = alpha * l + jnp.sum(p, axis=0, keepdims=True)
    p = p.astype(BF16)
    acc = (alpha * acc + jnp.dot(vsT_ref[0, 0, i + pad_tiles - 1], p[:QT], preferred_element_type=F32)
           + jnp.dot(vsT_ref[0, 0, i + pad_tiles], p[QT:], preferred_element_type=F32))
    o_slc = acc * (1.0 / l)

    kw = kw_ref[0, 0, pl.ds(pl.multiple_of(i * QT, QT), slab), :]
    sw = jnp.dot(kw, qcat, preferred_element_type=F32) + wb
    m_w = jnp.max(sw, axis=0, keepdims=True)
    e_w = jnp.exp(sw - m_w)
    l_w = jnp.sum(e_w, axis=0, keepdims=True)
    e_wb = e_w.astype(BF16)
    o_win = None
    for jj in range(slab // QT):
        term = jnp.dot(vwT_ref[0, 0, i + jj], e_wb[jj * QT:(jj + 1) * QT], preferred_element_type=F32)
        o_win = term if o_win is None else o_win + term
    o_win = o_win * (1.0 / l_w)

    gates = [jax.nn.sigmoid(_lane_cat([gl_ref[0, 0, br, r:r + 1, :] for r in range(R)])) for br in range(3)]
    total = gates[0] * o_cmp + gates[1] * o_slc + gates[2] * o_win
    for r in range(R):
        o_ref[0, 0, r * HEAD_DIM:(r + 1) * HEAD_DIM, :] = total[:, r * QT:(r + 1) * QT]


def _nsa_tables(rel_table, L):
    g, r = KV_HEADS, Q_PER_KV
    table_g = rel_table.reshape(REL_BUCKETS, g, r).transpose(1, 2, 0)
    ncp = L // CMP_STRIDE
    ni = L // QT
    t = jnp.arange(L)[None, :]
    n_end = (jnp.arange(ncp) * CMP_STRIDE + CMP_BLOCK - 1)[:, None]
    bias_c = table_g[:, :, _rel_bucket(t - n_end)]
    bias_c = bias_c.reshape(g, r, ncp, ni, QT).transpose(0, 1, 3, 2, 4)
    slab = WINDOW + QT
    dist = (jnp.arange(QT)[None, :] + WINDOW) - jnp.arange(slab)[:, None]
    wb = jnp.where((dist >= 0) & (dist < WINDOW), table_g[:, :, _rel_bucket(dist)], NEG)
    c_row = jnp.repeat(table_g[:, :, REL_BUCKETS - 1], QT, axis=1).reshape(g, 1, r * QT)
    return bias_c, wb, c_row


def _overlap_t(L):
    n_cmp = (L - CMP_BLOCK) // CMP_STRIDE + 1
    n_slc = L // SLC_BLOCK
    cmp_idx = np.arange(n_cmp)[:, None] * CMP_STRIDE + np.arange(CMP_BLOCK)[None, :]
    overlap = ((cmp_idx[:, :, None] // SLC_BLOCK) == np.arange(n_slc)[None, None, :]).sum(1) / CMP_BLOCK
    out = np.zeros((N_SLC_PAD, L // CMP_STRIDE), np.float32)
    out[:n_slc, :n_cmp] = overlap.T
    return jnp.asarray(out, dtype=BF16)


def _key_tiles(k, L):
    bsz, _, g, c = k.shape
    kp = jnp.pad(k.astype(BF16), ((0, 0), (WINDOW, 0), (0, 0), (0, 0)))
    return kp.reshape(bsz, (L + WINDOW) // QT, QT, g, c).transpose(0, 3, 1, 2, 4)


def _nsa_mixer(proj, bsz, L, pe_ck, w_ck1, w_ck2, pe_cv, w_cv1, w_cv2, rel_table):
    g, r = KV_HEADS, Q_PER_KV
    ni = L // QT
    ncp = L // CMP_STRIDE
    n_slc = L // SLC_BLOCK
    p3 = proj.reshape(bsz, L, D_IN_PAD)
    off = SSM_WIDTH
    q = p3[..., off:off + NSA_WIDTH]
    kv = [p3[..., off + NSA_WIDTH + j * KV_WIDTH: off + NSA_WIDTH + (j + 1) * KV_WIDTH].reshape(bsz, L, g, HEAD_DIM)
          for j in range(6)]
    kc, vc, ks, vs, kw, vw = kv
    gl = p3[..., GATE_COL:GATE_COL + 3 * NSA_HEADS]

    k_cmp = _compress(kc, pe_ck, w_ck1, w_ck2).astype(BF16)
    v_cmp_t = _compress(vc, pe_cv, w_cv1, w_cv2).astype(BF16).transpose(0, 1, 3, 2)

    q_t = q.astype(BF16).reshape(bsz, L, g, r * HEAD_DIM).transpose(0, 2, 3, 1)
    onehot = (np.arange(L)[:, None] // SLC_BLOCK == np.arange(N_SLC_PAD)[None, :])
    onehot = jnp.broadcast_to(jnp.asarray(onehot, dtype=BF16)[None, :, None, :], (bsz, L, g, N_SLC_PAD))
    ks_t = _key_tiles(jnp.concatenate([ks.astype(BF16), onehot], axis=-1), L)
    vs_t = _key_tiles(vs, L).transpose(0, 1, 2, 4, 3)
    kw_p = jnp.pad(kw.astype(BF16), ((0, 0), (WINDOW, 0), (0, 0), (0, 0))).transpose(0, 2, 1, 3)
    vw_t = _key_tiles(vw, L).transpose(0, 1, 2, 4, 3)
    gl_t = gl.reshape(bsz, L, g, r, 3).transpose(0, 2, 4, 3, 1)
    bias_c, wb, c_row = _nsa_tables(rel_table, L)
    nt = (L + WINDOW) // QT
    slab = WINDOW + QT

    o_t = pl.pallas_call(
        functools.partial(_nsa_kernel, n_slc=n_slc),
        grid=(bsz, g, ni),
        in_specs=[
            pl.BlockSpec((1, 1, r * HEAD_DIM, QT), lambda b, h, i: (b, h, 0, i)),
            pl.BlockSpec((1, 1, ncp, HEAD_DIM), lambda b, h, i: (b, h, 0, 0)),
            pl.BlockSpec((1, 1, HEAD_DIM, ncp), lambda b, h, i: (b, h, 0, 0)),
            pl.BlockSpec((1, r, 1, ncp, QT), lambda b, h, i: (h, 0, i, 0, 0)),
            pl.BlockSpec((N_SLC_PAD, ncp), lambda b, h, i: (0, 0)),
            pl.BlockSpec((1, 1, nt, QT, HEAD_DIM + N_SLC_PAD), lambda b, h, i: (b, h, 0, 0, 0)),
            pl.BlockSpec((1, 1, nt, HEAD_DIM, QT), lambda b, h, i: (b, h, 0, 0, 0)),
            pl.BlockSpec((1, 1, L + WINDOW, HEAD_DIM), lambda b, h, i: (b, h, 0, 0)),
            pl.BlockSpec((1, 1, nt, HEAD_DIM, QT), lambda b, h, i: (b, h, 0, 0, 0)),
            pl.BlockSpec((1, r, slab, QT), lambda b, h, i: (h, 0, 0, 0)),
            pl.BlockSpec((1, 1, r * QT), lambda b, h, i: (h, 0, 0)),
            pl.BlockSpec((1, 1, 3, r, QT), lambda b, h, i: (b, h, 0, 0, i)),
        ],
        out_specs=pl.BlockSpec((1, 1, r * HEAD_DIM, QT), lambda b, h, i: (b, h, 0, i)),
        out_shape=jax.ShapeDtypeStruct((bsz, g, r * HEAD_DIM, L), F32),
        compiler_params=_cparams("parallel", "parallel", "arbitrary"),
        name="nsa_attention",
    )(q_t, k_cmp, v_cmp_t, bias_c, _overlap_t(L), ks_t, vs_t, kw_p, vw_t, wb, c_row, gl_t)
    return o_t.transpose(0, 3, 1, 2).reshape(bsz * L, NSA_WIDTH)


def _out_proj_kernel(x_ref, ys_ref, yn_ref, w_ref, o_ref):
    half = ys_ref.shape[1]
    acc = jnp.dot(ys_ref[...].astype(BF16), w_ref[:half, :], preferred_element_type=F32)
    acc = acc + jnp.dot(yn_ref[...].astype(BF16), w_ref[half:, :], preferred_element_type=F32)
    o_ref[...] = x_ref[...] + acc


def _out_proj(x2, y_ssm, y_nsa, w, tm=512, tn=512):
    t, d = x2.shape
    return pl.pallas_call(
        _out_proj_kernel,
        grid=(t // tm, d // tn),
        in_specs=[pl.BlockSpec((tm, tn), lambda i, j: (i, j)),
                  pl.BlockSpec((tm, SSM_WIDTH), lambda i, j: (i, 0)),
                  pl.BlockSpec((tm, NSA_WIDTH), lambda i, j: (i, 0)),
                  pl.BlockSpec((SSM_WIDTH + NSA_WIDTH, tn), lambda i, j: (0, j))],
        out_specs=pl.BlockSpec((tm, tn), lambda i, j: (i, j)),
        out_shape=jax.ShapeDtypeStruct((t, d), F32),
        compiler_params=_cparams("parallel", "arbitrary"),
        name="out_proj",
    )(x2, y_ssm, y_nsa, w)


def _mlp_kernel(x_ref, n2_ref, wu_ref, wd_ref, nf_ref, o_ref, h_ref, acc_ref):
    f = pl.program_id(1)

    @pl.when(f == 0)
    def _():
        h_ref[...] = _rms(x_ref[...], n2_ref[...]).astype(BF16)
        acc_ref[...] = jnp.zeros_like(acc_ref)

    a = jnp.dot(h_ref[...], wu_ref[...], preferred_element_type=F32)
    a = jnp.square(jnp.maximum(a, 0.0))
    acc_ref[...] += jnp.dot(a.astype(BF16), wd_ref[...], preferred_element_type=F32)

    @pl.when(f == pl.num_programs(1) - 1)
    def _():
        o_ref[...] = _rms(x_ref[...] + acc_ref[...], nf_ref[...])


def _mlp(x2, n2, wu, wd, nf, tm=512, tf=512):
    t, d = x2.shape
    ff = wu.shape[1]
    return pl.pallas_call(
        _mlp_kernel,
        grid=(t // tm, ff // tf),
        in_specs=[pl.BlockSpec((tm, d), lambda i, f: (i, 0)),
                  pl.BlockSpec((1, d), lambda i, f: (0, 0)),
                  pl.BlockSpec((d, tf), lambda i, f: (0, f)),
                  pl.BlockSpec((tf, d), lambda i, f: (f, 0)),
                  pl.BlockSpec((1, d), lambda i, f: (0, 0))],
        out_specs=pl.BlockSpec((tm, d), lambda i, f: (i, 0)),
        out_shape=jax.ShapeDtypeStruct((t, d), F32),
        scratch_shapes=[pltpu.VMEM((tm, d), BF16), pltpu.VMEM((tm, d), F32)],
        compiler_params=_cparams("parallel", "arbitrary"),
        name="mlp_final_norm",
    )(x2, n2, wu, wd, nf)


def kernel(x, norm1_w, w_in, ssm_a_re, ssm_a_im, ssm_log_dt, ssm_b_re, ssm_b_im, ssm_c_re, ssm_c_im, ssm_d,
           w_glu, b_glu, pe_ck, w_ck1, w_ck2, pe_cv, w_cv1, w_cv2, w_out, norm2_w, w_up, w_down, rel_table,
           norm_f_w):
    bsz, L, d = x.shape
    x2 = x.reshape(bsz * L, d)
    depth = w_in.shape[0]
    for layer in range(depth):
        w_in_p = jnp.pad(w_in[layer].astype(BF16), ((0, 0), (0, D_IN_PAD - D_IN)))
        proj = _norm_matmul(x2, norm1_w[layer].reshape(1, d), w_in_p)
        ops = _s5_operators(ssm_a_re[layer], ssm_a_im[layer], ssm_log_dt[layer], ssm_b_re[layer], ssm_b_im[layer],
                            ssm_c_re[layer], ssm_c_im[layer], ssm_d[layer])
        u = proj[:, :SSM_WIDTH].reshape(bsz, L, SSM_WIDTH)
        y_ssm = _s5_mixer(u, ops, w_glu[layer], b_glu[layer])
        y_nsa = _nsa_mixer(proj, bsz, L, pe_ck[layer], w_ck1[layer], w_ck2[layer], pe_cv[layer], w_cv1[layer],
                           w_cv2[layer], rel_table)
        x2 = _out_proj(x2, y_ssm, y_nsa, w_out[layer].astype(BF16))
        if layer < depth - 1:
            raise NotImplementedError("only the final layer fuses the closing norm")
        x2 = _mlp(x2, norm2_w[layer].reshape(1, d), w_up[layer].astype(BF16), w_down[layer].astype(BF16),
                  norm_f_w.reshape(1, d))
    return x2.reshape(bsz, L, d)
```

```python
import functools
import math

import numpy as np
import jax
import jax.numpy as jnp
from jax import lax
from jax.experimental import pallas as pl
from jax.experimental.pallas import tpu as pltpu

F32 = jnp.float32
BF16 = jnp.bfloat16

D_MODEL = 2048
SSM_WIDTH = 1024
SSM_GROUP = 16
SSM_GROUPS = 64
SSM_STATE = 64
NSA_WIDTH = 1024
HEAD_DIM = 64
NSA_HEADS = 16
KV_HEADS = 4
Q_PER_KV = 4
KV_WIDTH = 256
CMP_BLOCK = 32
CMP_STRIDE = 16
CMP_HIDDEN = 256
SLC_BLOCK = 64
SLC_TOPK = 16
WINDOW = 512
REL_BUCKETS = 32
REL_MAX_DIST = 128
D_FF = 8192
EPS = 1e-6
NEG = -1e30
FORCED_SCORE = 1e4
D_IN = SSM_WIDTH + NSA_WIDTH + 6 * KV_WIDTH + 3 * NSA_HEADS
D_IN_PAD = 3840
GATE_COL = SSM_WIDTH + NSA_WIDTH + 6 * KV_WIDTH

S5_CHUNK = 16
QT = 128
N_SLC_PAD = 64
UNSELECTED = -1e9
VMEM_LIMIT = 56 * 1024 * 1024
HI = lax.Precision.HIGHEST


def _cparams(*sem):
    return pltpu.CompilerParams(dimension_semantics=sem, vmem_limit_bytes=VMEM_LIMIT)


def _rms(x, w):
    ms = jnp.mean(x * x, axis=-1, keepdims=True)
    return x * lax.rsqrt(ms + EPS) * w


def _norm_matmul_kernel(x_ref, nw_ref, w_ref, o_ref, h_ref):
    @pl.when(pl.program_id(1) == 0)
    def _():
        h_ref[...] = _rms(x_ref[...], nw_ref[...]).astype(BF16)

    o_ref[...] = jnp.dot(h_ref[...], w_ref[...], preferred_element_type=F32)


def _norm_matmul(x2, nw, w, tm=512, tn=256):
    t, d = x2.shape
    n = w.shape[1]
    return pl.pallas_call(
        _norm_matmul_kernel,
        grid=(t // tm, n // tn),
        in_specs=[pl.BlockSpec((tm, d), lambda i, j: (i, 0)),
                  pl.BlockSpec((1, d), lambda i, j: (0, 0)),
                  pl.BlockSpec((d, tn), lambda i, j: (0, j))],
        out_specs=pl.BlockSpec((tm, tn), lambda i, j: (i, j)),
        out_shape=jax.ShapeDtypeStruct((t, n), F32),
        scratch_shapes=[pltpu.VMEM((tm, d), BF16)],
        compiler_params=_cparams("parallel", "arbitrary"),
        name="norm_in_proj",
    )(x2, nw, w)


def _s5_operators(a_re, a_im, log_dt, b_re, b_im, c_re, c_im, d):
    q = S5_CHUNK
    g, p = a_re.shape
    dt = jnp.exp(log_dt)[:, None]
    lam_re, lam_im = dt * a_re, dt * a_im
    mag1 = jnp.exp(lam_re)
    abar_re, abar_im = mag1 * jnp.cos(lam_im), mag1 * jnp.sin(lam_im)
    den = a_re * a_re + a_im * a_im
    zr, zi = abar_re - 1.0, abar_im
    coef_re = (zr * a_re + zi * a_im) / den
    coef_im = (zi * a_re - zr * a_im) / den
    bb_re = coef_re[..., None] * b_re - coef_im[..., None] * b_im
    bb_im = coef_re[..., None] * b_im + coef_im[..., None] * b_re
    k = jnp.arange(q + 1, dtype=F32)[:, None, None]
    mag = jnp.exp(k * lam_re)
    pw_re, pw_im = mag * jnp.cos(k * lam_im), mag * jnp.sin(k * lam_im)
    m_re = pw_re[..., None] * bb_re - pw_im[..., None] * bb_im
    m_im = pw_re[..., None] * bb_im + pw_im[..., None] * bb_re
    kern = (jnp.einsum('ghp,kgpi->gkhi', c_re, m_re[:q], precision=HI)
            - jnp.einsum('ghp,kgpi->gkhi', c_im, m_im[:q], precision=HI))
    kern = kern.at[:, 0].add(jax.vmap(jnp.diag)(d))
    s_ix = np.arange(q)[:, None]
    t_ix = np.arange(q)[None, :]
    lag = np.clip(t_ix - s_ix, 0, q - 1)
    t_intra = jnp.where((t_ix >= s_ix)[None, :, :, None, None], kern[:, lag], 0.0)
    t_intra = t_intra.transpose(0, 1, 4, 2, 3).reshape(g, q * SSM_GROUP, q * SSM_GROUP)
    rev = np.arange(q - 1, -1, -1)
    s_re = m_re[rev].transpose(1, 0, 3, 2).reshape(g, q * SSM_GROUP, p)
    s_im = m_im[rev].transpose(1, 0, 3, 2).reshape(g, q * SSM_GROUP, p)
    s_end = jnp.concatenate([s_re, s_im], axis=-1)
    pr, pi = pw_re[1:], pw_im[1:]
    o_re = c_re[None] * pr[:, :, None, :] - c_im[None] * pi[:, :, None, :]
    o_im = -(c_re[None] * pi[:, :, None, :] + c_im[None] * pr[:, :, None, :])
    o_re = o_re.transpose(1, 3, 0, 2).reshape(g, p, q * SSM_GROUP)
    o_im = o_im.transpose(1, 3, 0, 2).reshape(g, p, q * SSM_GROUP)
    o_carry = jnp.concatenate([o_re, o_im], axis=1)
    a1 = jnp.concatenate([pw_re[q], pw_re[q]], axis=-1)
    a2 = jnp.concatenate([-pw_im[q], pw_im[q]], axis=-1)
    return t_intra, s_end, o_carry, a1, a2


def _s5_state_kernel(u_ref, s_ref, o_ref):
    o_ref[0] = jnp.dot(u_ref[0], s_ref[0], precision=HI, preferred_element_type=F32)


def _s5_scan_kernel(h_ref, a1_ref, a2_ref, o_ref):
    a1 = a1_ref[...]
    a2 = a2_ref[...]

    def body(c, h):
        o_ref[c] = h
        return a1 * h + a2 * pltpu.roll(h, SSM_STATE, axis=1) + h_ref[c]

    lax.fori_loop(0, h_ref.shape[0], body, jnp.zeros(a1.shape, F32))


def _s5_out_kernel(u_ref, t_ref, h_ref, oc_ref, y_ref):
    y_ref[0] = (jnp.dot(u_ref[0], t_ref[0], precision=HI, preferred_element_type=F32)
                + jnp.dot(h_ref[0], oc_ref[0], precision=HI, preferred_element_type=F32))


def _glu_kernel(y_ref, w_ref, b_ref, o_ref):
    z = jax.nn.gelu(y_ref[...])
    gate = jnp.dot(z.astype(BF16), w_ref[...], preferred_element_type=F32) + b_ref[...]
    o_ref[...] = z * jax.nn.sigmoid(gate)


def _s5_mixer(u, ops, w_glu, b_glu):
    t_intra, s_end, o_carry, a1, a2 = ops
    bsz, L, _ = u.shape
    q = S5_CHUNK
    nc = L // q
    g = SSM_GROUPS
    lanes = q * SSM_GROUP
    rows = bsz * nc
    ur = u.reshape(bsz, nc, q, g, SSM_GROUP).transpose(3, 0, 1, 2, 4).reshape(g, rows, lanes)
    hend = pl.pallas_call(
        _s5_state_kernel,
        grid=(g,),
        in_specs=[pl.BlockSpec((1, rows, lanes), lambda i: (i, 0, 0)),
                  pl.BlockSpec((1, lanes, 2 * SSM_STATE), lambda i: (i, 0, 0))],
        out_specs=pl.BlockSpec((1, rows, 2 * SSM_STATE), lambda i: (i, 0, 0)),
        out_shape=jax.ShapeDtypeStruct((g, rows, 2 * SSM_STATE), F32),
        compiler_params=_cparams("parallel"),
        name="s5_chunk_state",
    )(ur, s_end)
    hend_t = hend.reshape(g, bsz, nc, 2 * SSM_STATE).transpose(2, 0, 1, 3).reshape(nc, g * bsz, 2 * SSM_STATE)
    a1r = jnp.repeat(a1, bsz, axis=0)
    a2r = jnp.repeat(a2, bsz, axis=0)
    rb = 32
    hprev_t = pl.pallas_call(
        _s5_scan_kernel,
        grid=(g * bsz // rb,),
        in_specs=[pl.BlockSpec((nc, rb, 2 * SSM_STATE), lambda i: (0, i, 0)),
                  pl.BlockSpec((rb, 2 * SSM_STATE), lambda i: (i, 0)),
                  pl.BlockSpec((rb, 2 * SSM_STATE), lambda i: (i, 0))],
        out_specs=pl.BlockSpec((nc, rb, 2 * SSM_STATE), lambda i: (0, i, 0)),
        out_shape=jax.ShapeDtypeStruct((nc, g * bsz, 2 * SSM_STATE), F32),
        compiler_params=_cparams("parallel"),
        name="s5_chunk_scan",
    )(hend_t, a1r, a2r)
    hprev = hprev_t.reshape(nc, g, bsz, 2 * SSM_STATE).transpose(1, 2, 0, 3).reshape(g, rows, 2 * SSM_STATE)
    yr = pl.pallas_call(
        _s5_out_kernel,
        grid=(g,),
        in_specs=[pl.BlockSpec((1, rows, lanes), lambda i: (i, 0, 0)),
                  pl.BlockSpec((1, lanes, lanes), lambda i: (i, 0, 0)),
                  pl.BlockSpec((1, rows, 2 * SSM_STATE), lambda i: (i, 0, 0)),
                  pl.BlockSpec((1, 2 * SSM_STATE, lanes), lambda i: (i, 0, 0))],
        out_specs=pl.BlockSpec((1, rows, lanes), lambda i: (i, 0, 0)),
        out_shape=jax.ShapeDtypeStruct((g, rows, lanes), F32),
        compiler_params=_cparams("parallel"),
        name="s5_chunk_out",
    )(ur, t_intra, hprev, o_carry)
    y = yr.reshape(g, bsz, nc, q, SSM_GROUP).transpose(1, 2, 3, 0, 4).reshape(bsz * L, SSM_WIDTH)
    tm = 512
    return pl.pallas_call(
        _glu_kernel,
        grid=(bsz * L // tm,),
        in_specs=[pl.BlockSpec((tm, SSM_WIDTH), lambda i: (i, 0)),
                  pl.BlockSpec((SSM_WIDTH, SSM_WIDTH), lambda i: (0, 0)),
                  pl.BlockSpec((1, SSM_WIDTH), lambda i: (0, 0))],
        out_specs=pl.BlockSpec((tm, SSM_WIDTH), lambda i: (i, 0)),
        out_shape=jax.ShapeDtypeStruct((bsz * L, SSM_WIDTH), F32),
        compiler_params=_cparams("parallel"),
        name="s5_gelu_glu",
    )(y, w_glu.astype(BF16), b_glu.reshape(1, SSM_WIDTH))


def _compress_kernel(cf_ref, w1_ref, pe_ref, w1full_ref, w2_ref, o_ref):
    prod = jnp.dot(cf_ref[0, 0], w1_ref[...], precision=HI, preferred_element_type=F32)
    nc = prod.shape[0]
    pe_h = jnp.dot(jnp.broadcast_to(pe_ref[...], (8, pe_ref.shape[1])), w1full_ref[...],
                   precision=HI, preferred_element_type=F32)[0:1]
    hid = prod[:, :CMP_HIDDEN] + pltpu.roll(prod[:, CMP_HIDDEN:], nc - 1, axis=0) + pe_h
    o_ref[0, 0] = jnp.dot(jax.nn.gelu(hid), w2_ref[...], precision=HI, preferred_element_type=F32)


def _compress(k, pe, w1, w2):
    bsz, L, g, dh = k.shape
    nc = L // CMP_STRIDE
    half = CMP_STRIDE * dh
    cf = k.reshape(bsz, nc, CMP_STRIDE, g, dh).transpose(0, 3, 1, 2, 4).reshape(bsz, g, nc, half)
    w1cat = jnp.concatenate([w1[:half], w1[half:]], axis=1)
    return pl.pallas_call(
        _compress_kernel,
        grid=(bsz, g),
        in_specs=[pl.BlockSpec((1, 1, nc, half), lambda b, h: (b, h, 0, 0)),
                  pl.BlockSpec((half, 2 * CMP_HIDDEN), lambda b, h: (0, 0)),
                  pl.BlockSpec((1, 2 * half), lambda b, h: (0, 0)),
                  pl.BlockSpec((2 * half, CMP_HIDDEN), lambda b, h: (0, 0)),
                  pl.BlockSpec((CMP_HIDDEN, dh), lambda b, h: (0, 0))],
        out_specs=pl.BlockSpec((1, 1, nc, dh), lambda b, h: (b, h, 0, 0)),
        out_shape=jax.ShapeDtypeStruct((bsz, g, nc, dh), F32),
        compiler_params=_cparams("parallel", "parallel"),
        name="nsa_compress",
    )(cf, w1cat, pe.reshape(1, 2 * half), w1, w2)


def _rel_bucket(dist):
    dist = jnp.maximum(dist, 0)
    max_exact = REL_BUCKETS // 2
    large = max_exact + (jnp.log(jnp.maximum(dist, 1).astype(F32) / max_exact)
                         / math.log(REL_MAX_DIST / max_exact) * (REL_BUCKETS - max_exact)).astype(jnp.int32)
    large = jnp.minimum(large, REL_BUCKETS - 1)
    return jnp.where(dist < max_exact, dist, large)


def _lane_cat(parts):
    return jnp.concatenate(parts, axis=1)


def _split3(x):
    hi = x.astype(BF16)
    r1 = x - hi.astype(F32)
    mid = r1.astype(BF16)
    lo = (r1 - mid.astype(F32)).astype(BF16)
    return hi, mid, lo


def _nsa_kernel(qT_ref, kc_ref, vcT_ref, bc_ref, ovT_ref, ks_ref, vsT_ref, kw_ref, vwT_ref, wb_ref, crow_ref,
                gl_ref, o_ref, *, n_slc):
    i = pl.program_id(2)
    R = Q_PER_KV
    qT = qT_ref[0, 0] * 0.125
    q_heads = [qT[r * HEAD_DIM:(r + 1) * HEAD_DIM] for r in range(R)]
    qcat = _lane_cat(q_heads)
    t_pos = lax.broadcasted_iota(jnp.int32, (1, QT), 1) + i * QT
    t_pos_r = (lax.broadcasted_iota(jnp.int32, (1, R * QT), 1) & (QT - 1)) + i * QT

    ncp = kc_ref.shape[2]
    sc = jnp.dot(kc_ref[0, 0], qcat, preferred_element_type=F32)
    strip_row = pl.multiple_of(ncp - i * (QT // CMP_STRIDE), QT // CMP_STRIDE)
    sc = sc + _lane_cat([bc_ref[0, r, pl.ds(strip_row, ncp), :] for r in range(R)])
    n_end = lax.broadcasted_iota(jnp.int32, (ncp, 1), 0) * CMP_STRIDE + (CMP_BLOCK - 1)
    mask_c = n_end <= t_pos_r
    sm = jnp.where(mask_c, sc, NEG)
    m_c = jnp.max(sm, axis=0, keepdims=True)
    e_c = jnp.where(mask_c, jnp.exp(sm - m_c), 0.0)
    l_c = jnp.sum(e_c, axis=0, keepdims=True)
    inv_c = jnp.where(l_c > 0.0, 1.0 / jnp.where(l_c > 0.0, l_c, 1.0), 0.0)
    o_cmp = jnp.dot(vcT_ref[0, 0], e_c.astype(BF16), preferred_element_type=F32) * inv_c
    p_c = e_c * inv_c
    p_sum = p_c[:, 0:QT]
    for r in range(1, R):
        p_sum = p_sum + p_c[:, r * QT:(r + 1) * QT]

    ov = ovT_ref[...]
    imp = None
    for part in _split3(p_sum):
        term = jnp.dot(ov, part, preferred_element_type=F32)
        imp = term if imp is None else imp + term
    s_ix = lax.broadcasted_iota(jnp.int32, (N_SLC_PAD, QT), 0)
    cur = t_pos >> int(math.log2(SLC_BLOCK))
    forced = (s_ix == 0) | (s_ix == cur) | (s_ix == cur - 1)
    score = jnp.where(forced, FORCED_SCORE, jnp.where(s_ix <= cur, imp, -1.0))
    score = jnp.where(s_ix < n_slc, score, -2.0)
    rank = jnp.zeros((N_SLC_PAD, QT), F32)
    for sp in range(n_slc):
        row = score[sp:sp + 1, :]
        beats = (row > score) | ((row == score) & (s_ix > sp))
        rank = rank + jnp.where(beats, 1.0, 0.0)
    sel_neg = jnp.where(rank < float(SLC_TOPK), 0.0, UNSELECTED).astype(BF16)

    slab = WINDOW + QT
    k_rel = lax.broadcasted_iota(jnp.int32, (slab, 1), 0) + (i - WINDOW // QT) * QT
    wb = _lane_cat([wb_ref[0, r] for r in range(R)])
    wb = jnp.where(k_rel >= 0, wb, NEG)

    qsel = _lane_cat([jnp.concatenate([q_heads[r], sel_neg], axis=0) for r in range(R)])
    c_row = crow_ref[0]
    pad_tiles = WINDOW // QT

    def far_tile(j, carry):
        m, l, acc = carry
        s = jnp.dot(ks_ref[0, 0, j + pad_tiles], qsel, preferred_element_type=F32) + c_row
        m_new = jnp.maximum(m, jnp.max(s, axis=0, keepdims=True))
        alpha = jnp.exp(m - m_new)
        p = jnp.exp(s - m_new)
        l = alpha * l + jnp.sum(p, axis=0, keepdims=True)
        acc = alpha * acc + jnp.dot(vsT_ref[0, 0, j + pad_tiles], p.astype(BF16), preferred_element_type=F32)
        return m_new, l, acc

    init = (jnp.full((1, R * QT), NEG, F32), jnp.zeros((1, R * QT), F32), jnp.zeros((HEAD_DIM, R * QT), F32))
    m, l, acc = lax.fori_loop(0, jnp.maximum(i - 1, 0), far_tile, init)
    k_near = jnp.concatenate([ks_ref[0, 0, i + pad_tiles - 1], ks_ref[0, 0, i + pad_tiles]], axis=0)
    s = jnp.dot(k_near, qsel, preferred_element_type=F32) + wb[slab - 2 * QT:]
    m_new = jnp.maximum(m, jnp.max(s, axis=0, keepdims=True))
    alpha = jnp.exp(m - m_new)
    p = jnp.exp(s - m_new)
    l = alpha * l + jnp.sum(p, axis=0, keepdims=True)
    p = p.astype(BF16)
    acc = (alpha * acc + jnp.dot(vsT_ref[0, 0, i + pad_tiles - 1], p[:QT], preferred_element_type=F32)
           + jnp.dot(vsT_ref[0, 0, i + pad_tiles], p[QT:], preferred_element_type=F32))
    o_slc = acc * (1.0 / l)

    kw = kw_ref[0, 0, pl.ds(pl.multiple_of(i * QT, QT), slab), :]
    sw = jnp.dot(kw, qcat, preferred_element_type=F32) + wb
    m_w = jnp.max(sw, axis=0, keepdims=True)
    e_w = jnp.exp(sw - m_w)
    l_w = jnp.sum(e_w, axis=0, keepdims=True)
    e_wb = e_w.astype(BF16)
    o_win = None
    for jj in range(slab // QT):
        term = jnp.dot(vwT_ref[0, 0, i + jj], e_wb[jj * QT:(jj + 1) * QT], preferred_element_type=F32)
        o_win = term if o_win is None else o_win + term
    o_win = o_win * (1.0 / l_w)

    gates = [jax.nn.sigmoid(_lane_cat([gl_ref[0, 0, br, r:r + 1, :] for r in range(R)])) for br in range(3)]
    total = gates[0] * o_cmp + gates[1] * o_slc + gates[2] * o_win
    for r in range(R):
        o_ref[0, 0, r * HEAD_DIM:(r + 1) * HEAD_DIM, :] = total[:, r * QT:(r + 1) * QT]


def _nsa_tables(rel_table, L):
    g, r = KV_HEADS, Q_PER_KV
    table_g = rel_table.reshape(REL_BUCKETS, g, r).transpose(1, 2, 0)
    ncp = L // CMP_STRIDE
    ni = L // QT
    n_end = ((jnp.arange(2 * ncp) - ncp) * CMP_STRIDE + CMP_BLOCK - 1)[:, None]
    bias_c = table_g[:, :, _rel_bucket(jnp.arange(QT)[None, :] - n_end)]
    slab = WINDOW + QT
    dist = (jnp.arange(QT)[None, :] + WINDOW) - jnp.arange(slab)[:, None]
    wb = jnp.where((dist >= 0) & (dist < WINDOW), table_g[:, :, _rel_bucket(dist)], NEG)
    c_row = jnp.repeat(table_g[:, :, REL_BUCKETS - 1], QT, axis=1).reshape(g, 1, r * QT)
    return bias_c, wb, c_row


def _overlap_t(L):
    n_cmp = (L - CMP_BLOCK) // CMP_STRIDE + 1
    n_slc = L // SLC_BLOCK
    cmp_idx = np.arange(n_cmp)[:, None] * CMP_STRIDE + np.arange(CMP_BLOCK)[None, :]
    overlap = ((cmp_idx[:, :, None] // SLC_BLOCK) == np.arange(n_slc)[None, None, :]).sum(1) / CMP_BLOCK
    out = np.zeros((N_SLC_PAD, L // CMP_STRIDE), np.float32)
    out[:n_slc, :n_cmp] = overlap.T
    return jnp.asarray(out, dtype=BF16)


def _key_tiles(k, L):
    bsz, _, g, c = k.shape
    kp = jnp.pad(k.astype(BF16), ((0, 0), (WINDOW, 0), (0, 0), (0, 0)))
    return kp.reshape(bsz, (L + WINDOW) // QT, QT, g, c).transpose(0, 3, 1, 2, 4)


def _nsa_mixer(proj, bsz, L, pe_ck, w_ck1, w_ck2, pe_cv, w_cv1, w_cv2, rel_table):
    g, r = KV_HEADS, Q_PER_KV
    ni = L // QT
    ncp = L // CMP_STRIDE
    n_slc = L // SLC_BLOCK
    p3 = proj.reshape(bsz, L, D_IN_PAD)
    off = SSM_WIDTH
    q = p3[..., off:off + NSA_WIDTH]
    kv = [p3[..., off + NSA_WIDTH + j * KV_WIDTH: off + NSA_WIDTH + (j + 1) * KV_WIDTH].reshape(bsz, L, g, HEAD_DIM)
          for j in range(6)]
    kc, vc, ks, vs, kw, vw = kv
    gl = p3[..., GATE_COL:GATE_COL + 3 * NSA_HEADS]

    k_cmp = _compress(kc, pe_ck, w_ck1, w_ck2).astype(BF16)
    v_cmp_t = _compress(vc, pe_cv, w_cv1, w_cv2).astype(BF16).transpose(0, 1, 3, 2)

    q_t = q.astype(BF16).reshape(bsz, L, g, r * HEAD_DIM).transpose(0, 2, 3, 1)
    onehot = (np.arange(L)[:, None] // SLC_BLOCK == np.arange(N_SLC_PAD)[None, :])
    onehot = jnp.broadcast_to(jnp.asarray(onehot, dtype=BF16)[None, :, None, :], (bsz, L, g, N_SLC_PAD))
    ks_t = _key_tiles(jnp.concatenate([ks.astype(BF16), onehot], axis=-1), L)
    vs_t = _key_tiles(vs, L).transpose(0, 1, 2, 4, 3)
    kw_p = jnp.pad(kw.astype(BF16), ((0, 0), (WINDOW, 0), (0, 0), (0, 0))).transpose(0, 2, 1, 3)
    vw_t = _key_tiles(vw, L).transpose(0, 1, 2, 4, 3)
    gl_t = gl.reshape(bsz, L, g, r, 3).transpose(0, 2, 4, 3, 1)
    bias_c, wb, c_row = _nsa_tables(rel_table, L)
    nt = (L + WINDOW) // QT
    slab = WINDOW + QT

    o_t = pl.pallas_call(
        functools.partial(_nsa_kernel, n_slc=n_slc),
        grid=(bsz, g, ni),
        in_specs=[
            pl.BlockSpec((1, 1, r * HEAD_DIM, QT), lambda b, h, i: (b, h, 0, i)),
            pl.BlockSpec((1, 1, ncp, HEAD_DIM), lambda b, h, i: (b, h, 0, 0)),
            pl.BlockSpec((1, 1, HEAD_DIM, ncp), lambda b, h, i: (b, h, 0, 0)),
            pl.BlockSpec((1, r, 2 * ncp, QT), lambda b, h, i: (h, 0, 0, 0)),
            pl.BlockSpec((N_SLC_PAD, ncp), lambda b, h, i: (0, 0)),
            pl.BlockSpec((1, 1, nt, QT, HEAD_DIM + N_SLC_PAD), lambda b, h, i: (b, h, 0, 0, 0)),
            pl.BlockSpec((1, 1, nt, HEAD_DIM, QT), lambda b, h, i: (b, h, 0, 0, 0)),
            pl.BlockSpec((1, 1, L + WINDOW, HEAD_DIM), lambda b, h, i: (b, h, 0, 0)),
            pl.BlockSpec((1, 1, nt, HEAD_DIM, QT), lambda b, h, i: (b, h, 0, 0, 0)),
            pl.BlockSpec((1, r, slab, QT), lambda b, h, i: (h, 0, 0, 0)),
            pl.BlockSpec((1, 1, r * QT), lambda b, h, i: (h, 0, 0)),
            pl.BlockSpec((1, 1, 3, r, QT), lambda b, h, i: (b, h, 0, 0, i)),
        ],
        out_specs=pl.BlockSpec((1, 1, r * HEAD_DIM, QT), lambda b, h, i: (b, h, 0, i)),
        out_shape=jax.ShapeDtypeStruct((bsz, g, r * HEAD_DIM, L), F32),
        compiler_params=_cparams("parallel", "parallel", "arbitrary"),
        name="nsa_attention",
    )(q_t, k_cmp, v_cmp_t, bias_c, _overlap_t(L), ks_t, vs_t, kw_p, vw_t, wb, c_row, gl_t)
    return o_t.transpose(0, 3, 1, 2).reshape(bsz * L, NSA_WIDTH)


def _out_proj_kernel(x_ref, ys_ref, yn_ref, w_ref, o_ref):
    half = ys_ref.shape[1]
    acc = jnp.dot(ys_ref[...].astype(BF16), w_ref[:half, :], preferred_element_type=F32)
    acc = acc + jnp.dot(yn_ref[...].astype(BF16), w_ref[half:, :], preferred_element_type=F32)
    o_ref[...] = x_ref[...] + acc


def _out_proj(x2, y_ssm, y_nsa, w, tm=512, tn=512):
    t, d = x2.shape
    return pl.pallas_call(
        _out_proj_kernel,
        grid=(t // tm, d // tn),
        in_specs=[pl.BlockSpec((tm, tn), lambda i, j: (i, j)),
                  pl.BlockSpec((tm, SSM_WIDTH), lambda i, j: (i, 0)),
                  pl.BlockSpec((tm, NSA_WIDTH), lambda i, j: (i, 0)),
                  pl.BlockSpec((SSM_WIDTH + NSA_WIDTH, tn), lambda i, j: (0, j))],
        out_specs=pl.BlockSpec((tm, tn), lambda i, j: (i, j)),
        out_shape=jax.ShapeDtypeStruct((t, d), F32),
        compiler_params=_cparams("parallel", "arbitrary"),
        name="out_proj",
    )(x2, y_ssm, y_nsa, w)


def _mlp_kernel(x_ref, n2_ref, wu_ref, wd_ref, nf_ref, o_ref, h_ref, acc_ref):
    f = pl.program_id(1)

    @pl.when(f == 0)
    def _():
        h_ref[...] = _rms(x_ref[...], n2_ref[...]).astype(BF16)
        acc_ref[...] = jnp.zeros_like(acc_ref)

    a = jnp.dot(h_ref[...], wu_ref[...], preferred_element_type=F32)
    a = jnp.square(jnp.maximum(a, 0.0))
    acc_ref[...] += jnp.dot(a.astype(BF16), wd_ref[...], preferred_element_type=F32)

    @pl.when(f == pl.num_programs(1) - 1)
    def _():
        o_ref[...] = _rms(x_ref[...] + acc_ref[...], nf_ref[...])


def _mlp(x2, n2, wu, wd, nf, tm=512, tf=512):
    t, d = x2.shape
    ff = wu.shape[1]
    return pl.pallas_call(
        _mlp_kernel,
        grid=(t // tm, ff // tf),
        in_specs=[pl.BlockSpec((tm, d), lambda i, f: (i, 0)),
                  pl.BlockSpec((1, d), lambda i, f: (0, 0)),
                  pl.BlockSpec((d, tf), lambda i, f: (0, f)),
                  pl.BlockSpec((tf, d), lambda i, f: (f, 0)),
                  pl.BlockSpec((1, d), lambda i, f: (0, 0))],
        out_specs=pl.BlockSpec((tm, d), lambda i, f: (i, 0)),
        out_shape=jax.ShapeDtypeStruct((t, d), F32),
        scratch_shapes=[pltpu.VMEM((tm, d), BF16), pltpu.VMEM((tm, d), F32)],
        compiler_params=_cparams("parallel", "arbitrary"),
        name="mlp_final_norm",
    )(x2, n2, wu, wd, nf)


def kernel(x, norm1_w, w_in, ssm_a_re, ssm_a_im, ssm_log_dt, ssm_b_re, ssm_b_im, ssm_c_re, ssm_c_im, ssm_d,
           w_glu, b_glu, pe_ck, w_ck1, w_ck2, pe_cv, w_cv1, w_cv2, w_out, norm2_w, w_up, w_down, rel_table,
           norm_f_w):
    bsz, L, d = x.shape
    x2 = x.reshape(bsz * L, d)
    depth = w_in.shape[0]
    for layer in range(depth):
        w_in_p = jnp.pad(w_in[layer].astype(BF16), ((0, 0), (0, D_IN_PAD - D_IN)))
        proj = _norm_matmul(x2, norm1_w[layer].reshape(1, d), w_in_p)
        ops = _s5_operators(ssm_a_re[layer], ssm_a_im[layer], ssm_log_dt[layer], ssm_b_re[layer], ssm_b_im[layer],
                            ssm_c_re[layer], ssm_c_im[layer], ssm_d[layer])
        u = proj[:, :SSM_WIDTH].reshape(bsz, L, SSM_WIDTH)
        y_ssm = _s5_mixer(u, ops, w_glu[layer], b_glu[layer])
        y_nsa = _nsa_mixer(proj, bsz, L, pe_ck[layer], w_ck1[layer], w_ck2[layer], pe_cv[layer], w_cv1[layer],
                           w_cv2[layer], rel_table)
        x2 = _out_proj(x2, y_ssm, y_nsa, w_out[layer].astype(BF16))
        if layer < depth - 1:
            raise NotImplementedError("only the final layer fuses the closing norm")
        x2 = _mlp(x2, norm2_w[layer].reshape(1, d), w_up[layer].astype(BF16), w_down[layer].astype(BF16),
                  norm_f_w.reshape(1, d))
    return x2.reshape(bsz, L, d)
```

```python
import functools
import math

import numpy as np
import jax
import jax.numpy as jnp
from jax import lax
from jax.experimental import pallas as pl
from jax.experimental.pallas import tpu as pltpu

F32 = jnp.float32
BF16 = jnp.bfloat16

D_MODEL = 2048
SSM_WIDTH = 1024
SSM_GROUP = 16
SSM_GROUPS = 64
SSM_STATE = 64
NSA_WIDTH = 1024
HEAD_DIM = 64
NSA_HEADS = 16
KV_HEADS = 4
Q_PER_KV = 4
KV_WIDTH = 256
CMP_BLOCK = 32
CMP_STRIDE = 16
CMP_HIDDEN = 256
SLC_BLOCK = 64
SLC_TOPK = 16
WINDOW = 512
REL_BUCKETS = 32
REL_MAX_DIST = 128
D_FF = 8192
EPS = 1e-6
NEG = -1e30
FORCED_SCORE = 1e4
D_IN = SSM_WIDTH + NSA_WIDTH + 6 * KV_WIDTH + 3 * NSA_HEADS
D_IN_PAD = 3840
COL_Q = SSM_WIDTH
COL_KC = COL_Q + NSA_WIDTH
COL_VC = COL_KC + KV_WIDTH
COL_KS = COL_VC + KV_WIDTH
COL_VS = COL_KS + KV_WIDTH
COL_KW = COL_VS + KV_WIDTH
COL_VW = COL_KW + KV_WIDTH
COL_GATE = COL_VW + KV_WIDTH

LANES = 128
S5_CHUNK = 16
SLAB_GROUPS = LANES // SSM_GROUP
N_SLABS = SSM_GROUPS // SLAB_GROUPS
QT = 128
N_SLC_PAD = 64
UNSELECTED = -1e9
TABLE_W = 1024
VMEM_LIMIT = 56 * 1024 * 1024
HI = lax.Precision.HIGHEST


def _cparams(*sem):
    return pltpu.CompilerParams(dimension_semantics=sem, vmem_limit_bytes=VMEM_LIMIT)


def _rms(x, w):
    ms = jnp.mean(x * x, axis=-1, keepdims=True)
    return x * lax.rsqrt(ms + EPS) * w


def _lane_cat(parts):
    return jnp.concatenate(parts, axis=1)


def _norm_matmul_kernel(x_ref, nw_ref, w_ref, o_ref, h_ref):
    @pl.when(pl.program_id(1) == 0)
    def _():
        h_ref[...] = _rms(x_ref[...], nw_ref[...]).astype(BF16)

    o_ref[...] = jnp.dot(h_ref[...], w_ref[...], preferred_element_type=F32)


def _norm_matmul(x2, nw, w, tm=512, tn=256):
    t, d = x2.shape
    n = w.shape[1]
    return pl.pallas_call(
        _norm_matmul_kernel,
        grid=(t // tm, n // tn),
        in_specs=[pl.BlockSpec((tm, d), lambda i, j: (i, 0)),
                  pl.BlockSpec((1, d), lambda i, j: (0, 0)),
                  pl.BlockSpec((d, tn), lambda i, j: (0, j))],
        out_specs=pl.BlockSpec((tm, tn), lambda i, j: (i, j)),
        out_shape=jax.ShapeDtypeStruct((t, n), F32),
        scratch_shapes=[pltpu.VMEM((tm, d), BF16)],
        compiler_params=_cparams("parallel", "arbitrary"),
        name="norm_in_proj",
    )(x2, nw, w)


def _s5_operators(a_re, a_im, log_dt, b_re, b_im, c_re, c_im, d):
    q = S5_CHUNK
    g, p = a_re.shape
    h = SSM_GROUP
    dt = jnp.exp(log_dt)[:, None]
    lam_re, lam_im = dt * a_re, dt * a_im
    mag1 = jnp.exp(lam_re)
    abar_re, abar_im = mag1 * jnp.cos(lam_im), mag1 * jnp.sin(lam_im)
    den = a_re * a_re + a_im * a_im
    zr, zi = abar_re - 1.0, abar_im
    coef_re = (zr * a_re + zi * a_im) / den
    coef_im = (zi * a_re - zr * a_im) / den
    bb_re = coef_re[..., None] * b_re - coef_im[..., None] * b_im
    bb_im = coef_re[..., None] * b_im + coef_im[..., None] * b_re
    k = jnp.arange(q + 1, dtype=F32)[:, None, None]
    mag = jnp.exp(k * lam_re)
    pw_re, pw_im = mag * jnp.cos(k * lam_im), mag * jnp.sin(k * lam_im)
    m_re = pw_re[..., None] * bb_re - pw_im[..., None] * bb_im
    m_im = pw_re[..., None] * bb_im + pw_im[..., None] * bb_re
    kern = (jnp.einsum('ghp,kgpi->gkhi', c_re, m_re[:q], precision=HI)
            - jnp.einsum('ghp,kgpi->gkhi', c_im, m_im[:q], precision=HI))
    kern = kern.at[:, 0].add(jax.vmap(jnp.diag)(d))
    lag = np.arange(q)[None, :] - np.arange(q)[:, None]
    lag_sel = (lag[:, :, None] == np.arange(q)[None, None, :]).astype(np.float32)
    t_intra = jnp.einsum('stk,gkoi->gsito', jnp.asarray(lag_sel), kern, precision=HI)
    rev = np.arange(q - 1, -1, -1)
    s_end = jnp.concatenate([m_re[rev].transpose(1, 0, 3, 2), m_im[rev].transpose(1, 0, 3, 2)], axis=-1)
    pr, pi = pw_re[1:], pw_im[1:]
    o_re = c_re[None] * pr[:, :, None, :] - c_im[None] * pi[:, :, None, :]
    o_im = -(c_re[None] * pi[:, :, None, :] + c_im[None] * pr[:, :, None, :])
    o_carry = jnp.concatenate([o_re.transpose(1, 3, 0, 2), o_im.transpose(1, 3, 0, 2)], axis=1)
    a1 = jnp.concatenate([pw_re[q], pw_re[q]], axis=-1)
    a2 = jnp.concatenate([-pw_im[q], pw_im[q]], axis=-1)
    return t_intra, s_end, o_carry, a1, a2


def _s5_slab_weights(t_intra, s_end, o_carry):
    q, h, sg, ns = S5_CHUNK, SSM_GROUP, SLAB_GROUPS, N_SLABS
    eye = jnp.eye(sg, dtype=F32)
    p2 = 2 * SSM_STATE
    w_state = jnp.einsum('jgshp,ag->jsahgp', s_end.reshape(ns, sg, q, h, p2), eye)
    w_state = w_state.reshape(ns, q * LANES, sg * p2).astype(BF16)
    w_intra = jnp.einsum('jgshto,ag->jsahtgo', t_intra.reshape(ns, sg, q, h, q, h), eye)
    w_intra = w_intra.reshape(ns, q * LANES, q * LANES).astype(BF16)
    w_carry = jnp.einsum('jgpto,ag->japtgo', o_carry.reshape(ns, sg, p2, q, h), eye)
    w_carry = w_carry.reshape(ns, sg * p2, q * LANES).astype(BF16)
    return w_state, w_intra, w_carry


def _chunk_rows(u_ref, nc):
    return _lane_cat([u_ref[pl.ds(s, nc, stride=S5_CHUNK), :] for s in range(S5_CHUNK)]).astype(BF16)


def _s5_state_kernel(u_ref, w_ref, o_ref):
    nc = u_ref.shape[0] // S5_CHUNK
    o_ref[0] = jnp.dot(_chunk_rows(u_ref, nc), w_ref[0], preferred_element_type=F32)


def _s5_scan_kernel(h_ref, a1_ref, a2_ref, o_ref):
    a1 = a1_ref[...]
    a2 = a2_ref[...]

    def body(c, h):
        o_ref[0, c] = h
        return a1 * h + a2 * pltpu.roll(h, SSM_STATE, axis=1) + h_ref[0, c]

    lax.fori_loop(0, h_ref.shape[1], body, jnp.zeros(a1.shape, F32))


def _s5_out_kernel(u_ref, wi_ref, h_ref, wc_ref, y_ref):
    nc = u_ref.shape[0] // S5_CHUNK
    y = (jnp.dot(_chunk_rows(u_ref, nc), wi_ref[0], preferred_element_type=F32)
         + jnp.dot(h_ref[0].astype(BF16), wc_ref[0], preferred_element_type=F32))
    for t in range(S5_CHUNK):
        y_ref[pl.ds(t, nc, stride=S5_CHUNK), :] = y[:, t * LANES:(t + 1) * LANES]


def _glu_kernel(y_ref, w_ref, b_ref, o_ref):
    z = jax.nn.gelu(y_ref[...])
    gate = jnp.dot(z.astype(BF16), w_ref[...], preferred_element_type=F32) + b_ref[...]
    o_ref[...] = z * jax.nn.sigmoid(gate)


def _s5_mixer(proj, bsz, L, ops, w_glu, b_glu):
    t_intra, s_end, o_carry, a1, a2 = ops
    w_state, w_intra, w_carry = _s5_slab_weights(t_intra, s_end, o_carry)
    q = S5_CHUNK
    nc = L // q
    ns, sg = N_SLABS, SLAB_GROUPS
    p2 = 2 * SSM_STATE
    kq = q * LANES
    hend = pl.pallas_call(
        _s5_state_kernel,
        grid=(ns, bsz),
        in_specs=[pl.BlockSpec((L, LANES), lambda j, b: (b, j)),
                  pl.BlockSpec((1, kq, sg * p2), lambda j, b: (j, 0, 0))],
        out_specs=pl.BlockSpec((1, nc, sg * p2), lambda j, b: (b, 0, j)),
        out_shape=jax.ShapeDtypeStruct((bsz, nc, SSM_GROUPS * p2), F32),
        compiler_params=_cparams("parallel", "arbitrary"),
        name="s5_chunk_state",
    )(proj, w_state)
    hend4 = hend.reshape(bsz, nc, SSM_GROUPS, p2)
    gb = 32
    hprev4 = pl.pallas_call(
        _s5_scan_kernel,
        grid=(bsz, SSM_GROUPS // gb),
        in_specs=[pl.BlockSpec((1, nc, gb, p2), lambda b, k: (b, 0, k, 0)),
                  pl.BlockSpec((gb, p2), lambda b, k: (k, 0)),
                  pl.BlockSpec((gb, p2), lambda b, k: (k, 0))],
        out_specs=pl.BlockSpec((1, nc, gb, p2), lambda b, k: (b, 0, k, 0)),
        out_shape=jax.ShapeDtypeStruct((bsz, nc, SSM_GROUPS, p2), F32),
        compiler_params=_cparams("parallel", "parallel"),
        name="s5_chunk_scan",
    )(hend4, a1, a2)
    hprev = hprev4.reshape(bsz, nc, SSM_GROUPS * p2)
    y = pl.pallas_call(
        _s5_out_kernel,
        grid=(ns, bsz),
        in_specs=[pl.BlockSpec((L, LANES), lambda j, b: (b, j)),
                  pl.BlockSpec((1, kq, kq), lambda j, b: (j, 0, 0)),
                  pl.BlockSpec((1, nc, sg * p2), lambda j, b: (b, 0, j)),
                  pl.BlockSpec((1, sg * p2, kq), lambda j, b: (j, 0, 0))],
        out_specs=pl.BlockSpec((L, LANES), lambda j, b: (b, j)),
        out_shape=jax.ShapeDtypeStruct((bsz * L, SSM_WIDTH), F32),
        compiler_params=_cparams("parallel", "arbitrary"),
        name="s5_chunk_out",
    )(proj, w_intra, hprev, w_carry)
    tm = 512
    return pl.pallas_call(
        _glu_kernel,
        grid=(bsz * L // tm,),
        in_specs=[pl.BlockSpec((tm, SSM_WIDTH), lambda i: (i, 0)),
                  pl.BlockSpec((SSM_WIDTH, SSM_WIDTH), lambda i: (0, 0)),
                  pl.BlockSpec((1, SSM_WIDTH), lambda i: (0, 0))],
        out_specs=pl.BlockSpec((tm, SSM_WIDTH), lambda i: (i, 0)),
        out_shape=jax.ShapeDtypeStruct((bsz * L, SSM_WIDTH), F32),
        compiler_params=_cparams("parallel"),
        name="s5_gelu_glu",
    )(y, w_glu.astype(BF16), b_glu.reshape(1, SSM_WIDTH))


def _compress_kernel(k_ref, w1_ref, pe_ref, w1full_ref, w2_ref, o_ref, *, transpose_out):
    nc = k_ref.shape[0] // CMP_STRIDE
    prod = jnp.dot(_chunk_rows(k_ref, nc), w1_ref[...], preferred_element_type=F32)
    pe_h = jnp.dot(jnp.broadcast_to(pe_ref[...], (8, pe_ref.shape[1])), w1full_ref[...],
                   precision=HI, preferred_element_type=F32)[0:1]
    two_h = 2 * CMP_HIDDEN
    for gl in range(LANES // HEAD_DIM):
        first = prod[:, gl * two_h: gl * two_h + CMP_HIDDEN]
        second = prod[:, gl * two_h + CMP_HIDDEN: (gl + 1) * two_h]
        hid = first + pltpu.roll(second, nc - 1, axis=0) + pe_h
        out = jnp.dot(jax.nn.gelu(hid).astype(BF16), w2_ref[...], preferred_element_type=F32)
        if transpose_out:
            o_ref[0, gl] = _lane_cat([out, jnp.zeros_like(out)]).T[:HEAD_DIM].astype(BF16)
        else:
            o_ref[0, gl] = out.astype(BF16)


def _compress(proj, col, bsz, L, pe, w1, w2, transpose_out):
    nc = L // CMP_STRIDE
    half = CMP_STRIDE * HEAD_DIM
    gpl = LANES // HEAD_DIM
    w1cat = jnp.concatenate([w1[:half], w1[half:]], axis=1)
    eye = jnp.eye(gpl, dtype=F32)
    w1slab = jnp.einsum('ldh,ag->ladgh', w1cat.reshape(CMP_STRIDE, HEAD_DIM, 2 * CMP_HIDDEN), eye)
    w1slab = w1slab.reshape(CMP_STRIDE * LANES, gpl * 2 * CMP_HIDDEN).astype(BF16)
    out_block = (1, gpl, HEAD_DIM, nc) if transpose_out else (1, gpl, nc, HEAD_DIM)
    out_full = (bsz, KV_HEADS) + out_block[2:]
    return pl.pallas_call(
        functools.partial(_compress_kernel, transpose_out=transpose_out),
        grid=(bsz, KV_HEADS // gpl),
        in_specs=[pl.BlockSpec((L, LANES), lambda b, j: (b, col // LANES + j)),
                  pl.BlockSpec(w1slab.shape, lambda b, j: (0, 0)),
                  pl.BlockSpec((1, 2 * half), lambda b, j: (0, 0)),
                  pl.BlockSpec((2 * half, CMP_HIDDEN), lambda b, j: (0, 0)),
                  pl.BlockSpec((CMP_HIDDEN, HEAD_DIM), lambda b, j: (0, 0))],
        out_specs=pl.BlockSpec(out_block, lambda b, j: (b, j, 0, 0)),
        out_shape=jax.ShapeDtypeStruct(out_full, BF16),
        compiler_params=_cparams("parallel", "parallel"),
        name="nsa_compress",
    )(proj, w1slab, pe.reshape(1, 2 * half), w1, w2.astype(BF16))


def _kv_tiles_kernel(ks_ref, vs_ref, kw_ref, vw_ref, kso_ref, vso_ref, kwo_ref, vwo_ref):
    tm = ks_ref.shape[0]
    row0 = pl.program_id(1) * tm
    lane_blk = lax.broadcasted_iota(jnp.int32, (QT, N_SLC_PAD), 1)
    for k in range(tm // QT):
        rows = slice(k * QT, (k + 1) * QT)
        tok = lax.broadcasted_iota(jnp.int32, (QT, N_SLC_PAD), 0) + (row0 + k * QT)
        onehot = jnp.where((tok >> int(math.log2(SLC_BLOCK))) == lane_blk, 1.0, 0.0).astype(BF16)
        vs_t = vs_ref[rows, :].T
        vw_t = vw_ref[rows, :].T
        for g in range(KV_HEADS):
            cols = slice(g * HEAD_DIM, (g + 1) * HEAD_DIM)
            kso_ref[0, g, k, :, :HEAD_DIM] = ks_ref[rows, cols].astype(BF16)
            kso_ref[0, g, k, :, HEAD_DIM:] = onehot
            kwo_ref[0, g, k] = kw_ref[rows, cols].astype(BF16)
            vso_ref[0, g, k] = vs_t[cols].astype(BF16)
            vwo_ref[0, g, k] = vw_t[cols].astype(BF16)


def _kv_tiles(proj, bsz, L, tm=512):
    nt = L // QT
    g = KV_HEADS
    kt = tm // QT
    in_spec = lambda col: pl.BlockSpec((tm, KV_WIDTH), lambda b, i: (b * (L // tm) + i, col // KV_WIDTH))
    out_spec = lambda a, c: pl.BlockSpec((1, g, kt, a, c), lambda b, i: (b, 0, i, 0, 0))
    shape = lambda a, c: jax.ShapeDtypeStruct((bsz, g, nt, a, c), BF16)
    return pl.pallas_call(
        _kv_tiles_kernel,
        grid=(bsz, L // tm),
        in_specs=[in_spec(COL_KS), in_spec(COL_VS), in_spec(COL_KW), in_spec(COL_VW)],
        out_specs=[out_spec(QT, HEAD_DIM + N_SLC_PAD), out_spec(HEAD_DIM, QT), out_spec(QT, HEAD_DIM),
                   out_spec(HEAD_DIM, QT)],
        out_shape=[shape(QT, HEAD_DIM + N_SLC_PAD), shape(HEAD_DIM, QT), shape(QT, HEAD_DIM), shape(HEAD_DIM, QT)],
        compiler_params=_cparams("parallel", "parallel"),
        name="nsa_kv_tiles",
    )(proj, proj, proj, proj)


def _rel_bucket(dist):
    dist = jnp.maximum(dist, 0)
    max_exact = REL_BUCKETS // 2
    large = max_exact + (jnp.log(jnp.maximum(dist, 1).astype(F32) / max_exact)
                         / math.log(REL_MAX_DIST / max_exact) * (REL_BUCKETS - max_exact)).astype(jnp.int32)
    large = jnp.minimum(large, REL_BUCKETS - 1)
    return jnp.where(dist < max_exact, dist, large)


def _toeplitz_rows(base, nrows, step):
    w = base.shape[1]
    x = jnp.broadcast_to(base, (nrows, w))
    k_ix = lax.broadcasted_iota(jnp.int32, (nrows, w), 0)
    bit = 1
    while bit < nrows:
        x = jnp.where((k_ix & bit) != 0, pltpu.roll(x, (bit * step) % w, axis=1), x)
        bit *= 2
    return x


def _bias_tables_kernel(bw_ref, bc_ref, far_ref, wb_ref, nb_ref, strip_ref, *, ncp):
    slab = WINDOW + QT
    y = _toeplitz_rows(bw_ref[0], QT, 1)
    for kh in range(slab // QT):
        tile = y[:, WINDOW - kh * QT: WINDOW - kh * QT + QT]
        wb_ref[0, kh * QT:(kh + 1) * QT, :] = tile
        if kh >= slab // QT - 2:
            row = kh - (slab // QT - 2)
            nb_ref[0, row * QT:(row + 1) * QT, :] = tile - far_ref[0]
    band = 2 * CMP_STRIDE
    z = _toeplitz_rows(bc_ref[0], band, CMP_STRIDE)[:, 2 * QT: 3 * QT]
    strip_ref[0, :ncp - CMP_STRIDE, :] = jnp.broadcast_to(far_ref[0], (ncp - CMP_STRIDE, QT))
    strip_ref[0, ncp - CMP_STRIDE: ncp + CMP_STRIDE, :] = z
    strip_ref[0, ncp + CMP_STRIDE:, :] = jnp.full((ncp - CMP_STRIDE, QT), NEG, F32)


def _nsa_tables(rel_table, L):
    ncp = L // CMP_STRIDE
    slab = WINDOW + QT
    dist = jnp.arange(TABLE_W)
    per_dist = rel_table[_rel_bucket(dist)].T
    base_w = jnp.where(dist < WINDOW, per_dist, NEG)
    shift = CMP_BLOCK - 1
    shifted = jnp.concatenate([jnp.full((NSA_HEADS, shift), NEG, F32), per_dist[:, :TABLE_W - shift]], axis=1)
    base_c = jnp.where(dist < TABLE_W // 2, shifted, NEG)
    far = jnp.broadcast_to(rel_table[REL_BUCKETS - 1][:, None, None], (NSA_HEADS, 1, QT))
    wb, nb, strip = pl.pallas_call(
        functools.partial(_bias_tables_kernel, ncp=ncp),
        grid=(NSA_HEADS,),
        in_specs=[pl.BlockSpec((1, 1, TABLE_W), lambda h: (h, 0, 0)),
                  pl.BlockSpec((1, 1, TABLE_W), lambda h: (h, 0, 0)),
                  pl.BlockSpec((1, 1, QT), lambda h: (h, 0, 0))],
        out_specs=[pl.BlockSpec((1, slab, QT), lambda h: (h, 0, 0)),
                   pl.BlockSpec((1, 2 * QT, QT), lambda h: (h, 0, 0)),
                   pl.BlockSpec((1, 2 * ncp, QT), lambda h: (h, 0, 0))],
        out_shape=[jax.ShapeDtypeStruct((NSA_HEADS, slab, QT), F32),
                   jax.ShapeDtypeStruct((NSA_HEADS, 2 * QT, QT), F32),
                   jax.ShapeDtypeStruct((NSA_HEADS, 2 * ncp, QT), F32)],
        compiler_params=_cparams("parallel"),
        name="nsa_bias_tables",
    )(base_w.reshape(NSA_HEADS, 1, TABLE_W), base_c.reshape(NSA_HEADS, 1, TABLE_W), far)
    g, r = KV_HEADS, Q_PER_KV
    return wb.reshape(g, r, slab, QT), nb.reshape(g, r, 2 * QT, QT), strip.reshape(g, r, 2 * ncp, QT)


def _overlap_t(L):
    n_cmp = (L - CMP_BLOCK) // CMP_STRIDE + 1
    n_slc = L // SLC_BLOCK
    cmp_idx = np.arange(n_cmp)[:, None] * CMP_STRIDE + np.arange(CMP_BLOCK)[None, :]
    overlap = ((cmp_idx[:, :, None] // SLC_BLOCK) == np.arange(n_slc)[None, None, :]).sum(1) / CMP_BLOCK
    out = np.zeros((N_SLC_PAD, L // CMP_STRIDE), np.float32)
    out[:n_slc, :n_cmp] = overlap.T
    return jnp.asarray(out, dtype=BF16)


def _split3(x):
    hi = x.astype(BF16)
    r1 = x - hi.astype(F32)
    mid = r1.astype(BF16)
    lo = (r1 - mid.astype(F32)).astype(BF16)
    return hi, mid, lo


def _nsa_kernel(q_ref, gl_ref, kc_ref, vcT_ref, strip_ref, ovT_ref, ks_ref, vsT_ref, kw_ref, vwT_ref, wb_ref,
                nb_ref, o_ref, gate_ref, *, n_slc):
    g = pl.program_id(1)
    i = pl.program_id(2)
    R = Q_PER_KV
    RQ = R * QT
    qT = (q_ref[...].T * 0.125).astype(BF16)
    q_heads = [qT[r * HEAD_DIM:(r + 1) * HEAD_DIM] for r in range(R)]
    qcat = _lane_cat(q_heads)
    t_pos = lax.broadcasted_iota(jnp.int32, (1, QT), 1) + i * QT
    t_pos_r = (lax.broadcasted_iota(jnp.int32, (1, RQ), 1) & (QT - 1)) + i * QT

    ncp = kc_ref.shape[2]
    sc = jnp.dot(kc_ref[0, 0], qcat, preferred_element_type=F32)
    strip_row = pl.multiple_of(ncp - i * (QT // CMP_STRIDE), QT // CMP_STRIDE)
    sc = sc + _lane_cat([strip_ref[0, r, pl.ds(strip_row, ncp), :] for r in range(R)])
    n_end = lax.broadcasted_iota(jnp.int32, (ncp, 1), 0) * CMP_STRIDE + (CMP_BLOCK - 1)
    mask_c = n_end <= t_pos_r
    sm = jnp.where(mask_c, sc, NEG)
    m_c = jnp.max(sm, axis=0, keepdims=True)
    e_c = jnp.where(mask_c, jnp.exp(sm - m_c), 0.0)
    l_c = jnp.sum(e_c, axis=0, keepdims=True)
    inv_c = jnp.where(l_c > 0.0, 1.0 / jnp.where(l_c > 0.0, l_c, 1.0), 0.0)
    o_cmp = jnp.dot(vcT_ref[0, 0], e_c.astype(BF16), preferred_element_type=F32) * inv_c
    p_c = e_c * inv_c
    p_sum = p_c[:, 0:QT]
    for r in range(1, R):
        p_sum = p_sum + p_c[:, r * QT:(r + 1) * QT]

    ov = ovT_ref[...]
    imp = None
    for part in _split3(p_sum):
        term = jnp.dot(ov, part, preferred_element_type=F32)
        imp = term if imp is None else imp + term
    s_ix = lax.broadcasted_iota(jnp.int32, (N_SLC_PAD, QT), 0)
    cur = t_pos >> int(math.log2(SLC_BLOCK))
    forced = (s_ix == 0) | (s_ix == cur) | (s_ix == cur - 1)
    score = jnp.where(forced, FORCED_SCORE, jnp.where(s_ix <= cur, imp, -1.0))
    score = jnp.where(s_ix < n_slc, score, -2.0)
    rank = jnp.zeros((N_SLC_PAD, QT), F32)
    for sp in range(n_slc):
        row = score[sp:sp + 1, :]
        beats = (row > score) | ((row == score) & (s_ix > sp))
        rank = rank + jnp.where(beats, 1.0, 0.0)
    sel_neg = jnp.where(rank < float(SLC_TOPK), 0.0, UNSELECTED).astype(BF16)

    slab = WINDOW + QT
    n_win = slab // QT
    k_rel = lax.broadcasted_iota(jnp.int32, (slab, 1), 0) + (i - WINDOW // QT) * QT

    qsel = _lane_cat([jnp.concatenate([q_heads[r], sel_neg], axis=0) for r in range(R)])

    def online_step(carry, s, v_tiles):
        m, l, acc = carry
        m_new = jnp.maximum(m, jnp.max(s, axis=0, keepdims=True))
        alpha = jnp.exp(m - m_new)
        p = jnp.exp(s - m_new)
        l = alpha * l + jnp.sum(p, axis=0, keepdims=True)
        p = p.astype(BF16)
        acc = alpha * acc
        for n, v in enumerate(v_tiles):
            acc = acc + jnp.dot(v, p[n * QT:(n + 1) * QT], preferred_element_type=F32)
        return m_new, l, acc

    def far_quad(t, carry):
        tiles = (4 * t, 4 * t + 2)
        scores = [jnp.dot(jnp.concatenate([ks_ref[0, 0, j], ks_ref[0, 0, j + 1]], axis=0), qsel,
                          preferred_element_type=F32) for j in tiles]
        return tuple(online_step(c, s, [vsT_ref[0, 0, j], vsT_ref[0, 0, j + 1]])
                     for c, s, j in zip(carry, scores, tiles))

    n_quads = jnp.maximum(i - 1, 0) // 4
    init = (jnp.full((1, RQ), NEG, F32), jnp.zeros((1, RQ), F32), jnp.zeros((HEAD_DIM, RQ), F32))
    (m_a, l_a, acc_a), (m_b, l_b, acc_b) = lax.fori_loop(0, n_quads, far_quad, (init, init))
    m_ab = jnp.maximum(m_a, m_b)
    w_a, w_b = jnp.exp(m_a - m_ab), jnp.exp(m_b - m_ab)
    carry = (m_ab, w_a * l_a + w_b * l_b, w_a * acc_a + w_b * acc_b)
    win_idx = [jnp.maximum(i + jj - (n_win - 1), 0) for jj in range(n_win)]
    near0 = slab - 2 * QT
    first_valid = jnp.where(lax.broadcasted_iota(jnp.int32, (slab, 1), 0) < near0, n_quads * (4 * QT), 0)
    nb = jnp.concatenate([jnp.zeros((near0, RQ), F32), _lane_cat([nb_ref[0, r] for r in range(R)])], axis=0)
    k5 = jnp.concatenate([ks_ref[0, 0, j] for j in win_idx], axis=0)
    s = jnp.dot(k5, qsel, preferred_element_type=F32) + jnp.where(k_rel >= first_valid, nb, NEG)
    _, l_s, acc_s = online_step(carry, s, [vsT_ref[0, 0, j] for j in win_idx])
    o_slc = acc_s * (1.0 / l_s)

    kw = jnp.concatenate([kw_ref[0, 0, j] for j in win_idx], axis=0)
    wb = _lane_cat([wb_ref[0, r] for r in range(R)])
    sw = jnp.dot(kw, qcat, preferred_element_type=F32) + jnp.where(k_rel >= 0, wb, NEG)
    m_w = jnp.max(sw, axis=0, keepdims=True)
    e_w = jnp.exp(sw - m_w)
    l_w = jnp.sum(e_w, axis=0, keepdims=True)
    e_wb = e_w.astype(BF16)
    o_win = None
    for jj, j in enumerate(win_idx):
        term = jnp.dot(vwT_ref[0, 0, j], e_wb[jj * QT:(jj + 1) * QT], preferred_element_type=F32)
        o_win = term if o_win is None else o_win + term
    o_win = o_win * (1.0 / l_w)

    gate_ref[...] = jax.nn.sigmoid(gl_ref[...].T)
    gates = [_lane_cat([gate_ref[pl.ds((g * R + r) * 3 + br, 1), :] for r in range(R)]) for br in range(3)]
    total = gates[0] * o_cmp + gates[1] * o_slc + gates[2] * o_win
    o_ref[...] = jnp.concatenate([total[:, r * QT:(r + 1) * QT] for r in range(R)], axis=0).T


def _nsa_mixer(proj, bsz, L, pe_ck, w_ck1, w_ck2, pe_cv, w_cv1, w_cv2, rel_table):
    g, r = KV_HEADS, Q_PER_KV
    ni = L // QT
    nt = L // QT
    ncp = L // CMP_STRIDE
    n_slc = L // SLC_BLOCK
    slab = WINDOW + QT
    k_cmp = _compress(proj, COL_KC, bsz, L, pe_ck, w_ck1, w_ck2, transpose_out=False)
    v_cmp_t = _compress(proj, COL_VC, bsz, L, pe_cv, w_cv1, w_cv2, transpose_out=True)
    ks_t, vs_t, kw_t, vw_t = _kv_tiles(proj, bsz, L)
    wb, nb, strip = _nsa_tables(rel_table, L)
    rw = r * HEAD_DIM
    whole = lambda *shape: pl.BlockSpec((1, 1) + shape, lambda b, h, i: (b, h) + (0,) * len(shape))
    per_group = lambda *shape: pl.BlockSpec((1,) + shape, lambda b, h, i: (h,) + (0,) * len(shape))
    return pl.pallas_call(
        functools.partial(_nsa_kernel, n_slc=n_slc),
        grid=(bsz, g, ni),
        in_specs=[
            pl.BlockSpec((QT, rw), lambda b, h, i: (b * ni + i, COL_Q // rw + h)),
            pl.BlockSpec((QT, LANES), lambda b, h, i: (b * ni + i, COL_GATE // LANES)),
            whole(ncp, HEAD_DIM),
            whole(HEAD_DIM, ncp),
            per_group(r, 2 * ncp, QT),
            pl.BlockSpec((N_SLC_PAD, ncp), lambda b, h, i: (0, 0)),
            whole(nt, QT, HEAD_DIM + N_SLC_PAD),
            whole(nt, HEAD_DIM, QT),
            whole(nt, QT, HEAD_DIM),
            whole(nt, HEAD_DIM, QT),
            per_group(r, slab, QT),
            per_group(r, 2 * QT, QT),
        ],
        out_specs=pl.BlockSpec((QT, rw), lambda b, h, i: (b * ni + i, h)),
        out_shape=jax.ShapeDtypeStruct((bsz * L, NSA_WIDTH), F32),
        scratch_shapes=[pltpu.VMEM((LANES, QT), F32)],
        compiler_params=_cparams("parallel", "parallel", "arbitrary"),
        name="nsa_attention",
    )(proj, proj, k_cmp, v_cmp_t, strip, _overlap_t(L), ks_t, vs_t, kw_t, vw_t, wb, nb)


def _out_proj_kernel(x_ref, ys_ref, yn_ref, w_ref, o_ref):
    half = ys_ref.shape[1]
    acc = jnp.dot(ys_ref[...].astype(BF16), w_ref[:half, :], preferred_element_type=F32)
    acc = acc + jnp.dot(yn_ref[...].astype(BF16), w_ref[half:, :], preferred_element_type=F32)
    o_ref[...] = x_ref[...] + acc


def _out_proj(x2, y_ssm, y_nsa, w, tm=512, tn=512):
    t, d = x2.shape
    return pl.pallas_call(
        _out_proj_kernel,
        grid=(t // tm, d // tn),
        in_specs=[pl.BlockSpec((tm, tn), lambda i, j: (i, j)),
                  pl.BlockSpec((tm, SSM_WIDTH), lambda i, j: (i, 0)),
                  pl.BlockSpec((tm, NSA_WIDTH), lambda i, j: (i, 0)),
                  pl.BlockSpec((SSM_WIDTH + NSA_WIDTH, tn), lambda i, j: (0, j))],
        out_specs=pl.BlockSpec((tm, tn), lambda i, j: (i, j)),
        out_shape=jax.ShapeDtypeStruct((t, d), F32),
        compiler_params=_cparams("parallel", "arbitrary"),
        name="out_proj",
    )(x2, y_ssm, y_nsa, w)


def _mlp_kernel(x_ref, n2_ref, wu_ref, wd_ref, nf_ref, o_ref, h_ref, acc_ref):
    f = pl.program_id(1)

    @pl.when(f == 0)
    def _():
        h_ref[...] = _rms(x_ref[...], n2_ref[...]).astype(BF16)
        acc_ref[...] = jnp.zeros_like(acc_ref)

    a = jnp.dot(h_ref[...], wu_ref[...], preferred_element_type=F32)
    a = jnp.square(jnp.maximum(a, 0.0))
    acc_ref[...] += jnp.dot(a.astype(BF16), wd_ref[...], preferred_element_type=F32)

    @pl.when(f == pl.num_programs(1) - 1)
    def _():
        o_ref[...] = _rms(x_ref[...] + acc_ref[...], nf_ref[...])


def _mlp(x2, n2, wu, wd, nf, tm=512, tf=512):
    t, d = x2.shape
    ff = wu.shape[1]
    return pl.pallas_call(
        _mlp_kernel,
        grid=(t // tm, ff // tf),
        in_specs=[pl.BlockSpec((tm, d), lambda i, f: (i, 0)),
                  pl.BlockSpec((1, d), lambda i, f: (0, 0)),
                  pl.BlockSpec((d, tf), lambda i, f: (0, f)),
                  pl.BlockSpec((tf, d), lambda i, f: (f, 0)),
                  pl.BlockSpec((1, d), lambda i, f: (0, 0))],
        out_specs=pl.BlockSpec((tm, d), lambda i, f: (i, 0)),
        out_shape=jax.ShapeDtypeStruct((t, d), F32),
        scratch_shapes=[pltpu.VMEM((tm, d), BF16), pltpu.VMEM((tm, d), F32)],
        compiler_params=_cparams("parallel", "arbitrary"),
        name="mlp_final_norm",
    )(x2, n2, wu, wd, nf)


def kernel(x, norm1_w, w_in, ssm_a_re, ssm_a_im, ssm_log_dt, ssm_b_re, ssm_b_im, ssm_c_re, ssm_c_im, ssm_d,
           w_glu, b_glu, pe_ck, w_ck1, w_ck2, pe_cv, w_cv1, w_cv2, w_out, norm2_w, w_up, w_down, rel_table,
           norm_f_w):
    bsz, L, d = x.shape
    assert w_in.shape[0] == 1, "the closing rmsnorm is fused into the (single) layer's MLP kernel"
    x2 = x.reshape(bsz * L, d)
    w_in_p = jnp.pad(w_in[0].astype(BF16), ((0, 0), (0, D_IN_PAD - D_IN)))
    proj = _norm_matmul(x2, norm1_w[0].reshape(1, d), w_in_p)
    ops = _s5_operators(ssm_a_re[0], ssm_a_im[0], ssm_log_dt[0], ssm_b_re[0], ssm_b_im[0], ssm_c_re[0], ssm_c_im[0],
                        ssm_d[0])
    y_ssm = _s5_mixer(proj, bsz, L, ops, w_glu[0], b_glu[0])
    y_nsa = _nsa_mixer(proj, bsz, L, pe_ck[0], w_ck1[0], w_ck2[0], pe_cv[0], w_cv1[0], w_cv2[0], rel_table)
    x2 = _out_proj(x2, y_ssm, y_nsa, w_out[0].astype(BF16))
    x2 = _mlp(x2, norm2_w[0].reshape(1, d), w_up[0].astype(BF16), w_down[0].astype(BF16), norm_f_w.reshape(1, d))
    return x2.reshape(bsz, L, d)
```

```python
import functools
import math

import numpy as np
import jax
import jax.numpy as jnp
from jax import lax
from jax.experimental import pallas as pl
from jax.experimental.pallas import tpu as pltpu

F32 = jnp.float32
BF16 = jnp.bfloat16

D_MODEL = 2048
SSM_WIDTH = 1024
SSM_GROUP = 16
SSM_GROUPS = 64
SSM_STATE = 64
NSA_WIDTH = 1024
HEAD_DIM = 64
NSA_HEADS = 16
KV_HEADS = 4
Q_PER_KV = 4
KV_WIDTH = 256
CMP_BLOCK = 32
CMP_STRIDE = 16
CMP_HIDDEN = 256
SLC_BLOCK = 64
SLC_TOPK = 16
WINDOW = 512
REL_BUCKETS = 32
REL_MAX_DIST = 128
D_FF = 8192
EPS = 1e-6
NEG = -1e30
FORCED_SCORE = 1e4
D_IN = SSM_WIDTH + NSA_WIDTH + 6 * KV_WIDTH + 3 * NSA_HEADS
D_IN_PAD = 3840
COL_Q = SSM_WIDTH
COL_KC = COL_Q + NSA_WIDTH
COL_VC = COL_KC + KV_WIDTH
COL_KS = COL_VC + KV_WIDTH
COL_VS = COL_KS + KV_WIDTH
COL_KW = COL_VS + KV_WIDTH
COL_VW = COL_KW + KV_WIDTH
COL_GATE = COL_VW + KV_WIDTH

LANES = 128
S5_CHUNK = 16
SLAB_GROUPS = LANES // SSM_GROUP
N_SLABS = SSM_GROUPS // SLAB_GROUPS
QT = 128
N_SLC_PAD = 64
RANK_SEG = 16
UNSELECTED = -1e9
TABLE_W = 1024
VMEM_LIMIT = 56 * 1024 * 1024
HI = lax.Precision.HIGHEST


def _cparams(*sem):
    return pltpu.CompilerParams(dimension_semantics=sem, vmem_limit_bytes=VMEM_LIMIT)


def _rms(x, w):
    ms = jnp.mean(x * x, axis=-1, keepdims=True)
    return x * lax.rsqrt(ms + EPS) * w


def _lane_cat(parts):
    return jnp.concatenate(parts, axis=1)


def _norm_matmul_kernel(x_ref, nw_ref, w_ref, o_ref, h_ref):
    @pl.when(pl.program_id(1) == 0)
    def _():
        h_ref[...] = _rms(x_ref[...], nw_ref[...]).astype(BF16)

    o_ref[...] = jnp.dot(h_ref[...], w_ref[...], preferred_element_type=F32)


def _norm_matmul(x2, nw, w, tm=1024, tn=768):
    t, d = x2.shape
    n = w.shape[1]
    return pl.pallas_call(
        _norm_matmul_kernel,
        grid=(t // tm, n // tn),
        in_specs=[pl.BlockSpec((tm, d), lambda i, j: (i, 0)),
                  pl.BlockSpec((1, d), lambda i, j: (0, 0)),
                  pl.BlockSpec((d, tn), lambda i, j: (0, j))],
        out_specs=pl.BlockSpec((tm, tn), lambda i, j: (i, j)),
        out_shape=jax.ShapeDtypeStruct((t, n), F32),
        scratch_shapes=[pltpu.VMEM((tm, d), BF16)],
        compiler_params=_cparams("parallel", "arbitrary"),
        name="norm_in_proj",
    )(x2, nw, w)


def _s5_operators(a_re, a_im, log_dt, b_re, b_im, c_re, c_im, d):
    q = S5_CHUNK
    g, p = a_re.shape
    h = SSM_GROUP
    dt = jnp.exp(log_dt)[:, None]
    lam_re, lam_im = dt * a_re, dt * a_im
    mag1 = jnp.exp(lam_re)
    abar_re, abar_im = mag1 * jnp.cos(lam_im), mag1 * jnp.sin(lam_im)
    den = a_re * a_re + a_im * a_im
    zr, zi = abar_re - 1.0, abar_im
    coef_re = (zr * a_re + zi * a_im) / den
    coef_im = (zi * a_re - zr * a_im) / den
    bb_re = coef_re[..., None] * b_re - coef_im[..., None] * b_im
    bb_im = coef_re[..., None] * b_im + coef_im[..., None] * b_re
    k = jnp.arange(q + 1, dtype=F32)[:, None, None]
    mag = jnp.exp(k * lam_re)
    pw_re, pw_im = mag * jnp.cos(k * lam_im), mag * jnp.sin(k * lam_im)
    m_re = pw_re[..., None] * bb_re - pw_im[..., None] * bb_im
    m_im = pw_re[..., None] * bb_im + pw_im[..., None] * bb_re
    kern = (jnp.einsum('ghp,kgpi->gkhi', c_re, m_re[:q], precision=HI)
            - jnp.einsum('ghp,kgpi->gkhi', c_im, m_im[:q], precision=HI))
    kern = kern.at[:, 0].add(jax.vmap(jnp.diag)(d))
    rev = np.arange(q - 1, -1, -1)
    s_end = jnp.stack([m_re[rev], m_im[rev]], axis=0).transpose(2, 1, 4, 0, 3)
    pr, pi = pw_re[1:], pw_im[1:]
    o_re = c_re[None] * pr[:, :, None, :] - c_im[None] * pi[:, :, None, :]
    o_im = -(c_re[None] * pi[:, :, None, :] + c_im[None] * pr[:, :, None, :])
    o_carry = jnp.stack([o_re, o_im], axis=0).transpose(2, 0, 4, 1, 3)
    a_q = jnp.stack([pw_re[q], pw_im[q]], axis=0)
    return kern, s_end, o_carry, a_q


def _s5_slab_weights(kern, s_end, o_carry, a_q):
    q, h, sg, ns = S5_CHUNK, SSM_GROUP, SLAB_GROUPS, N_SLABS
    p = SSM_STATE
    pairs = sg // 2
    eye = jnp.eye(sg, dtype=F32).reshape(sg, pairs, 2)
    w_state = jnp.einsum('jqgshcp,aqg->jsahqcgp', s_end.reshape(ns, pairs, 2, q, h, 2, p), eye)
    w_state = w_state.reshape(ns, q * LANES, sg * 2 * p).astype(BF16)
    k_lag = jnp.einsum('jqgkoi,aqg->jkaiqgo', kern.reshape(ns, pairs, 2, q, h, h), eye)
    k_lag = k_lag.reshape(ns, q, LANES, LANES).astype(BF16)
    w_carry = jnp.einsum('jqgcpto,aqg->jtqcgpao', o_carry.reshape(ns, pairs, 2, 2, p, q, h), eye)
    w_carry = w_carry.reshape(ns, q, sg * 2 * p, LANES).astype(BF16)
    a_re = a_q[0].reshape(SSM_GROUPS // 2, 2 * p)
    a_im = a_q[1].reshape(SSM_GROUPS // 2, 2 * p)
    return w_state, k_lag, w_carry, a_re, a_im


def _chunk_rows(u_ref, nc):
    return _lane_cat([u_ref[pl.ds(s, nc, stride=S5_CHUNK), :] for s in range(S5_CHUNK)]).astype(BF16)


def _s5_state_kernel(u_ref, w_ref, o_ref):
    nc = u_ref.shape[0] // S5_CHUNK
    o_ref[0] = jnp.dot(_chunk_rows(u_ref, nc), w_ref[0], preferred_element_type=F32)


def _s5_scan_kernel(h_ref, ar_ref, ai_ref, o_ref):
    ar = ar_ref[...]
    ai = ai_ref[...]

    def body(c, carry):
        hr, hi = carry
        o_ref[0, c, :, :LANES] = hr
        o_ref[0, c, :, LANES:] = hi
        xr = h_ref[0, c, :, :LANES]
        xi = h_ref[0, c, :, LANES:]
        return ar * hr - ai * hi + xr, ar * hi + ai * hr + xi

    zero = jnp.zeros(ar.shape, F32)
    lax.fori_loop(0, h_ref.shape[1], body, (zero, zero))


def _s5_out_kernel(u_ref, kl_ref, h_ref, wc_ref, y_ref, wi_scr, wc_scr):
    nc = u_ref.shape[0] // S5_CHUNK
    q = S5_CHUNK

    @pl.when(pl.program_id(1) == 0)
    def _():
        wi_scr[...] = jnp.zeros_like(wi_scr)
        for s in range(q):
            for t in range(s, q):
                wi_scr[s * LANES:(s + 1) * LANES, t * LANES:(t + 1) * LANES] = kl_ref[0, t - s]
        for t in range(q):
            wc_scr[:, t * LANES:(t + 1) * LANES] = wc_ref[0, t]

    y = (jnp.dot(_chunk_rows(u_ref, nc), wi_scr[...], preferred_element_type=F32)
         + jnp.dot(h_ref[0].astype(BF16), wc_scr[...], preferred_element_type=F32))
    for t in range(q):
        y_ref[pl.ds(t, nc, stride=q), :] = y[:, t * LANES:(t + 1) * LANES]


def _glu_kernel(y_ref, w_ref, b_ref, o_ref):
    z = jax.nn.gelu(y_ref[...])
    gate = jnp.dot(z.astype(BF16), w_ref[...], preferred_element_type=F32) + b_ref[...]
    o_ref[...] = z * jax.nn.sigmoid(gate)


def _s5_mixer(proj, bsz, L, ops, w_glu, b_glu):
    w_state, k_lag, w_carry, a_re, a_im = _s5_slab_weights(*ops)
    q = S5_CHUNK
    nc = L // q
    ns, sg = N_SLABS, SLAB_GROUPS
    p2 = 2 * SSM_STATE
    kq = q * LANES
    hend = pl.pallas_call(
        _s5_state_kernel,
        grid=(ns, bsz),
        in_specs=[pl.BlockSpec((L, LANES), lambda j, b: (b, j)),
                  pl.BlockSpec((1, kq, sg * p2), lambda j, b: (j, 0, 0))],
        out_specs=pl.BlockSpec((1, nc, sg * p2), lambda j, b: (b, 0, j)),
        out_shape=jax.ShapeDtypeStruct((bsz, nc, SSM_GROUPS * p2), F32),
        compiler_params=_cparams("parallel", "arbitrary"),
        name="s5_chunk_state",
    )(proj, w_state)
    n_pairs = SSM_GROUPS // 2
    hend4 = hend.reshape(bsz, nc, n_pairs, 2 * p2)
    pb = 16
    hprev4 = pl.pallas_call(
        _s5_scan_kernel,
        grid=(bsz, n_pairs // pb),
        in_specs=[pl.BlockSpec((1, nc, pb, 2 * p2), lambda b, k: (b, 0, k, 0)),
                  pl.BlockSpec((pb, p2), lambda b, k: (k, 0)),
                  pl.BlockSpec((pb, p2), lambda b, k: (k, 0))],
        out_specs=pl.BlockSpec((1, nc, pb, 2 * p2), lambda b, k: (b, 0, k, 0)),
        out_shape=jax.ShapeDtypeStruct((bsz, nc, n_pairs, 2 * p2), F32),
        compiler_params=_cparams("parallel", "parallel"),
        name="s5_chunk_scan",
    )(hend4, a_re, a_im)
    hprev = hprev4.reshape(bsz, nc, SSM_GROUPS * p2)
    y = pl.pallas_call(
        _s5_out_kernel,
        grid=(ns, bsz),
        in_specs=[pl.BlockSpec((L, LANES), lambda j, b: (b, j)),
                  pl.BlockSpec((1, q, LANES, LANES), lambda j, b: (j, 0, 0, 0)),
                  pl.BlockSpec((1, nc, sg * p2), lambda j, b: (b, 0, j)),
                  pl.BlockSpec((1, q, sg * p2, LANES), lambda j, b: (j, 0, 0, 0))],
        out_specs=pl.BlockSpec((L, LANES), lambda j, b: (b, j)),
        out_shape=jax.ShapeDtypeStruct((bsz * L, SSM_WIDTH), F32),
        scratch_shapes=[pltpu.VMEM((kq, kq), BF16), pltpu.VMEM((sg * p2, kq), BF16)],
        compiler_params=_cparams("parallel", "arbitrary"),
        name="s5_chunk_out",
    )(proj, k_lag, hprev, w_carry)
    tm = 512
    return pl.pallas_call(
        _glu_kernel,
        grid=(bsz * L // tm,),
        in_specs=[pl.BlockSpec((tm, SSM_WIDTH), lambda i: (i, 0)),
                  pl.BlockSpec((SSM_WIDTH, SSM_WIDTH), lambda i: (0, 0)),
                  pl.BlockSpec((1, SSM_WIDTH), lambda i: (0, 0))],
        out_specs=pl.BlockSpec((tm, SSM_WIDTH), lambda i: (i, 0)),
        out_shape=jax.ShapeDtypeStruct((bsz * L, SSM_WIDTH), F32),
        compiler_params=_cparams("parallel"),
        name="s5_gelu_glu",
    )(y, w_glu.astype(BF16), b_glu.reshape(1, SSM_WIDTH))


def _compress_kernel(k_ref, w1_ref, pe_ref, w1full_ref, w2_ref, o_ref, *, transpose_out):
    nc = k_ref.shape[0] // CMP_STRIDE
    prod = jnp.dot(_chunk_rows(k_ref, nc), w1_ref[...], preferred_element_type=F32)
    pe_h = jnp.dot(jnp.broadcast_to(pe_ref[...], (8, pe_ref.shape[1])), w1full_ref[...],
                   precision=HI, preferred_element_type=F32)[0:1]
    two_h = 2 * CMP_HIDDEN
    for gl in range(LANES // HEAD_DIM):
        first = prod[:, gl * two_h: gl * two_h + CMP_HIDDEN]
        second = prod[:, gl * two_h + CMP_HIDDEN: (gl + 1) * two_h]
        hid = first + pltpu.roll(second, nc - 1, axis=0) + pe_h
        out = jnp.dot(jax.nn.gelu(hid).astype(BF16), w2_ref[...], preferred_element_type=F32)
        if transpose_out:
            o_ref[0, gl] = _lane_cat([out, jnp.zeros_like(out)]).T[:HEAD_DIM].astype(BF16)
        else:
            o_ref[0, gl] = out.astype(BF16)


def _compress(proj, col, bsz, L, pe, w1, w2, transpose_out):
    nc = L // CMP_STRIDE
    half = CMP_STRIDE * HEAD_DIM
    gpl = LANES // HEAD_DIM
    w1cat = jnp.concatenate([w1[:half], w1[half:]], axis=1)
    eye = jnp.eye(gpl, dtype=F32)
    w1slab = jnp.einsum('ldh,ag->ladgh', w1cat.reshape(CMP_STRIDE, HEAD_DIM, 2 * CMP_HIDDEN), eye)
    w1slab = w1slab.reshape(CMP_STRIDE * LANES, gpl * 2 * CMP_HIDDEN).astype(BF16)
    out_block = (1, gpl, HEAD_DIM, nc) if transpose_out else (1, gpl, nc, HEAD_DIM)
    out_full = (bsz, KV_HEADS) + out_block[2:]
    return pl.pallas_call(
        functools.partial(_compress_kernel, transpose_out=transpose_out),
        grid=(bsz, KV_HEADS // gpl),
        in_specs=[pl.BlockSpec((L, LANES), lambda b, j: (b, col // LANES + j)),
                  pl.BlockSpec(w1slab.shape, lambda b, j: (0, 0)),
                  pl.BlockSpec((1, 2 * half), lambda b, j: (0, 0)),
                  pl.BlockSpec((2 * half, CMP_HIDDEN), lambda b, j: (0, 0)),
                  pl.BlockSpec((CMP_HIDDEN, HEAD_DIM), lambda b, j: (0, 0))],
        out_specs=pl.BlockSpec(out_block, lambda b, j: (b, j, 0, 0)),
        out_shape=jax.ShapeDtypeStruct(out_full, BF16),
        compiler_params=_cparams("parallel", "parallel"),
        name="nsa_compress",
    )(proj, w1slab, pe.reshape(1, 2 * half), w1, w2.astype(BF16))


def _kv_tiles_kernel(ks_ref, vs_ref, kw_ref, vw_ref, kso_ref, vso_ref, kwo_ref, vwo_ref):
    tm = ks_ref.shape[0]
    row0 = pl.program_id(1) * tm
    lane_blk = lax.broadcasted_iota(jnp.int32, (QT, N_SLC_PAD), 1)
    for k in range(tm // QT):
        rows = slice(k * QT, (k + 1) * QT)
        tok = lax.broadcasted_iota(jnp.int32, (QT, N_SLC_PAD), 0) + (row0 + k * QT)
        onehot = jnp.where((tok >> int(math.log2(SLC_BLOCK))) == lane_blk, 1.0, 0.0).astype(BF16)
        vs_t = vs_ref[rows, :].T
        vw_t = vw_ref[rows, :].T
        for g in range(KV_HEADS):
            cols = slice(g * HEAD_DIM, (g + 1) * HEAD_DIM)
            kso_ref[0, g, k, :, :HEAD_DIM] = ks_ref[rows, cols].astype(BF16)
            kso_ref[0, g, k, :, HEAD_DIM:] = onehot
            kwo_ref[0, g, k] = kw_ref[rows, cols].astype(BF16)
            vso_ref[0, g, k] = vs_t[cols].astype(BF16)
            vwo_ref[0, g, k] = vw_t[cols].astype(BF16)


def _kv_tiles(proj, bsz, L, tm=512):
    nt = L // QT
    g = KV_HEADS
    kt = tm // QT
    in_spec = lambda col: pl.BlockSpec((tm, KV_WIDTH), lambda b, i: (b * (L // tm) + i, col // KV_WIDTH))
    out_spec = lambda a, c: pl.BlockSpec((1, g, kt, a, c), lambda b, i: (b, 0, i, 0, 0))
    shape = lambda a, c: jax.ShapeDtypeStruct((bsz, g, nt, a, c), BF16)
    return pl.pallas_call(
        _kv_tiles_kernel,
        grid=(bsz, L // tm),
        in_specs=[in_spec(COL_KS), in_spec(COL_VS), in_spec(COL_KW), in_spec(COL_VW)],
        out_specs=[out_spec(QT, HEAD_DIM + N_SLC_PAD), out_spec(HEAD_DIM, QT), out_spec(QT, HEAD_DIM),
                   out_spec(HEAD_DIM, QT)],
        out_shape=[shape(QT, HEAD_DIM + N_SLC_PAD), shape(HEAD_DIM, QT), shape(QT, HEAD_DIM), shape(HEAD_DIM, QT)],
        compiler_params=_cparams("parallel", "parallel"),
        name="nsa_kv_tiles",
    )(proj, proj, proj, proj)


def _rel_bucket(dist):
    dist = jnp.maximum(dist, 0)
    max_exact = REL_BUCKETS // 2
    large = max_exact + (jnp.log(jnp.maximum(dist, 1).astype(F32) / max_exact)
                         / math.log(REL_MAX_DIST / max_exact) * (REL_BUCKETS - max_exact)).astype(jnp.int32)
    large = jnp.minimum(large, REL_BUCKETS - 1)
    return jnp.where(dist < max_exact, dist, large)


def _toeplitz_rows(base, nrows, step):
    w = base.shape[1]
    x = jnp.broadcast_to(base, (nrows, w))
    k_ix = lax.broadcasted_iota(jnp.int32, (nrows, w), 0)
    bit = 1
    while bit < nrows:
        x = jnp.where((k_ix & bit) != 0, pltpu.roll(x, (bit * step) % w, axis=1), x)
        bit *= 2
    return x


def _bias_tables_kernel(bw_ref, bc_ref, far_ref, wb_ref, nb_ref, strip_ref, *, ncp):
    slab = WINDOW + QT
    y = _toeplitz_rows(bw_ref[0], QT, 1)
    for kh in range(slab // QT):
        tile = y[:, WINDOW - kh * QT: WINDOW - kh * QT + QT]
        wb_ref[0, kh * QT:(kh + 1) * QT, :] = tile
        if kh >= slab // QT - 2:
            row = kh - (slab // QT - 2)
            nb_ref[0, row * QT:(row + 1) * QT, :] = tile - far_ref[0]
    band = 2 * CMP_STRIDE
    z = _toeplitz_rows(bc_ref[0], band, CMP_STRIDE)[:, 2 * QT: 3 * QT]
    strip_ref[0, :ncp - CMP_STRIDE, :] = jnp.broadcast_to(far_ref[0], (ncp - CMP_STRIDE, QT))
    strip_ref[0, ncp - CMP_STRIDE: ncp + CMP_STRIDE, :] = z
    strip_ref[0, ncp + CMP_STRIDE:, :] = jnp.full((ncp - CMP_STRIDE, QT), NEG, F32)


def _nsa_tables(rel_table, L):
    ncp = L // CMP_STRIDE
    slab = WINDOW + QT
    dist = jnp.arange(TABLE_W)
    per_dist = rel_table[_rel_bucket(dist)].T
    base_w = jnp.where(dist < WINDOW, per_dist, NEG)
    shift = CMP_BLOCK - 1
    shifted = jnp.concatenate([jnp.full((NSA_HEADS, shift), NEG, F32), per_dist[:, :TABLE_W - shift]], axis=1)
    base_c = jnp.where(dist < TABLE_W // 2, shifted, NEG)
    far = jnp.broadcast_to(rel_table[REL_BUCKETS - 1][:, None, None], (NSA_HEADS, 1, QT))
    wb, nb, strip = pl.pallas_call(
        functools.partial(_bias_tables_kernel, ncp=ncp),
        grid=(NSA_HEADS,),
        in_specs=[pl.BlockSpec((1, 1, TABLE_W), lambda h: (h, 0, 0)),
                  pl.BlockSpec((1, 1, TABLE_W), lambda h: (h, 0, 0)),
                  pl.BlockSpec((1, 1, QT), lambda h: (h, 0, 0))],
        out_specs=[pl.BlockSpec((1, slab, QT), lambda h: (h, 0, 0)),
                   pl.BlockSpec((1, 2 * QT, QT), lambda h: (h, 0, 0)),
                   pl.BlockSpec((1, 2 * ncp, QT), lambda h: (h, 0, 0))],
        out_shape=[jax.ShapeDtypeStruct((NSA_HEADS, slab, QT), F32),
                   jax.ShapeDtypeStruct((NSA_HEADS, 2 * QT, QT), F32),
                   jax.ShapeDtypeStruct((NSA_HEADS, 2 * ncp, QT), F32)],
        compiler_params=_cparams("parallel"),
        name="nsa_bias_tables",
    )(base_w.reshape(NSA_HEADS, 1, TABLE_W), base_c.reshape(NSA_HEADS, 1, TABLE_W), far)
    g, r = KV_HEADS, Q_PER_KV
    return wb.reshape(g, r, slab, QT), nb.reshape(g, r, 2 * QT, QT), strip.reshape(g, r, 2 * ncp, QT)


def _overlap_t(L):
    n_cmp = (L - CMP_BLOCK) // CMP_STRIDE + 1
    n_slc = L // SLC_BLOCK
    cmp_idx = np.arange(n_cmp)[:, None] * CMP_STRIDE + np.arange(CMP_BLOCK)[None, :]
    overlap = ((cmp_idx[:, :, None] // SLC_BLOCK) == np.arange(n_slc)[None, None, :]).sum(1) / CMP_BLOCK
    out = np.zeros((N_SLC_PAD, L // CMP_STRIDE), np.float32)
    out[:n_slc, :n_cmp] = overlap.T
    return jnp.asarray(out, dtype=BF16)


def _split3(x):
    hi = x.astype(BF16)
    r1 = x - hi.astype(F32)
    mid = r1.astype(BF16)
    lo = (r1 - mid.astype(F32)).astype(BF16)
    return hi, mid, lo


def _nsa_kernel(q_ref, gl_ref, kc_ref, vcT_ref, strip_ref, ovT_ref, ks_ref, vsT_ref, kw_ref, vwT_ref, wb_ref,
                nb_ref, o_ref, gate_ref, s_scr, *, n_slc):
    g = pl.program_id(1)
    i = pl.program_id(2)
    R = Q_PER_KV
    RQ = R * QT
    qT = (q_ref[...].T * 0.125).astype(BF16)
    q_heads = [qT[r * HEAD_DIM:(r + 1) * HEAD_DIM] for r in range(R)]
    qcat = _lane_cat(q_heads)
    t_pos = lax.broadcasted_iota(jnp.int32, (1, QT), 1) + i * QT

    ncp = kc_ref.shape[2]
    sc = jnp.dot(kc_ref[0, 0], qcat, preferred_element_type=F32)
    strip_row = pl.multiple_of(ncp - i * (QT // CMP_STRIDE), QT // CMP_STRIDE)
    sm = sc + _lane_cat([strip_ref[0, r, pl.ds(strip_row, ncp), :] for r in range(R)])
    m_c = jnp.max(sm, axis=0, keepdims=True)
    e_c = jnp.exp(sm - m_c)
    l_c = jnp.sum(e_c, axis=0, keepdims=True)
    inv_c = jnp.where(m_c > 0.5 * NEG, 1.0 / l_c, 0.0)
    o_cmp = jnp.dot(vcT_ref[0, 0], e_c.astype(BF16), preferred_element_type=F32) * inv_c
    p_c = e_c * inv_c
    p_sum = p_c[:, 0:QT]
    for r in range(1, R):
        p_sum = p_sum + p_c[:, r * QT:(r + 1) * QT]

    ov = ovT_ref[...]
    imp = None
    for part in _split3(p_sum):
        term = jnp.dot(ov, part, preferred_element_type=F32)
        imp = term if imp is None else imp + term
    s_ix = lax.broadcasted_iota(jnp.int32, (N_SLC_PAD, QT), 0)
    cur = t_pos >> int(math.log2(SLC_BLOCK))
    forced = (s_ix == 0) | (s_ix == cur) | (s_ix == cur - 1)
    score = jnp.where(forced, FORCED_SCORE, jnp.where(s_ix <= cur, imp, -1.0))
    score = jnp.where(s_ix < n_slc, score, -2.0)
    SUB = 8
    score_rows = [score[v * SUB:(v + 1) * SUB] for v in range(N_SLC_PAD // SUB)]
    sub_ix = lax.broadcasted_iota(jnp.int32, (SUB, QT), 0)

    def rank_segment(seg, ranks):
        ranks = list(ranks)
        for sp in range(seg * RANK_SEG, min((seg + 1) * RANK_SEG, n_slc)):
            row = score[sp:sp + 1, :]
            for v, blk in enumerate(score_rows):
                if v * SUB > sp:
                    beats = row >= blk
                elif v * SUB + SUB - 1 <= sp:
                    beats = row > blk
                else:
                    beats = (row > blk) | ((row == blk) & (sub_ix > sp - v * SUB))
                ranks[v] = ranks[v] + jnp.where(beats, 1.0, 0.0)
        return tuple(ranks)

    ranks = rank_segment(0, tuple(jnp.zeros((SUB, QT), F32) for _ in score_rows))
    last_block = (i + 1) * (QT // SLC_BLOCK) - 1
    for seg in range(1, -(-n_slc // RANK_SEG)):
        ranks = lax.cond(last_block >= seg * RANK_SEG, functools.partial(rank_segment, seg), lambda rk: rk, ranks)
    rank = jnp.concatenate(ranks, axis=0)
    sel_neg = jnp.where(rank < float(SLC_TOPK), 0.0, UNSELECTED).astype(BF16)

    slab = WINDOW + QT
    n_win = slab // QT
    k_rel = lax.broadcasted_iota(jnp.int32, (slab, 1), 0) + (i - WINDOW // QT) * QT

    qsel = _lane_cat([jnp.concatenate([q_heads[r], sel_neg], axis=0) for r in range(R)])

    FT = 4 * QT
    n_trips = jnp.maximum(i - 1, 0) // 4

    def far_rows(t):
        return pl.ds(pl.multiple_of(t * FT, FT), FT)

    def score_trip(t, m):
        k4 = jnp.concatenate([ks_ref[0, 0, 4 * t + n] for n in range(4)], axis=0)
        s = jnp.dot(k4, qsel, preferred_element_type=F32)
        s_scr[far_rows(t), :] = s
        return jnp.maximum(m, jnp.max(s, axis=0, keepdims=True))

    m_far = lax.fori_loop(0, n_trips, score_trip, jnp.full((1, RQ), NEG, F32))
    win_idx = [jnp.maximum(i + jj - (n_win - 1), 0) for jj in range(n_win)]
    near0 = slab - 2 * QT
    first_valid = jnp.where(lax.broadcasted_iota(jnp.int32, (slab, 1), 0) < near0, n_trips * FT, 0)
    nb = jnp.concatenate([jnp.zeros((near0, RQ), F32), _lane_cat([nb_ref[0, r] for r in range(R)])], axis=0)
    k5 = jnp.concatenate([ks_ref[0, 0, j] for j in win_idx], axis=0)
    s_tail = jnp.dot(k5, qsel, preferred_element_type=F32) + jnp.where(k_rel >= first_valid, nb, NEG)
    m_s = jnp.maximum(m_far, jnp.max(s_tail, axis=0, keepdims=True))

    def value_trip(t, carry):
        l, acc = carry
        p = jnp.exp(s_scr[far_rows(t), :] - m_s)
        l = l + jnp.sum(p, axis=0, keepdims=True)
        p = p.astype(BF16)
        for n in range(4):
            acc = acc + jnp.dot(vsT_ref[0, 0, 4 * t + n], p[n * QT:(n + 1) * QT], preferred_element_type=F32)
        return l, acc

    l_s, acc_s = lax.fori_loop(0, n_trips, value_trip,
                               (jnp.zeros((1, RQ), F32), jnp.zeros((HEAD_DIM, RQ), F32)))
    p_tail = jnp.exp(s_tail - m_s)
    l_s = l_s + jnp.sum(p_tail, axis=0, keepdims=True)
    p_tail = p_tail.astype(BF16)
    for jj, j in enumerate(win_idx):
        acc_s = acc_s + jnp.dot(vsT_ref[0, 0, j], p_tail[jj * QT:(jj + 1) * QT], preferred_element_type=F32)
    o_slc = acc_s * (1.0 / l_s)

    kw = jnp.concatenate([kw_ref[0, 0, j] for j in win_idx], axis=0)
    wb = _lane_cat([wb_ref[0, r] for r in range(R)])
    sw = jnp.dot(kw, qcat, preferred_element_type=F32) + jnp.where(k_rel >= 0, wb, NEG)
    m_w = jnp.max(sw, axis=0, keepdims=True)
    e_w = jnp.exp(sw - m_w)
    l_w = jnp.sum(e_w, axis=0, keepdims=True)
    e_wb = e_w.astype(BF16)
    o_win = None
    for jj, j in enumerate(win_idx):
        term = jnp.dot(vwT_ref[0, 0, j], e_wb[jj * QT:(jj + 1) * QT], preferred_element_type=F32)
        o_win = term if o_win is None else o_win + term
    o_win = o_win * (1.0 / l_w)

    gate_ref[...] = jax.nn.sigmoid(gl_ref[...].T)
    gates = [_lane_cat([gate_ref[pl.ds((g * R + r) * 3 + br, 1), :] for r in range(R)]) for br in range(3)]
    total = gates[0] * o_cmp + gates[1] * o_slc + gates[2] * o_win
    o_ref[...] = jnp.concatenate([total[:, r * QT:(r + 1) * QT] for r in range(R)], axis=0).T


def _nsa_mixer(proj, bsz, L, pe_ck, w_ck1, w_ck2, pe_cv, w_cv1, w_cv2, rel_table):
    g, r = KV_HEADS, Q_PER_KV
    ni = L // QT
    nt = L // QT
    ncp = L // CMP_STRIDE
    n_slc = L // SLC_BLOCK
    slab = WINDOW + QT
    k_cmp = _compress(proj, COL_KC, bsz, L, pe_ck, w_ck1, w_ck2, transpose_out=False)
    v_cmp_t = _compress(proj, COL_VC, bsz, L, pe_cv, w_cv1, w_cv2, transpose_out=True)
    ks_t, vs_t, kw_t, vw_t = _kv_tiles(proj, bsz, L)
    wb, nb, strip = _nsa_tables(rel_table, L)
    rw = r * HEAD_DIM
    whole = lambda *shape: pl.BlockSpec((1, 1) + shape, lambda b, h, i: (b, h) + (0,) * len(shape))
    per_group = lambda *shape: pl.BlockSpec((1,) + shape, lambda b, h, i: (h,) + (0,) * len(shape))
    return pl.pallas_call(
        functools.partial(_nsa_kernel, n_slc=n_slc),
        grid=(bsz, g, ni),
        in_specs=[
            pl.BlockSpec((QT, rw), lambda b, h, i: (b * ni + i, COL_Q // rw + h)),
            pl.BlockSpec((QT, LANES), lambda b, h, i: (b * ni + i, COL_GATE // LANES)),
            whole(ncp, HEAD_DIM),
            whole(HEAD_DIM, ncp),
            per_group(r, 2 * ncp, QT),
            pl.BlockSpec((N_SLC_PAD, ncp), lambda b, h, i: (0, 0)),
            whole(nt, QT, HEAD_DIM + N_SLC_PAD),
            whole(nt, HEAD_DIM, QT),
            whole(nt, QT, HEAD_DIM),
            whole(nt, HEAD_DIM, QT),
            per_group(r, slab, QT),
            per_group(r, 2 * QT, QT),
        ],
        out_specs=pl.BlockSpec((QT, rw), lambda b, h, i: (b * ni + i, h)),
        out_shape=jax.ShapeDtypeStruct((bsz * L, NSA_WIDTH), F32),
        scratch_shapes=[pltpu.VMEM((LANES, QT), F32), pltpu.VMEM((L, r * QT), F32)],
        compiler_params=_cparams("parallel", "parallel", "arbitrary"),
        name="nsa_attention",
    )(proj, proj, k_cmp, v_cmp_t, strip, _overlap_t(L), ks_t, vs_t, kw_t, vw_t, wb, nb)


def _out_proj_kernel(x_ref, ys_ref, yn_ref, w_ref, o_ref):
    half = ys_ref.shape[1]
    acc = jnp.dot(ys_ref[...].astype(BF16), w_ref[:half, :], preferred_element_type=F32)
    acc = acc + jnp.dot(yn_ref[...].astype(BF16), w_ref[half:, :], preferred_element_type=F32)
    o_ref[...] = x_ref[...] + acc


def _out_proj(x2, y_ssm, y_nsa, w, tm=1024, tn=1024):
    t, d = x2.shape
    return pl.pallas_call(
        _out_proj_kernel,
        grid=(t // tm, d // tn),
        in_specs=[pl.BlockSpec((tm, tn), lambda i, j: (i, j)),
                  pl.BlockSpec((tm, SSM_WIDTH), lambda i, j: (i, 0)),
                  pl.BlockSpec((tm, NSA_WIDTH), lambda i, j: (i, 0)),
                  pl.BlockSpec((SSM_WIDTH + NSA_WIDTH, tn), lambda i, j: (0, j))],
        out_specs=pl.BlockSpec((tm, tn), lambda i, j: (i, j)),
        out_shape=jax.ShapeDtypeStruct((t, d), F32),
        compiler_params=_cparams("parallel", "arbitrary"),
        name="out_proj",
    )(x2, y_ssm, y_nsa, w)


def _mlp_kernel(x_ref, n2_ref, wu_ref, wd_ref, nf_ref, o_ref, h_ref, acc_ref):
    f = pl.program_id(1)

    @pl.when(f == 0)
    def _():
        h_ref[...] = _rms(x_ref[...], n2_ref[...]).astype(BF16)
        acc_ref[...] = jnp.zeros_like(acc_ref)

    a = jnp.dot(h_ref[...], wu_ref[...], preferred_element_type=F32)
    a = jnp.square(jnp.maximum(a, 0.0))
    acc_ref[...] += jnp.dot(a.astype(BF16), wd_ref[...], preferred_element_type=F32)

    @pl.when(f == pl.num_programs(1) - 1)
    def _():
        o_ref[...] = _rms(x_ref[...] + acc_ref[...], nf_ref[...])


def _mlp(x2, n2, wu, wd, nf, tm=512, tf=1024):
    t, d = x2.shape
    ff = wu.shape[1]
    return pl.pallas_call(
        _mlp_kernel,
        grid=(t // tm, ff // tf),
        in_specs=[pl.BlockSpec((tm, d), lambda i, f: (i, 0)),
                  pl.BlockSpec((1, d), lambda i, f: (0, 0)),
                  pl.BlockSpec((d, tf), lambda i, f: (0, f)),
                  pl.BlockSpec((tf, d), lambda i, f: (f, 0)),
                  pl.BlockSpec((1, d), lambda i, f: (0, 0))],
        out_specs=pl.BlockSpec((tm, d), lambda i, f: (i, 0)),
        out_shape=jax.ShapeDtypeStruct((t, d), F32),
        scratch_shapes=[pltpu.VMEM((tm, d), BF16), pltpu.VMEM((tm, d), F32)],
        compiler_params=_cparams("parallel", "arbitrary"),
        name="mlp_final_norm",
    )(x2, n2, wu, wd, nf)


def kernel(x, norm1_w, w_in, ssm_a_re, ssm_a_im, ssm_log_dt, ssm_b_re, ssm_b_im, ssm_c_re, ssm_c_im, ssm_d,
           w_glu, b_glu, pe_ck, w_ck1, w_ck2, pe_cv, w_cv1, w_cv2, w_out, norm2_w, w_up, w_down, rel_table,
           norm_f_w):
    bsz, L, d = x.shape
    assert w_in.shape[0] == 1, "the closing rmsnorm is fused into the (single) layer's MLP kernel"
    x2 = x.reshape(bsz * L, d)
    w_in_p = jnp.pad(w_in[0].astype(BF16), ((0, 0), (0, D_IN_PAD - D_IN)))
    proj = _norm_matmul(x2, norm1_w[0].reshape(1, d), w_in_p)
    ops = _s5_operators(ssm_a_re[0], ssm_a_im[0], ssm_log_dt[0], ssm_b_re[0], ssm_b_im[0], ssm_c_re[0], ssm_c_im[0],
                        ssm_d[0])
    y_ssm = _s5_mixer(proj, bsz, L, ops, w_glu[0], b_glu[0])
    y_nsa = _nsa_mixer(proj, bsz, L, pe_ck[0], w_ck1[0], w_ck2[0], pe_cv[0], w_cv1[0], w_cv2[0], rel_table)
    x2 = _out_proj(x2, y_ssm, y_nsa, w_out[0].astype(BF16))
    x2 = _mlp(x2, norm2_w[0].reshape(1, d), w_up[0].astype(BF16), w_down[0].astype(BF16), norm_f_w.reshape(1, d))
    return x2.reshape(bsz, L, d)
```

```python
import functools
import math

import numpy as np
import jax
import jax.numpy as jnp
from jax import lax
from jax.experimental import pallas as pl
from jax.experimental.pallas import tpu as pltpu

F32 = jnp.float32
BF16 = jnp.bfloat16

D_MODEL = 2048
SSM_WIDTH = 1024
SSM_GROUP = 16
SSM_GROUPS = 64
SSM_STATE = 64
NSA_WIDTH = 1024
HEAD_DIM = 64
NSA_HEADS = 16
KV_HEADS = 4
Q_PER_KV = 4
KV_WIDTH = 256
CMP_BLOCK = 32
CMP_STRIDE = 16
CMP_HIDDEN = 256
SLC_BLOCK = 64
SLC_TOPK = 16
WINDOW = 512
REL_BUCKETS = 32
REL_MAX_DIST = 128
D_FF = 8192
EPS = 1e-6
NEG = -1e30
FORCED_SCORE = 1e4
D_IN = SSM_WIDTH + NSA_WIDTH + 6 * KV_WIDTH + 3 * NSA_HEADS
D_IN_PAD = 3840
COL_Q = SSM_WIDTH
COL_KC = COL_Q + NSA_WIDTH
COL_VC = COL_KC + KV_WIDTH
COL_KS = COL_VC + KV_WIDTH
COL_VS = COL_KS + KV_WIDTH
COL_KW = COL_VS + KV_WIDTH
COL_VW = COL_KW + KV_WIDTH
COL_GATE = COL_VW + KV_WIDTH

LANES = 128
S5_CHUNK = 16
SLAB_GROUPS = LANES // SSM_GROUP
N_SLABS = SSM_GROUPS // SLAB_GROUPS
QT = 128
N_SLC_PAD = 64
RANK_SEG = 16
UNSELECTED = -1e9
TABLE_W = 1024
VMEM_LIMIT = 56 * 1024 * 1024
HI = lax.Precision.HIGHEST


def _cparams(*sem):
    return pltpu.CompilerParams(dimension_semantics=sem, vmem_limit_bytes=VMEM_LIMIT)


def _rms(x, w):
    ms = jnp.mean(x * x, axis=-1, keepdims=True)
    return x * lax.rsqrt(ms + EPS) * w


def _lane_cat(parts):
    return jnp.concatenate(parts, axis=1)


def _norm_matmul_kernel(x_ref, nw_ref, w_ref, o_ref, h_ref):
    @pl.when(pl.program_id(1) == 0)
    def _():
        h_ref[...] = _rms(x_ref[...], nw_ref[...]).astype(BF16)

    o_ref[...] = jnp.dot(h_ref[...], w_ref[...], preferred_element_type=F32)


def _norm_matmul(x2, nw, w, tm=1024, tn=768):
    t, d = x2.shape
    n = w.shape[1]
    return pl.pallas_call(
        _norm_matmul_kernel,
        grid=(t // tm, n // tn),
        in_specs=[pl.BlockSpec((tm, d), lambda i, j: (i, 0)),
                  pl.BlockSpec((1, d), lambda i, j: (0, 0)),
                  pl.BlockSpec((d, tn), lambda i, j: (0, j))],
        out_specs=pl.BlockSpec((tm, tn), lambda i, j: (i, j)),
        out_shape=jax.ShapeDtypeStruct((t, n), F32),
        scratch_shapes=[pltpu.VMEM((tm, d), BF16)],
        compiler_params=_cparams("parallel", "arbitrary"),
        name="norm_in_proj",
    )(x2, nw, w)


def _s5_operators(a_re, a_im, log_dt, b_re, b_im, c_re, c_im, d):
    q = S5_CHUNK
    g, p = a_re.shape
    h = SSM_GROUP
    dt = jnp.exp(log_dt)[:, None]
    lam_re, lam_im = dt * a_re, dt * a_im
    mag1 = jnp.exp(lam_re)
    abar_re, abar_im = mag1 * jnp.cos(lam_im), mag1 * jnp.sin(lam_im)
    den = a_re * a_re + a_im * a_im
    zr, zi = abar_re - 1.0, abar_im
    coef_re = (zr * a_re + zi * a_im) / den
    coef_im = (zi * a_re - zr * a_im) / den
    bb_re = coef_re[..., None] * b_re - coef_im[..., None] * b_im
    bb_im = coef_re[..., None] * b_im + coef_im[..., None] * b_re
    k = jnp.arange(q + 1, dtype=F32)[:, None, None]
    mag = jnp.exp(k * lam_re)
    pw_re, pw_im = mag * jnp.cos(k * lam_im), mag * jnp.sin(k * lam_im)
    m_re = pw_re[..., None] * bb_re - pw_im[..., None] * bb_im
    m_im = pw_re[..., None] * bb_im + pw_im[..., None] * bb_re
    kern = (jnp.einsum('ghp,kgpi->gkhi', c_re, m_re[:q], precision=HI)
            - jnp.einsum('ghp,kgpi->gkhi', c_im, m_im[:q], precision=HI))
    kern = kern.at[:, 0].add(jax.vmap(jnp.diag)(d))
    rev = np.arange(q - 1, -1, -1)
    s_end = jnp.stack([m_re[rev], m_im[rev]], axis=0).transpose(2, 1, 4, 0, 3)
    pr, pi = pw_re[1:], pw_im[1:]
    o_re = c_re[None] * pr[:, :, None, :] - c_im[None] * pi[:, :, None, :]
    o_im = -(c_re[None] * pi[:, :, None, :] + c_im[None] * pr[:, :, None, :])
    o_carry = jnp.stack([o_re, o_im], axis=0).transpose(2, 0, 4, 1, 3)
    a_q = jnp.stack([pw_re[q], pw_im[q]], axis=0)
    return kern, s_end, o_carry, a_q


def _s5_slab_weights(kern, s_end, o_carry, a_q):
    q, h, p = S5_CHUNK, SSM_GROUP, SSM_STATE
    s_dense = s_end.reshape(SSM_GROUPS, q, h, 2 * p).astype(BF16)
    k_dense = kern.transpose(0, 1, 3, 2).astype(BF16)
    o_dense = o_carry.reshape(SSM_GROUPS, 2, p, q * h).astype(BF16)
    a_re = a_q[0].reshape(N_SLABS, SLAB_GROUPS * p)
    a_im = a_q[1].reshape(N_SLABS, SLAB_GROUPS * p)
    return s_dense, k_dense, o_dense, a_re, a_im


def _chunk_rows(u_ref, nc):
    return _lane_cat([u_ref[pl.ds(s, nc, stride=S5_CHUNK), :] for s in range(S5_CHUNK)]).astype(BF16)


def _s5_state_kernel(u_ref, sd_ref, o_ref, ws_scr):
    nc = u_ref.shape[0] // S5_CHUNK
    q, h, p, sg = S5_CHUNK, SSM_GROUP, SSM_STATE, SLAB_GROUPS

    @pl.when(pl.program_id(1) == 0)
    def _():
        ws_scr[...] = jnp.zeros_like(ws_scr)
        for s in range(q):
            for a in range(sg):
                for c in range(2):
                    ws_scr[s * LANES + a * h: s * LANES + (a + 1) * h,
                           (c * sg + a) * p: (c * sg + a + 1) * p] = sd_ref[a, s, :, c * p:(c + 1) * p]

    o_ref[0] = jnp.dot(_chunk_rows(u_ref, nc), ws_scr[...], preferred_element_type=F32)


def _s5_scan_kernel(h_ref, ar_ref, ai_ref, o_ref):
    half = ar_ref.shape[1]
    ar = ar_ref[...]
    ai = ai_ref[...]

    def body(c, carry):
        hr, hi = carry
        o_ref[0, c, :, :half] = hr
        o_ref[0, c, :, half:] = hi
        xr = h_ref[0, c, :, :half]
        xi = h_ref[0, c, :, half:]
        return ar * hr - ai * hi + xr, ar * hi + ai * hr + xi

    zero = jnp.zeros(ar.shape, F32)
    lax.fori_loop(0, h_ref.shape[1], body, (zero, zero))


def _s5_out_kernel(u_ref, kd_ref, h_ref, od_ref, y_ref, kl_scr, wi_scr, wc_scr):
    nc = u_ref.shape[0] // S5_CHUNK
    q, h, p, sg = S5_CHUNK, SSM_GROUP, SSM_STATE, SLAB_GROUPS

    @pl.when(pl.program_id(1) == 0)
    def _():
        kl_scr[...] = jnp.zeros_like(kl_scr)
        for k in range(q):
            for a in range(sg):
                kl_scr[k, a * h:(a + 1) * h, a * h:(a + 1) * h] = kd_ref[a, k]
        wi_scr[...] = jnp.zeros_like(wi_scr)
        for s in range(q):
            for t in range(s, q):
                wi_scr[s * LANES:(s + 1) * LANES, t * LANES:(t + 1) * LANES] = kl_scr[t - s]
        wc_scr[...] = jnp.zeros_like(wc_scr)
        for a in range(sg):
            for c in range(2):
                for t in range(q):
                    wc_scr[(c * sg + a) * p:(c * sg + a + 1) * p,
                           t * LANES + a * h: t * LANES + (a + 1) * h] = od_ref[a, c, :, t * h:(t + 1) * h]

    y = (jnp.dot(_chunk_rows(u_ref, nc), wi_scr[...], preferred_element_type=F32)
         + jnp.dot(h_ref[0].astype(BF16), wc_scr[...], preferred_element_type=F32))
    for t in range(q):
        y_ref[pl.ds(t, nc, stride=q), :] = y[:, t * LANES:(t + 1) * LANES]


def _glu_kernel(y_ref, w_ref, b_ref, o_ref):
    z = jax.nn.gelu(y_ref[...])
    gate = jnp.dot(z.astype(BF16), w_ref[...], preferred_element_type=F32) + b_ref[...]
    o_ref[...] = z * jax.nn.sigmoid(gate)


def _s5_mixer(proj, bsz, L, ops, w_glu, b_glu):
    s_dense, k_dense, o_dense, a_re, a_im = _s5_slab_weights(*ops)
    q, h, p = S5_CHUNK, SSM_GROUP, SSM_STATE
    nc = L // q
    ns, sg = N_SLABS, SLAB_GROUPS
    sw = sg * 2 * p
    kq = q * LANES
    hend = pl.pallas_call(
        _s5_state_kernel,
        grid=(ns, bsz),
        in_specs=[pl.BlockSpec((L, LANES), lambda j, b: (b, j)),
                  pl.BlockSpec((sg, q, h, 2 * p), lambda j, b: (j, 0, 0, 0))],
        out_specs=pl.BlockSpec((1, nc, sw), lambda j, b: (b, 0, j)),
        out_shape=jax.ShapeDtypeStruct((bsz, nc, ns * sw), F32),
        scratch_shapes=[pltpu.VMEM((kq, sw), BF16)],
        compiler_params=_cparams("parallel", "arbitrary"),
        name="s5_chunk_state",
    )(proj, s_dense)
    hprev4 = pl.pallas_call(
        _s5_scan_kernel,
        grid=(bsz,),
        in_specs=[pl.BlockSpec((1, nc, ns, sw), lambda b: (b, 0, 0, 0)),
                  pl.BlockSpec((ns, sw // 2), lambda b: (0, 0)),
                  pl.BlockSpec((ns, sw // 2), lambda b: (0, 0))],
        out_specs=pl.BlockSpec((1, nc, ns, sw), lambda b: (b, 0, 0, 0)),
        out_shape=jax.ShapeDtypeStruct((bsz, nc, ns, sw), F32),
        compiler_params=_cparams("parallel"),
        name="s5_chunk_scan",
    )(hend.reshape(bsz, nc, ns, sw), a_re, a_im)
    y = pl.pallas_call(
        _s5_out_kernel,
        grid=(ns, bsz),
        in_specs=[pl.BlockSpec((L, LANES), lambda j, b: (b, j)),
                  pl.BlockSpec((sg, q, h, h), lambda j, b: (j, 0, 0, 0)),
                  pl.BlockSpec((1, nc, sw), lambda j, b: (b, 0, j)),
                  pl.BlockSpec((sg, 2, p, q * h), lambda j, b: (j, 0, 0, 0))],
        out_specs=pl.BlockSpec((L, LANES), lambda j, b: (b, j)),
        out_shape=jax.ShapeDtypeStruct((bsz * L, SSM_WIDTH), F32),
        scratch_shapes=[pltpu.VMEM((q, LANES, LANES), BF16), pltpu.VMEM((kq, kq), BF16), pltpu.VMEM((sw, kq), BF16)],
        compiler_params=_cparams("parallel", "arbitrary"),
        name="s5_chunk_out",
    )(proj, k_dense, hprev4.reshape(bsz, nc, ns * sw), o_dense)
    tm = 512
    return pl.pallas_call(
        _glu_kernel,
        grid=(bsz * L // tm,),
        in_specs=[pl.BlockSpec((tm, SSM_WIDTH), lambda i: (i, 0)),
                  pl.BlockSpec((SSM_WIDTH, SSM_WIDTH), lambda i: (0, 0)),
                  pl.BlockSpec((1, SSM_WIDTH), lambda i: (0, 0))],
        out_specs=pl.BlockSpec((tm, SSM_WIDTH), lambda i: (i, 0)),
        out_shape=jax.ShapeDtypeStruct((bsz * L, SSM_WIDTH), F32),
        compiler_params=_cparams("parallel"),
        name="s5_gelu_glu",
    )(y, w_glu.astype(BF16), b_glu.reshape(1, SSM_WIDTH))


def _compress_kernel(k_ref, w1_ref, pe_ref, w1full_ref, w2_ref, o_ref, *, transpose_out):
    nc = k_ref.shape[0] // CMP_STRIDE
    prod = jnp.dot(_chunk_rows(k_ref, nc), w1_ref[...], preferred_element_type=F32)
    pe_h = jnp.dot(jnp.broadcast_to(pe_ref[...], (8, pe_ref.shape[1])), w1full_ref[...],
                   precision=HI, preferred_element_type=F32)[0:1]
    two_h = 2 * CMP_HIDDEN
    for gl in range(LANES // HEAD_DIM):
        first = prod[:, gl * two_h: gl * two_h + CMP_HIDDEN]
        second = prod[:, gl * two_h + CMP_HIDDEN: (gl + 1) * two_h]
        hid = first + pltpu.roll(second, nc - 1, axis=0) + pe_h
        out = jnp.dot(jax.nn.gelu(hid).astype(BF16), w2_ref[...], preferred_element_type=F32)
        if transpose_out:
            o_ref[0, gl] = _lane_cat([out, jnp.zeros_like(out)]).T[:HEAD_DIM].astype(BF16)
        else:
            o_ref[0, gl] = out.astype(BF16)


def _compress(proj, col, bsz, L, pe, w1, w2, transpose_out):
    nc = L // CMP_STRIDE
    half = CMP_STRIDE * HEAD_DIM
    gpl = LANES // HEAD_DIM
    w1cat = jnp.concatenate([w1[:half], w1[half:]], axis=1)
    eye = jnp.eye(gpl, dtype=F32)
    w1slab = jnp.einsum('ldh,ag->ladgh', w1cat.reshape(CMP_STRIDE, HEAD_DIM, 2 * CMP_HIDDEN), eye)
    w1slab = w1slab.reshape(CMP_STRIDE * LANES, gpl * 2 * CMP_HIDDEN).astype(BF16)
    out_block = (1, gpl, HEAD_DIM, nc) if transpose_out else (1, gpl, nc, HEAD_DIM)
    out_full = (bsz, KV_HEADS) + out_block[2:]
    return pl.pallas_call(
        functools.partial(_compress_kernel, transpose_out=transpose_out),
        grid=(bsz, KV_HEADS // gpl),
        in_specs=[pl.BlockSpec((L, LANES), lambda b, j: (b, col // LANES + j)),
                  pl.BlockSpec(w1slab.shape, lambda b, j: (0, 0)),
                  pl.BlockSpec((1, 2 * half), lambda b, j: (0, 0)),
                  pl.BlockSpec((2 * half, CMP_HIDDEN), lambda b, j: (0, 0)),
                  pl.BlockSpec((CMP_HIDDEN, HEAD_DIM), lambda b, j: (0, 0))],
        out_specs=pl.BlockSpec(out_block, lambda b, j: (b, j, 0, 0)),
        out_shape=jax.ShapeDtypeStruct(out_full, BF16),
        compiler_params=_cparams("parallel", "parallel"),
        name="nsa_compress",
    )(proj, w1slab, pe.reshape(1, 2 * half), w1, w2.astype(BF16))


def _kv_tiles_kernel(ks_ref, vs_ref, kw_ref, vw_ref, kso_ref, vso_ref, kwo_ref, vwo_ref):
    tm = ks_ref.shape[0]
    row0 = pl.program_id(1) * tm
    lane_blk = lax.broadcasted_iota(jnp.int32, (QT, N_SLC_PAD), 1)
    for k in range(tm // QT):
        rows = slice(k * QT, (k + 1) * QT)
        tok = lax.broadcasted_iota(jnp.int32, (QT, N_SLC_PAD), 0) + (row0 + k * QT)
        onehot = jnp.where((tok >> int(math.log2(SLC_BLOCK))) == lane_blk, 1.0, 0.0).astype(BF16)
        vs_t = vs_ref[rows, :].T
        vw_t = vw_ref[rows, :].T
        for g in range(KV_HEADS):
            cols = slice(g * HEAD_DIM, (g + 1) * HEAD_DIM)
            kso_ref[0, g, k, :, :HEAD_DIM] = ks_ref[rows, cols].astype(BF16)
            kso_ref[0, g, k, :, HEAD_DIM:] = onehot
            kwo_ref[0, g, k] = kw_ref[rows, cols].astype(BF16)
            vso_ref[0, g, k] = vs_t[cols].astype(BF16)
            vwo_ref[0, g, k] = vw_t[cols].astype(BF16)


def _kv_tiles(proj, bsz, L, tm=512):
    nt = L // QT
    g = KV_HEADS
    kt = tm // QT
    in_spec = lambda col: pl.BlockSpec((tm, KV_WIDTH), lambda b, i: (b * (L // tm) + i, col // KV_WIDTH))
    out_spec = lambda a, c: pl.BlockSpec((1, g, kt, a, c), lambda b, i: (b, 0, i, 0, 0))
    shape = lambda a, c: jax.ShapeDtypeStruct((bsz, g, nt, a, c), BF16)
    return pl.pallas_call(
        _kv_tiles_kernel,
        grid=(bsz, L // tm),
        in_specs=[in_spec(COL_KS), in_spec(COL_VS), in_spec(COL_KW), in_spec(COL_VW)],
        out_specs=[out_spec(QT, HEAD_DIM + N_SLC_PAD), out_spec(HEAD_DIM, QT), out_spec(QT, HEAD_DIM),
                   out_spec(HEAD_DIM, QT)],
        out_shape=[shape(QT, HEAD_DIM + N_SLC_PAD), shape(HEAD_DIM, QT), shape(QT, HEAD_DIM), shape(HEAD_DIM, QT)],
        compiler_params=_cparams("parallel", "parallel"),
        name="nsa_kv_tiles",
    )(proj, proj, proj, proj)


def _rel_bucket(dist):
    dist = jnp.maximum(dist, 0)
    max_exact = REL_BUCKETS // 2
    large = max_exact + (jnp.log(jnp.maximum(dist, 1).astype(F32) / max_exact)
                         / math.log(REL_MAX_DIST / max_exact) * (REL_BUCKETS - max_exact)).astype(jnp.int32)
    large = jnp.minimum(large, REL_BUCKETS - 1)
    return jnp.where(dist < max_exact, dist, large)


def _toeplitz_rows(base, nrows, step):
    w = base.shape[1]
    x = jnp.broadcast_to(base, (nrows, w))
    k_ix = lax.broadcasted_iota(jnp.int32, (nrows, w), 0)
    bit = 1
    while bit < nrows:
        x = jnp.where((k_ix & bit) != 0, pltpu.roll(x, (bit * step) % w, axis=1), x)
        bit *= 2
    return x


def _bias_tables_kernel(bw_ref, bc_ref, far_ref, wb_ref, nb_ref, strip_ref, *, ncp):
    slab = WINDOW + QT
    y = _toeplitz_rows(bw_ref[0], QT, 1)
    for kh in range(slab // QT):
        tile = y[:, WINDOW - kh * QT: WINDOW - kh * QT + QT]
        wb_ref[0, kh * QT:(kh + 1) * QT, :] = tile
        if kh >= slab // QT - 2:
            row = kh - (slab // QT - 2)
            nb_ref[0, row * QT:(row + 1) * QT, :] = tile - far_ref[0]
    band = 2 * CMP_STRIDE
    z = _toeplitz_rows(bc_ref[0], band, CMP_STRIDE)[:, 2 * QT: 3 * QT]
    strip_ref[0, :ncp - CMP_STRIDE, :] = jnp.broadcast_to(far_ref[0], (ncp - CMP_STRIDE, QT))
    strip_ref[0, ncp - CMP_STRIDE: ncp + CMP_STRIDE, :] = z
    strip_ref[0, ncp + CMP_STRIDE:, :] = jnp.full((ncp - CMP_STRIDE, QT), NEG, F32)


def _nsa_tables(rel_table, L):
    ncp = L // CMP_STRIDE
    slab = WINDOW + QT
    dist = jnp.arange(TABLE_W)
    per_dist = rel_table[_rel_bucket(dist)].T
    base_w = jnp.where(dist < WINDOW, per_dist, NEG)
    shift = CMP_BLOCK - 1
    shifted = jnp.concatenate([jnp.full((NSA_HEADS, shift), NEG, F32), per_dist[:, :TABLE_W - shift]], axis=1)
    base_c = jnp.where(dist < TABLE_W // 2, shifted, NEG)
    far = jnp.broadcast_to(rel_table[REL_BUCKETS - 1][:, None, None], (NSA_HEADS, 1, QT))
    wb, nb, strip = pl.pallas_call(
        functools.partial(_bias_tables_kernel, ncp=ncp),
        grid=(NSA_HEADS,),
        in_specs=[pl.BlockSpec((1, 1, TABLE_W), lambda h: (h, 0, 0)),
                  pl.BlockSpec((1, 1, TABLE_W), lambda h: (h, 0, 0)),
                  pl.BlockSpec((1, 1, QT), lambda h: (h, 0, 0))],
        out_specs=[pl.BlockSpec((1, slab, QT), lambda h: (h, 0, 0)),
                   pl.BlockSpec((1, 2 * QT, QT), lambda h: (h, 0, 0)),
                   pl.BlockSpec((1, 2 * ncp, QT), lambda h: (h, 0, 0))],
        out_shape=[jax.ShapeDtypeStruct((NSA_HEADS, slab, QT), F32),
                   jax.ShapeDtypeStruct((NSA_HEADS, 2 * QT, QT), F32),
                   jax.ShapeDtypeStruct((NSA_HEADS, 2 * ncp, QT), F32)],
        compiler_params=_cparams("parallel"),
        name="nsa_bias_tables",
    )(base_w.reshape(NSA_HEADS, 1, TABLE_W), base_c.reshape(NSA_HEADS, 1, TABLE_W), far)
    g, r = KV_HEADS, Q_PER_KV
    return wb.reshape(g, r, slab, QT), nb.reshape(g, r, 2 * QT, QT), strip.reshape(g, r, 2 * ncp, QT)


def _overlap_t(L):
    n_cmp = (L - CMP_BLOCK) // CMP_STRIDE + 1
    n_slc = L // SLC_BLOCK
    cmp_idx = np.arange(n_cmp)[:, None] * CMP_STRIDE + np.arange(CMP_BLOCK)[None, :]
    overlap = ((cmp_idx[:, :, None] // SLC_BLOCK) == np.arange(n_slc)[None, None, :]).sum(1) / CMP_BLOCK
    out = np.zeros((N_SLC_PAD, L // CMP_STRIDE), np.float32)
    out[:n_slc, :n_cmp] = overlap.T
    return jnp.asarray(out, dtype=BF16)


def _split3(x):
    hi = x.astype(BF16)
    r1 = x - hi.astype(F32)
    mid = r1.astype(BF16)
    lo = (r1 - mid.astype(F32)).astype(BF16)
    return hi, mid, lo


def _nsa_kernel(q_ref, gl_ref, kc_ref, vcT_ref, strip_ref, ovT_ref, ks_ref, vsT_ref, kw_ref, vwT_ref, wb_ref,
                nb_ref, o_ref, gate_ref, s_scr, *, n_slc):
    g = pl.program_id(1)
    i = pl.program_id(2)
    R = Q_PER_KV
    RQ = R * QT
    qT = (q_ref[...].T * 0.125).astype(BF16)
    q_heads = [qT[r * HEAD_DIM:(r + 1) * HEAD_DIM] for r in range(R)]
    qcat = _lane_cat(q_heads)
    t_pos = lax.broadcasted_iota(jnp.int32, (1, QT), 1) + i * QT

    slab = WINDOW + QT
    n_win = slab // QT
    k_rel = lax.broadcasted_iota(jnp.int32, (slab, 1), 0) + (i - WINDOW // QT) * QT
    win_idx = [jnp.maximum(i + jj - (n_win - 1), 0) for jj in range(n_win)]

    kw = jnp.concatenate([kw_ref[0, 0, j] for j in win_idx], axis=0)
    wb = _lane_cat([wb_ref[0, r] for r in range(R)])
    sw = jnp.dot(kw, qcat, preferred_element_type=F32) + jnp.where(k_rel >= 0, wb, NEG)
    m_w = jnp.max(sw, axis=0, keepdims=True)
    e_w = jnp.exp(sw - m_w)
    l_w = jnp.sum(e_w, axis=0, keepdims=True)
    e_wb = e_w.astype(BF16)
    o_win = None
    for jj, j in enumerate(win_idx):
        term = jnp.dot(vwT_ref[0, 0, j], e_wb[jj * QT:(jj + 1) * QT], preferred_element_type=F32)
        o_win = term if o_win is None else o_win + term
    o_win = o_win * (1.0 / l_w)

    ncp = kc_ref.shape[2]
    sc = jnp.dot(kc_ref[0, 0], qcat, preferred_element_type=F32)
    strip_row = pl.multiple_of(ncp - i * (QT // CMP_STRIDE), QT // CMP_STRIDE)
    sm = sc + _lane_cat([strip_ref[0, r, pl.ds(strip_row, ncp), :] for r in range(R)])
    m_c = jnp.max(sm, axis=0, keepdims=True)
    e_c = jnp.exp(sm - m_c)
    l_c = jnp.sum(e_c, axis=0, keepdims=True)
    inv_c = jnp.where(m_c > 0.5 * NEG, 1.0 / l_c, 0.0)
    o_cmp = jnp.dot(vcT_ref[0, 0], e_c.astype(BF16), preferred_element_type=F32) * inv_c
    p_c = e_c * inv_c
    p_sum = p_c[:, 0:QT]
    for r in range(1, R):
        p_sum = p_sum + p_c[:, r * QT:(r + 1) * QT]

    ov = ovT_ref[...]
    imp = None
    for part in _split3(p_sum):
        term = jnp.dot(ov, part, preferred_element_type=F32)
        imp = term if imp is None else imp + term
    s_ix = lax.broadcasted_iota(jnp.int32, (N_SLC_PAD, QT), 0)
    cur = t_pos >> int(math.log2(SLC_BLOCK))
    forced = (s_ix == 0) | (s_ix == cur) | (s_ix == cur - 1)
    score = jnp.where(forced, FORCED_SCORE, jnp.where(s_ix <= cur, imp, -1.0))
    score = jnp.where(s_ix < n_slc, score, -2.0)
    SUB = 8
    score_rows = [score[v * SUB:(v + 1) * SUB] for v in range(N_SLC_PAD // SUB)]
    sub_ix = lax.broadcasted_iota(jnp.int32, (SUB, QT), 0)

    def rank_segment(seg, ranks):
        ranks = list(ranks)
        for sp in range(seg * RANK_SEG, min((seg + 1) * RANK_SEG, n_slc)):
            row = score[sp:sp + 1, :]
            for v, blk in enumerate(score_rows):
                if v * SUB > sp:
                    beats = row >= blk
                elif v * SUB + SUB - 1 <= sp:
                    beats = row > blk
                else:
                    beats = (row > blk) | ((row == blk) & (sub_ix > sp - v * SUB))
                ranks[v] = ranks[v] + jnp.where(beats, 1.0, 0.0)
        return tuple(ranks)

    ranks = rank_segment(0, tuple(jnp.zeros((SUB, QT), F32) for _ in score_rows))
    last_block = (i + 1) * (QT // SLC_BLOCK) - 1
    for seg in range(1, -(-n_slc // RANK_SEG)):
        ranks = lax.cond(last_block >= seg * RANK_SEG, functools.partial(rank_segment, seg), lambda rk: rk, ranks)
    rank = jnp.concatenate(ranks, axis=0)
    sel_neg = jnp.where(rank < float(SLC_TOPK), 0.0, UNSELECTED).astype(BF16)

    qsel = _lane_cat([jnp.concatenate([q_heads[r], sel_neg], axis=0) for r in range(R)])

    FT = 4 * QT
    n_trips = jnp.maximum(i - 1, 0) // 4

    def far_rows(t):
        return pl.ds(pl.multiple_of(t * FT, FT), FT)

    def score_trip(t, m):
        k4 = jnp.concatenate([ks_ref[0, 0, 4 * t + n] for n in range(4)], axis=0)
        s = jnp.dot(k4, qsel, preferred_element_type=F32)
        s_scr[far_rows(t), :] = s
        return jnp.maximum(m, jnp.max(s, axis=0, keepdims=True))

    m_far = lax.fori_loop(0, n_trips, score_trip, jnp.full((1, RQ), NEG, F32))
    near0 = slab - 2 * QT
    first_valid = jnp.where(lax.broadcasted_iota(jnp.int32, (slab, 1), 0) < near0, n_trips * FT, 0)
    nb = jnp.concatenate([jnp.zeros((near0, RQ), F32), _lane_cat([nb_ref[0, r] for r in range(R)])], axis=0)
    k5 = jnp.concatenate([ks_ref[0, 0, j] for j in win_idx], axis=0)
    s_tail = jnp.dot(k5, qsel, preferred_element_type=F32) + jnp.where(k_rel >= first_valid, nb, NEG)
    m_s = jnp.maximum(m_far, jnp.max(s_tail, axis=0, keepdims=True))

    def value_trip(t, carry):
        l, acc = carry
        p = jnp.exp(s_scr[far_rows(t), :] - m_s)
        l = l + jnp.sum(p, axis=0, keepdims=True)
        p = p.astype(BF16)
        for n in range(4):
            acc = acc + jnp.dot(vsT_ref[0, 0, 4 * t + n], p[n * QT:(n + 1) * QT], preferred_element_type=F32)
        return l, acc

    l_s, acc_s = lax.fori_loop(0, n_trips, value_trip,
                               (jnp.zeros((1, RQ), F32), jnp.zeros((HEAD_DIM, RQ), F32)))
    p_tail = jnp.exp(s_tail - m_s)
    l_s = l_s + jnp.sum(p_tail, axis=0, keepdims=True)
    p_tail = p_tail.astype(BF16)
    for jj, j in enumerate(win_idx):
        acc_s = acc_s + jnp.dot(vsT_ref[0, 0, j], p_tail[jj * QT:(jj + 1) * QT], preferred_element_type=F32)
    o_slc = acc_s * (1.0 / l_s)

    gate_ref[...] = jax.nn.sigmoid(gl_ref[...].T)
    gates = [_lane_cat([gate_ref[pl.ds((g * R + r) * 3 + br, 1), :] for r in range(R)]) for br in range(3)]
    total = gates[0] * o_cmp + gates[1] * o_slc + gates[2] * o_win
    o_ref[...] = jnp.concatenate([total[:, r * QT:(r + 1) * QT] for r in range(R)], axis=0).T


def _nsa_mixer(proj, bsz, L, pe_ck, w_ck1, w_ck2, pe_cv, w_cv1, w_cv2, rel_table):
    g, r = KV_HEADS, Q_PER_KV
    ni = L // QT
    nt = L // QT
    ncp = L // CMP_STRIDE
    n_slc = L // SLC_BLOCK
    slab = WINDOW + QT
    k_cmp = _compress(proj, COL_KC, bsz, L, pe_ck, w_ck1, w_ck2, transpose_out=False)
    v_cmp_t = _compress(proj, COL_VC, bsz, L, pe_cv, w_cv1, w_cv2, transpose_out=True)
    ks_t, vs_t, kw_t, vw_t = _kv_tiles(proj, bsz, L)
    wb, nb, strip = _nsa_tables(rel_table, L)
    rw = r * HEAD_DIM
    whole = lambda *shape: pl.BlockSpec((1, 1) + shape, lambda b, h, i: (b, h) + (0,) * len(shape))
    per_group = lambda *shape: pl.BlockSpec((1,) + shape, lambda b, h, i: (h,) + (0,) * len(shape))
    return pl.pallas_call(
        functools.partial(_nsa_kernel, n_slc=n_slc),
        grid=(bsz, g, ni),
        in_specs=[
            pl.BlockSpec((QT, rw), lambda b, h, i: (b * ni + i, COL_Q // rw + h)),
            pl.BlockSpec((QT, LANES), lambda b, h, i: (b * ni + i, COL_GATE // LANES)),
            whole(ncp, HEAD_DIM),
            whole(HEAD_DIM, ncp),
            per_group(r, 2 * ncp, QT),
            pl.BlockSpec((N_SLC_PAD, ncp), lambda b, h, i: (0, 0)),
            whole(nt, QT, HEAD_DIM + N_SLC_PAD),
            whole(nt, HEAD_DIM, QT),
            whole(nt, QT, HEAD_DIM),
            whole(nt, HEAD_DIM, QT),
            per_group(r, slab, QT),
            per_group(r, 2 * QT, QT),
        ],
        out_specs=pl.BlockSpec((QT, rw), lambda b, h, i: (b * ni + i, h)),
        out_shape=jax.ShapeDtypeStruct((bsz * L, NSA_WIDTH), F32),
        scratch_shapes=[pltpu.VMEM((LANES, QT), F32), pltpu.VMEM((L, r * QT), F32)],
        compiler_params=_cparams("parallel", "parallel", "arbitrary"),
        name="nsa_attention",
    )(proj, proj, k_cmp, v_cmp_t, strip, _overlap_t(L), ks_t, vs_t, kw_t, vw_t, wb, nb)


def _out_proj_kernel(x_ref, ys_ref, yn_ref, w_ref, o_ref):
    half = ys_ref.shape[1]
    acc = jnp.dot(ys_ref[...].astype(BF16), w_ref[:half, :], preferred_element_type=F32)
    acc = acc + jnp.dot(yn_ref[...].astype(BF16), w_ref[half:, :], preferred_element_type=F32)
    o_ref[...] = x_ref[...] + acc


def _out_proj(x2, y_ssm, y_nsa, w, tm=1024, tn=1024):
    t, d = x2.shape
    return pl.pallas_call(
        _out_proj_kernel,
        grid=(t // tm, d // tn),
        in_specs=[pl.BlockSpec((tm, tn), lambda i, j: (i, j)),
                  pl.BlockSpec((tm, SSM_WIDTH), lambda i, j: (i, 0)),
                  pl.BlockSpec((tm, NSA_WIDTH), lambda i, j: (i, 0)),
                  pl.BlockSpec((SSM_WIDTH + NSA_WIDTH, tn), lambda i, j: (0, j))],
        out_specs=pl.BlockSpec((tm, tn), lambda i, j: (i, j)),
        out_shape=jax.ShapeDtypeStruct((t, d), F32),
        compiler_params=_cparams("parallel", "arbitrary"),
        name="out_proj",
    )(x2, y_ssm, y_nsa, w)


def _mlp_kernel(x_ref, n2_ref, wu_ref, wd_ref, nf_ref, o_ref, h_ref, acc_ref):
    f = pl.program_id(1)

    @pl.when(f == 0)
    def _():
        h_ref[...] = _rms(x_ref[...], n2_ref[...]).astype(BF16)
        acc_ref[...] = jnp.zeros_like(acc_ref)

    a = jnp.dot(h_ref[...], wu_ref[...], preferred_element_type=F32)
    a = jnp.square(jnp.maximum(a, 0.0))
    acc_ref[...] += jnp.dot(a.astype(BF16), wd_ref[...], preferred_element_type=F32)

    @pl.when(f == pl.num_programs(1) - 1)
    def _():
        o_ref[...] = _rms(x_ref[...] + acc_ref[...], nf_ref[...])


def _mlp(x2, n2, wu, wd, nf, tm=512, tf=1024):
    t, d = x2.shape
    ff = wu.shape[1]
    return pl.pallas_call(
        _mlp_kernel,
        grid=(t // tm, ff // tf),
        in_specs=[pl.BlockSpec((tm, d), lambda i, f: (i, 0)),
                  pl.BlockSpec((1, d), lambda i, f: (0, 0)),
                  pl.BlockSpec((d, tf), lambda i, f: (0, f)),
                  pl.BlockSpec((tf, d), lambda i, f: (f, 0)),
                  pl.BlockSpec((1, d), lambda i, f: (0, 0))],
        out_specs=pl.BlockSpec((tm, d), lambda i, f: (i, 0)),
        out_shape=jax.ShapeDtypeStruct((t, d), F32),
        scratch_shapes=[pltpu.VMEM((tm, d), BF16), pltpu.VMEM((tm, d), F32)],
        compiler_params=_cparams("parallel", "arbitrary"),
        name="mlp_final_norm",
    )(x2, n2, wu, wd, nf)


def kernel(x, norm1_w, w_in, ssm_a_re, ssm_a_im, ssm_log_dt, ssm_b_re, ssm_b_im, ssm_c_re, ssm_c_im, ssm_d,
           w_glu, b_glu, pe_ck, w_ck1, w_ck2, pe_cv, w_cv1, w_cv2, w_out, norm2_w, w_up, w_down, rel_table,
           norm_f_w):
    bsz, L, d = x.shape
    assert w_in.shape[0] == 1, "the closing rmsnorm is fused into the (single) layer's MLP kernel"
    x2 = x.reshape(bsz * L, d)
    w_in_p = jnp.pad(w_in[0].astype(BF16), ((0, 0), (0, D_IN_PAD - D_IN)))
    proj = _norm_matmul(x2, norm1_w[0].reshape(1, d), w_in_p)
    ops = _s5_operators(ssm_a_re[0], ssm_a_im[0], ssm_log_dt[0], ssm_b_re[0], ssm_b_im[0], ssm_c_re[0], ssm_c_im[0],
                        ssm_d[0])
    y_ssm = _s5_mixer(proj, bsz, L, ops, w_glu[0], b_glu[0])
    y_nsa = _nsa_mixer(proj, bsz, L, pe_ck[0], w_ck1[0], w_ck2[0], pe_cv[0], w_cv1[0], w_cv2[0], rel_table)
    x2 = _out_proj(x2, y_ssm, y_nsa, w_out[0].astype(BF16))
    x2 = _mlp(x2, norm2_w[0].reshape(1, d), w_up[0].astype(BF16), w_down[0].astype(BF16), norm_f_w.reshape(1, d))
    return x2.reshape(bsz, L, d)
```

```python
import functools
import math

import numpy as np
import jax
import jax.numpy as jnp
from jax import lax
from jax.experimental import pallas as pl
from jax.experimental.pallas import tpu as pltpu

F32 = jnp.float32
BF16 = jnp.bfloat16

D_MODEL = 2048
SSM_WIDTH = 1024
SSM_GROUP = 16
SSM_GROUPS = 64
SSM_STATE = 64
NSA_WIDTH = 1024
HEAD_DIM = 64
NSA_HEADS = 16
KV_HEADS = 4
Q_PER_KV = 4
KV_WIDTH = 256
CMP_BLOCK = 32
CMP_STRIDE = 16
CMP_HIDDEN = 256
SLC_BLOCK = 64
SLC_TOPK = 16
WINDOW = 512
REL_BUCKETS = 32
REL_MAX_DIST = 128
D_FF = 8192
EPS = 1e-6
NEG = -1e30
FORCED_SCORE = 1e4
D_IN = SSM_WIDTH + NSA_WIDTH + 6 * KV_WIDTH + 3 * NSA_HEADS
D_IN_PAD = 3840
COL_Q = SSM_WIDTH
COL_KC = COL_Q + NSA_WIDTH
COL_VC = COL_KC + KV_WIDTH
COL_KS = COL_VC + KV_WIDTH
COL_VS = COL_KS + KV_WIDTH
COL_KW = COL_VS + KV_WIDTH
COL_VW = COL_KW + KV_WIDTH
COL_GATE = COL_VW + KV_WIDTH

LANES = 128
S5_CHUNK = 16
SLAB_GROUPS = LANES // SSM_GROUP
N_SLABS = SSM_GROUPS // SLAB_GROUPS
QT = 128
N_SLC_PAD = 64
RANK_SEG = 16
UNSELECTED = -1e9
LOG2E = math.log2(math.e)
TABLE_W = 1024
VMEM_LIMIT = 56 * 1024 * 1024
HI = lax.Precision.HIGHEST


def _cparams(*sem):
    return pltpu.CompilerParams(dimension_semantics=sem, vmem_limit_bytes=VMEM_LIMIT)


def _rms(x, w):
    ms = jnp.mean(x * x, axis=-1, keepdims=True)
    return x * lax.rsqrt(ms + EPS) * w


def _lane_cat(parts):
    return jnp.concatenate(parts, axis=1)


def _norm_matmul_kernel(x_ref, nw_ref, w_ref, o_ref, h_ref):
    @pl.when(pl.program_id(1) == 0)
    def _():
        h_ref[...] = _rms(x_ref[...], nw_ref[...]).astype(BF16)

    o_ref[...] = jnp.dot(h_ref[...], w_ref[...], preferred_element_type=F32)


def _norm_matmul(x2, nw, w, tm=1024, tn=768):
    t, d = x2.shape
    n = w.shape[1]
    return pl.pallas_call(
        _norm_matmul_kernel,
        grid=(t // tm, n // tn),
        in_specs=[pl.BlockSpec((tm, d), lambda i, j: (i, 0)),
                  pl.BlockSpec((1, d), lambda i, j: (0, 0)),
                  pl.BlockSpec((d, tn), lambda i, j: (0, j))],
        out_specs=pl.BlockSpec((tm, tn), lambda i, j: (i, j)),
        out_shape=jax.ShapeDtypeStruct((t, n), F32),
        scratch_shapes=[pltpu.VMEM((tm, d), BF16)],
        compiler_params=_cparams("parallel", "arbitrary"),
        name="norm_in_proj",
    )(x2, nw, w)


def _s5_operators(a_re, a_im, log_dt, b_re, b_im, c_re, c_im, d):
    q = S5_CHUNK
    g, p = a_re.shape
    h = SSM_GROUP
    dt = jnp.exp(log_dt)[:, None]
    lam_re, lam_im = dt * a_re, dt * a_im
    mag1 = jnp.exp(lam_re)
    abar_re, abar_im = mag1 * jnp.cos(lam_im), mag1 * jnp.sin(lam_im)
    den = a_re * a_re + a_im * a_im
    zr, zi = abar_re - 1.0, abar_im
    coef_re = (zr * a_re + zi * a_im) / den
    coef_im = (zi * a_re - zr * a_im) / den
    bb_re = coef_re[..., None] * b_re - coef_im[..., None] * b_im
    bb_im = coef_re[..., None] * b_im + coef_im[..., None] * b_re
    k = jnp.arange(q + 1, dtype=F32)[:, None, None]
    mag = jnp.exp(k * lam_re)
    pw_re, pw_im = mag * jnp.cos(k * lam_im), mag * jnp.sin(k * lam_im)
    m_re = pw_re[..., None] * bb_re - pw_im[..., None] * bb_im
    m_im = pw_re[..., None] * bb_im + pw_im[..., None] * bb_re
    kern = (jnp.einsum('ghp,kgpi->gkhi', c_re, m_re[:q], precision=HI)
            - jnp.einsum('ghp,kgpi->gkhi', c_im, m_im[:q], precision=HI))
    kern = kern.at[:, 0].add(jax.vmap(jnp.diag)(d))
    rev = np.arange(q - 1, -1, -1)
    s_end = jnp.stack([m_re[rev], m_im[rev]], axis=0).transpose(2, 1, 4, 0, 3)
    pr, pi = pw_re[1:], pw_im[1:]
    o_re = c_re[None] * pr[:, :, None, :] - c_im[None] * pi[:, :, None, :]
    o_im = -(c_re[None] * pi[:, :, None, :] + c_im[None] * pr[:, :, None, :])
    o_carry = jnp.stack([o_re, o_im], axis=0).transpose(2, 0, 4, 1, 3)
    a_q = jnp.stack([pw_re[q], pw_im[q]], axis=0)
    return kern, s_end, o_carry, a_q


def _s5_slab_weights(kern, s_end, o_carry, a_q):
    q, h, p = S5_CHUNK, SSM_GROUP, SSM_STATE
    s_dense = s_end.reshape(SSM_GROUPS, q, h, 2 * p).astype(BF16)
    k_dense = kern.transpose(0, 1, 3, 2).astype(BF16)
    o_dense = o_carry.reshape(SSM_GROUPS, 2, p, q * h).astype(BF16)
    a_re = a_q[0].reshape(N_SLABS, SLAB_GROUPS * p)
    a_im = a_q[1].reshape(N_SLABS, SLAB_GROUPS * p)
    return s_dense, k_dense, o_dense, a_re, a_im


def _chunk_rows(u_ref, nc):
    return _lane_cat([u_ref[pl.ds(s, nc, stride=S5_CHUNK), :] for s in range(S5_CHUNK)]).astype(BF16)


def _s5_state_kernel(u_ref, sd_ref, o_ref, ws_scr):
    nc = u_ref.shape[0] // S5_CHUNK
    q, h, p, sg = S5_CHUNK, SSM_GROUP, SSM_STATE, SLAB_GROUPS

    @pl.when(pl.program_id(1) == 0)
    def _():
        ws_scr[...] = jnp.zeros_like(ws_scr)
        for s in range(q):
            for a in range(sg):
                for c in range(2):
                    ws_scr[s * LANES + a * h: s * LANES + (a + 1) * h,
                           (c * sg + a) * p: (c * sg + a + 1) * p] = sd_ref[a, s, :, c * p:(c + 1) * p]

    o_ref[0] = jnp.dot(_chunk_rows(u_ref, nc), ws_scr[...], preferred_element_type=F32)


def _s5_scan_kernel(h_ref, ar_ref, ai_ref, o_ref):
    half = ar_ref.shape[1]
    ar = ar_ref[...]
    ai = ai_ref[...]

    def body(c, carry):
        hr, hi = carry
        o_ref[0, c, :, :half] = hr
        o_ref[0, c, :, half:] = hi
        xr = h_ref[0, c, :, :half]
        xi = h_ref[0, c, :, half:]
        return ar * hr - ai * hi + xr, ar * hi + ai * hr + xi

    zero = jnp.zeros(ar.shape, F32)
    lax.fori_loop(0, h_ref.shape[1], body, (zero, zero))


def _s5_out_kernel(u_ref, kd_ref, h_ref, od_ref, y_ref, kl_scr, wi_scr, wc_scr):
    nc = u_ref.shape[0] // S5_CHUNK
    q, h, p, sg = S5_CHUNK, SSM_GROUP, SSM_STATE, SLAB_GROUPS

    @pl.when(pl.program_id(1) == 0)
    def _():
        kl_scr[...] = jnp.zeros_like(kl_scr)
        for k in range(q):
            for a in range(sg):
                kl_scr[k, a * h:(a + 1) * h, a * h:(a + 1) * h] = kd_ref[a, k]
        wi_scr[...] = jnp.zeros_like(wi_scr)
        for s in range(q):
            for t in range(s, q):
                wi_scr[s * LANES:(s + 1) * LANES, t * LANES:(t + 1) * LANES] = kl_scr[t - s]
        wc_scr[...] = jnp.zeros_like(wc_scr)
        for a in range(sg):
            for c in range(2):
                for t in range(q):
                    wc_scr[(c * sg + a) * p:(c * sg + a + 1) * p,
                           t * LANES + a * h: t * LANES + (a + 1) * h] = od_ref[a, c, :, t * h:(t + 1) * h]

    y = (jnp.dot(_chunk_rows(u_ref, nc), wi_scr[...], preferred_element_type=F32)
         + jnp.dot(h_ref[0].astype(BF16), wc_scr[...], preferred_element_type=F32))
    for t in range(q):
        y_ref[pl.ds(t, nc, stride=q), :] = y[:, t * LANES:(t + 1) * LANES]


def _glu_kernel(y_ref, w_ref, b_ref, o_ref):
    z = jax.nn.gelu(y_ref[...])
    gate = jnp.dot(z.astype(BF16), w_ref[...], preferred_element_type=F32) + b_ref[...]
    o_ref[...] = z * jax.nn.sigmoid(gate)


def _s5_mixer(proj, bsz, L, ops, w_glu, b_glu):
    s_dense, k_dense, o_dense, a_re, a_im = _s5_slab_weights(*ops)
    q, h, p = S5_CHUNK, SSM_GROUP, SSM_STATE
    nc = L // q
    ns, sg = N_SLABS, SLAB_GROUPS
    sw = sg * 2 * p
    kq = q * LANES
    hend = pl.pallas_call(
        _s5_state_kernel,
        grid=(ns, bsz),
        in_specs=[pl.BlockSpec((L, LANES), lambda j, b: (b, j)),
                  pl.BlockSpec((sg, q, h, 2 * p), lambda j, b: (j, 0, 0, 0))],
        out_specs=pl.BlockSpec((1, nc, sw), lambda j, b: (b, 0, j)),
        out_shape=jax.ShapeDtypeStruct((bsz, nc, ns * sw), F32),
        scratch_shapes=[pltpu.VMEM((kq, sw), BF16)],
        compiler_params=_cparams("parallel", "arbitrary"),
        name="s5_chunk_state",
    )(proj, s_dense)
    hprev4 = pl.pallas_call(
        _s5_scan_kernel,
        grid=(bsz,),
        in_specs=[pl.BlockSpec((1, nc, ns, sw), lambda b: (b, 0, 0, 0)),
                  pl.BlockSpec((ns, sw // 2), lambda b: (0, 0)),
                  pl.BlockSpec((ns, sw // 2), lambda b: (0, 0))],
        out_specs=pl.BlockSpec((1, nc, ns, sw), lambda b: (b, 0, 0, 0)),
        out_shape=jax.ShapeDtypeStruct((bsz, nc, ns, sw), F32),
        compiler_params=_cparams("parallel"),
        name="s5_chunk_scan",
    )(hend.reshape(bsz, nc, ns, sw), a_re, a_im)
    y = pl.pallas_call(
        _s5_out_kernel,
        grid=(ns, bsz),
        in_specs=[pl.BlockSpec((L, LANES), lambda j, b: (b, j)),
                  pl.BlockSpec((sg, q, h, h), lambda j, b: (j, 0, 0, 0)),
                  pl.BlockSpec((1, nc, sw), lambda j, b: (b, 0, j)),
                  pl.BlockSpec((sg, 2, p, q * h), lambda j, b: (j, 0, 0, 0))],
        out_specs=pl.BlockSpec((L, LANES), lambda j, b: (b, j)),
        out_shape=jax.ShapeDtypeStruct((bsz * L, SSM_WIDTH), F32),
        scratch_shapes=[pltpu.VMEM((q, LANES, LANES), BF16), pltpu.VMEM((kq, kq), BF16), pltpu.VMEM((sw, kq), BF16)],
        compiler_params=_cparams("parallel", "arbitrary"),
        name="s5_chunk_out",
    )(proj, k_dense, hprev4.reshape(bsz, nc, ns * sw), o_dense)
    tm = 512
    return pl.pallas_call(
        _glu_kernel,
        grid=(bsz * L // tm,),
        in_specs=[pl.BlockSpec((tm, SSM_WIDTH), lambda i: (i, 0)),
                  pl.BlockSpec((SSM_WIDTH, SSM_WIDTH), lambda i: (0, 0)),
                  pl.BlockSpec((1, SSM_WIDTH), lambda i: (0, 0))],
        out_specs=pl.BlockSpec((tm, SSM_WIDTH), lambda i: (i, 0)),
        out_shape=jax.ShapeDtypeStruct((bsz * L, SSM_WIDTH), F32),
        compiler_params=_cparams("parallel"),
        name="s5_gelu_glu",
    )(y, w_glu.astype(BF16), b_glu.reshape(1, SSM_WIDTH))


def _compress_kernel(k_ref, w1_ref, pe_ref, w1full_ref, w2_ref, o_ref, *, transpose_out):
    nc = k_ref.shape[0] // CMP_STRIDE
    prod = jnp.dot(_chunk_rows(k_ref, nc), w1_ref[...], preferred_element_type=F32)
    pe_h = jnp.dot(jnp.broadcast_to(pe_ref[...], (8, pe_ref.shape[1])), w1full_ref[...],
                   precision=HI, preferred_element_type=F32)[0:1]
    two_h = 2 * CMP_HIDDEN
    for gl in range(LANES // HEAD_DIM):
        first = prod[:, gl * two_h: gl * two_h + CMP_HIDDEN]
        second = prod[:, gl * two_h + CMP_HIDDEN: (gl + 1) * two_h]
        hid = first + pltpu.roll(second, nc - 1, axis=0) + pe_h
        out = jnp.dot(jax.nn.gelu(hid).astype(BF16), w2_ref[...], preferred_element_type=F32)
        if transpose_out:
            o_ref[0, gl] = _lane_cat([out, jnp.zeros_like(out)]).T[:HEAD_DIM].astype(BF16)
        else:
            o_ref[0, gl] = out.astype(BF16)


def _compress(proj, col, bsz, L, pe, w1, w2, transpose_out):
    nc = L // CMP_STRIDE
    half = CMP_STRIDE * HEAD_DIM
    gpl = LANES // HEAD_DIM
    w1cat = jnp.concatenate([w1[:half], w1[half:]], axis=1)
    eye = jnp.eye(gpl, dtype=F32)
    w1slab = jnp.einsum('ldh,ag->ladgh', w1cat.reshape(CMP_STRIDE, HEAD_DIM, 2 * CMP_HIDDEN), eye)
    w1slab = w1slab.reshape(CMP_STRIDE * LANES, gpl * 2 * CMP_HIDDEN).astype(BF16)
    out_block = (1, gpl, HEAD_DIM, nc) if transpose_out else (1, gpl, nc, HEAD_DIM)
    out_full = (bsz, KV_HEADS) + out_block[2:]
    return pl.pallas_call(
        functools.partial(_compress_kernel, transpose_out=transpose_out),
        grid=(bsz, KV_HEADS // gpl),
        in_specs=[pl.BlockSpec((L, LANES), lambda b, j: (b, col // LANES + j)),
                  pl.BlockSpec(w1slab.shape, lambda b, j: (0, 0)),
                  pl.BlockSpec((1, 2 * half), lambda b, j: (0, 0)),
                  pl.BlockSpec((2 * half, CMP_HIDDEN), lambda b, j: (0, 0)),
                  pl.BlockSpec((CMP_HIDDEN, HEAD_DIM), lambda b, j: (0, 0))],
        out_specs=pl.BlockSpec(out_block, lambda b, j: (b, j, 0, 0)),
        out_shape=jax.ShapeDtypeStruct(out_full, BF16),
        compiler_params=_cparams("parallel", "parallel"),
        name="nsa_compress",
    )(proj, w1slab, pe.reshape(1, 2 * half), w1, w2.astype(BF16))


KS_COLS = 2 * LANES
KW_COLS = LANES
KS_FLAG = HEAD_DIM + N_SLC_PAD
KW_FLAG = HEAD_DIM
PAD_TILES = 4


def _kv_tiles_kernel(ks_ref, vs_ref, kw_ref, vw_ref, kso_ref, vso_ref, kwo_ref, vwo_ref):
    tm = ks_ref.shape[0]
    step = pl.program_id(1)
    row0 = step * tm
    kso_ref[...] = jnp.zeros_like(kso_ref)
    kwo_ref[...] = jnp.zeros_like(kwo_ref)

    @pl.when(step < pl.num_programs(1) - 1)
    def _():
        lane_blk = lax.broadcasted_iota(jnp.int32, (QT, N_SLC_PAD), 1)
        for k in range(tm // QT):
            rows = slice(k * QT, (k + 1) * QT)
            tok = lax.broadcasted_iota(jnp.int32, (QT, N_SLC_PAD), 0) + (row0 + k * QT)
            onehot = jnp.where((tok >> int(math.log2(SLC_BLOCK))) == lane_blk, 1.0, 0.0).astype(BF16)
            vs_t = vs_ref[rows, :].T
            vw_t = vw_ref[rows, :].T
            for g in range(KV_HEADS):
                cols = slice(g * HEAD_DIM, (g + 1) * HEAD_DIM)
                kso_ref[0, g, k, :, :HEAD_DIM] = ks_ref[rows, cols].astype(BF16)
                kso_ref[0, g, k, :, HEAD_DIM:KS_FLAG] = onehot
                kwo_ref[0, g, k, :, :HEAD_DIM] = kw_ref[rows, cols].astype(BF16)
                vso_ref[0, g, k] = vs_t[cols].astype(BF16)
                vwo_ref[0, g, k] = vw_t[cols].astype(BF16)

    @pl.when(step == pl.num_programs(1) - 1)
    def _():
        ones = jnp.ones(kso_ref.shape[:4] + (1,), BF16)
        kso_ref[:, :, :, :, KS_FLAG:KS_FLAG + 1] = ones
        kwo_ref[:, :, :, :, KW_FLAG:KW_FLAG + 1] = ones
        vso_ref[...] = jnp.zeros_like(vso_ref)
        vwo_ref[...] = jnp.zeros_like(vwo_ref)


def _kv_tiles(proj, bsz, L):
    nt = L // QT
    g = KV_HEADS
    kt = PAD_TILES
    tm = kt * QT
    steps = L // tm
    in_spec = lambda col: pl.BlockSpec((tm, KV_WIDTH),
                                       lambda b, i: (b * steps + jnp.minimum(i, steps - 1), col // KV_WIDTH))
    out_spec = lambda a, c: pl.BlockSpec((1, g, kt, a, c), lambda b, i: (b, 0, i, 0, 0))
    shape = lambda a, c: jax.ShapeDtypeStruct((bsz, g, nt + kt, a, c), BF16)
    return pl.pallas_call(
        _kv_tiles_kernel,
        grid=(bsz, steps + 1),
        in_specs=[in_spec(COL_KS), in_spec(COL_VS), in_spec(COL_KW), in_spec(COL_VW)],
        out_specs=[out_spec(QT, KS_COLS), out_spec(HEAD_DIM, QT), out_spec(QT, KW_COLS), out_spec(HEAD_DIM, QT)],
        out_shape=[shape(QT, KS_COLS), shape(HEAD_DIM, QT), shape(QT, KW_COLS), shape(HEAD_DIM, QT)],
        compiler_params=_cparams("parallel", "arbitrary"),
        name="nsa_kv_tiles",
    )(proj, proj, proj, proj)


def _rel_bucket(dist):
    dist = jnp.maximum(dist, 0)
    max_exact = REL_BUCKETS // 2
    large = max_exact + (jnp.log(jnp.maximum(dist, 1).astype(F32) / max_exact)
                         / math.log(REL_MAX_DIST / max_exact) * (REL_BUCKETS - max_exact)).astype(jnp.int32)
    large = jnp.minimum(large, REL_BUCKETS - 1)
    return jnp.where(dist < max_exact, dist, large)


def _toeplitz_rows(base, nrows, step):
    w = base.shape[1]
    x = jnp.broadcast_to(base, (nrows, w))
    k_ix = lax.broadcasted_iota(jnp.int32, (nrows, w), 0)
    bit = 1
    while bit < nrows:
        x = jnp.where((k_ix & bit) != 0, pltpu.roll(x, (bit * step) % w, axis=1), x)
        bit *= 2
    return x


def _bias_tables_kernel(bw_ref, bc_ref, far_ref, wb_ref, nb_ref, strip_ref, *, ncp):
    slab = WINDOW + QT
    y = _toeplitz_rows(bw_ref[0], QT, 1)
    for kh in range(slab // QT):
        tile = y[:, WINDOW - kh * QT: WINDOW - kh * QT + QT]
        wb_ref[0, kh * QT:(kh + 1) * QT, :] = tile
        if kh >= slab // QT - 2:
            row = kh - (slab // QT - 2)
            nb_ref[0, row * QT:(row + 1) * QT, :] = tile - far_ref[0]
    band = 2 * CMP_STRIDE
    z = _toeplitz_rows(bc_ref[0], band, CMP_STRIDE)[:, 2 * QT: 3 * QT]
    strip_ref[0, :ncp - CMP_STRIDE, :] = jnp.broadcast_to(far_ref[0], (ncp - CMP_STRIDE, QT))
    strip_ref[0, ncp - CMP_STRIDE: ncp + CMP_STRIDE, :] = z
    strip_ref[0, ncp + CMP_STRIDE:, :] = jnp.full((ncp - CMP_STRIDE, QT), NEG, F32)


def _nsa_tables(rel_table, L):
    ncp = L // CMP_STRIDE
    slab = WINDOW + QT
    dist = jnp.arange(TABLE_W)
    rel_table = rel_table * LOG2E
    per_dist = rel_table[_rel_bucket(dist)].T
    base_w = jnp.where(dist < WINDOW, per_dist, NEG)
    shift = CMP_BLOCK - 1
    shifted = jnp.concatenate([jnp.full((NSA_HEADS, shift), NEG, F32), per_dist[:, :TABLE_W - shift]], axis=1)
    base_c = jnp.where(dist < TABLE_W // 2, shifted, NEG)
    far = jnp.broadcast_to(rel_table[REL_BUCKETS - 1][:, None, None], (NSA_HEADS, 1, QT))
    wb, nb, strip = pl.pallas_call(
        functools.partial(_bias_tables_kernel, ncp=ncp),
        grid=(NSA_HEADS,),
        in_specs=[pl.BlockSpec((1, 1, TABLE_W), lambda h: (h, 0, 0)),
                  pl.BlockSpec((1, 1, TABLE_W), lambda h: (h, 0, 0)),
                  pl.BlockSpec((1, 1, QT), lambda h: (h, 0, 0))],
        out_specs=[pl.BlockSpec((1, slab, QT), lambda h: (h, 0, 0)),
                   pl.BlockSpec((1, 2 * QT, QT), lambda h: (h, 0, 0)),
                   pl.BlockSpec((1, 2 * ncp, QT), lambda h: (h, 0, 0))],
        out_shape=[jax.ShapeDtypeStruct((NSA_HEADS, slab, QT), F32),
                   jax.ShapeDtypeStruct((NSA_HEADS, 2 * QT, QT), F32),
                   jax.ShapeDtypeStruct((NSA_HEADS, 2 * ncp, QT), F32)],
        compiler_params=_cparams("parallel"),
        name="nsa_bias_tables",
    )(base_w.reshape(NSA_HEADS, 1, TABLE_W), base_c.reshape(NSA_HEADS, 1, TABLE_W), far)
    g, r = KV_HEADS, Q_PER_KV
    return wb.reshape(g, r, slab, QT), nb.reshape(g, r, 2 * QT, QT), strip.reshape(g, r, 2 * ncp, QT)


def _overlap_t(L):
    n_cmp = (L - CMP_BLOCK) // CMP_STRIDE + 1
    n_slc = L // SLC_BLOCK
    cmp_idx = np.arange(n_cmp)[:, None] * CMP_STRIDE + np.arange(CMP_BLOCK)[None, :]
    overlap = ((cmp_idx[:, :, None] // SLC_BLOCK) == np.arange(n_slc)[None, None, :]).sum(1) / CMP_BLOCK
    out = np.zeros((N_SLC_PAD, L // CMP_STRIDE), np.float32)
    out[:n_slc, :n_cmp] = overlap.T
    return jnp.asarray(out, dtype=BF16)


def _split3(x):
    hi = x.astype(BF16)
    r1 = x - hi.astype(F32)
    mid = r1.astype(BF16)
    lo = (r1 - mid.astype(F32)).astype(BF16)
    return hi, mid, lo


def _nsa_kernel(q_ref, gl_ref, kc_ref, vcT_ref, strip_ref, ovT_ref, ks_ref, vsT_ref, kw_ref, vwT_ref, wb_ref,
                nb_ref, o_ref, gate_ref, sa_scr, sb_scr, *, n_slc):
    g = pl.program_id(1)
    i = pl.program_id(2)
    R = Q_PER_KV
    RQ = R * QT
    pad_tile = ks_ref.shape[2] - PAD_TILES
    qT = (q_ref[...].T * (0.125 * LOG2E)).astype(BF16)
    q_heads = [qT[r * HEAD_DIM:(r + 1) * HEAD_DIM] for r in range(R)]
    qcat = _lane_cat(q_heads)
    t_pos = lax.broadcasted_iota(jnp.int32, (1, QT), 1) + i * QT

    def flag_rows(n):
        return jnp.where(lax.broadcasted_iota(jnp.int32, (n, QT), 0) == 0, NEG, 0.0).astype(BF16)

    slab = WINDOW + QT
    n_win = slab // QT
    win_idx = [jnp.where(i + jj >= n_win - 1, i + jj - (n_win - 1), pad_tile) for jj in range(n_win)]
    kw = jnp.concatenate([kw_ref[0, 0, j] for j in win_idx], axis=0)
    qwin = _lane_cat([jnp.concatenate([q_heads[r], flag_rows(KW_COLS - HEAD_DIM)], axis=0) for r in range(R)])
    wb = _lane_cat([wb_ref[0, r] for r in range(R)])
    sw = jnp.dot(kw, qwin, preferred_element_type=F32) + wb
    m_w = jnp.max(sw, axis=0, keepdims=True)
    e_w = jnp.exp2(sw - m_w)
    l_w = jnp.sum(e_w, axis=0, keepdims=True)
    e_wb = e_w.astype(BF16)
    o_win = None
    for jj, j in enumerate(win_idx):
        term = jnp.dot(vwT_ref[0, 0, j], e_wb[jj * QT:(jj + 1) * QT], preferred_element_type=F32)
        o_win = term if o_win is None else o_win + term
    o_win = o_win * (1.0 / l_w)

    ncp = kc_ref.shape[2]
    sc = jnp.dot(kc_ref[0, 0], qcat, preferred_element_type=F32)
    strip_row = pl.multiple_of(ncp - i * (QT // CMP_STRIDE), QT // CMP_STRIDE)
    sm = sc + _lane_cat([strip_ref[0, r, pl.ds(strip_row, ncp), :] for r in range(R)])
    m_c = jnp.max(sm, axis=0, keepdims=True)
    e_c = jnp.exp2(sm - m_c)
    l_c = jnp.sum(e_c, axis=0, keepdims=True)
    inv_c = jnp.where(m_c > 0.5 * NEG, 1.0 / l_c, 0.0)
    o_cmp = jnp.dot(vcT_ref[0, 0], e_c.astype(BF16), preferred_element_type=F32) * inv_c
    p_c = e_c * inv_c
    p_sum = p_c[:, 0:QT]
    for r in range(1, R):
        p_sum = p_sum + p_c[:, r * QT:(r + 1) * QT]

    ov = ovT_ref[...]
    imp = None
    for part in _split3(p_sum):
        term = jnp.dot(ov, part, preferred_element_type=F32)
        imp = term if imp is None else imp + term
    s_ix = lax.broadcasted_iota(jnp.int32, (N_SLC_PAD, QT), 0)
    cur = t_pos >> int(math.log2(SLC_BLOCK))
    forced = (s_ix == 0) | (s_ix == cur) | (s_ix == cur - 1)
    score = jnp.where(forced, FORCED_SCORE, jnp.where(s_ix <= cur, imp, -1.0))
    score = jnp.where(s_ix < n_slc, score, -2.0)
    SUB = 8
    score_rows = [score[v * SUB:(v + 1) * SUB] for v in range(N_SLC_PAD // SUB)]
    sub_ix = lax.broadcasted_iota(jnp.int32, (SUB, QT), 0)

    def rank_segment(seg, ranks):
        ranks = list(ranks)
        for sp in range(seg * RANK_SEG, min((seg + 1) * RANK_SEG, n_slc)):
            row = score[sp:sp + 1, :]
            for v, blk in enumerate(score_rows):
                if v * SUB > sp:
                    beats = row >= blk
                elif v * SUB + SUB - 1 <= sp:
                    beats = row > blk
                else:
                    beats = (row > blk) | ((row == blk) & (sub_ix > sp - v * SUB))
                ranks[v] = ranks[v] + jnp.where(beats, 1.0, 0.0)
        return tuple(ranks)

    ranks = rank_segment(0, tuple(jnp.zeros((SUB, QT), F32) for _ in score_rows))
    last_block = (i + 1) * (QT // SLC_BLOCK) - 1
    for seg in range(1, -(-n_slc // RANK_SEG)):
        ranks = lax.cond(last_block >= seg * RANK_SEG, functools.partial(rank_segment, seg), lambda rk: rk, ranks)
    rank = jnp.concatenate(ranks, axis=0)
    sel_neg = jnp.where(rank < float(SLC_TOPK), 0.0, UNSELECTED).astype(BF16)

    qsel = _lane_cat([jnp.concatenate([q_heads[r], sel_neg, flag_rows(KS_COLS - KS_FLAG)], axis=0)
                      for r in range(R)])

    n_far = jnp.maximum(i - 1, 0)

    def pair_tiles(j):
        return [jnp.where(j + n < n_far, j + n, pad_tile) for n in range(2)]

    def pair_scores(tiles):
        k2 = jnp.concatenate([ks_ref[0, 0, t] for t in tiles], axis=0)
        return jnp.dot(k2, qsel, preferred_element_type=F32)

    def online_step(carry, s, tiles):
        m, l, acc = carry
        m_new = jnp.maximum(m, jnp.max(s, axis=0, keepdims=True))
        alpha = jnp.exp2(m - m_new)
        p = jnp.exp2(s - m_new)
        l = alpha * l + jnp.sum(p, axis=0, keepdims=True)
        p = p.astype(BF16)
        acc = alpha * acc
        for n, t in enumerate(tiles):
            acc = acc + jnp.dot(vsT_ref[0, 0, t], p[n * QT:(n + 1) * QT], preferred_element_type=F32)
        return m_new, l, acc

    def far_trip(u, carry):
        j = 4 * u
        sb_scr[...] = pair_scores(pair_tiles(j + 2))
        carry = online_step(carry, sa_scr[...], pair_tiles(j))
        sa_scr[...] = pair_scores(pair_tiles(j + 4))
        return online_step(carry, sb_scr[...], pair_tiles(j + 2))

    sa_scr[...] = pair_scores(pair_tiles(0))
    init = (jnp.full((1, RQ), NEG, F32), jnp.zeros((1, RQ), F32), jnp.zeros((HEAD_DIM, RQ), F32))
    carry = lax.fori_loop(0, (n_far + 3) // 4, far_trip, init)
    near = [jnp.where(i >= 1, i - 1, pad_tile), i]
    s_near = pair_scores(near) + _lane_cat([nb_ref[0, r] for r in range(R)])
    _, l_s, acc_s = online_step(carry, s_near, near)
    o_slc = acc_s * (1.0 / l_s)

    gate_ref[...] = jax.nn.sigmoid(gl_ref[...].T)
    gates = [_lane_cat([gate_ref[pl.ds((g * R + r) * 3 + br, 1), :] for r in range(R)]) for br in range(3)]
    total = gates[0] * o_cmp + gates[1] * o_slc + gates[2] * o_win
    o_ref[...] = jnp.concatenate([total[:, r * QT:(r + 1) * QT] for r in range(R)], axis=0).T


def _nsa_mixer(proj, bsz, L, pe_ck, w_ck1, w_ck2, pe_cv, w_cv1, w_cv2, rel_table):
    g, r = KV_HEADS, Q_PER_KV
    ni = L // QT
    nt = L // QT + PAD_TILES
    ncp = L // CMP_STRIDE
    n_slc = L // SLC_BLOCK
    slab = WINDOW + QT
    k_cmp = _compress(proj, COL_KC, bsz, L, pe_ck, w_ck1, w_ck2, transpose_out=False)
    v_cmp_t = _compress(proj, COL_VC, bsz, L, pe_cv, w_cv1, w_cv2, transpose_out=True)
    ks_t, vs_t, kw_t, vw_t = _kv_tiles(proj, bsz, L)
    wb, nb, strip = _nsa_tables(rel_table, L)
    rw = r * HEAD_DIM
    whole = lambda *shape: pl.BlockSpec((1, 1) + shape, lambda b, h, i: (b, h) + (0,) * len(shape))
    per_group = lambda *shape: pl.BlockSpec((1,) + shape, lambda b, h, i: (h,) + (0,) * len(shape))
    return pl.pallas_call(
        functools.partial(_nsa_kernel, n_slc=n_slc),
        grid=(bsz, g, ni),
        in_specs=[
            pl.BlockSpec((QT, rw), lambda b, h, i: (b * ni + i, COL_Q // rw + h)),
            pl.BlockSpec((QT, LANES), lambda b, h, i: (b * ni + i, COL_GATE // LANES)),
            whole(ncp, HEAD_DIM),
            whole(HEAD_DIM, ncp),
            per_group(r, 2 * ncp, QT),
            pl.BlockSpec((N_SLC_PAD, ncp), lambda b, h, i: (0, 0)),
            whole(nt, QT, KS_COLS),
            whole(nt, HEAD_DIM, QT),
            whole(nt, QT, KW_COLS),
            whole(nt, HEAD_DIM, QT),
            per_group(r, slab, QT),
            per_group(r, 2 * QT, QT),
        ],
        out_specs=pl.BlockSpec((QT, rw), lambda b, h, i: (b * ni + i, h)),
        out_shape=jax.ShapeDtypeStruct((bsz * L, NSA_WIDTH), F32),
        scratch_shapes=[pltpu.VMEM((LANES, QT), F32), pltpu.VMEM((2 * QT, r * QT), F32),
                        pltpu.VMEM((2 * QT, r * QT), F32)],
        compiler_params=_cparams("parallel", "parallel", "arbitrary"),
        name="nsa_attention",
    )(proj, proj, k_cmp, v_cmp_t, strip, _overlap_t(L), ks_t, vs_t, kw_t, vw_t, wb, nb)


def _out_proj_kernel(x_ref, ys_ref, yn_ref, w_ref, o_ref):
    half = ys_ref.shape[1]
    acc = jnp.dot(ys_ref[...].astype(BF16), w_ref[:half, :], preferred_element_type=F32)
    acc = acc + jnp.dot(yn_ref[...].astype(BF16), w_ref[half:, :], preferred_element_type=F32)
    o_ref[...] = x_ref[...] + acc


def _out_proj(x2, y_ssm, y_nsa, w, tm=1024, tn=1024):
    t, d = x2.shape
    return pl.pallas_call(
        _out_proj_kernel,
        grid=(t // tm, d // tn),
        in_specs=[pl.BlockSpec((tm, tn), lambda i, j: (i, j)),
                  pl.BlockSpec((tm, SSM_WIDTH), lambda i, j: (i, 0)),
                  pl.BlockSpec((tm, NSA_WIDTH), lambda i, j: (i, 0)),
                  pl.BlockSpec((SSM_WIDTH + NSA_WIDTH, tn), lambda i, j: (0, j))],
        out_specs=pl.BlockSpec((tm, tn), lambda i, j: (i, j)),
        out_shape=jax.ShapeDtypeStruct((t, d), F32),
        compiler_params=_cparams("parallel", "arbitrary"),
        name="out_proj",
    )(x2, y_ssm, y_nsa, w)


def _mlp_kernel(x_ref, n2_ref, wu_ref, wd_ref, nf_ref, o_ref, h_ref, acc_ref):
    f = pl.program_id(1)

    @pl.when(f == 0)
    def _():
        h_ref[...] = _rms(x_ref[...], n2_ref[...]).astype(BF16)
        acc_ref[...] = jnp.zeros_like(acc_ref)

    a = jnp.dot(h_ref[...], wu_ref[...], preferred_element_type=F32)
    a = jnp.square(jnp.maximum(a, 0.0))
    acc_ref[...] += jnp.dot(a.astype(BF16), wd_ref[...], preferred_element_type=F32)

    @pl.when(f == pl.num_programs(1) - 1)
    def _():
        o_ref[...] = _rms(x_ref[...] + acc_ref[...], nf_ref[...])


def _mlp(x2, n2, wu, wd, nf, tm=512, tf=1024):
    t, d = x2.shape
    ff = wu.shape[1]
    return pl.pallas_call(
        _mlp_kernel,
        grid=(t // tm, ff // tf),
        in_specs=[pl.BlockSpec((tm, d), lambda i, f: (i, 0)),
                  pl.BlockSpec((1, d), lambda i, f: (0, 0)),
                  pl.BlockSpec((d, tf), lambda i, f: (0, f)),
                  pl.BlockSpec((tf, d), lambda i, f: (f, 0)),
                  pl.BlockSpec((1, d), lambda i, f: (0, 0))],
        out_specs=pl.BlockSpec((tm, d), lambda i, f: (i, 0)),
        out_shape=jax.ShapeDtypeStruct((t, d), F32),
        scratch_shapes=[pltpu.VMEM((tm, d), BF16), pltpu.VMEM((tm, d), F32)],
        compiler_params=_cparams("parallel", "arbitrary"),
        name="mlp_final_norm",
    )(x2, n2, wu, wd, nf)


def kernel(x, norm1_w, w_in, ssm_a_re, ssm_a_im, ssm_log_dt, ssm_b_re, ssm_b_im, ssm_c_re, ssm_c_im, ssm_d,
           w_glu, b_glu, pe_ck, w_ck1, w_ck2, pe_cv, w_cv1, w_cv2, w_out, norm2_w, w_up, w_down, rel_table,
           norm_f_w):
    bsz, L, d = x.shape
    assert w_in.shape[0] == 1, "the closing rmsnorm is fused into the (single) layer's MLP kernel"
    x2 = x.reshape(bsz * L, d)
    w_in_p = jnp.pad(w_in[0].astype(BF16), ((0, 0), (0, D_IN_PAD - D_IN)))
    proj = _norm_matmul(x2, norm1_w[0].reshape(1, d), w_in_p)
    ops = _s5_operators(ssm_a_re[0], ssm_a_im[0], ssm_log_dt[0], ssm_b_re[0], ssm_b_im[0], ssm_c_re[0], ssm_c_im[0],
                        ssm_d[0])
    y_ssm = _s5_mixer(proj, bsz, L, ops, w_glu[0], b_glu[0])
    y_nsa = _nsa_mixer(proj, bsz, L, pe_ck[0], w_ck1[0], w_ck2[0], pe_cv[0], w_cv1[0], w_cv2[0], rel_table)
    x2 = _out_proj(x2, y_ssm, y_nsa, w_out[0].astype(BF16))
    x2 = _mlp(x2, norm2_w[0].reshape(1, d), w_up[0].astype(BF16), w_down[0].astype(BF16), norm_f_w.reshape(1, d))
    return x2.reshape(bsz, L, d)
```

```python
import functools
import math

import numpy as np
import jax
import jax.numpy as jnp
from jax import lax
from jax.experimental import pallas as pl
from jax.experimental.pallas import tpu as pltpu

F32 = jnp.float32
BF16 = jnp.bfloat16

D_MODEL = 2048
SSM_WIDTH = 1024
SSM_GROUP = 16
SSM_GROUPS = 64
SSM_STATE = 64
NSA_WIDTH = 1024
HEAD_DIM = 64
NSA_HEADS = 16
KV_HEADS = 4
Q_PER_KV = 4
KV_WIDTH = 256
CMP_BLOCK = 32
CMP_STRIDE = 16
CMP_HIDDEN = 256
SLC_BLOCK = 64
SLC_TOPK = 16
WINDOW = 512
REL_BUCKETS = 32
REL_MAX_DIST = 128
D_FF = 8192
EPS = 1e-6
NEG = -1e30
FORCED_SCORE = 1e4
D_IN = SSM_WIDTH + NSA_WIDTH + 6 * KV_WIDTH + 3 * NSA_HEADS
D_IN_PAD = 3840
COL_Q = SSM_WIDTH
COL_KC = COL_Q + NSA_WIDTH
COL_VC = COL_KC + KV_WIDTH
COL_KS = COL_VC + KV_WIDTH
COL_VS = COL_KS + KV_WIDTH
COL_KW = COL_VS + KV_WIDTH
COL_VW = COL_KW + KV_WIDTH
COL_GATE = COL_VW + KV_WIDTH

LANES = 128
S5_CHUNK = 16
SLAB_GROUPS = LANES // SSM_GROUP
N_SLABS = SSM_GROUPS // SLAB_GROUPS
QT = 128
N_SLC_PAD = 64
RANK_SEG = 16
NSA_GROUPS = 2
UNSELECTED = -1e9
LOG2E = math.log2(math.e)
TABLE_W = 1024
VMEM_LIMIT = 56 * 1024 * 1024
HI = lax.Precision.HIGHEST


def _cparams(*sem):
    return pltpu.CompilerParams(dimension_semantics=sem, vmem_limit_bytes=VMEM_LIMIT)


def _rms(x, w):
    ms = jnp.mean(x * x, axis=-1, keepdims=True)
    return x * lax.rsqrt(ms + EPS) * w


def _lane_cat(parts):
    return jnp.concatenate(parts, axis=1)


def _norm_matmul_kernel(x_ref, nw_ref, w_ref, o_ref, h_ref):
    @pl.when(pl.program_id(1) == 0)
    def _():
        h_ref[...] = _rms(x_ref[...], nw_ref[...]).astype(BF16)

    o_ref[...] = jnp.dot(h_ref[...], w_ref[...], preferred_element_type=F32)


def _norm_matmul(x2, nw, w, tm=1024, tn=768):
    t, d = x2.shape
    n = w.shape[1]
    return pl.pallas_call(
        _norm_matmul_kernel,
        grid=(t // tm, n // tn),
        in_specs=[pl.BlockSpec((tm, d), lambda i, j: (i, 0)),
                  pl.BlockSpec((1, d), lambda i, j: (0, 0)),
                  pl.BlockSpec((d, tn), lambda i, j: (0, j))],
        out_specs=pl.BlockSpec((tm, tn), lambda i, j: (i, j)),
        out_shape=jax.ShapeDtypeStruct((t, n), F32),
        scratch_shapes=[pltpu.VMEM((tm, d), BF16)],
        compiler_params=_cparams("parallel", "arbitrary"),
        name="norm_in_proj",
    )(x2, nw, w)


def _s5_operators(a_re, a_im, log_dt, b_re, b_im, c_re, c_im, d):
    q = S5_CHUNK
    g, p = a_re.shape
    h = SSM_GROUP
    dt = jnp.exp(log_dt)[:, None]
    lam_re, lam_im = dt * a_re, dt * a_im
    mag1 = jnp.exp(lam_re)
    abar_re, abar_im = mag1 * jnp.cos(lam_im), mag1 * jnp.sin(lam_im)
    den = a_re * a_re + a_im * a_im
    zr, zi = abar_re - 1.0, abar_im
    coef_re = (zr * a_re + zi * a_im) / den
    coef_im = (zi * a_re - zr * a_im) / den
    bb_re = coef_re[..., None] * b_re - coef_im[..., None] * b_im
    bb_im = coef_re[..., None] * b_im + coef_im[..., None] * b_re
    k = jnp.arange(q + 1, dtype=F32)[:, None, None]
    mag = jnp.exp(k * lam_re)
    pw_re, pw_im = mag * jnp.cos(k * lam_im), mag * jnp.sin(k * lam_im)
    m_re = pw_re[..., None] * bb_re - pw_im[..., None] * bb_im
    m_im = pw_re[..., None] * bb_im + pw_im[..., None] * bb_re
    kern = (jnp.einsum('ghp,kgpi->gkhi', c_re, m_re[:q], precision=HI)
            - jnp.einsum('ghp,kgpi->gkhi', c_im, m_im[:q], precision=HI))
    kern = kern.at[:, 0].add(jax.vmap(jnp.diag)(d))
    rev = np.arange(q - 1, -1, -1)
    s_end = jnp.stack([m_re[rev], m_im[rev]], axis=0).transpose(2, 1, 4, 0, 3)
    pr, pi = pw_re[1:], pw_im[1:]
    o_re = c_re[None] * pr[:, :, None, :] - c_im[None] * pi[:, :, None, :]
    o_im = -(c_re[None] * pi[:, :, None, :] + c_im[None] * pr[:, :, None, :])
    o_carry = jnp.stack([o_re, o_im], axis=0).transpose(2, 0, 4, 1, 3)
    a_q = jnp.stack([pw_re[q], pw_im[q]], axis=0)
    return kern, s_end, o_carry, a_q


def _s5_slab_weights(kern, s_end, o_carry, a_q):
    q, h, p = S5_CHUNK, SSM_GROUP, SSM_STATE
    s_dense = s_end.reshape(SSM_GROUPS, q, h, 2 * p).astype(BF16)
    k_dense = kern.transpose(0, 1, 3, 2).astype(BF16)
    o_dense = o_carry.reshape(SSM_GROUPS, 2, p, q * h).astype(BF16)
    a_re = a_q[0].reshape(N_SLABS, SLAB_GROUPS * p)
    a_im = a_q[1].reshape(N_SLABS, SLAB_GROUPS * p)
    return s_dense, k_dense, o_dense, a_re, a_im


def _chunk_rows(u_ref, nc):
    return _lane_cat([u_ref[pl.ds(s, nc, stride=S5_CHUNK), :] for s in range(S5_CHUNK)]).astype(BF16)


def _s5_state_kernel(u_ref, sd_ref, o_ref, ws_scr):
    nc = u_ref.shape[0] // S5_CHUNK
    q, h, p, sg = S5_CHUNK, SSM_GROUP, SSM_STATE, SLAB_GROUPS

    @pl.when(pl.program_id(1) == 0)
    def _():
        ws_scr[...] = jnp.zeros_like(ws_scr)
        for s in range(q):
            for a in range(sg):
                for c in range(2):
                    ws_scr[s * LANES + a * h: s * LANES + (a + 1) * h,
                           (c * sg + a) * p: (c * sg + a + 1) * p] = sd_ref[a, s, :, c * p:(c + 1) * p]

    o_ref[0] = jnp.dot(_chunk_rows(u_ref, nc), ws_scr[...], preferred_element_type=F32)


def _s5_scan_kernel(h_ref, ar_ref, ai_ref, o_ref):
    half = ar_ref.shape[1]
    ar = ar_ref[...]
    ai = ai_ref[...]

    def body(c, carry):
        hr, hi = carry
        o_ref[0, c, :, :half] = hr
        o_ref[0, c, :, half:] = hi
        xr = h_ref[0, c, :, :half]
        xi = h_ref[0, c, :, half:]
        return ar * hr - ai * hi + xr, ar * hi + ai * hr + xi

    zero = jnp.zeros(ar.shape, F32)
    lax.fori_loop(0, h_ref.shape[1], body, (zero, zero))


def _s5_out_kernel(u_ref, kd_ref, h_ref, od_ref, y_ref, kl_scr, wi_scr, wc_scr):
    nc = u_ref.shape[0] // S5_CHUNK
    q, h, p, sg = S5_CHUNK, SSM_GROUP, SSM_STATE, SLAB_GROUPS

    @pl.when(pl.program_id(1) == 0)
    def _():
        kl_scr[...] = jnp.zeros_like(kl_scr)
        for k in range(q):
            for a in range(sg):
                kl_scr[k, a * h:(a + 1) * h, a * h:(a + 1) * h] = kd_ref[a, k]
        wi_scr[...] = jnp.zeros_like(wi_scr)
        for s in range(q):
            for t in range(s, q):
                wi_scr[s * LANES:(s + 1) * LANES, t * LANES:(t + 1) * LANES] = kl_scr[t - s]
        wc_scr[...] = jnp.zeros_like(wc_scr)
        for a in range(sg):
            for c in range(2):
                for t in range(q):
                    wc_scr[(c * sg + a) * p:(c * sg + a + 1) * p,
                           t * LANES + a * h: t * LANES + (a + 1) * h] = od_ref[a, c, :, t * h:(t + 1) * h]

    y = (jnp.dot(_chunk_rows(u_ref, nc), wi_scr[...], preferred_element_type=F32)
         + jnp.dot(h_ref[0].astype(BF16), wc_scr[...], preferred_element_type=F32))
    for t in range(q):
        y_ref[pl.ds(t, nc, stride=q), :] = y[:, t * LANES:(t + 1) * LANES]


def _glu_kernel(y_ref, w_ref, b_ref, o_ref):
    z = jax.nn.gelu(y_ref[...])
    gate = jnp.dot(z.astype(BF16), w_ref[...], preferred_element_type=F32) + b_ref[...]
    o_ref[...] = z * jax.nn.sigmoid(gate)


def _s5_mixer(proj, bsz, L, ops, w_glu, b_glu):
    s_dense, k_dense, o_dense, a_re, a_im = _s5_slab_weights(*ops)
    q, h, p = S5_CHUNK, SSM_GROUP, SSM_STATE
    nc = L // q
    ns, sg = N_SLABS, SLAB_GROUPS
    sw = sg * 2 * p
    kq = q * LANES
    hend = pl.pallas_call(
        _s5_state_kernel,
        grid=(ns, bsz),
        in_specs=[pl.BlockSpec((L, LANES), lambda j, b: (b, j)),
                  pl.BlockSpec((sg, q, h, 2 * p), lambda j, b: (j, 0, 0, 0))],
        out_specs=pl.BlockSpec((1, nc, sw), lambda j, b: (b, 0, j)),
        out_shape=jax.ShapeDtypeStruct((bsz, nc, ns * sw), F32),
        scratch_shapes=[pltpu.VMEM((kq, sw), BF16)],
        compiler_params=_cparams("parallel", "arbitrary"),
        name="s5_chunk_state",
    )(proj, s_dense)
    hprev4 = pl.pallas_call(
        _s5_scan_kernel,
        grid=(bsz,),
        in_specs=[pl.BlockSpec((1, nc, ns, sw), lambda b: (b, 0, 0, 0)),
                  pl.BlockSpec((ns, sw // 2), lambda b: (0, 0)),
                  pl.BlockSpec((ns, sw // 2), lambda b: (0, 0))],
        out_specs=pl.BlockSpec((1, nc, ns, sw), lambda b: (b, 0, 0, 0)),
        out_shape=jax.ShapeDtypeStruct((bsz, nc, ns, sw), F32),
        compiler_params=_cparams("parallel"),
        name="s5_chunk_scan",
    )(hend.reshape(bsz, nc, ns, sw), a_re, a_im)
    y = pl.pallas_call(
        _s5_out_kernel,
        grid=(ns, bsz),
        in_specs=[pl.BlockSpec((L, LANES), lambda j, b: (b, j)),
                  pl.BlockSpec((sg, q, h, h), lambda j, b: (j, 0, 0, 0)),
                  pl.BlockSpec((1, nc, sw), lambda j, b: (b, 0, j)),
                  pl.BlockSpec((sg, 2, p, q * h), lambda j, b: (j, 0, 0, 0))],
        out_specs=pl.BlockSpec((L, LANES), lambda j, b: (b, j)),
        out_shape=jax.ShapeDtypeStruct((bsz * L, SSM_WIDTH), F32),
        scratch_shapes=[pltpu.VMEM((q, LANES, LANES), BF16), pltpu.VMEM((kq, kq), BF16), pltpu.VMEM((sw, kq), BF16)],
        compiler_params=_cparams("parallel", "arbitrary"),
        name="s5_chunk_out",
    )(proj, k_dense, hprev4.reshape(bsz, nc, ns * sw), o_dense)
    tm = 512
    return pl.pallas_call(
        _glu_kernel,
        grid=(bsz * L // tm,),
        in_specs=[pl.BlockSpec((tm, SSM_WIDTH), lambda i: (i, 0)),
                  pl.BlockSpec((SSM_WIDTH, SSM_WIDTH), lambda i: (0, 0)),
                  pl.BlockSpec((1, SSM_WIDTH), lambda i: (0, 0))],
        out_specs=pl.BlockSpec((tm, SSM_WIDTH), lambda i: (i, 0)),
        out_shape=jax.ShapeDtypeStruct((bsz * L, SSM_WIDTH), F32),
        compiler_params=_cparams("parallel"),
        name="s5_gelu_glu",
    )(y, w_glu.astype(BF16), b_glu.reshape(1, SSM_WIDTH))


def _compress_kernel(k_ref, w1_ref, pe_ref, w1full_ref, w2_ref, o_ref, *, transpose_out):
    nc = k_ref.shape[0] // CMP_STRIDE
    prod = jnp.dot(_chunk_rows(k_ref, nc), w1_ref[...], preferred_element_type=F32)
    pe_h = jnp.dot(jnp.broadcast_to(pe_ref[...], (8, pe_ref.shape[1])), w1full_ref[...],
                   precision=HI, preferred_element_type=F32)[0:1]
    two_h = 2 * CMP_HIDDEN
    for gl in range(LANES // HEAD_DIM):
        first = prod[:, gl * two_h: gl * two_h + CMP_HIDDEN]
        second = prod[:, gl * two_h + CMP_HIDDEN: (gl + 1) * two_h]
        hid = first + pltpu.roll(second, nc - 1, axis=0) + pe_h
        out = jnp.dot(jax.nn.gelu(hid).astype(BF16), w2_ref[...], preferred_element_type=F32)
        if transpose_out:
            o_ref[0, gl] = _lane_cat([out, jnp.zeros_like(out)]).T[:HEAD_DIM].astype(BF16)
        else:
            o_ref[0, gl] = out.astype(BF16)


def _compress(proj, col, bsz, L, pe, w1, w2, transpose_out):
    nc = L // CMP_STRIDE
    half = CMP_STRIDE * HEAD_DIM
    gpl = LANES // HEAD_DIM
    w1cat = jnp.concatenate([w1[:half], w1[half:]], axis=1)
    eye = jnp.eye(gpl, dtype=F32)
    w1slab = jnp.einsum('ldh,ag->ladgh', w1cat.reshape(CMP_STRIDE, HEAD_DIM, 2 * CMP_HIDDEN), eye)
    w1slab = w1slab.reshape(CMP_STRIDE * LANES, gpl * 2 * CMP_HIDDEN).astype(BF16)
    out_block = (1, gpl, HEAD_DIM, nc) if transpose_out else (1, gpl, nc, HEAD_DIM)
    out_full = (bsz, KV_HEADS) + out_block[2:]
    return pl.pallas_call(
        functools.partial(_compress_kernel, transpose_out=transpose_out),
        grid=(bsz, KV_HEADS // gpl),
        in_specs=[pl.BlockSpec((L, LANES), lambda b, j: (b, col // LANES + j)),
                  pl.BlockSpec(w1slab.shape, lambda b, j: (0, 0)),
                  pl.BlockSpec((1, 2 * half), lambda b, j: (0, 0)),
                  pl.BlockSpec((2 * half, CMP_HIDDEN), lambda b, j: (0, 0)),
                  pl.BlockSpec((CMP_HIDDEN, HEAD_DIM), lambda b, j: (0, 0))],
        out_specs=pl.BlockSpec(out_block, lambda b, j: (b, j, 0, 0)),
        out_shape=jax.ShapeDtypeStruct(out_full, BF16),
        compiler_params=_cparams("parallel", "parallel"),
        name="nsa_compress",
    )(proj, w1slab, pe.reshape(1, 2 * half), w1, w2.astype(BF16))


KS_COLS = 2 * LANES
KW_COLS = LANES
KS_FLAG = HEAD_DIM + N_SLC_PAD
KW_FLAG = HEAD_DIM
PAD_TILES = 4


def _kv_tiles_kernel(ks_ref, vs_ref, kw_ref, vw_ref, kso_ref, vso_ref, kwo_ref, vwo_ref):
    tm = ks_ref.shape[0]
    step = pl.program_id(1)
    row0 = step * tm
    kso_ref[...] = jnp.zeros_like(kso_ref)
    kwo_ref[...] = jnp.zeros_like(kwo_ref)

    @pl.when(step < pl.num_programs(1) - 1)
    def _():
        lane_blk = lax.broadcasted_iota(jnp.int32, (QT, N_SLC_PAD), 1)
        for k in range(tm // QT):
            rows = slice(k * QT, (k + 1) * QT)
            tok = lax.broadcasted_iota(jnp.int32, (QT, N_SLC_PAD), 0) + (row0 + k * QT)
            onehot = jnp.where((tok >> int(math.log2(SLC_BLOCK))) == lane_blk, 1.0, 0.0).astype(BF16)
            vs_t = vs_ref[rows, :].T
            vw_t = vw_ref[rows, :].T
            for g in range(KV_HEADS):
                cols = slice(g * HEAD_DIM, (g + 1) * HEAD_DIM)
                kso_ref[0, g, k, :, :HEAD_DIM] = ks_ref[rows, cols].astype(BF16)
                kso_ref[0, g, k, :, HEAD_DIM:KS_FLAG] = onehot
                kwo_ref[0, g, k, :, :HEAD_DIM] = kw_ref[rows, cols].astype(BF16)
                vso_ref[0, g, k] = vs_t[cols].astype(BF16)
                vwo_ref[0, g, k] = vw_t[cols].astype(BF16)

    @pl.when(step == pl.num_programs(1) - 1)
    def _():
        ones = jnp.ones(kso_ref.shape[:4] + (1,), BF16)
        kso_ref[:, :, :, :, KS_FLAG:KS_FLAG + 1] = ones
        kwo_ref[:, :, :, :, KW_FLAG:KW_FLAG + 1] = ones
        vso_ref[...] = jnp.zeros_like(vso_ref)
        vwo_ref[...] = jnp.zeros_like(vwo_ref)


def _kv_tiles(proj, bsz, L):
    nt = L // QT
    g = KV_HEADS
    kt = PAD_TILES
    tm = kt * QT
    steps = L // tm
    in_spec = lambda col: pl.BlockSpec((tm, KV_WIDTH),
                                       lambda b, i: (b * steps + jnp.minimum(i, steps - 1), col // KV_WIDTH))
    out_spec = lambda a, c: pl.BlockSpec((1, g, kt, a, c), lambda b, i: (b, 0, i, 0, 0))
    shape = lambda a, c: jax.ShapeDtypeStruct((bsz, g, nt + kt, a, c), BF16)
    return pl.pallas_call(
        _kv_tiles_kernel,
        grid=(bsz, steps + 1),
        in_specs=[in_spec(COL_KS), in_spec(COL_VS), in_spec(COL_KW), in_spec(COL_VW)],
        out_specs=[out_spec(QT, KS_COLS), out_spec(HEAD_DIM, QT), out_spec(QT, KW_COLS), out_spec(HEAD_DIM, QT)],
        out_shape=[shape(QT, KS_COLS), shape(HEAD_DIM, QT), shape(QT, KW_COLS), shape(HEAD_DIM, QT)],
        compiler_params=_cparams("parallel", "arbitrary"),
        name="nsa_kv_tiles",
    )(proj, proj, proj, proj)


def _rel_bucket(dist):
    dist = jnp.maximum(dist, 0)
    max_exact = REL_BUCKETS // 2
    large = max_exact + (jnp.log(jnp.maximum(dist, 1).astype(F32) / max_exact)
                         / math.log(REL_MAX_DIST / max_exact) * (REL_BUCKETS - max_exact)).astype(jnp.int32)
    large = jnp.minimum(large, REL_BUCKETS - 1)
    return jnp.where(dist < max_exact, dist, large)


def _toeplitz_rows(base, nrows, step):
    w = base.shape[1]
    x = jnp.broadcast_to(base, (nrows, w))
    k_ix = lax.broadcasted_iota(jnp.int32, (nrows, w), 0)
    bit = 1
    while bit < nrows:
        x = jnp.where((k_ix & bit) != 0, pltpu.roll(x, (bit * step) % w, axis=1), x)
        bit *= 2
    return x


def _bias_tables_kernel(bw_ref, bc_ref, far_ref, wb_ref, nb_ref, strip_ref, *, ncp):
    slab = WINDOW + QT
    y = _toeplitz_rows(bw_ref[0], QT, 1)
    for kh in range(slab // QT):
        tile = y[:, WINDOW - kh * QT: WINDOW - kh * QT + QT]
        wb_ref[0, kh * QT:(kh + 1) * QT, :] = tile
        if kh >= slab // QT - 2:
            row = kh - (slab // QT - 2)
            nb_ref[0, row * QT:(row + 1) * QT, :] = tile - far_ref[0]
    band = 2 * CMP_STRIDE
    z = _toeplitz_rows(bc_ref[0], band, CMP_STRIDE)[:, 2 * QT: 3 * QT]
    strip_ref[0, :ncp - CMP_STRIDE, :] = jnp.broadcast_to(far_ref[0], (ncp - CMP_STRIDE, QT))
    strip_ref[0, ncp - CMP_STRIDE: ncp + CMP_STRIDE, :] = z
    strip_ref[0, ncp + CMP_STRIDE:, :] = jnp.full((ncp - CMP_STRIDE, QT), NEG, F32)


def _nsa_tables(rel_table, L):
    ncp = L // CMP_STRIDE
    slab = WINDOW + QT
    dist = jnp.arange(TABLE_W)
    rel_table = rel_table * LOG2E
    per_dist = rel_table[_rel_bucket(dist)].T
    base_w = jnp.where(dist < WINDOW, per_dist, NEG)
    shift = CMP_BLOCK - 1
    shifted = jnp.concatenate([jnp.full((NSA_HEADS, shift), NEG, F32), per_dist[:, :TABLE_W - shift]], axis=1)
    base_c = jnp.where(dist < TABLE_W // 2, shifted, NEG)
    far = jnp.broadcast_to(rel_table[REL_BUCKETS - 1][:, None, None], (NSA_HEADS, 1, QT))
    wb, nb, strip = pl.pallas_call(
        functools.partial(_bias_tables_kernel, ncp=ncp),
        grid=(NSA_HEADS,),
        in_specs=[pl.BlockSpec((1, 1, TABLE_W), lambda h: (h, 0, 0)),
                  pl.BlockSpec((1, 1, TABLE_W), lambda h: (h, 0, 0)),
                  pl.BlockSpec((1, 1, QT), lambda h: (h, 0, 0))],
        out_specs=[pl.BlockSpec((1, slab, QT), lambda h: (h, 0, 0)),
                   pl.BlockSpec((1, 2 * QT, QT), lambda h: (h, 0, 0)),
                   pl.BlockSpec((1, 2 * ncp, QT), lambda h: (h, 0, 0))],
        out_shape=[jax.ShapeDtypeStruct((NSA_HEADS, slab, QT), F32),
                   jax.ShapeDtypeStruct((NSA_HEADS, 2 * QT, QT), F32),
                   jax.ShapeDtypeStruct((NSA_HEADS, 2 * ncp, QT), F32)],
        compiler_params=_cparams("parallel"),
        name="nsa_bias_tables",
    )(base_w.reshape(NSA_HEADS, 1, TABLE_W), base_c.reshape(NSA_HEADS, 1, TABLE_W), far)
    g, r = KV_HEADS, Q_PER_KV
    return wb.reshape(g, r, slab, QT), nb.reshape(g, r, 2 * QT, QT), strip.reshape(g, r, 2 * ncp, QT)


def _overlap_t(L):
    n_cmp = (L - CMP_BLOCK) // CMP_STRIDE + 1
    n_slc = L // SLC_BLOCK
    cmp_idx = np.arange(n_cmp)[:, None] * CMP_STRIDE + np.arange(CMP_BLOCK)[None, :]
    overlap = ((cmp_idx[:, :, None] // SLC_BLOCK) == np.arange(n_slc)[None, None, :]).sum(1) / CMP_BLOCK
    out = np.zeros((N_SLC_PAD, L // CMP_STRIDE), np.float32)
    out[:n_slc, :n_cmp] = overlap.T
    return jnp.asarray(out, dtype=BF16)


def _split3(x):
    hi = x.astype(BF16)
    r1 = x - hi.astype(F32)
    mid = r1.astype(BF16)
    lo = (r1 - mid.astype(F32)).astype(BF16)
    return hi, mid, lo


def _nsa_kernel(q_ref, gl_ref, kc_ref, vcT_ref, strip_ref, ovT_ref, ks_ref, vsT_ref, kw_ref, vwT_ref, wb_ref,
                nb_ref, o_ref, gate_ref, sa_scr, sb_scr, *, n_slc):
    i = pl.program_id(2)
    R = Q_PER_KV
    RQ = R * QT
    RW = R * HEAD_DIM
    groups = range(kc_ref.shape[1])
    first_group = pl.program_id(1) * len(groups)
    pad_tile = ks_ref.shape[2] - PAD_TILES
    t_pos = lax.broadcasted_iota(jnp.int32, (1, QT), 1) + i * QT

    def flag_rows(n):
        return jnp.where(lax.broadcasted_iota(jnp.int32, (n, QT), 0) == 0, NEG, 0.0).astype(BF16)

    q_heads = []
    for gg in groups:
        qT = (q_ref[:, gg * RW:(gg + 1) * RW].T * (0.125 * LOG2E)).astype(BF16)
        q_heads.append([qT[r * HEAD_DIM:(r + 1) * HEAD_DIM] for r in range(R)])

    slab = WINDOW + QT
    n_win = slab // QT
    win_idx = [jnp.where(i + jj >= n_win - 1, i + jj - (n_win - 1), pad_tile) for jj in range(n_win)]

    def window_branch(gg):
        kw = jnp.concatenate([kw_ref[0, gg, j] for j in win_idx], axis=0)
        qwin = _lane_cat([jnp.concatenate([q_heads[gg][r], flag_rows(KW_COLS - HEAD_DIM)], axis=0)
                          for r in range(R)])
        wb = _lane_cat([wb_ref[gg, r] for r in range(R)])
        sw = jnp.dot(kw, qwin, preferred_element_type=F32) + wb
        m_w = jnp.max(sw, axis=0, keepdims=True)
        e_w = jnp.exp2(sw - m_w)
        l_w = jnp.sum(e_w, axis=0, keepdims=True)
        e_wb = e_w.astype(BF16)
        o_win = None
        for jj, j in enumerate(win_idx):
            term = jnp.dot(vwT_ref[0, gg, j], e_wb[jj * QT:(jj + 1) * QT], preferred_element_type=F32)
            o_win = term if o_win is None else o_win + term
        return o_win * (1.0 / l_w)

    o_win = [window_branch(gg) for gg in groups]

    ncp = kc_ref.shape[2]
    strip_row = pl.multiple_of(ncp - i * (QT // CMP_STRIDE), QT // CMP_STRIDE)
    ov = ovT_ref[...]
    s_ix = lax.broadcasted_iota(jnp.int32, (N_SLC_PAD, QT), 0)
    cur = t_pos >> int(math.log2(SLC_BLOCK))
    forced = (s_ix == 0) | (s_ix == cur) | (s_ix == cur - 1)

    def compressed_branch(gg):
        sc = jnp.dot(kc_ref[0, gg], _lane_cat(q_heads[gg]), preferred_element_type=F32)
        sm = sc + _lane_cat([strip_ref[gg, r, pl.ds(strip_row, ncp), :] for r in range(R)])
        m_c = jnp.max(sm, axis=0, keepdims=True)
        e_c = jnp.exp2(sm - m_c)
        l_c = jnp.sum(e_c, axis=0, keepdims=True)
        inv_c = jnp.where(m_c > 0.5 * NEG, 1.0 / l_c, 0.0)
        o_cmp = jnp.dot(vcT_ref[0, gg], e_c.astype(BF16), preferred_element_type=F32) * inv_c
        p_c = e_c * inv_c
        p_sum = p_c[:, 0:QT]
        for r in range(1, R):
            p_sum = p_sum + p_c[:, r * QT:(r + 1) * QT]
        imp = None
        for part in _split3(p_sum):
            term = jnp.dot(ov, part, preferred_element_type=F32)
            imp = term if imp is None else imp + term
        score = jnp.where(forced, FORCED_SCORE, jnp.where(s_ix <= cur, imp, -1.0))
        return o_cmp, jnp.where(s_ix < n_slc, score, -2.0)

    cmp_out = [compressed_branch(gg) for gg in groups]
    o_cmp = [c[0] for c in cmp_out]
    scores = [c[1] for c in cmp_out]

    SUB = 8
    n_sub = N_SLC_PAD // SUB
    score_rows = [[score[v * SUB:(v + 1) * SUB] for v in range(n_sub)] for score in scores]
    sub_ix = lax.broadcasted_iota(jnp.int32, (SUB, QT), 0)

    def rank_segment(seg, ranks):
        out = []
        for gg in groups:
            rk = list(ranks[gg])
            for sp in range(seg * RANK_SEG, min((seg + 1) * RANK_SEG, n_slc)):
                row = scores[gg][sp:sp + 1, :]
                for v, blk in enumerate(score_rows[gg]):
                    if v * SUB > sp:
                        beats = row >= blk
                    elif v * SUB + SUB - 1 <= sp:
                        beats = row > blk
                    else:
                        beats = (row > blk) | ((row == blk) & (sub_ix > sp - v * SUB))
                    rk[v] = rk[v] + jnp.where(beats, 1.0, 0.0)
            out.append(tuple(rk))
        return tuple(out)

    ranks = rank_segment(0, tuple(tuple(jnp.zeros((SUB, QT), F32) for _ in range(n_sub)) for _ in groups))
    last_block = (i + 1) * (QT // SLC_BLOCK) - 1
    for seg in range(1, -(-n_slc // RANK_SEG)):
        ranks = lax.cond(last_block >= seg * RANK_SEG, functools.partial(rank_segment, seg), lambda rk: rk, ranks)

    qsel = []
    for gg in groups:
        sel_neg = jnp.where(jnp.concatenate(ranks[gg], axis=0) < float(SLC_TOPK), 0.0, UNSELECTED).astype(BF16)
        qsel.append(_lane_cat([jnp.concatenate([q_heads[gg][r], sel_neg, flag_rows(KS_COLS - KS_FLAG)], axis=0)
                               for r in range(R)]))

    n_far = jnp.maximum(i - 1, 0)

    def pair_tiles(j):
        return [jnp.where(j + n < n_far, j + n, pad_tile) for n in range(2)]

    def pair_scores(gg, tiles):
        k2 = jnp.concatenate([ks_ref[0, gg, t] for t in tiles], axis=0)
        return jnp.dot(k2, qsel[gg], preferred_element_type=F32)

    def online_step(gg, carry, s, tiles):
        m, l, acc = carry
        m_new = jnp.maximum(m, jnp.max(s, axis=0, keepdims=True))
        alpha = jnp.exp2(m - m_new)
        p = jnp.exp2(s - m_new)
        l = alpha * l + jnp.sum(p, axis=0, keepdims=True)
        p = p.astype(BF16)
        acc = alpha * acc
        for n, t in enumerate(tiles):
            acc = acc + jnp.dot(vsT_ref[0, gg, t], p[n * QT:(n + 1) * QT], preferred_element_type=F32)
        return m_new, l, acc

    def far_trip(u, carries):
        j = 4 * u
        for gg in groups:
            sb_scr[gg] = pair_scores(gg, pair_tiles(j + 2))
        carries = [online_step(gg, carries[gg], sa_scr[gg], pair_tiles(j)) for gg in groups]
        for gg in groups:
            sa_scr[gg] = pair_scores(gg, pair_tiles(j + 4))
        return tuple(online_step(gg, carries[gg], sb_scr[gg], pair_tiles(j + 2)) for gg in groups)

    for gg in groups:
        sa_scr[gg] = pair_scores(gg, pair_tiles(0))
    init = (jnp.full((1, RQ), NEG, F32), jnp.zeros((1, RQ), F32), jnp.zeros((HEAD_DIM, RQ), F32))
    carries = lax.fori_loop(0, (n_far + 3) // 4, far_trip, tuple(init for _ in groups))
    near = [jnp.where(i >= 1, i - 1, pad_tile), i]

    gate_ref[...] = jax.nn.sigmoid(gl_ref[...].T)
    for gg in groups:
        s_near = pair_scores(gg, near) + _lane_cat([nb_ref[gg, r] for r in range(R)])
        _, l_s, acc_s = online_step(gg, carries[gg], s_near, near)
        o_slc = acc_s * (1.0 / l_s)
        head0 = (first_group + gg) * R
        gates = [_lane_cat([gate_ref[pl.ds((head0 + r) * 3 + br, 1), :] for r in range(R)]) for br in range(3)]
        total = gates[0] * o_cmp[gg] + gates[1] * o_slc + gates[2] * o_win[gg]
        o_ref[:, gg * RW:(gg + 1) * RW] = jnp.concatenate(
            [total[:, r * QT:(r + 1) * QT] for r in range(R)], axis=0).T


def _nsa_mixer(proj, bsz, L, pe_ck, w_ck1, w_ck2, pe_cv, w_cv1, w_cv2, rel_table):
    g, r = KV_HEADS, Q_PER_KV
    ng = NSA_GROUPS
    ni = L // QT
    nt = L // QT + PAD_TILES
    ncp = L // CMP_STRIDE
    n_slc = L // SLC_BLOCK
    slab = WINDOW + QT
    k_cmp = _compress(proj, COL_KC, bsz, L, pe_ck, w_ck1, w_ck2, transpose_out=False)
    v_cmp_t = _compress(proj, COL_VC, bsz, L, pe_cv, w_cv1, w_cv2, transpose_out=True)
    ks_t, vs_t, kw_t, vw_t = _kv_tiles(proj, bsz, L)
    wb, nb, strip = _nsa_tables(rel_table, L)
    qw = ng * r * HEAD_DIM
    whole = lambda *shape: pl.BlockSpec((1, ng) + shape, lambda b, h, i: (b, h) + (0,) * len(shape))
    per_group = lambda *shape: pl.BlockSpec((ng,) + shape, lambda b, h, i: (h,) + (0,) * len(shape))
    return pl.pallas_call(
        functools.partial(_nsa_kernel, n_slc=n_slc),
        grid=(bsz, g // ng, ni),
        in_specs=[
            pl.BlockSpec((QT, qw), lambda b, h, i: (b * ni + i, COL_Q // qw + h)),
            pl.BlockSpec((QT, LANES), lambda b, h, i: (b * ni + i, COL_GATE // LANES)),
            whole(ncp, HEAD_DIM),
            whole(HEAD_DIM, ncp),
            per_group(r, 2 * ncp, QT),
            pl.BlockSpec((N_SLC_PAD, ncp), lambda b, h, i: (0, 0)),
            whole(nt, QT, KS_COLS),
            whole(nt, HEAD_DIM, QT),
            whole(nt, QT, KW_COLS),
            whole(nt, HEAD_DIM, QT),
            per_group(r, slab, QT),
            per_group(r, 2 * QT, QT),
        ],
        out_specs=pl.BlockSpec((QT, qw), lambda b, h, i: (b * ni + i, h)),
        out_shape=jax.ShapeDtypeStruct((bsz * L, NSA_WIDTH), F32),
        scratch_shapes=[pltpu.VMEM((LANES, QT), F32), pltpu.VMEM((ng, 2 * QT, r * QT), F32),
                        pltpu.VMEM((ng, 2 * QT, r * QT), F32)],
        compiler_params=_cparams("parallel", "parallel", "arbitrary"),
        name="nsa_attention",
    )(proj, proj, k_cmp, v_cmp_t, strip, _overlap_t(L), ks_t, vs_t, kw_t, vw_t, wb, nb)


def _out_proj_kernel(x_ref, ys_ref, yn_ref, w_ref, o_ref):
    half = ys_ref.shape[1]
    acc = jnp.dot(ys_ref[...].astype(BF16), w_ref[:half, :], preferred_element_type=F32)
    acc = acc + jnp.dot(yn_ref[...].astype(BF16), w_ref[half:, :], preferred_element_type=F32)
    o_ref[...] = x_ref[...] + acc


def _out_proj(x2, y_ssm, y_nsa, w, tm=1024, tn=1024):
    t, d = x2.shape
    return pl.pallas_call(
        _out_proj_kernel,
        grid=(t // tm, d // tn),
        in_specs=[pl.BlockSpec((tm, tn), lambda i, j: (i, j)),
                  pl.BlockSpec((tm, SSM_WIDTH), lambda i, j: (i, 0)),
                  pl.BlockSpec((tm, NSA_WIDTH), lambda i, j: (i, 0)),
                  pl.BlockSpec((SSM_WIDTH + NSA_WIDTH, tn), lambda i, j: (0, j))],
        out_specs=pl.BlockSpec((tm, tn), lambda i, j: (i, j)),
        out_shape=jax.ShapeDtypeStruct((t, d), F32),
        compiler_params=_cparams("parallel", "arbitrary"),
        name="out_proj",
    )(x2, y_ssm, y_nsa, w)


def _mlp_kernel(x_ref, n2_ref, wu_ref, wd_ref, nf_ref, o_ref, h_ref, acc_ref):
    f = pl.program_id(1)

    @pl.when(f == 0)
    def _():
        h_ref[...] = _rms(x_ref[...], n2_ref[...]).astype(BF16)
        acc_ref[...] = jnp.zeros_like(acc_ref)

    a = jnp.dot(h_ref[...], wu_ref[...], preferred_element_type=F32)
    a = jnp.square(jnp.maximum(a, 0.0))
    acc_ref[...] += jnp.dot(a.astype(BF16), wd_ref[...], preferred_element_type=F32)

    @pl.when(f == pl.num_programs(1) - 1)
    def _():
        o_ref[...] = _rms(x_ref[...] + acc_ref[...], nf_ref[...])


def _mlp(x2, n2, wu, wd, nf, tm=512, tf=1024):
    t, d = x2.shape
    ff = wu.shape[1]
    return pl.pallas_call(
        _mlp_kernel,
        grid=(t // tm, ff // tf),
        in_specs=[pl.BlockSpec((tm, d), lambda i, f: (i, 0)),
                  pl.BlockSpec((1, d), lambda i, f: (0, 0)),
                  pl.BlockSpec((d, tf), lambda i, f: (0, f)),
                  pl.BlockSpec((tf, d), lambda i, f: (f, 0)),
                  pl.BlockSpec((1, d), lambda i, f: (0, 0))],
        out_specs=pl.BlockSpec((tm, d), lambda i, f: (i, 0)),
        out_shape=jax.ShapeDtypeStruct((t, d), F32),
        scratch_shapes=[pltpu.VMEM((tm, d), BF16), pltpu.VMEM((tm, d), F32)],
        compiler_params=_cparams("parallel", "arbitrary"),
        name="mlp_final_norm",
    )(x2, n2, wu, wd, nf)


def kernel(x, norm1_w, w_in, ssm_a_re, ssm_a_im, ssm_log_dt, ssm_b_re, ssm_b_im, ssm_c_re, ssm_c_im, ssm_d,
           w_glu, b_glu, pe_ck, w_ck1, w_ck2, pe_cv, w_cv1, w_cv2, w_out, norm2_w, w_up, w_down, rel_table,
           norm_f_w):
    bsz, L, d = x.shape
    assert w_in.shape[0] == 1, "the closing rmsnorm is fused into the (single) layer's MLP kernel"
    x2 = x.reshape(bsz * L, d)
    w_in_p = jnp.pad(w_in[0].astype(BF16), ((0, 0), (0, D_IN_PAD - D_IN)))
    proj = _norm_matmul(x2, norm1_w[0].reshape(1, d), w_in_p)
    ops = _s5_operators(ssm_a_re[0], ssm_a_im[0], ssm_log_dt[0], ssm_b_re[0], ssm_b_im[0], ssm_c_re[0], ssm_c_im[0],
                        ssm_d[0])
    y_ssm = _s5_mixer(proj, bsz, L, ops, w_glu[0], b_glu[0])
    y_nsa = _nsa_mixer(proj, bsz, L, pe_ck[0], w_ck1[0], w_ck2[0], pe_cv[0], w_cv1[0], w_cv2[0], rel_table)
    x2 = _out_proj(x2, y_ssm, y_nsa, w_out[0].astype(BF16))
    x2 = _mlp(x2, norm2_w[0].reshape(1, d), w_up[0].astype(BF16), w_down[0].astype(BF16), norm_f_w.reshape(1, d))
    return x2.reshape(bsz, L, d)
```

```python
import functools
import math

import numpy as np
import jax
import jax.numpy as jnp
from jax import lax
from jax.experimental import pallas as pl
from jax.experimental.pallas import tpu as pltpu

F32 = jnp.float32
BF16 = jnp.bfloat16

D_MODEL = 2048
SSM_WIDTH = 1024
SSM_GROUP = 16
SSM_GROUPS = 64
SSM_STATE = 64
NSA_WIDTH = 1024
HEAD_DIM = 64
NSA_HEADS = 16
KV_HEADS = 4
Q_PER_KV = 4
KV_WIDTH = 256
CMP_BLOCK = 32
CMP_STRIDE = 16
CMP_HIDDEN = 256
SLC_BLOCK = 64
SLC_TOPK = 16
WINDOW = 512
REL_BUCKETS = 32
REL_MAX_DIST = 128
D_FF = 8192
EPS = 1e-6
NEG = -1e30
FORCED_SCORE = 1e4
D_IN = SSM_WIDTH + NSA_WIDTH + 6 * KV_WIDTH + 3 * NSA_HEADS
D_IN_PAD = 3840
COL_Q = SSM_WIDTH
COL_KC = COL_Q + NSA_WIDTH
COL_VC = COL_KC + KV_WIDTH
COL_KS = COL_VC + KV_WIDTH
COL_VS = COL_KS + KV_WIDTH
COL_KW = COL_VS + KV_WIDTH
COL_VW = COL_KW + KV_WIDTH
COL_GATE = COL_VW + KV_WIDTH

LANES = 128
S5_CHUNK = 16
SLAB_GROUPS = LANES // SSM_GROUP
N_SLABS = SSM_GROUPS // SLAB_GROUPS
QT = 128
N_SLC_PAD = 64
RANK_SEG = 16
NSA_GROUPS = 4
UNSELECTED = -1e9
LOG2E = math.log2(math.e)
TABLE_W = 1024
VMEM_LIMIT = 56 * 1024 * 1024
HI = lax.Precision.HIGHEST


def _cparams(*sem):
    return pltpu.CompilerParams(dimension_semantics=sem, vmem_limit_bytes=VMEM_LIMIT)


def _rms(x, w):
    ms = jnp.mean(x * x, axis=-1, keepdims=True)
    return x * lax.rsqrt(ms + EPS) * w


def _lane_cat(parts):
    return jnp.concatenate(parts, axis=1)


def _norm_matmul_kernel(x_ref, nw_ref, w_ref, o_ref, h_ref):
    @pl.when(pl.program_id(1) == 0)
    def _():
        h_ref[...] = _rms(x_ref[...], nw_ref[...]).astype(BF16)

    o_ref[...] = jnp.dot(h_ref[...], w_ref[...], preferred_element_type=F32)


def _norm_matmul(x2, nw, w, tm=1024, tn=1280):
    t, d = x2.shape
    n = w.shape[1]
    return pl.pallas_call(
        _norm_matmul_kernel,
        grid=(t // tm, n // tn),
        in_specs=[pl.BlockSpec((tm, d), lambda i, j: (i, 0)),
                  pl.BlockSpec((1, d), lambda i, j: (0, 0)),
                  pl.BlockSpec((d, tn), lambda i, j: (0, j))],
        out_specs=pl.BlockSpec((tm, tn), lambda i, j: (i, j)),
        out_shape=jax.ShapeDtypeStruct((t, n), F32),
        scratch_shapes=[pltpu.VMEM((tm, d), BF16)],
        compiler_params=_cparams("parallel", "arbitrary"),
        name="norm_in_proj",
    )(x2, nw, w)


def _s5_operators(a_re, a_im, log_dt, b_re, b_im, c_re, c_im, d):
    q = S5_CHUNK
    g, p = a_re.shape
    h = SSM_GROUP
    dt = jnp.exp(log_dt)[:, None]
    lam_re, lam_im = dt * a_re, dt * a_im
    mag1 = jnp.exp(lam_re)
    abar_re, abar_im = mag1 * jnp.cos(lam_im), mag1 * jnp.sin(lam_im)
    den = a_re * a_re + a_im * a_im
    zr, zi = abar_re - 1.0, abar_im
    coef_re = (zr * a_re + zi * a_im) / den
    coef_im = (zi * a_re - zr * a_im) / den
    bb_re = coef_re[..., None] * b_re - coef_im[..., None] * b_im
    bb_im = coef_re[..., None] * b_im + coef_im[..., None] * b_re
    k = jnp.arange(q + 1, dtype=F32)[:, None, None]
    mag = jnp.exp(k * lam_re)
    pw_re, pw_im = mag * jnp.cos(k * lam_im), mag * jnp.sin(k * lam_im)
    m_re = pw_re[..., None] * bb_re - pw_im[..., None] * bb_im
    m_im = pw_re[..., None] * bb_im + pw_im[..., None] * bb_re
    kern = (jnp.einsum('ghp,kgpi->gkhi', c_re, m_re[:q], precision=HI)
            - jnp.einsum('ghp,kgpi->gkhi', c_im, m_im[:q], precision=HI))
    kern = kern.at[:, 0].add(jax.vmap(jnp.diag)(d))
    rev = np.arange(q - 1, -1, -1)
    s_end = jnp.stack([m_re[rev], m_im[rev]], axis=0).transpose(2, 1, 4, 0, 3)
    pr, pi = pw_re[1:], pw_im[1:]
    o_re = c_re[None] * pr[:, :, None, :] - c_im[None] * pi[:, :, None, :]
    o_im = -(c_re[None] * pi[:, :, None, :] + c_im[None] * pr[:, :, None, :])
    o_carry = jnp.stack([o_re, o_im], axis=0).transpose(2, 0, 4, 1, 3)
    a_q = jnp.stack([pw_re[q], pw_im[q]], axis=0)
    return kern, s_end, o_carry, a_q


def _s5_slab_weights(kern, s_end, o_carry, a_q):
    q, h, p = S5_CHUNK, SSM_GROUP, SSM_STATE
    s_dense = s_end.reshape(SSM_GROUPS, q, h, 2 * p).astype(BF16)
    k_dense = kern.transpose(0, 1, 3, 2).astype(BF16)
    o_dense = o_carry.reshape(SSM_GROUPS, 2, p, q * h).astype(BF16)
    a_re = a_q[0].reshape(N_SLABS, SLAB_GROUPS * p)
    a_im = a_q[1].reshape(N_SLABS, SLAB_GROUPS * p)
    return s_dense, k_dense, o_dense, a_re, a_im


def _chunk_rows(u_ref, nc):
    return _lane_cat([u_ref[pl.ds(s, nc, stride=S5_CHUNK), :] for s in range(S5_CHUNK)]).astype(BF16)


def _s5_state_kernel(u_ref, sd_ref, o_ref, ws_scr):
    nc = u_ref.shape[0] // S5_CHUNK
    q, h, p, sg = S5_CHUNK, SSM_GROUP, SSM_STATE, SLAB_GROUPS

    @pl.when(pl.program_id(1) == 0)
    def _():
        ws_scr[...] = jnp.zeros_like(ws_scr)
        for s in range(q):
            for a in range(sg):
                for c in range(2):
                    ws_scr[s * LANES + a * h: s * LANES + (a + 1) * h,
                           (c * sg + a) * p: (c * sg + a + 1) * p] = sd_ref[a, s, :, c * p:(c + 1) * p]

    o_ref[0] = jnp.dot(_chunk_rows(u_ref, nc), ws_scr[...], preferred_element_type=F32)


def _s5_scan_kernel(h_ref, ar_ref, ai_ref, o_ref):
    half = ar_ref.shape[1]
    ar = ar_ref[...]
    ai = ai_ref[...]

    def body(c, carry):
        hr, hi = carry
        o_ref[0, c, :, :half] = hr
        o_ref[0, c, :, half:] = hi
        xr = h_ref[0, c, :, :half]
        xi = h_ref[0, c, :, half:]
        return ar * hr - ai * hi + xr, ar * hi + ai * hr + xi

    zero = jnp.zeros(ar.shape, F32)
    lax.fori_loop(0, h_ref.shape[1], body, (zero, zero))


def _s5_out_kernel(u_ref, kd_ref, h_ref, od_ref, y_ref, kl_scr, wi_scr, wc_scr):
    nc = u_ref.shape[0] // S5_CHUNK
    q, h, p, sg = S5_CHUNK, SSM_GROUP, SSM_STATE, SLAB_GROUPS

    @pl.when(pl.program_id(1) == 0)
    def _():
        kl_scr[...] = jnp.zeros_like(kl_scr)
        for k in range(q):
            for a in range(sg):
                kl_scr[k, a * h:(a + 1) * h, a * h:(a + 1) * h] = kd_ref[a, k]
        wi_scr[...] = jnp.zeros_like(wi_scr)
        for s in range(q):
            for t in range(s, q):
                wi_scr[s * LANES:(s + 1) * LANES, t * LANES:(t + 1) * LANES] = kl_scr[t - s]
        wc_scr[...] = jnp.zeros_like(wc_scr)
        for a in range(sg):
            for c in range(2):
                for t in range(q):
                    wc_scr[(c * sg + a) * p:(c * sg + a + 1) * p,
                           t * LANES + a * h: t * LANES + (a + 1) * h] = od_ref[a, c, :, t * h:(t + 1) * h]

    u_rows = _chunk_rows(u_ref, nc)
    h_rows = h_ref[0].astype(BF16)
    pair = 2 * LANES
    for tp in range(q // 2):
        cols = slice(tp * pair, (tp + 1) * pair)
        k_used = (tp + 1) * pair
        y = (jnp.dot(u_rows[:, :k_used], wi_scr[:k_used, cols], preferred_element_type=F32)
             + jnp.dot(h_rows, wc_scr[:, cols], preferred_element_type=F32))
        for tt in range(2):
            y_ref[pl.ds(2 * tp + tt, nc, stride=q), :] = y[:, tt * LANES:(tt + 1) * LANES]


def _glu_kernel(y_ref, w_ref, b_ref, o_ref):
    z = jax.nn.gelu(y_ref[...])
    gate = jnp.dot(z.astype(BF16), w_ref[...], preferred_element_type=F32) + b_ref[...]
    o_ref[...] = (z * jax.nn.sigmoid(gate)).astype(o_ref.dtype)


def _s5_mixer(proj, bsz, L, ops, w_glu, b_glu):
    s_dense, k_dense, o_dense, a_re, a_im = _s5_slab_weights(*ops)
    q, h, p = S5_CHUNK, SSM_GROUP, SSM_STATE
    nc = L // q
    ns, sg = N_SLABS, SLAB_GROUPS
    sw = sg * 2 * p
    kq = q * LANES
    hend = pl.pallas_call(
        _s5_state_kernel,
        grid=(ns, bsz),
        in_specs=[pl.BlockSpec((L, LANES), lambda j, b: (b, j)),
                  pl.BlockSpec((sg, q, h, 2 * p), lambda j, b: (j, 0, 0, 0))],
        out_specs=pl.BlockSpec((1, nc, sw), lambda j, b: (b, 0, j)),
        out_shape=jax.ShapeDtypeStruct((bsz, nc, ns * sw), F32),
        scratch_shapes=[pltpu.VMEM((kq, sw), BF16)],
        compiler_params=_cparams("parallel", "arbitrary"),
        name="s5_chunk_state",
    )(proj, s_dense)
    hprev4 = pl.pallas_call(
        _s5_scan_kernel,
        grid=(bsz,),
        in_specs=[pl.BlockSpec((1, nc, ns, sw), lambda b: (b, 0, 0, 0)),
                  pl.BlockSpec((ns, sw // 2), lambda b: (0, 0)),
                  pl.BlockSpec((ns, sw // 2), lambda b: (0, 0))],
        out_specs=pl.BlockSpec((1, nc, ns, sw), lambda b: (b, 0, 0, 0)),
        out_shape=jax.ShapeDtypeStruct((bsz, nc, ns, sw), F32),
        compiler_params=_cparams("parallel"),
        name="s5_chunk_scan",
    )(hend.reshape(bsz, nc, ns, sw), a_re, a_im)
    y = pl.pallas_call(
        _s5_out_kernel,
        grid=(ns, bsz),
        in_specs=[pl.BlockSpec((L, LANES), lambda j, b: (b, j)),
                  pl.BlockSpec((sg, q, h, h), lambda j, b: (j, 0, 0, 0)),
                  pl.BlockSpec((1, nc, sw), lambda j, b: (b, 0, j)),
                  pl.BlockSpec((sg, 2, p, q * h), lambda j, b: (j, 0, 0, 0))],
        out_specs=pl.BlockSpec((L, LANES), lambda j, b: (b, j)),
        out_shape=jax.ShapeDtypeStruct((bsz * L, SSM_WIDTH), F32),
        scratch_shapes=[pltpu.VMEM((q, LANES, LANES), BF16), pltpu.VMEM((kq, kq), BF16), pltpu.VMEM((sw, kq), BF16)],
        compiler_params=_cparams("parallel", "arbitrary"),
        name="s5_chunk_out",
    )(proj, k_dense, hprev4.reshape(bsz, nc, ns * sw), o_dense)
    tm = 512
    return pl.pallas_call(
        _glu_kernel,
        grid=(bsz * L // tm,),
        in_specs=[pl.BlockSpec((tm, SSM_WIDTH), lambda i: (i, 0)),
                  pl.BlockSpec((SSM_WIDTH, SSM_WIDTH), lambda i: (0, 0)),
                  pl.BlockSpec((1, SSM_WIDTH), lambda i: (0, 0))],
        out_specs=pl.BlockSpec((tm, SSM_WIDTH), lambda i: (i, 0)),
        out_shape=jax.ShapeDtypeStruct((bsz * L, SSM_WIDTH), BF16),
        compiler_params=_cparams("parallel"),
        name="s5_gelu_glu",
    )(y, w_glu.astype(BF16), b_glu.reshape(1, SSM_WIDTH))


def _compress_kernel(k_ref, w1_ref, pe_ref, w1full_ref, w2_ref, o_ref, *, transpose_out):
    nc = k_ref.shape[0] // CMP_STRIDE
    prod = jnp.dot(_chunk_rows(k_ref, nc), w1_ref[...], preferred_element_type=F32)
    pe_h = jnp.dot(jnp.broadcast_to(pe_ref[...], (8, pe_ref.shape[1])), w1full_ref[...],
                   precision=HI, preferred_element_type=F32)[0:1]
    two_h = 2 * CMP_HIDDEN
    for gl in range(LANES // HEAD_DIM):
        first = prod[:, gl * two_h: gl * two_h + CMP_HIDDEN]
        second = prod[:, gl * two_h + CMP_HIDDEN: (gl + 1) * two_h]
        hid = first + pltpu.roll(second, nc - 1, axis=0) + pe_h
        out = jnp.dot(jax.nn.gelu(hid).astype(BF16), w2_ref[...], preferred_element_type=F32)
        if transpose_out:
            o_ref[0, gl] = _lane_cat([out, jnp.zeros_like(out)]).T[:HEAD_DIM].astype(BF16)
        else:
            o_ref[0, gl] = out.astype(BF16)


def _compress(proj, col, bsz, L, pe, w1, w2, transpose_out):
    nc = L // CMP_STRIDE
    half = CMP_STRIDE * HEAD_DIM
    gpl = LANES // HEAD_DIM
    w1cat = jnp.concatenate([w1[:half], w1[half:]], axis=1)
    eye = jnp.eye(gpl, dtype=F32)
    w1slab = jnp.einsum('ldh,ag->ladgh', w1cat.reshape(CMP_STRIDE, HEAD_DIM, 2 * CMP_HIDDEN), eye)
    w1slab = w1slab.reshape(CMP_STRIDE * LANES, gpl * 2 * CMP_HIDDEN).astype(BF16)
    out_block = (1, gpl, HEAD_DIM, nc) if transpose_out else (1, gpl, nc, HEAD_DIM)
    out_full = (bsz, KV_HEADS) + out_block[2:]
    return pl.pallas_call(
        functools.partial(_compress_kernel, transpose_out=transpose_out),
        grid=(bsz, KV_HEADS // gpl),
        in_specs=[pl.BlockSpec((L, LANES), lambda b, j: (b, col // LANES + j)),
                  pl.BlockSpec(w1slab.shape, lambda b, j: (0, 0)),
                  pl.BlockSpec((1, 2 * half), lambda b, j: (0, 0)),
                  pl.BlockSpec((2 * half, CMP_HIDDEN), lambda b, j: (0, 0)),
                  pl.BlockSpec((CMP_HIDDEN, HEAD_DIM), lambda b, j: (0, 0))],
        out_specs=pl.BlockSpec(out_block, lambda b, j: (b, j, 0, 0)),
        out_shape=jax.ShapeDtypeStruct(out_full, BF16),
        compiler_params=_cparams("parallel", "parallel"),
        name="nsa_compress",
    )(proj, w1slab, pe.reshape(1, 2 * half), w1, w2.astype(BF16))


KS_COLS = 2 * LANES
KW_COLS = LANES
KS_FLAG = HEAD_DIM + N_SLC_PAD
KW_FLAG = HEAD_DIM
PAD_TILES = 4


def _kv_tiles_kernel(ks_ref, vs_ref, kw_ref, vw_ref, kso_ref, vso_ref, kwo_ref, vwo_ref):
    tm = ks_ref.shape[0]
    step = pl.program_id(1)
    row0 = step * tm
    kso_ref[...] = jnp.zeros_like(kso_ref)
    kwo_ref[...] = jnp.zeros_like(kwo_ref)

    @pl.when(step < pl.num_programs(1) - 1)
    def _():
        lane_blk = lax.broadcasted_iota(jnp.int32, (QT, N_SLC_PAD), 1)
        for k in range(tm // QT):
            rows = slice(k * QT, (k + 1) * QT)
            tok = lax.broadcasted_iota(jnp.int32, (QT, N_SLC_PAD), 0) + (row0 + k * QT)
            onehot = jnp.where((tok >> int(math.log2(SLC_BLOCK))) == lane_blk, 1.0, 0.0).astype(BF16)
            vs_t = vs_ref[rows, :].T
            vw_t = vw_ref[rows, :].T
            for g in range(KV_HEADS):
                cols = slice(g * HEAD_DIM, (g + 1) * HEAD_DIM)
                kso_ref[0, g, k, :, :HEAD_DIM] = ks_ref[rows, cols].astype(BF16)
                kso_ref[0, g, k, :, HEAD_DIM:KS_FLAG] = onehot
                kwo_ref[0, g, k, :, :HEAD_DIM] = kw_ref[rows, cols].astype(BF16)
                vso_ref[0, g, k] = vs_t[cols].astype(BF16)
                vwo_ref[0, g, k] = vw_t[cols].astype(BF16)

    @pl.when(step == pl.num_programs(1) - 1)
    def _():
        ones = jnp.ones(kso_ref.shape[:4] + (1,), BF16)
        kso_ref[:, :, :, :, KS_FLAG:KS_FLAG + 1] = ones
        kwo_ref[:, :, :, :, KW_FLAG:KW_FLAG + 1] = ones
        vso_ref[...] = jnp.zeros_like(vso_ref)
        vwo_ref[...] = jnp.zeros_like(vwo_ref)


def _kv_tiles(proj, bsz, L):
    nt = L // QT
    g = KV_HEADS
    kt = PAD_TILES
    tm = kt * QT
    steps = L // tm
    in_spec = lambda col: pl.BlockSpec((tm, KV_WIDTH),
                                       lambda b, i: (b * steps + jnp.minimum(i, steps - 1), col // KV_WIDTH))
    out_spec = lambda a, c: pl.BlockSpec((1, g, kt, a, c), lambda b, i: (b, 0, i, 0, 0))
    shape = lambda a, c: jax.ShapeDtypeStruct((bsz, g, nt + kt, a, c), BF16)
    return pl.pallas_call(
        _kv_tiles_kernel,
        grid=(bsz, steps + 1),
        in_specs=[in_spec(COL_KS), in_spec(COL_VS), in_spec(COL_KW), in_spec(COL_VW)],
        out_specs=[out_spec(QT, KS_COLS), out_spec(HEAD_DIM, QT), out_spec(QT, KW_COLS), out_spec(HEAD_DIM, QT)],
        out_shape=[shape(QT, KS_COLS), shape(HEAD_DIM, QT), shape(QT, KW_COLS), shape(HEAD_DIM, QT)],
        compiler_params=_cparams("parallel", "arbitrary"),
        name="nsa_kv_tiles",
    )(proj, proj, proj, proj)


def _rel_bucket(dist):
    dist = jnp.maximum(dist, 0)
    max_exact = REL_BUCKETS // 2
    large = max_exact + (jnp.log(jnp.maximum(dist, 1).astype(F32) / max_exact)
                         / math.log(REL_MAX_DIST / max_exact) * (REL_BUCKETS - max_exact)).astype(jnp.int32)
    large = jnp.minimum(large, REL_BUCKETS - 1)
    return jnp.where(dist < max_exact, dist, large)


def _toeplitz_rows(base, nrows, step):
    w = base.shape[1]
    x = jnp.broadcast_to(base, (nrows, w))
    k_ix = lax.broadcasted_iota(jnp.int32, (nrows, w), 0)
    bit = 1
    while bit < nrows:
        x = jnp.where((k_ix & bit) != 0, pltpu.roll(x, (bit * step) % w, axis=1), x)
        bit *= 2
    return x


def _bias_tables_kernel(bw_ref, bc_ref, far_ref, wb_ref, nb_ref, strip_ref, *, ncp):
    slab = WINDOW + QT
    y = _toeplitz_rows(bw_ref[0], QT, 1)
    for kh in range(slab // QT):
        tile = y[:, WINDOW - kh * QT: WINDOW - kh * QT + QT]
        wb_ref[0, kh * QT:(kh + 1) * QT, :] = tile
        if kh >= slab // QT - 2:
            row = kh - (slab // QT - 2)
            nb_ref[0, row * QT:(row + 1) * QT, :] = tile - far_ref[0]
    band = 2 * CMP_STRIDE
    z = _toeplitz_rows(bc_ref[0], band, CMP_STRIDE)[:, 2 * QT: 3 * QT]
    strip_ref[0, :ncp - CMP_STRIDE, :] = jnp.broadcast_to(far_ref[0], (ncp - CMP_STRIDE, QT))
    strip_ref[0, ncp - CMP_STRIDE: ncp + CMP_STRIDE, :] = z
    strip_ref[0, ncp + CMP_STRIDE:, :] = jnp.full((ncp - CMP_STRIDE, QT), NEG, F32)


def _nsa_tables(rel_table, L):
    ncp = L // CMP_STRIDE
    slab = WINDOW + QT
    dist = jnp.arange(TABLE_W)
    rel_table = rel_table * LOG2E
    per_dist = rel_table[_rel_bucket(dist)].T
    base_w = jnp.where(dist < WINDOW, per_dist, NEG)
    shift = CMP_BLOCK - 1
    shifted = jnp.concatenate([jnp.full((NSA_HEADS, shift), NEG, F32), per_dist[:, :TABLE_W - shift]], axis=1)
    base_c = jnp.where(dist < TABLE_W // 2, shifted, NEG)
    far = jnp.broadcast_to(rel_table[REL_BUCKETS - 1][:, None, None], (NSA_HEADS, 1, QT))
    wb, nb, strip = pl.pallas_call(
        functools.partial(_bias_tables_kernel, ncp=ncp),
        grid=(NSA_HEADS,),
        in_specs=[pl.BlockSpec((1, 1, TABLE_W), lambda h: (h, 0, 0)),
                  pl.BlockSpec((1, 1, TABLE_W), lambda h: (h, 0, 0)),
                  pl.BlockSpec((1, 1, QT), lambda h: (h, 0, 0))],
        out_specs=[pl.BlockSpec((1, slab, QT), lambda h: (h, 0, 0)),
                   pl.BlockSpec((1, 2 * QT, QT), lambda h: (h, 0, 0)),
                   pl.BlockSpec((1, 2 * ncp, QT), lambda h: (h, 0, 0))],
        out_shape=[jax.ShapeDtypeStruct((NSA_HEADS, slab, QT), F32),
                   jax.ShapeDtypeStruct((NSA_HEADS, 2 * QT, QT), F32),
                   jax.ShapeDtypeStruct((NSA_HEADS, 2 * ncp, QT), F32)],
        compiler_params=_cparams("parallel"),
        name="nsa_bias_tables",
    )(base_w.reshape(NSA_HEADS, 1, TABLE_W), base_c.reshape(NSA_HEADS, 1, TABLE_W), far)
    g, r = KV_HEADS, Q_PER_KV
    return wb.reshape(g, r, slab, QT), nb.reshape(g, r, 2 * QT, QT), strip.reshape(g, r, 2 * ncp, QT)


def _overlap_t(L):
    n_cmp = (L - CMP_BLOCK) // CMP_STRIDE + 1
    n_slc = L // SLC_BLOCK
    cmp_idx = np.arange(n_cmp)[:, None] * CMP_STRIDE + np.arange(CMP_BLOCK)[None, :]
    overlap = ((cmp_idx[:, :, None] // SLC_BLOCK) == np.arange(n_slc)[None, None, :]).sum(1) / CMP_BLOCK
    out = np.zeros((N_SLC_PAD, L // CMP_STRIDE), np.float32)
    out[:n_slc, :n_cmp] = overlap.T
    return jnp.asarray(out, dtype=BF16)


def _split3(x):
    hi = x.astype(BF16)
    r1 = x - hi.astype(F32)
    mid = r1.astype(BF16)
    lo = (r1 - mid.astype(F32)).astype(BF16)
    return hi, mid, lo


def _nsa_kernel(q_ref, gl_ref, kc_ref, vcT_ref, strip_ref, ovT_ref, ks_ref, vsT_ref, kw_ref, vwT_ref, wb_ref,
                nb_ref, o_ref, gate_ref, sa_scr, sb_scr, *, n_slc):
    i = pl.program_id(2)
    R = Q_PER_KV
    RQ = R * QT
    RW = R * HEAD_DIM
    groups = range(kc_ref.shape[1])
    first_group = pl.program_id(1) * len(groups)
    pad_tile = ks_ref.shape[2] - PAD_TILES
    t_pos = lax.broadcasted_iota(jnp.int32, (1, QT), 1) + i * QT

    def flag_rows(n):
        return jnp.where(lax.broadcasted_iota(jnp.int32, (n, QT), 0) == 0, NEG, 0.0).astype(BF16)

    q_heads = []
    for gg in groups:
        qT = (q_ref[:, gg * RW:(gg + 1) * RW].T * (0.125 * LOG2E)).astype(BF16)
        q_heads.append([qT[r * HEAD_DIM:(r + 1) * HEAD_DIM] for r in range(R)])

    slab = WINDOW + QT
    n_win = slab // QT
    win_idx = [jnp.where(i + jj >= n_win - 1, i + jj - (n_win - 1), pad_tile) for jj in range(n_win)]

    def window_branch(gg):
        kw = jnp.concatenate([kw_ref[0, gg, j] for j in win_idx], axis=0)
        qwin = _lane_cat([jnp.concatenate([q_heads[gg][r], flag_rows(KW_COLS - HEAD_DIM)], axis=0)
                          for r in range(R)])
        wb = _lane_cat([wb_ref[gg, r] for r in range(R)])
        sw = jnp.dot(kw, qwin, preferred_element_type=F32) + wb
        m_w = jnp.max(sw, axis=0, keepdims=True)
        e_w = jnp.exp2(sw - m_w)
        l_w = jnp.sum(e_w, axis=0, keepdims=True)
        e_wb = e_w.astype(BF16)
        o_win = None
        for jj, j in enumerate(win_idx):
            term = jnp.dot(vwT_ref[0, gg, j], e_wb[jj * QT:(jj + 1) * QT], preferred_element_type=F32)
            o_win = term if o_win is None else o_win + term
        return o_win * (1.0 / l_w)

    o_win = [window_branch(gg) for gg in groups]

    ncp = kc_ref.shape[2]
    strip_row = pl.multiple_of(ncp - i * (QT // CMP_STRIDE), QT // CMP_STRIDE)
    ov = ovT_ref[...]
    s_ix = lax.broadcasted_iota(jnp.int32, (N_SLC_PAD, QT), 0)
    cur = t_pos >> int(math.log2(SLC_BLOCK))
    forced = (s_ix == 0) | (s_ix == cur) | (s_ix == cur - 1)

    def compressed_branch(gg):
        sc = jnp.dot(kc_ref[0, gg], _lane_cat(q_heads[gg]), preferred_element_type=F32)
        sm = sc + _lane_cat([strip_ref[gg, r, pl.ds(strip_row, ncp), :] for r in range(R)])
        m_c = jnp.max(sm, axis=0, keepdims=True)
        e_c = jnp.exp2(sm - m_c)
        l_c = jnp.sum(e_c, axis=0, keepdims=True)
        inv_c = jnp.where(m_c > 0.5 * NEG, 1.0 / l_c, 0.0)
        o_cmp = jnp.dot(vcT_ref[0, gg], e_c.astype(BF16), preferred_element_type=F32) * inv_c
        p_c = e_c * inv_c
        p_sum = p_c[:, 0:QT]
        for r in range(1, R):
            p_sum = p_sum + p_c[:, r * QT:(r + 1) * QT]
        imp = None
        for part in _split3(p_sum):
            term = jnp.dot(ov, part, preferred_element_type=F32)
            imp = term if imp is None else imp + term
        score = jnp.where(forced, FORCED_SCORE, jnp.where(s_ix <= cur, imp, -1.0))
        return o_cmp, jnp.where(s_ix < n_slc, score, -2.0)

    cmp_out = [compressed_branch(gg) for gg in groups]
    o_cmp = [c[0] for c in cmp_out]
    scores = [c[1] for c in cmp_out]

    SUB = 8
    n_sub = N_SLC_PAD // SUB
    score_rows = [[score[v * SUB:(v + 1) * SUB] for v in range(n_sub)] for score in scores]
    sub_ix = lax.broadcasted_iota(jnp.int32, (SUB, QT), 0)

    def rank_segment(seg, ranks):
        out = []
        for gg in groups:
            rk = list(ranks[gg])
            for sp in range(seg * RANK_SEG, min((seg + 1) * RANK_SEG, n_slc)):
                row = scores[gg][sp:sp + 1, :]
                for v, blk in enumerate(score_rows[gg]):
                    if v * SUB > sp:
                        beats = row >= blk
                    elif v * SUB + SUB - 1 <= sp:
                        beats = row > blk
                    else:
                        beats = (row > blk) | ((row == blk) & (sub_ix > sp - v * SUB))
                    rk[v] = rk[v] + jnp.where(beats, 1.0, 0.0)
            out.append(tuple(rk))
        return tuple(out)

    ranks = rank_segment(0, tuple(tuple(jnp.zeros((SUB, QT), F32) for _ in range(n_sub)) for _ in groups))
    last_block = (i + 1) * (QT // SLC_BLOCK) - 1
    for seg in range(1, -(-n_slc // RANK_SEG)):
        ranks = lax.cond(last_block >= seg * RANK_SEG, functools.partial(rank_segment, seg), lambda rk: rk, ranks)

    qsel = []
    for gg in groups:
        sel_neg = jnp.where(jnp.concatenate(ranks[gg], axis=0) < float(SLC_TOPK), 0.0, UNSELECTED).astype(BF16)
        qsel.append(_lane_cat([jnp.concatenate([q_heads[gg][r], sel_neg, flag_rows(KS_COLS - KS_FLAG)], axis=0)
                               for r in range(R)]))

    n_far = jnp.maximum(i - 1, 0)

    def pair_tiles(j):
        return [jnp.where(j + n < n_far, j + n, pad_tile) for n in range(2)]

    def pair_scores(gg, tiles):
        k2 = jnp.concatenate([ks_ref[0, gg, t] for t in tiles], axis=0)
        return jnp.dot(k2, qsel[gg], preferred_element_type=F32)

    def online_step(gg, carry, s, tiles):
        m, l, acc = carry
        m_new = jnp.maximum(m, jnp.max(s, axis=0, keepdims=True))
        alpha = jnp.exp2(m - m_new)
        p = jnp.exp2(s - m_new)
        l = alpha * l + jnp.sum(p, axis=0, keepdims=True)
        p = p.astype(BF16)
        acc = alpha * acc
        for n, t in enumerate(tiles):
            acc = acc + jnp.dot(vsT_ref[0, gg, t], p[n * QT:(n + 1) * QT], preferred_element_type=F32)
        return m_new, l, acc

    def far_trip(u, carries):
        j = 4 * u
        for gg in groups:
            sb_scr[gg] = pair_scores(gg, pair_tiles(j + 2))
        carries = [online_step(gg, carries[gg], sa_scr[gg], pair_tiles(j)) for gg in groups]
        for gg in groups:
            sa_scr[gg] = pair_scores(gg, pair_tiles(j + 4))
        return tuple(online_step(gg, carries[gg], sb_scr[gg], pair_tiles(j + 2)) for gg in groups)

    for gg in groups:
        sa_scr[gg] = pair_scores(gg, pair_tiles(0))
    init = (jnp.full((1, RQ), NEG, F32), jnp.zeros((1, RQ), F32), jnp.zeros((HEAD_DIM, RQ), F32))
    carries = lax.fori_loop(0, (n_far + 3) // 4, far_trip, tuple(init for _ in groups))
    near = [jnp.where(i >= 1, i - 1, pad_tile), i]

    gate_ref[...] = jax.nn.sigmoid(gl_ref[...].T)
    for gg in groups:
        s_near = pair_scores(gg, near) + _lane_cat([nb_ref[gg, r] for r in range(R)])
        _, l_s, acc_s = online_step(gg, carries[gg], s_near, near)
        o_slc = acc_s * (1.0 / l_s)
        head0 = (first_group + gg) * R
        gates = [_lane_cat([gate_ref[pl.ds((head0 + r) * 3 + br, 1), :] for r in range(R)]) for br in range(3)]
        total = gates[0] * o_cmp[gg] + gates[1] * o_slc + gates[2] * o_win[gg]
        o_ref[:, gg * RW:(gg + 1) * RW] = jnp.concatenate(
            [total[:, r * QT:(r + 1) * QT] for r in range(R)], axis=0).T.astype(o_ref.dtype)


def _nsa_mixer(proj, bsz, L, pe_ck, w_ck1, w_ck2, pe_cv, w_cv1, w_cv2, rel_table):
    g, r = KV_HEADS, Q_PER_KV
    ng = NSA_GROUPS
    ni = L // QT
    nt = L // QT + PAD_TILES
    ncp = L // CMP_STRIDE
    n_slc = L // SLC_BLOCK
    slab = WINDOW + QT
    k_cmp = _compress(proj, COL_KC, bsz, L, pe_ck, w_ck1, w_ck2, transpose_out=False)
    v_cmp_t = _compress(proj, COL_VC, bsz, L, pe_cv, w_cv1, w_cv2, transpose_out=True)
    ks_t, vs_t, kw_t, vw_t = _kv_tiles(proj, bsz, L)
    wb, nb, strip = _nsa_tables(rel_table, L)
    qw = ng * r * HEAD_DIM
    once = pl.Buffered(1)
    whole = lambda *shape: pl.BlockSpec((1, ng) + shape, lambda b, h, i: (b, h) + (0,) * len(shape), once)
    per_group = lambda *shape: pl.BlockSpec((ng,) + shape, lambda b, h, i: (h,) + (0,) * len(shape), once)
    return pl.pallas_call(
        functools.partial(_nsa_kernel, n_slc=n_slc),
        grid=(bsz, g // ng, ni),
        in_specs=[
            pl.BlockSpec((QT, qw), lambda b, h, i: (b * ni + i, COL_Q // qw + h)),
            pl.BlockSpec((QT, LANES), lambda b, h, i: (b * ni + i, COL_GATE // LANES)),
            whole(ncp, HEAD_DIM),
            whole(HEAD_DIM, ncp),
            per_group(r, 2 * ncp, QT),
            pl.BlockSpec((N_SLC_PAD, ncp), lambda b, h, i: (0, 0), once),
            whole(nt, QT, KS_COLS),
            whole(nt, HEAD_DIM, QT),
            whole(nt, QT, KW_COLS),
            whole(nt, HEAD_DIM, QT),
            per_group(r, slab, QT),
            per_group(r, 2 * QT, QT),
        ],
        out_specs=pl.BlockSpec((QT, qw), lambda b, h, i: (b * ni + i, h)),
        out_shape=jax.ShapeDtypeStruct((bsz * L, NSA_WIDTH), BF16),
        scratch_shapes=[pltpu.VMEM((LANES, QT), F32), pltpu.VMEM((ng, 2 * QT, r * QT), F32),
                        pltpu.VMEM((ng, 2 * QT, r * QT), F32)],
        compiler_params=_cparams("parallel", "parallel", "arbitrary"),
        name="nsa_attention",
    )(proj, proj, k_cmp, v_cmp_t, strip, _overlap_t(L), ks_t, vs_t, kw_t, vw_t, wb, nb)


def _out_proj_kernel(x_ref, ys_ref, yn_ref, w_ref, o_ref):
    half = ys_ref.shape[1]
    acc = jnp.dot(ys_ref[...], w_ref[:half, :], preferred_element_type=F32)
    acc = acc + jnp.dot(yn_ref[...], w_ref[half:, :], preferred_element_type=F32)
    o_ref[...] = x_ref[...] + acc


def _out_proj(x2, y_ssm, y_nsa, w, tm=512):
    t, d = x2.shape
    return pl.pallas_call(
        _out_proj_kernel,
        grid=(t // tm,),
        in_specs=[pl.BlockSpec((tm, d), lambda i: (i, 0)),
                  pl.BlockSpec((tm, SSM_WIDTH), lambda i: (i, 0)),
                  pl.BlockSpec((tm, NSA_WIDTH), lambda i: (i, 0)),
                  pl.BlockSpec((SSM_WIDTH + NSA_WIDTH, d), lambda i: (0, 0), pl.Buffered(1))],
        out_specs=pl.BlockSpec((tm, d), lambda i: (i, 0)),
        out_shape=jax.ShapeDtypeStruct((t, d), F32),
        compiler_params=_cparams("parallel"),
        name="out_proj",
    )(x2, y_ssm, y_nsa, w)


def _mlp_kernel(x_ref, n2_ref, wu_ref, wd_ref, nf_ref, o_ref, h_ref, acc_ref):
    f = pl.program_id(1)

    @pl.when(f == 0)
    def _():
        h_ref[...] = _rms(x_ref[...], n2_ref[...]).astype(BF16)
        acc_ref[...] = jnp.zeros_like(acc_ref)

    a = jnp.dot(h_ref[...], wu_ref[...], preferred_element_type=F32)
    a = jnp.square(jnp.maximum(a, 0.0))
    acc_ref[...] += jnp.dot(a.astype(BF16), wd_ref[...], preferred_element_type=F32)

    @pl.when(f == pl.num_programs(1) - 1)
    def _():
        o_ref[...] = _rms(x_ref[...] + acc_ref[...], nf_ref[...])


def _mlp(x2, n2, wu, wd, nf, tm=512, tf=1024):
    t, d = x2.shape
    ff = wu.shape[1]
    return pl.pallas_call(
        _mlp_kernel,
        grid=(t // tm, ff // tf),
        in_specs=[pl.BlockSpec((tm, d), lambda i, f: (i, 0)),
                  pl.BlockSpec((1, d), lambda i, f: (0, 0)),
                  pl.BlockSpec((d, tf), lambda i, f: (0, f)),
                  pl.BlockSpec((tf, d), lambda i, f: (f, 0)),
                  pl.BlockSpec((1, d), lambda i, f: (0, 0))],
        out_specs=pl.BlockSpec((tm, d), lambda i, f: (i, 0)),
        out_shape=jax.ShapeDtypeStruct((t, d), F32),
        scratch_shapes=[pltpu.VMEM((tm, d), BF16), pltpu.VMEM((tm, d), F32)],
        compiler_params=_cparams("parallel", "arbitrary"),
        name="mlp_final_norm",
    )(x2, n2, wu, wd, nf)


def kernel(x, norm1_w, w_in, ssm_a_re, ssm_a_im, ssm_log_dt, ssm_b_re, ssm_b_im, ssm_c_re, ssm_c_im, ssm_d,
           w_glu, b_glu, pe_ck, w_ck1, w_ck2, pe_cv, w_cv1, w_cv2, w_out, norm2_w, w_up, w_down, rel_table,
           norm_f_w):
    bsz, L, d = x.shape
    assert w_in.shape[0] == 1, "the closing rmsnorm is fused into the (single) layer's MLP kernel"
    x2 = x.reshape(bsz * L, d)
    w_in_p = jnp.pad(w_in[0].astype(BF16), ((0, 0), (0, D_IN_PAD - D_IN)))
    proj = _norm_matmul(x2, norm1_w[0].reshape(1, d), w_in_p)
    ops = _s5_operators(ssm_a_re[0], ssm_a_im[0], ssm_log_dt[0], ssm_b_re[0], ssm_b_im[0], ssm_c_re[0], ssm_c_im[0],
                        ssm_d[0])
    y_ssm = _s5_mixer(proj, bsz, L, ops, w_glu[0], b_glu[0])
    y_nsa = _nsa_mixer(proj, bsz, L, pe_ck[0], w_ck1[0], w_ck2[0], pe_cv[0], w_cv1[0], w_cv2[0], rel_table)
    x2 = _out_proj(x2, y_ssm, y_nsa, w_out[0].astype(BF16))
    x2 = _mlp(x2, norm2_w[0].reshape(1, d), w_up[0].astype(BF16), w_down[0].astype(BF16), norm_f_w.reshape(1, d))
    return x2.reshape(bsz, L, d)
```

```python
import functools
import math

import numpy as np
import jax
import jax.numpy as jnp
from jax import lax
from jax.experimental import pallas as pl
from jax.experimental.pallas import tpu as pltpu

F32 = jnp.float32
BF16 = jnp.bfloat16

D_MODEL = 2048
SSM_WIDTH = 1024
SSM_GROUP = 16
SSM_GROUPS = 64
SSM_STATE = 64
NSA_WIDTH = 1024
HEAD_DIM = 64
NSA_HEADS = 16
KV_HEADS = 4
Q_PER_KV = 4
KV_WIDTH = 256
CMP_BLOCK = 32
CMP_STRIDE = 16
CMP_HIDDEN = 256
SLC_BLOCK = 64
SLC_TOPK = 16
WINDOW = 512
REL_BUCKETS = 32
REL_MAX_DIST = 128
D_FF = 8192
EPS = 1e-6
NEG = -1e30
FORCED_SCORE = 1e4
D_IN = SSM_WIDTH + NSA_WIDTH + 6 * KV_WIDTH + 3 * NSA_HEADS
D_IN_PAD = 3840
COL_Q = SSM_WIDTH
COL_KC = COL_Q + NSA_WIDTH
COL_VC = COL_KC + KV_WIDTH
COL_KS = COL_VC + KV_WIDTH
COL_VS = COL_KS + KV_WIDTH
COL_KW = COL_VS + KV_WIDTH
COL_VW = COL_KW + KV_WIDTH
COL_GATE = COL_VW + KV_WIDTH

LANES = 128
S5_CHUNK = 16
SLAB_GROUPS = LANES // SSM_GROUP
N_SLABS = SSM_GROUPS // SLAB_GROUPS
QT = 128
N_SLC_PAD = 64
RANK_SEG = 16
NSA_GROUPS = 4
UNSELECTED = -1e9
LOG2E = math.log2(math.e)
TABLE_W = 1024
VMEM_LIMIT = 56 * 1024 * 1024
HI = lax.Precision.HIGHEST


def _cparams(*sem):
    return pltpu.CompilerParams(dimension_semantics=sem, vmem_limit_bytes=VMEM_LIMIT)


def _rms(x, w):
    ms = jnp.mean(x * x, axis=-1, keepdims=True)
    return x * lax.rsqrt(ms + EPS) * w


def _lane_cat(parts):
    return jnp.concatenate(parts, axis=1)


def _norm_matmul_kernel(x_ref, nw_ref, w_ref, o_ref, h_ref):
    @pl.when(pl.program_id(1) == 0)
    def _():
        h_ref[...] = _rms(x_ref[...], nw_ref[...]).astype(BF16)

    o_ref[...] = jnp.dot(h_ref[...], w_ref[...], preferred_element_type=F32)


def _norm_matmul(x2, nw, w, tm=1024, tn=1280):
    t, d = x2.shape
    n = w.shape[1]
    return pl.pallas_call(
        _norm_matmul_kernel,
        grid=(t // tm, n // tn),
        in_specs=[pl.BlockSpec((tm, d), lambda i, j: (i, 0)),
                  pl.BlockSpec((1, d), lambda i, j: (0, 0)),
                  pl.BlockSpec((d, tn), lambda i, j: (0, j))],
        out_specs=pl.BlockSpec((tm, tn), lambda i, j: (i, j)),
        out_shape=jax.ShapeDtypeStruct((t, n), F32),
        scratch_shapes=[pltpu.VMEM((tm, d), BF16)],
        compiler_params=_cparams("parallel", "arbitrary"),
        name="norm_in_proj",
    )(x2, nw, w)


def _s5_operators(a_re, a_im, log_dt, b_re, b_im, c_re, c_im, d):
    q = S5_CHUNK
    g, p = a_re.shape
    h = SSM_GROUP
    dt = jnp.exp(log_dt)[:, None]
    lam_re, lam_im = dt * a_re, dt * a_im
    mag1 = jnp.exp(lam_re)
    abar_re, abar_im = mag1 * jnp.cos(lam_im), mag1 * jnp.sin(lam_im)
    den = a_re * a_re + a_im * a_im
    zr, zi = abar_re - 1.0, abar_im
    coef_re = (zr * a_re + zi * a_im) / den
    coef_im = (zi * a_re - zr * a_im) / den
    bb_re = coef_re[..., None] * b_re - coef_im[..., None] * b_im
    bb_im = coef_re[..., None] * b_im + coef_im[..., None] * b_re
    k = jnp.arange(q + 1, dtype=F32)[:, None, None]
    mag = jnp.exp(k * lam_re)
    pw_re, pw_im = mag * jnp.cos(k * lam_im), mag * jnp.sin(k * lam_im)
    m_re = pw_re[..., None] * bb_re - pw_im[..., None] * bb_im
    m_im = pw_re[..., None] * bb_im + pw_im[..., None] * bb_re
    kern = (jnp.einsum('ghp,kgpi->gkhi', c_re, m_re[:q], precision=HI)
            - jnp.einsum('ghp,kgpi->gkhi', c_im, m_im[:q], precision=HI))
    kern = kern.at[:, 0].add(jax.vmap(jnp.diag)(d))
    rev = np.arange(q - 1, -1, -1)
    s_end = jnp.stack([m_re[rev], m_im[rev]], axis=0).transpose(2, 1, 4, 0, 3)
    pr, pi = pw_re[1:], pw_im[1:]
    o_re = c_re[None] * pr[:, :, None, :] - c_im[None] * pi[:, :, None, :]
    o_im = -(c_re[None] * pi[:, :, None, :] + c_im[None] * pr[:, :, None, :])
    o_carry = jnp.stack([o_re, o_im], axis=0).transpose(2, 0, 4, 1, 3)
    a_q = jnp.stack([pw_re[q], pw_im[q]], axis=0)
    return kern, s_end, o_carry, a_q


def _s5_slab_weights(kern, s_end, o_carry, a_q):
    q, h, p = S5_CHUNK, SSM_GROUP, SSM_STATE
    s_dense = s_end.reshape(SSM_GROUPS, q, h, 2 * p).astype(BF16)
    k_dense = kern.transpose(0, 1, 3, 2).astype(BF16)
    o_dense = o_carry.reshape(SSM_GROUPS, 2, p, q * h).astype(BF16)
    a_re = a_q[0].reshape(N_SLABS, SLAB_GROUPS * p)
    a_im = a_q[1].reshape(N_SLABS, SLAB_GROUPS * p)
    return s_dense, k_dense, o_dense, a_re, a_im


def _chunk_rows(u_ref, nc):
    return _lane_cat([u_ref[pl.ds(s, nc, stride=S5_CHUNK), :] for s in range(S5_CHUNK)]).astype(BF16)


def _s5_state_kernel(u_ref, sd_ref, o_ref, ws_scr):
    nc = u_ref.shape[0] // S5_CHUNK
    q, h, p, sg = S5_CHUNK, SSM_GROUP, SSM_STATE, SLAB_GROUPS

    @pl.when(pl.program_id(1) == 0)
    def _():
        ws_scr[...] = jnp.zeros_like(ws_scr)
        for s in range(q):
            for a in range(sg):
                for c in range(2):
                    ws_scr[s * LANES + a * h: s * LANES + (a + 1) * h,
                           (c * sg + a) * p: (c * sg + a + 1) * p] = sd_ref[a, s, :, c * p:(c + 1) * p]

    o_ref[0] = jnp.dot(_chunk_rows(u_ref, nc), ws_scr[...], preferred_element_type=F32)


def _s5_scan_kernel(h_ref, ar_ref, ai_ref, o_ref):
    half = ar_ref.shape[1]
    ar = ar_ref[...]
    ai = ai_ref[...]

    def body(c, carry):
        hr, hi = carry
        o_ref[0, c, :, :half] = hr
        o_ref[0, c, :, half:] = hi
        xr = h_ref[0, c, :, :half]
        xi = h_ref[0, c, :, half:]
        return ar * hr - ai * hi + xr, ar * hi + ai * hr + xi

    zero = jnp.zeros(ar.shape, F32)
    lax.fori_loop(0, h_ref.shape[1], body, (zero, zero))


def _s5_out_kernel(u_ref, kd_ref, h_ref, od_ref, y_ref, kl_scr, wi_scr, wc_scr):
    nc = u_ref.shape[0] // S5_CHUNK
    q, h, p, sg = S5_CHUNK, SSM_GROUP, SSM_STATE, SLAB_GROUPS

    @pl.when(pl.program_id(1) == 0)
    def _():
        kl_scr[...] = jnp.zeros_like(kl_scr)
        for k in range(q):
            for a in range(sg):
                kl_scr[k, a * h:(a + 1) * h, a * h:(a + 1) * h] = kd_ref[a, k]
        wi_scr[...] = jnp.zeros_like(wi_scr)
        for s in range(q):
            for t in range(s, q):
                wi_scr[s * LANES:(s + 1) * LANES, t * LANES:(t + 1) * LANES] = kl_scr[t - s]
        wc_scr[...] = jnp.zeros_like(wc_scr)
        for a in range(sg):
            for c in range(2):
                for t in range(q):
                    wc_scr[(c * sg + a) * p:(c * sg + a + 1) * p,
                           t * LANES + a * h: t * LANES + (a + 1) * h] = od_ref[a, c, :, t * h:(t + 1) * h]

    u_rows = _chunk_rows(u_ref, nc)
    h_rows = h_ref[0].astype(BF16)
    pair = 2 * LANES
    for tp in range(q // 2):
        cols = slice(tp * pair, (tp + 1) * pair)
        k_used = (tp + 1) * pair
        y = (jnp.dot(u_rows[:, :k_used], wi_scr[:k_used, cols], preferred_element_type=F32)
             + jnp.dot(h_rows, wc_scr[:, cols], preferred_element_type=F32))
        for tt in range(2):
            y_ref[pl.ds(2 * tp + tt, nc, stride=q), :] = y[:, tt * LANES:(tt + 1) * LANES]


def _glu_kernel(y_ref, w_ref, b_ref, o_ref):
    z = jax.nn.gelu(y_ref[...])
    gate = jnp.dot(z.astype(BF16), w_ref[...], preferred_element_type=F32) + b_ref[...]
    o_ref[...] = (z * jax.nn.sigmoid(gate)).astype(o_ref.dtype)


def _s5_mixer(proj, bsz, L, ops, w_glu, b_glu):
    s_dense, k_dense, o_dense, a_re, a_im = _s5_slab_weights(*ops)
    q, h, p = S5_CHUNK, SSM_GROUP, SSM_STATE
    nc = L // q
    ns, sg = N_SLABS, SLAB_GROUPS
    sw = sg * 2 * p
    kq = q * LANES
    hend = pl.pallas_call(
        _s5_state_kernel,
        grid=(ns, bsz),
        in_specs=[pl.BlockSpec((L, LANES), lambda j, b: (b, j)),
                  pl.BlockSpec((sg, q, h, 2 * p), lambda j, b: (j, 0, 0, 0))],
        out_specs=pl.BlockSpec((1, nc, sw), lambda j, b: (b, 0, j)),
        out_shape=jax.ShapeDtypeStruct((bsz, nc, ns * sw), F32),
        scratch_shapes=[pltpu.VMEM((kq, sw), BF16)],
        compiler_params=_cparams("parallel", "arbitrary"),
        name="s5_chunk_state",
    )(proj, s_dense)
    hprev4 = pl.pallas_call(
        _s5_scan_kernel,
        grid=(bsz,),
        in_specs=[pl.BlockSpec((1, nc, ns, sw), lambda b: (b, 0, 0, 0)),
                  pl.BlockSpec((ns, sw // 2), lambda b: (0, 0)),
                  pl.BlockSpec((ns, sw // 2), lambda b: (0, 0))],
        out_specs=pl.BlockSpec((1, nc, ns, sw), lambda b: (b, 0, 0, 0)),
        out_shape=jax.ShapeDtypeStruct((bsz, nc, ns, sw), F32),
        compiler_params=_cparams("parallel"),
        name="s5_chunk_scan",
    )(hend.reshape(bsz, nc, ns, sw), a_re, a_im)
    y = pl.pallas_call(
        _s5_out_kernel,
        grid=(ns, bsz),
        in_specs=[pl.BlockSpec((L, LANES), lambda j, b: (b, j)),
                  pl.BlockSpec((sg, q, h, h), lambda j, b: (j, 0, 0, 0)),
                  pl.BlockSpec((1, nc, sw), lambda j, b: (b, 0, j)),
                  pl.BlockSpec((sg, 2, p, q * h), lambda j, b: (j, 0, 0, 0))],
        out_specs=pl.BlockSpec((L, LANES), lambda j, b: (b, j)),
        out_shape=jax.ShapeDtypeStruct((bsz * L, SSM_WIDTH), F32),
        scratch_shapes=[pltpu.VMEM((q, LANES, LANES), BF16), pltpu.VMEM((kq, kq), BF16), pltpu.VMEM((sw, kq), BF16)],
        compiler_params=_cparams("parallel", "arbitrary"),
        name="s5_chunk_out",
    )(proj, k_dense, hprev4.reshape(bsz, nc, ns * sw), o_dense)
    tm = 512
    return pl.pallas_call(
        _glu_kernel,
        grid=(bsz * L // tm,),
        in_specs=[pl.BlockSpec((tm, SSM_WIDTH), lambda i: (i, 0)),
                  pl.BlockSpec((SSM_WIDTH, SSM_WIDTH), lambda i: (0, 0)),
                  pl.BlockSpec((1, SSM_WIDTH), lambda i: (0, 0))],
        out_specs=pl.BlockSpec((tm, SSM_WIDTH), lambda i: (i, 0)),
        out_shape=jax.ShapeDtypeStruct((bsz * L, SSM_WIDTH), BF16),
        compiler_params=_cparams("parallel"),
        name="s5_gelu_glu",
    )(y, w_glu.astype(BF16), b_glu.reshape(1, SSM_WIDTH))


def _compress_kernel(k_ref, w1_ref, pe_ref, w1full_ref, w2_ref, o_ref, *, transpose_out):
    nc = k_ref.shape[0] // CMP_STRIDE
    prod = jnp.dot(_chunk_rows(k_ref, nc), w1_ref[...], preferred_element_type=F32)
    pe_h = jnp.dot(jnp.broadcast_to(pe_ref[...], (8, pe_ref.shape[1])), w1full_ref[...],
                   precision=HI, preferred_element_type=F32)[0:1]
    two_h = 2 * CMP_HIDDEN
    for gl in range(LANES // HEAD_DIM):
        first = prod[:, gl * two_h: gl * two_h + CMP_HIDDEN]
        second = prod[:, gl * two_h + CMP_HIDDEN: (gl + 1) * two_h]
        hid = first + pltpu.roll(second, nc - 1, axis=0) + pe_h
        out = jnp.dot(jax.nn.gelu(hid).astype(BF16), w2_ref[...], preferred_element_type=F32)
        if transpose_out:
            o_ref[0, gl] = _lane_cat([out, jnp.zeros_like(out)]).T[:HEAD_DIM].astype(BF16)
        else:
            o_ref[0, gl] = out.astype(BF16)


def _compress(proj, col, bsz, L, pe, w1, w2, transpose_out):
    nc = L // CMP_STRIDE
    half = CMP_STRIDE * HEAD_DIM
    gpl = LANES // HEAD_DIM
    w1cat = jnp.concatenate([w1[:half], w1[half:]], axis=1)
    eye = jnp.eye(gpl, dtype=F32)
    w1slab = jnp.einsum('ldh,ag->ladgh', w1cat.reshape(CMP_STRIDE, HEAD_DIM, 2 * CMP_HIDDEN), eye)
    w1slab = w1slab.reshape(CMP_STRIDE * LANES, gpl * 2 * CMP_HIDDEN).astype(BF16)
    out_block = (1, gpl, HEAD_DIM, nc) if transpose_out else (1, gpl, nc, HEAD_DIM)
    out_full = (bsz, KV_HEADS) + out_block[2:]
    return pl.pallas_call(
        functools.partial(_compress_kernel, transpose_out=transpose_out),
        grid=(bsz, KV_HEADS // gpl),
        in_specs=[pl.BlockSpec((L, LANES), lambda b, j: (b, col // LANES + j)),
                  pl.BlockSpec(w1slab.shape, lambda b, j: (0, 0)),
                  pl.BlockSpec((1, 2 * half), lambda b, j: (0, 0)),
                  pl.BlockSpec((2 * half, CMP_HIDDEN), lambda b, j: (0, 0)),
                  pl.BlockSpec((CMP_HIDDEN, HEAD_DIM), lambda b, j: (0, 0))],
        out_specs=pl.BlockSpec(out_block, lambda b, j: (b, j, 0, 0)),
        out_shape=jax.ShapeDtypeStruct(out_full, BF16),
        compiler_params=_cparams("parallel", "parallel"),
        name="nsa_compress",
    )(proj, w1slab, pe.reshape(1, 2 * half), w1, w2.astype(BF16))


KS_COLS = 2 * LANES
KW_COLS = LANES
KS_FLAG = HEAD_DIM + N_SLC_PAD
KW_FLAG = HEAD_DIM
PAD_TILES = 4
V_ROWS = HEAD_DIM + 16
V_ONES = HEAD_DIM


def _kv_tiles_kernel(ks_ref, vs_ref, kw_ref, vw_ref, kso_ref, vso_ref, kwo_ref, vwo_ref):
    tm = ks_ref.shape[0]
    step = pl.program_id(1)
    row0 = step * tm
    kso_ref[:, :, :, :, KS_FLAG:] = jnp.zeros(kso_ref.shape[:4] + (KS_COLS - KS_FLAG,), BF16)
    kwo_ref[:, :, :, :, KW_FLAG:] = jnp.zeros(kwo_ref.shape[:4] + (KW_COLS - KW_FLAG,), BF16)
    is_real = step < pl.num_programs(1) - 1
    tail_rows = lax.broadcasted_iota(jnp.int32, vso_ref.shape[:3] + (V_ROWS - HEAD_DIM, QT), 3)
    v_tail = jnp.where(tail_rows == 0, jnp.where(is_real, 1.0, 0.0), 0.0).astype(BF16)
    vso_ref[:, :, :, HEAD_DIM:, :] = v_tail
    vwo_ref[:, :, :, HEAD_DIM:, :] = v_tail

    @pl.when(step < pl.num_programs(1) - 1)
    def _():
        lane_blk = lax.broadcasted_iota(jnp.int32, (QT, N_SLC_PAD), 1)
        for k in range(tm // QT):
            rows = slice(k * QT, (k + 1) * QT)
            tok = lax.broadcasted_iota(jnp.int32, (QT, N_SLC_PAD), 0) + (row0 + k * QT)
            onehot = jnp.where((tok >> int(math.log2(SLC_BLOCK))) == lane_blk, 1.0, 0.0).astype(BF16)
            vs_t = vs_ref[rows, :].T
            vw_t = vw_ref[rows, :].T
            for g in range(KV_HEADS):
                cols = slice(g * HEAD_DIM, (g + 1) * HEAD_DIM)
                kso_ref[0, g, k, :, :HEAD_DIM] = ks_ref[rows, cols].astype(BF16)
                kso_ref[0, g, k, :, HEAD_DIM:KS_FLAG] = onehot
                kwo_ref[0, g, k, :, :HEAD_DIM] = kw_ref[rows, cols].astype(BF16)
                vso_ref[0, g, k, :HEAD_DIM, :] = vs_t[cols].astype(BF16)
                vwo_ref[0, g, k, :HEAD_DIM, :] = vw_t[cols].astype(BF16)

    @pl.when(step == pl.num_programs(1) - 1)
    def _():
        ones = jnp.ones(kso_ref.shape[:4] + (1,), BF16)
        kso_ref[:, :, :, :, :KS_FLAG] = jnp.zeros(kso_ref.shape[:4] + (KS_FLAG,), BF16)
        kwo_ref[:, :, :, :, :KW_FLAG] = jnp.zeros(kwo_ref.shape[:4] + (KW_FLAG,), BF16)
        kso_ref[:, :, :, :, KS_FLAG:KS_FLAG + 1] = ones
        kwo_ref[:, :, :, :, KW_FLAG:KW_FLAG + 1] = ones
        vso_ref[:, :, :, :HEAD_DIM, :] = jnp.zeros(vso_ref.shape[:3] + (HEAD_DIM, QT), BF16)
        vwo_ref[:, :, :, :HEAD_DIM, :] = jnp.zeros(vwo_ref.shape[:3] + (HEAD_DIM, QT), BF16)


def _kv_tiles(proj, bsz, L):
    nt = L // QT
    g = KV_HEADS
    kt = PAD_TILES
    tm = kt * QT
    steps = L // tm
    in_spec = lambda col: pl.BlockSpec((tm, KV_WIDTH),
                                       lambda b, i: (b * steps + jnp.minimum(i, steps - 1), col // KV_WIDTH))
    out_spec = lambda a, c: pl.BlockSpec((1, g, kt, a, c), lambda b, i: (b, 0, i, 0, 0))
    shape = lambda a, c: jax.ShapeDtypeStruct((bsz, g, nt + kt, a, c), BF16)
    return pl.pallas_call(
        _kv_tiles_kernel,
        grid=(bsz, steps + 1),
        in_specs=[in_spec(COL_KS), in_spec(COL_VS), in_spec(COL_KW), in_spec(COL_VW)],
        out_specs=[out_spec(QT, KS_COLS), out_spec(V_ROWS, QT), out_spec(QT, KW_COLS), out_spec(V_ROWS, QT)],
        out_shape=[shape(QT, KS_COLS), shape(V_ROWS, QT), shape(QT, KW_COLS), shape(V_ROWS, QT)],
        compiler_params=_cparams("parallel", "arbitrary"),
        name="nsa_kv_tiles",
    )(proj, proj, proj, proj)


def _rel_bucket(dist):
    dist = jnp.maximum(dist, 0)
    max_exact = REL_BUCKETS // 2
    large = max_exact + (jnp.log(jnp.maximum(dist, 1).astype(F32) / max_exact)
                         / math.log(REL_MAX_DIST / max_exact) * (REL_BUCKETS - max_exact)).astype(jnp.int32)
    large = jnp.minimum(large, REL_BUCKETS - 1)
    return jnp.where(dist < max_exact, dist, large)


def _toeplitz_rows(base, nrows, step):
    w = base.shape[1]
    x = jnp.broadcast_to(base, (nrows, w))
    k_ix = lax.broadcasted_iota(jnp.int32, (nrows, w), 0)
    bit = 1
    while bit < nrows:
        x = jnp.where((k_ix & bit) != 0, pltpu.roll(x, (bit * step) % w, axis=1), x)
        bit *= 2
    return x


def _bias_tables_kernel(bw_ref, bc_ref, far_ref, wb_ref, nb_ref, strip_ref, *, ncp):
    slab = WINDOW + QT
    y = _toeplitz_rows(bw_ref[0], QT, 1)
    for kh in range(slab // QT):
        tile = y[:, WINDOW - kh * QT: WINDOW - kh * QT + QT]
        wb_ref[0, kh * QT:(kh + 1) * QT, :] = tile
        if kh >= slab // QT - 2:
            row = kh - (slab // QT - 2)
            nb_ref[0, row * QT:(row + 1) * QT, :] = tile - far_ref[0]
    band = 2 * CMP_STRIDE
    z = _toeplitz_rows(bc_ref[0], band, CMP_STRIDE)[:, 2 * QT: 3 * QT]
    strip_ref[0, :ncp - CMP_STRIDE, :] = jnp.broadcast_to(far_ref[0], (ncp - CMP_STRIDE, QT))
    strip_ref[0, ncp - CMP_STRIDE: ncp + CMP_STRIDE, :] = z
    strip_ref[0, ncp + CMP_STRIDE:, :] = jnp.full((ncp - CMP_STRIDE, QT), NEG, F32)


def _nsa_tables(rel_table, L):
    ncp = L // CMP_STRIDE
    slab = WINDOW + QT
    dist = jnp.arange(TABLE_W)
    rel_table = rel_table * LOG2E
    per_dist = rel_table[_rel_bucket(dist)].T
    base_w = jnp.where(dist < WINDOW, per_dist, NEG)
    shift = CMP_BLOCK - 1
    shifted = jnp.concatenate([jnp.full((NSA_HEADS, shift), NEG, F32), per_dist[:, :TABLE_W - shift]], axis=1)
    base_c = jnp.where(dist < TABLE_W // 2, shifted, NEG)
    far = jnp.broadcast_to(rel_table[REL_BUCKETS - 1][:, None, None], (NSA_HEADS, 1, QT))
    wb, nb, strip = pl.pallas_call(
        functools.partial(_bias_tables_kernel, ncp=ncp),
        grid=(NSA_HEADS,),
        in_specs=[pl.BlockSpec((1, 1, TABLE_W), lambda h: (h, 0, 0)),
                  pl.BlockSpec((1, 1, TABLE_W), lambda h: (h, 0, 0)),
                  pl.BlockSpec((1, 1, QT), lambda h: (h, 0, 0))],
        out_specs=[pl.BlockSpec((1, slab, QT), lambda h: (h, 0, 0)),
                   pl.BlockSpec((1, 2 * QT, QT), lambda h: (h, 0, 0)),
                   pl.BlockSpec((1, 2 * ncp, QT), lambda h: (h, 0, 0))],
        out_shape=[jax.ShapeDtypeStruct((NSA_HEADS, slab, QT), F32),
                   jax.ShapeDtypeStruct((NSA_HEADS, 2 * QT, QT), F32),
                   jax.ShapeDtypeStruct((NSA_HEADS, 2 * ncp, QT), F32)],
        compiler_params=_cparams("parallel"),
        name="nsa_bias_tables",
    )(base_w.reshape(NSA_HEADS, 1, TABLE_W), base_c.reshape(NSA_HEADS, 1, TABLE_W), far)
    g, r = KV_HEADS, Q_PER_KV
    return wb.reshape(g, r, slab, QT), nb.reshape(g, r, 2 * QT, QT), strip.reshape(g, r, 2 * ncp, QT)


def _overlap_t(L):
    n_cmp = (L - CMP_BLOCK) // CMP_STRIDE + 1
    n_slc = L // SLC_BLOCK
    cmp_idx = np.arange(n_cmp)[:, None] * CMP_STRIDE + np.arange(CMP_BLOCK)[None, :]
    overlap = ((cmp_idx[:, :, None] // SLC_BLOCK) == np.arange(n_slc)[None, None, :]).sum(1) / CMP_BLOCK
    out = np.zeros((N_SLC_PAD, L // CMP_STRIDE), np.float32)
    out[:n_slc, :n_cmp] = overlap.T
    return jnp.asarray(out, dtype=BF16)


def _split3(x):
    hi = x.astype(BF16)
    r1 = x - hi.astype(F32)
    mid = r1.astype(BF16)
    lo = (r1 - mid.astype(F32)).astype(BF16)
    return hi, mid, lo


def _nsa_kernel(q_ref, gl_ref, kc_ref, vcT_ref, strip_ref, ovT_ref, ks_ref, vsT_ref, kw_ref, vwT_ref, wb_ref,
                nb_ref, o_ref, gate_ref, sa_scr, sb_scr, *, n_slc):
    i = pl.program_id(2)
    R = Q_PER_KV
    RQ = R * QT
    RW = R * HEAD_DIM
    groups = range(kc_ref.shape[1])
    first_group = pl.program_id(1) * len(groups)
    pad_tile = ks_ref.shape[2] - PAD_TILES
    t_pos = lax.broadcasted_iota(jnp.int32, (1, QT), 1) + i * QT

    def flag_rows(n):
        return jnp.where(lax.broadcasted_iota(jnp.int32, (n, QT), 0) == 0, NEG, 0.0).astype(BF16)

    q_heads = []
    for gg in groups:
        qT = (q_ref[:, gg * RW:(gg + 1) * RW].T * (0.125 * LOG2E)).astype(BF16)
        q_heads.append([qT[r * HEAD_DIM:(r + 1) * HEAD_DIM] for r in range(R)])

    slab = WINDOW + QT
    n_win = slab // QT
    win_idx = [jnp.where(i + jj >= n_win - 1, i + jj - (n_win - 1), pad_tile) for jj in range(n_win)]

    def window_scores(gg):
        kw = jnp.concatenate([kw_ref[0, gg, j] for j in win_idx], axis=0)
        qwin = _lane_cat([jnp.concatenate([q_heads[gg][r], flag_rows(KW_COLS - HEAD_DIM)], axis=0)
                          for r in range(R)])
        wb = _lane_cat([wb_ref[gg, r] for r in range(R)])
        return jnp.dot(kw, qwin, preferred_element_type=F32) + wb

    def window_branch(gg, sw):
        m_w = jnp.max(sw, axis=0, keepdims=True)
        e_wb = jnp.exp2(sw - m_w).astype(BF16)
        acc = None
        for jj, j in enumerate(win_idx):
            term = jnp.dot(vwT_ref[0, gg, j], e_wb[jj * QT:(jj + 1) * QT], preferred_element_type=F32)
            acc = term if acc is None else acc + term
        return acc[:HEAD_DIM] * (1.0 / acc[V_ONES:V_ONES + 1])

    sw_all = [window_scores(gg) for gg in groups]

    ncp = kc_ref.shape[2]
    strip_row = pl.multiple_of(ncp - i * (QT // CMP_STRIDE), QT // CMP_STRIDE)
    ov = ovT_ref[...]
    s_ix = lax.broadcasted_iota(jnp.int32, (N_SLC_PAD, QT), 0)
    cur = t_pos >> int(math.log2(SLC_BLOCK))
    forced = (s_ix == 0) | (s_ix == cur) | (s_ix == cur - 1)

    def compressed_scores(gg):
        sc = jnp.dot(kc_ref[0, gg], _lane_cat(q_heads[gg]), preferred_element_type=F32)
        return sc + _lane_cat([strip_ref[gg, r, pl.ds(strip_row, ncp), :] for r in range(R)])

    def compressed_branch(gg, sm):
        m_c = jnp.max(sm, axis=0, keepdims=True)
        e_c = jnp.exp2(sm - m_c)
        l_c = jnp.sum(e_c, axis=0, keepdims=True)
        inv_c = jnp.where(m_c > 0.5 * NEG, 1.0 / l_c, 0.0)
        o_cmp = jnp.dot(vcT_ref[0, gg], e_c.astype(BF16), preferred_element_type=F32) * inv_c
        p_c = e_c * inv_c
        p_sum = p_c[:, 0:QT]
        for r in range(1, R):
            p_sum = p_sum + p_c[:, r * QT:(r + 1) * QT]
        imp = None
        for part in _split3(p_sum):
            term = jnp.dot(ov, part, preferred_element_type=F32)
            imp = term if imp is None else imp + term
        score = jnp.where(forced, FORCED_SCORE, jnp.where(s_ix <= cur, imp, -1.0))
        return o_cmp, jnp.where(s_ix < n_slc, score, -2.0)

    sm_all = [compressed_scores(gg) for gg in groups]
    o_win = [window_branch(gg, sw_all[gg]) for gg in groups]
    cmp_out = [compressed_branch(gg, sm_all[gg]) for gg in groups]
    o_cmp = [c[0] for c in cmp_out]
    scores = [c[1] for c in cmp_out]

    SUB = 8
    n_sub = N_SLC_PAD // SUB
    score_rows = [[score[v * SUB:(v + 1) * SUB] for v in range(n_sub)] for score in scores]
    sub_ix = lax.broadcasted_iota(jnp.int32, (SUB, QT), 0)

    def rank_segment(seg, ranks):
        out = []
        for gg in groups:
            rk = list(ranks[gg])
            for sp in range(seg * RANK_SEG, min((seg + 1) * RANK_SEG, n_slc)):
                row = scores[gg][sp:sp + 1, :]
                for v, blk in enumerate(score_rows[gg]):
                    if v * SUB > sp:
                        beats = row >= blk
                    elif v * SUB + SUB - 1 <= sp:
                        beats = row > blk
                    else:
                        beats = (row > blk) | ((row == blk) & (sub_ix > sp - v * SUB))
                    rk[v] = rk[v] + jnp.where(beats, 1.0, 0.0)
            out.append(tuple(rk))
        return tuple(out)

    ranks = rank_segment(0, tuple(tuple(jnp.zeros((SUB, QT), F32) for _ in range(n_sub)) for _ in groups))
    last_block = (i + 1) * (QT // SLC_BLOCK) - 1
    for seg in range(1, -(-n_slc // RANK_SEG)):
        ranks = lax.cond(last_block >= seg * RANK_SEG, functools.partial(rank_segment, seg), lambda rk: rk, ranks)

    qsel = []
    for gg in groups:
        sel_neg = jnp.where(jnp.concatenate(ranks[gg], axis=0) < float(SLC_TOPK), 0.0, UNSELECTED).astype(BF16)
        qsel.append(_lane_cat([jnp.concatenate([q_heads[gg][r], sel_neg, flag_rows(KS_COLS - KS_FLAG)], axis=0)
                               for r in range(R)]))

    n_far = jnp.maximum(i - 1, 0)

    def pair_tiles(j):
        return [jnp.where(j + n < n_far, j + n, pad_tile) for n in range(2)]

    def pair_scores(gg, tiles):
        k2 = jnp.concatenate([ks_ref[0, gg, t] for t in tiles], axis=0)
        return jnp.dot(k2, qsel[gg], preferred_element_type=F32)

    def online_step(gg, carry, s, tiles):
        m, acc = carry
        m_new = jnp.maximum(m, jnp.max(s, axis=0, keepdims=True))
        p = jnp.exp2(s - m_new).astype(BF16)
        acc = jnp.exp2(m - m_new) * acc
        for n, t in enumerate(tiles):
            acc = acc + jnp.dot(vsT_ref[0, gg, t], p[n * QT:(n + 1) * QT], preferred_element_type=F32)
        return m_new, acc

    def far_trip(u, carries):
        j = 4 * u
        for gg in groups:
            sb_scr[gg] = pair_scores(gg, pair_tiles(j + 2))
        carries = [online_step(gg, carries[gg], sa_scr[gg], pair_tiles(j)) for gg in groups]
        for gg in groups:
            sa_scr[gg] = pair_scores(gg, pair_tiles(j + 4))
        return tuple(online_step(gg, carries[gg], sb_scr[gg], pair_tiles(j + 2)) for gg in groups)

    for gg in groups:
        sa_scr[gg] = pair_scores(gg, pair_tiles(0))
    init = (jnp.full((1, RQ), NEG, F32), jnp.zeros((V_ROWS, RQ), F32))
    carries = lax.fori_loop(0, (n_far + 3) // 4, far_trip, tuple(init for _ in groups))
    near = [jnp.where(i >= 1, i - 1, pad_tile), i]

    gate_ref[...] = jax.nn.sigmoid(gl_ref[...].T)
    s_near = [pair_scores(gg, near) + _lane_cat([nb_ref[gg, r] for r in range(R)]) for gg in groups]
    for gg in groups:
        _, acc_s = online_step(gg, carries[gg], s_near[gg], near)
        o_slc = acc_s[:HEAD_DIM] * (1.0 / acc_s[V_ONES:V_ONES + 1])
        head0 = (first_group + gg) * R
        gates = [_lane_cat([gate_ref[pl.ds((head0 + r) * 3 + br, 1), :] for r in range(R)]) for br in range(3)]
        total = gates[0] * o_cmp[gg] + gates[1] * o_slc + gates[2] * o_win[gg]
        o_ref[:, gg * RW:(gg + 1) * RW] = jnp.concatenate(
            [total[:, r * QT:(r + 1) * QT] for r in range(R)], axis=0).T.astype(o_ref.dtype)


def _nsa_mixer(proj, bsz, L, pe_ck, w_ck1, w_ck2, pe_cv, w_cv1, w_cv2, rel_table):
    g, r = KV_HEADS, Q_PER_KV
    ng = NSA_GROUPS
    ni = L // QT
    nt = L // QT + PAD_TILES
    ncp = L // CMP_STRIDE
    n_slc = L // SLC_BLOCK
    slab = WINDOW + QT
    k_cmp = _compress(proj, COL_KC, bsz, L, pe_ck, w_ck1, w_ck2, transpose_out=False)
    v_cmp_t = _compress(proj, COL_VC, bsz, L, pe_cv, w_cv1, w_cv2, transpose_out=True)
    ks_t, vs_t, kw_t, vw_t = _kv_tiles(proj, bsz, L)
    wb, nb, strip = _nsa_tables(rel_table, L)
    qw = ng * r * HEAD_DIM
    once = pl.Buffered(1)
    whole = lambda *shape: pl.BlockSpec((1, ng) + shape, lambda b, h, i: (b, h) + (0,) * len(shape), once)
    per_group = lambda *shape: pl.BlockSpec((ng,) + shape, lambda b, h, i: (h,) + (0,) * len(shape), once)
    return pl.pallas_call(
        functools.partial(_nsa_kernel, n_slc=n_slc),
        grid=(bsz, g // ng, ni),
        in_specs=[
            pl.BlockSpec((QT, qw), lambda b, h, i: (b * ni + i, COL_Q // qw + h)),
            pl.BlockSpec((QT, LANES), lambda b, h, i: (b * ni + i, COL_GATE // LANES)),
            whole(ncp, HEAD_DIM),
            whole(HEAD_DIM, ncp),
            per_group(r, 2 * ncp, QT),
            pl.BlockSpec((N_SLC_PAD, ncp), lambda b, h, i: (0, 0), once),
            whole(nt, QT, KS_COLS),
            whole(nt, V_ROWS, QT),
            whole(nt, QT, KW_COLS),
            whole(nt, V_ROWS, QT),
            per_group(r, slab, QT),
            per_group(r, 2 * QT, QT),
        ],
        out_specs=pl.BlockSpec((QT, qw), lambda b, h, i: (b * ni + i, h)),
        out_shape=jax.ShapeDtypeStruct((bsz * L, NSA_WIDTH), BF16),
        scratch_shapes=[pltpu.VMEM((LANES, QT), F32), pltpu.VMEM((ng, 2 * QT, r * QT), F32),
                        pltpu.VMEM((ng, 2 * QT, r * QT), F32)],
        compiler_params=_cparams("parallel", "parallel", "arbitrary"),
        name="nsa_attention",
    )(proj, proj, k_cmp, v_cmp_t, strip, _overlap_t(L), ks_t, vs_t, kw_t, vw_t, wb, nb)


def _out_proj_kernel(x_ref, ys_ref, yn_ref, w_ref, o_ref):
    half = ys_ref.shape[1]
    acc = jnp.dot(ys_ref[...], w_ref[:half, :], preferred_element_type=F32)
    acc = acc + jnp.dot(yn_ref[...], w_ref[half:, :], preferred_element_type=F32)
    o_ref[...] = x_ref[...] + acc


def _out_proj(x2, y_ssm, y_nsa, w, tm=512):
    t, d = x2.shape
    return pl.pallas_call(
        _out_proj_kernel,
        grid=(t // tm,),
        in_specs=[pl.BlockSpec((tm, d), lambda i: (i, 0)),
                  pl.BlockSpec((tm, SSM_WIDTH), lambda i: (i, 0)),
                  pl.BlockSpec((tm, NSA_WIDTH), lambda i: (i, 0)),
                  pl.BlockSpec((SSM_WIDTH + NSA_WIDTH, d), lambda i: (0, 0), pl.Buffered(1))],
        out_specs=pl.BlockSpec((tm, d), lambda i: (i, 0)),
        out_shape=jax.ShapeDtypeStruct((t, d), F32),
        compiler_params=_cparams("parallel"),
        name="out_proj",
    )(x2, y_ssm, y_nsa, w)


def _mlp_kernel(x_ref, n2_ref, wu_ref, wd_ref, nf_ref, o_ref, h_ref, acc_ref):
    f = pl.program_id(1)

    @pl.when(f == 0)
    def _():
        h_ref[...] = _rms(x_ref[...], n2_ref[...]).astype(BF16)
        acc_ref[...] = jnp.zeros_like(acc_ref)

    a = jnp.dot(h_ref[...], wu_ref[...], preferred_element_type=F32)
    a = jnp.square(jnp.maximum(a, 0.0))
    acc_ref[...] += jnp.dot(a.astype(BF16), wd_ref[...], preferred_element_type=F32)

    @pl.when(f == pl.num_programs(1) - 1)
    def _():
        o_ref[...] = _rms(x_ref[...] + acc_ref[...], nf_ref[...])


def _mlp(x2, n2, wu, wd, nf, tm=512, tf=1024):
    t, d = x2.shape
    ff = wu.shape[1]
    return pl.pallas_call(
        _mlp_kernel,
        grid=(t // tm, ff // tf),
        in_specs=[pl.BlockSpec((tm, d), lambda i, f: (i, 0)),
                  pl.BlockSpec((1, d), lambda i, f: (0, 0)),
                  pl.BlockSpec((d, tf), lambda i, f: (0, f)),
                  pl.BlockSpec((tf, d), lambda i, f: (f, 0)),
                  pl.BlockSpec((1, d), lambda i, f: (0, 0))],
        out_specs=pl.BlockSpec((tm, d), lambda i, f: (i, 0)),
        out_shape=jax.ShapeDtypeStruct((t, d), F32),
        scratch_shapes=[pltpu.VMEM((tm, d), BF16), pltpu.VMEM((tm, d), F32)],
        compiler_params=_cparams("parallel", "arbitrary"),
        name="mlp_final_norm",
    )(x2, n2, wu, wd, nf)


def kernel(x, norm1_w, w_in, ssm_a_re, ssm_a_im, ssm_log_dt, ssm_b_re, ssm_b_im, ssm_c_re, ssm_c_im, ssm_d,
           w_glu, b_glu, pe_ck, w_ck1, w_ck2, pe_cv, w_cv1, w_cv2, w_out, norm2_w, w_up, w_down, rel_table,
           norm_f_w):
    bsz, L, d = x.shape
    assert w_in.shape[0] == 1, "the closing rmsnorm is fused into the (single) layer's MLP kernel"
    x2 = x.reshape(bsz * L, d)
    w_in_p = jnp.pad(w_in[0].astype(BF16), ((0, 0), (0, D_IN_PAD - D_IN)))
    proj = _norm_matmul(x2, norm1_w[0].reshape(1, d), w_in_p)
    ops = _s5_operators(ssm_a_re[0], ssm_a_im[0], ssm_log_dt[0], ssm_b_re[0], ssm_b_im[0], ssm_c_re[0], ssm_c_im[0],
                        ssm_d[0])
    y_ssm = _s5_mixer(proj, bsz, L, ops, w_glu[0], b_glu[0])
    y_nsa = _nsa_mixer(proj, bsz, L, pe_ck[0], w_ck1[0], w_ck2[0], pe_cv[0], w_cv1[0], w_cv2[0], rel_table)
    x2 = _out_proj(x2, y_ssm, y_nsa, w_out[0].astype(BF16))
    x2 = _mlp(x2, norm2_w[0].reshape(1, d), w_up[0].astype(BF16), w_down[0].astype(BF16), norm_f_w.reshape(1, d))
    return x2.reshape(bsz, L, d)
```

```python
import functools
import math

import numpy as np
import jax
import jax.numpy as jnp
from jax import lax
from jax.experimental import pallas as pl
from jax.experimental.pallas import tpu as pltpu

F32 = jnp.float32
BF16 = jnp.bfloat16

D_MODEL = 2048
SSM_WIDTH = 1024
SSM_GROUP = 16
SSM_GROUPS = 64
SSM_STATE = 64
NSA_WIDTH = 1024
HEAD_DIM = 64
NSA_HEADS = 16
KV_HEADS = 4
Q_PER_KV = 4
KV_WIDTH = 256
CMP_BLOCK = 32
CMP_STRIDE = 16
CMP_HIDDEN = 256
SLC_BLOCK = 64
SLC_TOPK = 16
WINDOW = 512
REL_BUCKETS = 32
REL_MAX_DIST = 128
D_FF = 8192
EPS = 1e-6
NEG = -1e30
FORCED_SCORE = 1e4
D_IN = SSM_WIDTH + NSA_WIDTH + 6 * KV_WIDTH + 3 * NSA_HEADS
D_IN_PAD = 3712
COL_Q = SSM_WIDTH
COL_KC = COL_Q + NSA_WIDTH
COL_VC = COL_KC + KV_WIDTH
COL_KS = COL_VC + KV_WIDTH
COL_VS = COL_KS + KV_WIDTH
COL_KW = COL_VS + KV_WIDTH
COL_VW = COL_KW + KV_WIDTH
COL_GATE = COL_VW + KV_WIDTH

LANES = 128
S5_CHUNK = 16
SLAB_GROUPS = LANES // SSM_GROUP
N_SLABS = SSM_GROUPS // SLAB_GROUPS
QT = 128
N_SLC_PAD = 64
RANK_SEG = 16
NSA_GROUPS = 4
UNSELECTED = -1e9
LOG2E = math.log2(math.e)
TABLE_W = 1024
VMEM_LIMIT = 56 * 1024 * 1024
HI = lax.Precision.HIGHEST


def _cparams(*sem):
    return pltpu.CompilerParams(dimension_semantics=sem, vmem_limit_bytes=VMEM_LIMIT)


def _rms(x, w):
    ms = jnp.mean(x * x, axis=-1, keepdims=True)
    return x * lax.rsqrt(ms + EPS) * w


def _lane_cat(parts):
    return jnp.concatenate(parts, axis=1)


def _norm_matmul_kernel(x_ref, nw_ref, w_ref, o_ref):
    h = _rms(x_ref[...], nw_ref[...]).astype(BF16)
    o_ref[...] = jnp.dot(h, w_ref[...], preferred_element_type=F32)


def _norm_matmul(x2, nw, w, tm=512):
    t, d = x2.shape
    n = w.shape[1]
    return pl.pallas_call(
        _norm_matmul_kernel,
        grid=(t // tm,),
        in_specs=[pl.BlockSpec((tm, d), lambda i: (i, 0)),
                  pl.BlockSpec((1, d), lambda i: (0, 0)),
                  pl.BlockSpec((d, n), lambda i: (0, 0), pl.Buffered(1))],
        out_specs=pl.BlockSpec((tm, n), lambda i: (i, 0)),
        out_shape=jax.ShapeDtypeStruct((t, n), F32),
        compiler_params=_cparams("parallel"),
        name="norm_in_proj",
    )(x2, nw, w)


def _s5_operators(a_re, a_im, log_dt, b_re, b_im, c_re, c_im, d):
    q = S5_CHUNK
    g, p = a_re.shape
    h = SSM_GROUP
    dt = jnp.exp(log_dt)[:, None]
    lam_re, lam_im = dt * a_re, dt * a_im
    mag1 = jnp.exp(lam_re)
    abar_re, abar_im = mag1 * jnp.cos(lam_im), mag1 * jnp.sin(lam_im)
    den = a_re * a_re + a_im * a_im
    zr, zi = abar_re - 1.0, abar_im
    coef_re = (zr * a_re + zi * a_im) / den
    coef_im = (zi * a_re - zr * a_im) / den
    bb_re = coef_re[..., None] * b_re - coef_im[..., None] * b_im
    bb_im = coef_re[..., None] * b_im + coef_im[..., None] * b_re
    k = jnp.arange(q + 1, dtype=F32)[:, None, None]
    mag = jnp.exp(k * lam_re)
    pw_re, pw_im = mag * jnp.cos(k * lam_im), mag * jnp.sin(k * lam_im)
    m_re = pw_re[..., None] * bb_re - pw_im[..., None] * bb_im
    m_im = pw_re[..., None] * bb_im + pw_im[..., None] * bb_re
    kern = (jnp.einsum('ghp,kgpi->gkhi', c_re, m_re[:q], precision=HI)
            - jnp.einsum('ghp,kgpi->gkhi', c_im, m_im[:q], precision=HI))
    kern = kern.at[:, 0].add(jax.vmap(jnp.diag)(d))
    rev = np.arange(q - 1, -1, -1)
    s_end = jnp.stack([m_re[rev], m_im[rev]], axis=0).transpose(2, 1, 4, 0, 3)
    pr, pi = pw_re[1:], pw_im[1:]
    o_re = c_re[None] * pr[:, :, None, :] - c_im[None] * pi[:, :, None, :]
    o_im = -(c_re[None] * pi[:, :, None, :] + c_im[None] * pr[:, :, None, :])
    o_carry = jnp.stack([o_re, o_im], axis=0).transpose(2, 0, 4, 1, 3)
    a_q = jnp.stack([pw_re[q], pw_im[q]], axis=0)
    return kern, s_end, o_carry, a_q


def _s5_slab_weights(kern, s_end, o_carry, a_q):
    q, h, p = S5_CHUNK, SSM_GROUP, SSM_STATE
    s_dense = s_end.reshape(SSM_GROUPS, q, h, 2 * p).astype(BF16)
    k_dense = kern.transpose(0, 1, 3, 2).astype(BF16)
    o_dense = o_carry.reshape(SSM_GROUPS, 2, p, q * h).astype(BF16)
    a_re = a_q[0].reshape(N_SLABS, SLAB_GROUPS * p)
    a_im = a_q[1].reshape(N_SLABS, SLAB_GROUPS * p)
    return s_dense, k_dense, o_dense, a_re, a_im


def _chunk_rows(u_ref, nc):
    return _lane_cat([u_ref[pl.ds(s, nc, stride=S5_CHUNK), :] for s in range(S5_CHUNK)]).astype(BF16)


def _s5_state_kernel(u_ref, sd_ref, o_ref, ws_scr):
    nc = u_ref.shape[0] // S5_CHUNK
    q, h, p, sg = S5_CHUNK, SSM_GROUP, SSM_STATE, SLAB_GROUPS

    @pl.when(pl.program_id(1) == 0)
    def _():
        ws_scr[...] = jnp.zeros_like(ws_scr)
        for s in range(q):
            for a in range(sg):
                for c in range(2):
                    ws_scr[s * LANES + a * h: s * LANES + (a + 1) * h,
                           (c * sg + a) * p: (c * sg + a + 1) * p] = sd_ref[a, s, :, c * p:(c + 1) * p]

    o_ref[0] = jnp.dot(_chunk_rows(u_ref, nc), ws_scr[...], preferred_element_type=F32)


def _s5_scan_kernel(h_ref, ar_ref, ai_ref, o_ref):
    half = ar_ref.shape[1]
    ar = ar_ref[...]
    ai = ai_ref[...]

    def body(c, carry):
        hr, hi = carry
        o_ref[0, c, :, :half] = hr
        o_ref[0, c, :, half:] = hi
        xr = h_ref[0, c, :, :half]
        xi = h_ref[0, c, :, half:]
        return ar * hr - ai * hi + xr, ar * hi + ai * hr + xi

    zero = jnp.zeros(ar.shape, F32)
    lax.fori_loop(0, h_ref.shape[1], body, (zero, zero))


def _s5_out_kernel(u_ref, kd_ref, h_ref, od_ref, y_ref, kl_scr, wi_scr, wc_scr):
    nc = u_ref.shape[0] // S5_CHUNK
    q, h, p, sg = S5_CHUNK, SSM_GROUP, SSM_STATE, SLAB_GROUPS

    @pl.when(pl.program_id(1) == 0)
    def _():
        kl_scr[...] = jnp.zeros_like(kl_scr)
        for k in range(q):
            for a in range(sg):
                kl_scr[k, a * h:(a + 1) * h, a * h:(a + 1) * h] = kd_ref[a, k]
        wi_scr[...] = jnp.zeros_like(wi_scr)
        for s in range(q):
            for t in range(s, q):
                wi_scr[s * LANES:(s + 1) * LANES, t * LANES:(t + 1) * LANES] = kl_scr[t - s]
        wc_scr[...] = jnp.zeros_like(wc_scr)
        for a in range(sg):
            for c in range(2):
                for t in range(q):
                    wc_scr[(c * sg + a) * p:(c * sg + a + 1) * p,
                           t * LANES + a * h: t * LANES + (a + 1) * h] = od_ref[a, c, :, t * h:(t + 1) * h]

    u_rows = _chunk_rows(u_ref, nc)
    h_rows = h_ref[0].astype(BF16)
    pair = 2 * LANES
    for tp in range(q // 2):
        cols = slice(tp * pair, (tp + 1) * pair)
        k_used = (tp + 1) * pair
        y = (jnp.dot(u_rows[:, :k_used], wi_scr[:k_used, cols], preferred_element_type=F32)
             + jnp.dot(h_rows, wc_scr[:, cols], preferred_element_type=F32))
        for tt in range(2):
            y_ref[pl.ds(2 * tp + tt, nc, stride=q), :] = y[:, tt * LANES:(tt + 1) * LANES]


def _glu_kernel(y_ref, w_ref, b_ref, o_ref):
    z = jax.nn.gelu(y_ref[...])
    gate = jnp.dot(z.astype(BF16), w_ref[...], preferred_element_type=F32) + b_ref[...]
    o_ref[...] = (z * jax.nn.sigmoid(gate)).astype(o_ref.dtype)


def _s5_mixer(proj, bsz, L, ops, w_glu, b_glu):
    s_dense, k_dense, o_dense, a_re, a_im = _s5_slab_weights(*ops)
    q, h, p = S5_CHUNK, SSM_GROUP, SSM_STATE
    nc = L // q
    ns, sg = N_SLABS, SLAB_GROUPS
    sw = sg * 2 * p
    kq = q * LANES
    hend = pl.pallas_call(
        _s5_state_kernel,
        grid=(ns, bsz),
        in_specs=[pl.BlockSpec((L, LANES), lambda j, b: (b, j)),
                  pl.BlockSpec((sg, q, h, 2 * p), lambda j, b: (j, 0, 0, 0))],
        out_specs=pl.BlockSpec((1, nc, sw), lambda j, b: (b, 0, j)),
        out_shape=jax.ShapeDtypeStruct((bsz, nc, ns * sw), F32),
        scratch_shapes=[pltpu.VMEM((kq, sw), BF16)],
        compiler_params=_cparams("parallel", "arbitrary"),
        name="s5_chunk_state",
    )(proj, s_dense)
    hprev4 = pl.pallas_call(
        _s5_scan_kernel,
        grid=(bsz,),
        in_specs=[pl.BlockSpec((1, nc, ns, sw), lambda b: (b, 0, 0, 0)),
                  pl.BlockSpec((ns, sw // 2), lambda b: (0, 0)),
                  pl.BlockSpec((ns, sw // 2), lambda b: (0, 0))],
        out_specs=pl.BlockSpec((1, nc, ns, sw), lambda b: (b, 0, 0, 0)),
        out_shape=jax.ShapeDtypeStruct((bsz, nc, ns, sw), F32),
        compiler_params=_cparams("parallel"),
        name="s5_chunk_scan",
    )(hend.reshape(bsz, nc, ns, sw), a_re, a_im)
    y = pl.pallas_call(
        _s5_out_kernel,
        grid=(ns, bsz),
        in_specs=[pl.BlockSpec((L, LANES), lambda j, b: (b, j)),
                  pl.BlockSpec((sg, q, h, h), lambda j, b: (j, 0, 0, 0)),
                  pl.BlockSpec((1, nc, sw), lambda j, b: (b, 0, j)),
                  pl.BlockSpec((sg, 2, p, q * h), lambda j, b: (j, 0, 0, 0))],
        out_specs=pl.BlockSpec((L, LANES), lambda j, b: (b, j)),
        out_shape=jax.ShapeDtypeStruct((bsz * L, SSM_WIDTH), F32),
        scratch_shapes=[pltpu.VMEM((q, LANES, LANES), BF16), pltpu.VMEM((kq, kq), BF16), pltpu.VMEM((sw, kq), BF16)],
        compiler_params=_cparams("parallel", "arbitrary"),
        name="s5_chunk_out",
    )(proj, k_dense, hprev4.reshape(bsz, nc, ns * sw), o_dense)
    tm = 512
    return pl.pallas_call(
        _glu_kernel,
        grid=(bsz * L // tm,),
        in_specs=[pl.BlockSpec((tm, SSM_WIDTH), lambda i: (i, 0)),
                  pl.BlockSpec((SSM_WIDTH, SSM_WIDTH), lambda i: (0, 0)),
                  pl.BlockSpec((1, SSM_WIDTH), lambda i: (0, 0))],
        out_specs=pl.BlockSpec((tm, SSM_WIDTH), lambda i: (i, 0)),
        out_shape=jax.ShapeDtypeStruct((bsz * L, SSM_WIDTH), BF16),
        compiler_params=_cparams("parallel"),
        name="s5_gelu_glu",
    )(y, w_glu.astype(BF16), b_glu.reshape(1, SSM_WIDTH))


def _compress_kernel(k_ref, w1_ref, pe_ref, w1full_ref, w2_ref, o_ref, *, transpose_out):
    nc = k_ref.shape[0] // CMP_STRIDE
    prod = jnp.dot(_chunk_rows(k_ref, nc), w1_ref[...], preferred_element_type=F32)
    pe_h = jnp.dot(jnp.broadcast_to(pe_ref[...], (8, pe_ref.shape[1])), w1full_ref[...],
                   precision=HI, preferred_element_type=F32)[0:1]
    two_h = 2 * CMP_HIDDEN
    for gl in range(LANES // HEAD_DIM):
        first = prod[:, gl * two_h: gl * two_h + CMP_HIDDEN]
        second = prod[:, gl * two_h + CMP_HIDDEN: (gl + 1) * two_h]
        hid = first + pltpu.roll(second, nc - 1, axis=0) + pe_h
        out = jnp.dot(jax.nn.gelu(hid).astype(BF16), w2_ref[...], preferred_element_type=F32)
        if transpose_out:
            o_ref[0, gl] = _lane_cat([out, jnp.zeros_like(out)]).T[:HEAD_DIM].astype(BF16)
        else:
            o_ref[0, gl] = out.astype(BF16)


def _compress(proj, col, bsz, L, pe, w1, w2, transpose_out):
    nc = L // CMP_STRIDE
    half = CMP_STRIDE * HEAD_DIM
    gpl = LANES // HEAD_DIM
    w1cat = jnp.concatenate([w1[:half], w1[half:]], axis=1)
    eye = jnp.eye(gpl, dtype=F32)
    w1slab = jnp.einsum('ldh,ag->ladgh', w1cat.reshape(CMP_STRIDE, HEAD_DIM, 2 * CMP_HIDDEN), eye)
    w1slab = w1slab.reshape(CMP_STRIDE * LANES, gpl * 2 * CMP_HIDDEN).astype(BF16)
    out_block = (1, gpl, HEAD_DIM, nc) if transpose_out else (1, gpl, nc, HEAD_DIM)
    out_full = (bsz, KV_HEADS) + out_block[2:]
    return pl.pallas_call(
        functools.partial(_compress_kernel, transpose_out=transpose_out),
        grid=(bsz, KV_HEADS // gpl),
        in_specs=[pl.BlockSpec((L, LANES), lambda b, j: (b, col // LANES + j)),
                  pl.BlockSpec(w1slab.shape, lambda b, j: (0, 0)),
                  pl.BlockSpec((1, 2 * half), lambda b, j: (0, 0)),
                  pl.BlockSpec((2 * half, CMP_HIDDEN), lambda b, j: (0, 0)),
                  pl.BlockSpec((CMP_HIDDEN, HEAD_DIM), lambda b, j: (0, 0))],
        out_specs=pl.BlockSpec(out_block, lambda b, j: (b, j, 0, 0)),
        out_shape=jax.ShapeDtypeStruct(out_full, BF16),
        compiler_params=_cparams("parallel", "parallel"),
        name="nsa_compress",
    )(proj, w1slab, pe.reshape(1, 2 * half), w1, w2.astype(BF16))


KS_COLS = 2 * LANES
KW_COLS = LANES
KS_FLAG = HEAD_DIM + N_SLC_PAD
KW_FLAG = HEAD_DIM
PAD_TILES = 8
V_ROWS = HEAD_DIM + 16
V_ONES = HEAD_DIM


def _kv_tiles_kernel(ks_ref, vs_ref, kw_ref, vw_ref, kso_ref, vso_ref, kwo_ref, vwo_ref):
    tm = ks_ref.shape[0]
    step = pl.program_id(1)
    row0 = step * tm
    kso_ref[:, :, :, :, KS_FLAG:] = jnp.zeros(kso_ref.shape[:4] + (KS_COLS - KS_FLAG,), BF16)
    kwo_ref[:, :, :, :, KW_FLAG:] = jnp.zeros(kwo_ref.shape[:4] + (KW_COLS - KW_FLAG,), BF16)
    is_real = step < pl.num_programs(1) - 1
    tail_rows = lax.broadcasted_iota(jnp.int32, vso_ref.shape[:3] + (V_ROWS - HEAD_DIM, QT), 3)
    v_tail = jnp.where(tail_rows == 0, jnp.where(is_real, 1.0, 0.0), 0.0).astype(BF16)
    vso_ref[:, :, :, HEAD_DIM:, :] = v_tail
    vwo_ref[:, :, :, HEAD_DIM:, :] = v_tail

    @pl.when(step < pl.num_programs(1) - 1)
    def _():
        lane_blk = lax.broadcasted_iota(jnp.int32, (QT, N_SLC_PAD), 1)
        for k in range(tm // QT):
            rows = slice(k * QT, (k + 1) * QT)
            tok = lax.broadcasted_iota(jnp.int32, (QT, N_SLC_PAD), 0) + (row0 + k * QT)
            onehot = jnp.where((tok >> int(math.log2(SLC_BLOCK))) == lane_blk, 1.0, 0.0).astype(BF16)
            vs_t = vs_ref[rows, :].T
            vw_t = vw_ref[rows, :].T
            for g in range(KV_HEADS):
                cols = slice(g * HEAD_DIM, (g + 1) * HEAD_DIM)
                kso_ref[0, g, k, :, :HEAD_DIM] = ks_ref[rows, cols].astype(BF16)
                kso_ref[0, g, k, :, HEAD_DIM:KS_FLAG] = onehot
                kwo_ref[0, g, k, :, :HEAD_DIM] = kw_ref[rows, cols].astype(BF16)
                vso_ref[0, g, k, :HEAD_DIM, :] = vs_t[cols].astype(BF16)
                vwo_ref[0, g, k, :HEAD_DIM, :] = vw_t[cols].astype(BF16)

    @pl.when(step == pl.num_programs(1) - 1)
    def _():
        ones = jnp.ones(kso_ref.shape[:4] + (1,), BF16)
        kso_ref[:, :, :, :, :KS_FLAG] = jnp.zeros(kso_ref.shape[:4] + (KS_FLAG,), BF16)
        kwo_ref[:, :, :, :, :KW_FLAG] = jnp.zeros(kwo_ref.shape[:4] + (KW_FLAG,), BF16)
        kso_ref[:, :, :, :, KS_FLAG:KS_FLAG + 1] = ones
        kwo_ref[:, :, :, :, KW_FLAG:KW_FLAG + 1] = ones
        vso_ref[:, :, :, :HEAD_DIM, :] = jnp.zeros(vso_ref.shape[:3] + (HEAD_DIM, QT), BF16)
        vwo_ref[:, :, :, :HEAD_DIM, :] = jnp.zeros(vwo_ref.shape[:3] + (HEAD_DIM, QT), BF16)


def _kv_tiles(proj, bsz, L):
    nt = L // QT
    g = KV_HEADS
    kt = PAD_TILES
    tm = kt * QT
    steps = L // tm
    in_spec = lambda col: pl.BlockSpec((tm, KV_WIDTH),
                                       lambda b, i: (b * steps + jnp.minimum(i, steps - 1), col // KV_WIDTH))
    out_spec = lambda a, c: pl.BlockSpec((1, g, kt, a, c), lambda b, i: (b, 0, i, 0, 0))
    shape = lambda a, c: jax.ShapeDtypeStruct((bsz, g, nt + kt, a, c), BF16)
    return pl.pallas_call(
        _kv_tiles_kernel,
        grid=(bsz, steps + 1),
        in_specs=[in_spec(COL_KS), in_spec(COL_VS), in_spec(COL_KW), in_spec(COL_VW)],
        out_specs=[out_spec(QT, KS_COLS), out_spec(V_ROWS, QT), out_spec(QT, KW_COLS), out_spec(V_ROWS, QT)],
        out_shape=[shape(QT, KS_COLS), shape(V_ROWS, QT), shape(QT, KW_COLS), shape(V_ROWS, QT)],
        compiler_params=_cparams("parallel", "arbitrary"),
        name="nsa_kv_tiles",
    )(proj, proj, proj, proj)


def _rel_bucket(dist):
    dist = jnp.maximum(dist, 0)
    max_exact = REL_BUCKETS // 2
    large = max_exact + (jnp.log(jnp.maximum(dist, 1).astype(F32) / max_exact)
                         / math.log(REL_MAX_DIST / max_exact) * (REL_BUCKETS - max_exact)).astype(jnp.int32)
    large = jnp.minimum(large, REL_BUCKETS - 1)
    return jnp.where(dist < max_exact, dist, large)


def _toeplitz_rows(base, nrows, step):
    w = base.shape[1]
    x = jnp.broadcast_to(base, (nrows, w))
    k_ix = lax.broadcasted_iota(jnp.int32, (nrows, w), 0)
    bit = 1
    while bit < nrows:
        x = jnp.where((k_ix & bit) != 0, pltpu.roll(x, (bit * step) % w, axis=1), x)
        bit *= 2
    return x


def _bias_tables_kernel(bw_ref, bc_ref, far_ref, wb_ref, nb_ref, strip_ref, *, ncp):
    slab = WINDOW + QT
    y = _toeplitz_rows(bw_ref[0], QT, 1)
    for kh in range(slab // QT):
        tile = y[:, WINDOW - kh * QT: WINDOW - kh * QT + QT]
        wb_ref[0, kh * QT:(kh + 1) * QT, :] = tile
        if kh >= slab // QT - 2:
            row = kh - (slab // QT - 2)
            nb_ref[0, row * QT:(row + 1) * QT, :] = tile - far_ref[0]
    band = 2 * CMP_STRIDE
    z = _toeplitz_rows(bc_ref[0], band, CMP_STRIDE)[:, 2 * QT: 3 * QT]
    strip_ref[0, :ncp - CMP_STRIDE, :] = jnp.broadcast_to(far_ref[0], (ncp - CMP_STRIDE, QT))
    strip_ref[0, ncp - CMP_STRIDE: ncp + CMP_STRIDE, :] = z
    strip_ref[0, ncp + CMP_STRIDE:, :] = jnp.full((ncp - CMP_STRIDE, QT), NEG, F32)


def _nsa_tables(rel_table, L):
    ncp = L // CMP_STRIDE
    slab = WINDOW + QT
    dist = jnp.arange(TABLE_W)
    rel_table = rel_table * LOG2E
    per_dist = rel_table[_rel_bucket(dist)].T
    base_w = jnp.where(dist < WINDOW, per_dist, NEG)
    shift = CMP_BLOCK - 1
    shifted = jnp.concatenate([jnp.full((NSA_HEADS, shift), NEG, F32), per_dist[:, :TABLE_W - shift]], axis=1)
    base_c = jnp.where(dist < TABLE_W // 2, shifted, NEG)
    far = jnp.broadcast_to(rel_table[REL_BUCKETS - 1][:, None, None], (NSA_HEADS, 1, QT))
    wb, nb, strip = pl.pallas_call(
        functools.partial(_bias_tables_kernel, ncp=ncp),
        grid=(NSA_HEADS,),
        in_specs=[pl.BlockSpec((1, 1, TABLE_W), lambda h: (h, 0, 0)),
                  pl.BlockSpec((1, 1, TABLE_W), lambda h: (h, 0, 0)),
                  pl.BlockSpec((1, 1, QT), lambda h: (h, 0, 0))],
        out_specs=[pl.BlockSpec((1, slab, QT), lambda h: (h, 0, 0)),
                   pl.BlockSpec((1, 2 * QT, QT), lambda h: (h, 0, 0)),
                   pl.BlockSpec((1, 2 * ncp, QT), lambda h: (h, 0, 0))],
        out_shape=[jax.ShapeDtypeStruct((NSA_HEADS, slab, QT), F32),
                   jax.ShapeDtypeStruct((NSA_HEADS, 2 * QT, QT), F32),
                   jax.ShapeDtypeStruct((NSA_HEADS, 2 * ncp, QT), F32)],
        compiler_params=_cparams("parallel"),
        name="nsa_bias_tables",
    )(base_w.reshape(NSA_HEADS, 1, TABLE_W), base_c.reshape(NSA_HEADS, 1, TABLE_W), far)
    g, r = KV_HEADS, Q_PER_KV
    return wb.reshape(g, r, slab, QT), nb.reshape(g, r, 2 * QT, QT), strip.reshape(g, r, 2 * ncp, QT)


def _overlap_t(L):
    n_cmp = (L - CMP_BLOCK) // CMP_STRIDE + 1
    n_slc = L // SLC_BLOCK
    cmp_idx = np.arange(n_cmp)[:, None] * CMP_STRIDE + np.arange(CMP_BLOCK)[None, :]
    overlap = ((cmp_idx[:, :, None] // SLC_BLOCK) == np.arange(n_slc)[None, None, :]).sum(1) / CMP_BLOCK
    out = np.zeros((N_SLC_PAD, L // CMP_STRIDE), np.float32)
    out[:n_slc, :n_cmp] = overlap.T
    return jnp.asarray(out, dtype=BF16)


def _split3(x):
    hi = x.astype(BF16)
    r1 = x - hi.astype(F32)
    mid = r1.astype(BF16)
    lo = (r1 - mid.astype(F32)).astype(BF16)
    return hi, mid, lo


def _nsa_kernel(q_ref, gl_ref, kc_ref, vcT_ref, strip_ref, ovT_ref, ks_ref, vsT_ref, kw_ref, vwT_ref, wb_ref,
                nb_ref, o_ref, gate_ref, *, n_slc):
    i = pl.program_id(2)
    R = Q_PER_KV
    RQ = R * QT
    RW = R * HEAD_DIM
    groups = range(kc_ref.shape[1])
    first_group = pl.program_id(1) * len(groups)
    pad_tile = ks_ref.shape[2] - PAD_TILES
    t_pos = lax.broadcasted_iota(jnp.int32, (1, QT), 1) + i * QT

    def flag_rows(n):
        return jnp.where(lax.broadcasted_iota(jnp.int32, (n, QT), 0) == 0, NEG, 0.0).astype(BF16)

    q_heads = []
    for gg in groups:
        qT = (q_ref[:, gg * RW:(gg + 1) * RW].T * (0.125 * LOG2E)).astype(BF16)
        q_heads.append([qT[r * HEAD_DIM:(r + 1) * HEAD_DIM] for r in range(R)])

    slab = WINDOW + QT
    n_win = slab // QT
    win_idx = [jnp.where(i + jj >= n_win - 1, i + jj - (n_win - 1), pad_tile) for jj in range(n_win)]

    def window_scores(gg):
        kw = jnp.concatenate([kw_ref[0, gg, j] for j in win_idx], axis=0)
        qwin = _lane_cat([jnp.concatenate([q_heads[gg][r], flag_rows(KW_COLS - HEAD_DIM)], axis=0)
                          for r in range(R)])
        wb = _lane_cat([wb_ref[gg, r] for r in range(R)])
        return jnp.dot(kw, qwin, preferred_element_type=F32) + wb

    def window_branch(gg, sw):
        m_w = jnp.max(sw, axis=0, keepdims=True)
        e_wb = jnp.exp2(sw - m_w).astype(BF16)
        acc = None
        for jj, j in enumerate(win_idx):
            term = jnp.dot(vwT_ref[0, gg, j], e_wb[jj * QT:(jj + 1) * QT], preferred_element_type=F32)
            acc = term if acc is None else acc + term
        return acc[:HEAD_DIM] * (1.0 / acc[V_ONES:V_ONES + 1])

    sw_all = [window_scores(gg) for gg in groups]

    ncp = kc_ref.shape[2]
    strip_row = pl.multiple_of(ncp - i * (QT // CMP_STRIDE), QT // CMP_STRIDE)
    ov = ovT_ref[...]
    s_ix = lax.broadcasted_iota(jnp.int32, (N_SLC_PAD, QT), 0)
    cur = t_pos >> int(math.log2(SLC_BLOCK))
    forced = (s_ix == 0) | (s_ix == cur) | (s_ix == cur - 1)

    def compressed_scores(gg):
        sc = jnp.dot(kc_ref[0, gg], _lane_cat(q_heads[gg]), preferred_element_type=F32)
        return sc + _lane_cat([strip_ref[gg, r, pl.ds(strip_row, ncp), :] for r in range(R)])

    def compressed_branch(gg, sm):
        m_c = jnp.max(sm, axis=0, keepdims=True)
        e_c = jnp.exp2(sm - m_c)
        l_c = jnp.sum(e_c, axis=0, keepdims=True)
        inv_c = jnp.where(m_c > 0.5 * NEG, 1.0 / l_c, 0.0)
        o_cmp = jnp.dot(vcT_ref[0, gg], e_c.astype(BF16), preferred_element_type=F32) * inv_c
        p_c = e_c * inv_c
        p_sum = p_c[:, 0:QT]
        for r in range(1, R):
            p_sum = p_sum + p_c[:, r * QT:(r + 1) * QT]
        imp = None
        for part in _split3(p_sum):
            term = jnp.dot(ov, part, preferred_element_type=F32)
            imp = term if imp is None else imp + term
        score = jnp.where(forced, FORCED_SCORE, jnp.where(s_ix <= cur, imp, -1.0))
        return o_cmp, jnp.where(s_ix < n_slc, score, -2.0)

    sm_all = [compressed_scores(gg) for gg in groups]
    o_win = [window_branch(gg, sw_all[gg]) for gg in groups]
    cmp_out = [compressed_branch(gg, sm_all[gg]) for gg in groups]
    o_cmp = [c[0] for c in cmp_out]
    scores = [c[1] for c in cmp_out]

    SUB = 8
    n_sub = N_SLC_PAD // SUB
    score_rows = [[score[v * SUB:(v + 1) * SUB] for v in range(n_sub)] for score in scores]
    sub_ix = lax.broadcasted_iota(jnp.int32, (SUB, QT), 0)

    def rank_segment(seg, ranks):
        out = []
        for gg in groups:
            rk = list(ranks[gg])
            for sp in range(seg * RANK_SEG, min((seg + 1) * RANK_SEG, n_slc)):
                row = scores[gg][sp:sp + 1, :]
                for v, blk in enumerate(score_rows[gg]):
                    if v * SUB > sp:
                        beats = row >= blk
                    elif v * SUB + SUB - 1 <= sp:
                        beats = row > blk
                    else:
                        beats = (row > blk) | ((row == blk) & (sub_ix > sp - v * SUB))
                    rk[v] = rk[v] + jnp.where(beats, 1.0, 0.0)
            out.append(tuple(rk))
        return tuple(out)

    ranks = rank_segment(0, tuple(tuple(jnp.zeros((SUB, QT), F32) for _ in range(n_sub)) for _ in groups))
    last_block = (i + 1) * (QT // SLC_BLOCK) - 1
    for seg in range(1, -(-n_slc // RANK_SEG)):
        ranks = lax.cond(last_block >= seg * RANK_SEG, functools.partial(rank_segment, seg), lambda rk: rk, ranks)

    qsel = []
    for gg in groups:
        sel_neg = jnp.where(jnp.concatenate(ranks[gg], axis=0) < float(SLC_TOPK), 0.0, UNSELECTED).astype(BF16)
        qsel.append(_lane_cat([jnp.concatenate([q_heads[gg][r], sel_neg, flag_rows(KS_COLS - KS_FLAG)], axis=0)
                               for r in range(R)]))

    n_far = jnp.maximum(i - 1, 0)

    def pair_tiles(j):
        return [jnp.where(j + n < n_far, j + n, pad_tile) for n in range(2)]

    def pair_scores(gg, tiles):
        k2 = jnp.concatenate([ks_ref[0, gg, t] for t in tiles], axis=0)
        return jnp.dot(k2, qsel[gg], preferred_element_type=F32)

    def online_step(gg, carry, s, tiles):
        m, acc = carry
        m_new = jnp.maximum(m, jnp.max(s, axis=0, keepdims=True))
        p = jnp.exp2(s - m_new).astype(BF16)
        acc = jnp.exp2(m - m_new) * acc
        for n, t in enumerate(tiles):
            acc = acc + jnp.dot(vsT_ref[0, gg, t], p[n * QT:(n + 1) * QT], preferred_element_type=F32)
        return m_new, acc

    def far_trip(u, carries):
        tiles = pair_tiles(2 * u)
        s_all = [pair_scores(gg, tiles) for gg in groups]
        return tuple(online_step(gg, carries[gg], s_all[gg], tiles) for gg in groups)

    init = (jnp.full((1, RQ), NEG, F32), jnp.zeros((V_ROWS, RQ), F32))
    carries = lax.fori_loop(0, (n_far + 1) // 2, far_trip, tuple(init for _ in groups))
    near = [jnp.where(i >= 1, i - 1, pad_tile), i]

    gate_ref[...] = jax.nn.sigmoid(gl_ref[...].T)
    s_near = [pair_scores(gg, near) + _lane_cat([nb_ref[gg, r] for r in range(R)]) for gg in groups]
    for gg in groups:
        _, acc_s = online_step(gg, carries[gg], s_near[gg], near)
        o_slc = acc_s[:HEAD_DIM] * (1.0 / acc_s[V_ONES:V_ONES + 1])
        head0 = (first_group + gg) * R
        gates = [_lane_cat([gate_ref[pl.ds((head0 + r) * 3 + br, 1), :] for r in range(R)]) for br in range(3)]
        total = gates[0] * o_cmp[gg] + gates[1] * o_slc + gates[2] * o_win[gg]
        o_ref[:, gg * RW:(gg + 1) * RW] = jnp.concatenate(
            [total[:, r * QT:(r + 1) * QT] for r in range(R)], axis=0).T.astype(o_ref.dtype)


def _nsa_mixer(proj, bsz, L, pe_ck, w_ck1, w_ck2, pe_cv, w_cv1, w_cv2, rel_table):
    g, r = KV_HEADS, Q_PER_KV
    ng = NSA_GROUPS
    ni = L // QT
    nt = L // QT + PAD_TILES
    ncp = L // CMP_STRIDE
    n_slc = L // SLC_BLOCK
    slab = WINDOW + QT
    k_cmp = _compress(proj, COL_KC, bsz, L, pe_ck, w_ck1, w_ck2, transpose_out=False)
    v_cmp_t = _compress(proj, COL_VC, bsz, L, pe_cv, w_cv1, w_cv2, transpose_out=True)
    ks_t, vs_t, kw_t, vw_t = _kv_tiles(proj, bsz, L)
    wb, nb, strip = _nsa_tables(rel_table, L)
    qw = ng * r * HEAD_DIM
    once = pl.Buffered(1)
    whole = lambda *shape: pl.BlockSpec((1, ng) + shape, lambda b, h, i: (b, h) + (0,) * len(shape), once)
    per_group = lambda *shape: pl.BlockSpec((ng,) + shape, lambda b, h, i: (h,) + (0,) * len(shape), once)
    return pl.pallas_call(
        functools.partial(_nsa_kernel, n_slc=n_slc),
        grid=(bsz, g // ng, ni),
        in_specs=[
            pl.BlockSpec((QT, qw), lambda b, h, i: (b * ni + i, COL_Q // qw + h)),
            pl.BlockSpec((QT, LANES), lambda b, h, i: (b * ni + i, COL_GATE // LANES)),
            whole(ncp, HEAD_DIM),
            whole(HEAD_DIM, ncp),
            per_group(r, 2 * ncp, QT),
            pl.BlockSpec((N_SLC_PAD, ncp), lambda b, h, i: (0, 0), once),
            whole(nt, QT, KS_COLS),
            whole(nt, V_ROWS, QT),
            whole(nt, QT, KW_COLS),
            whole(nt, V_ROWS, QT),
            per_group(r, slab, QT),
            per_group(r, 2 * QT, QT),
        ],
        out_specs=pl.BlockSpec((QT, qw), lambda b, h, i: (b * ni + i, h)),
        out_shape=jax.ShapeDtypeStruct((bsz * L, NSA_WIDTH), BF16),
        scratch_shapes=[pltpu.VMEM((LANES, QT), F32)],
        compiler_params=_cparams("parallel", "parallel", "arbitrary"),
        name="nsa_attention",
    )(proj, proj, k_cmp, v_cmp_t, strip, _overlap_t(L), ks_t, vs_t, kw_t, vw_t, wb, nb)


def _out_proj_kernel(x_ref, ys_ref, yn_ref, w_ref, o_ref):
    half = ys_ref.shape[1]
    acc = jnp.dot(ys_ref[...], w_ref[:half, :], preferred_element_type=F32)
    acc = acc + jnp.dot(yn_ref[...], w_ref[half:, :], preferred_element_type=F32)
    o_ref[...] = x_ref[...] + acc


def _out_proj(x2, y_ssm, y_nsa, w, tm=512):
    t, d = x2.shape
    return pl.pallas_call(
        _out_proj_kernel,
        grid=(t // tm,),
        in_specs=[pl.BlockSpec((tm, d), lambda i: (i, 0)),
                  pl.BlockSpec((tm, SSM_WIDTH), lambda i: (i, 0)),
                  pl.BlockSpec((tm, NSA_WIDTH), lambda i: (i, 0)),
                  pl.BlockSpec((SSM_WIDTH + NSA_WIDTH, d), lambda i: (0, 0), pl.Buffered(1))],
        out_specs=pl.BlockSpec((tm, d), lambda i: (i, 0)),
        out_shape=jax.ShapeDtypeStruct((t, d), F32),
        compiler_params=_cparams("parallel"),
        name="out_proj",
    )(x2, y_ssm, y_nsa, w)


def _mlp_kernel(x_ref, n2_ref, wu_ref, wd_ref, nf_ref, o_ref, h_ref, acc_ref):
    f = pl.program_id(1)

    @pl.when(f == 0)
    def _():
        h_ref[...] = _rms(x_ref[...], n2_ref[...]).astype(BF16)
        acc_ref[...] = jnp.zeros_like(acc_ref)

    a = jnp.dot(h_ref[...], wu_ref[...], preferred_element_type=F32)
    a = jnp.square(jnp.maximum(a, 0.0))
    acc_ref[...] += jnp.dot(a.astype(BF16), wd_ref[...], preferred_element_type=F32)

    @pl.when(f == pl.num_programs(1) - 1)
    def _():
        o_ref[...] = _rms(x_ref[...] + acc_ref[...], nf_ref[...])


def _mlp(x2, n2, wu, wd, nf, tm=512, tf=1024):
    t, d = x2.shape
    ff = wu.shape[1]
    return pl.pallas_call(
        _mlp_kernel,
        grid=(t // tm, ff // tf),
        in_specs=[pl.BlockSpec((tm, d), lambda i, f: (i, 0)),
                  pl.BlockSpec((1, d), lambda i, f: (0, 0)),
                  pl.BlockSpec((d, tf), lambda i, f: (0, f)),
                  pl.BlockSpec((tf, d), lambda i, f: (f, 0)),
                  pl.BlockSpec((1, d), lambda i, f: (0, 0))],
        out_specs=pl.BlockSpec((tm, d), lambda i, f: (i, 0)),
        out_shape=jax.ShapeDtypeStruct((t, d), F32),
        scratch_shapes=[pltpu.VMEM((tm, d), BF16), pltpu.VMEM((tm, d), F32)],
        compiler_params=_cparams("parallel", "arbitrary"),
        name="mlp_final_norm",
    )(x2, n2, wu, wd, nf)


def kernel(x, norm1_w, w_in, ssm_a_re, ssm_a_im, ssm_log_dt, ssm_b_re, ssm_b_im, ssm_c_re, ssm_c_im, ssm_d,
           w_glu, b_glu, pe_ck, w_ck1, w_ck2, pe_cv, w_cv1, w_cv2, w_out, norm2_w, w_up, w_down, rel_table,
           norm_f_w):
    bsz, L, d = x.shape
    assert w_in.shape[0] == 1, "the closing rmsnorm is fused into the (single) layer's MLP kernel"
    x2 = x.reshape(bsz * L, d)
    w_in_p = jnp.pad(w_in[0].astype(BF16), ((0, 0), (0, D_IN_PAD - D_IN)))
    proj = _norm_matmul(x2, norm1_w[0].reshape(1, d), w_in_p)
    ops = _s5_operators(ssm_a_re[0], ssm_a_im[0], ssm_log_dt[0], ssm_b_re[0], ssm_b_im[0], ssm_c_re[0], ssm_c_im[0],
                        ssm_d[0])
    y_ssm = _s5_mixer(proj, bsz, L, ops, w_glu[0], b_glu[0])
    y_nsa = _nsa_mixer(proj, bsz, L, pe_ck[0], w_ck1[0], w_ck2[0], pe_cv[0], w_cv1[0], w_cv2[0], rel_table)
    x2 = _out_proj(x2, y_ssm, y_nsa, w_out[0].astype(BF16))
    x2 = _mlp(x2, norm2_w[0].reshape(1, d), w_up[0].astype(BF16), w_down[0].astype(BF16), norm_f_w.reshape(1, d))
    return x2.reshape(bsz, L, d)
```

```python
import functools
import math

import numpy as np
import jax
import jax.numpy as jnp
from jax import lax
from jax.experimental import pallas as pl
from jax.experimental.pallas import tpu as pltpu

F32 = jnp.float32
BF16 = jnp.bfloat16

D_MODEL = 2048
SSM_WIDTH = 1024
SSM_GROUP = 16
SSM_GROUPS = 64
SSM_STATE = 64
NSA_WIDTH = 1024
HEAD_DIM = 64
NSA_HEADS = 16
KV_HEADS = 4
Q_PER_KV = 4
KV_WIDTH = 256
CMP_BLOCK = 32
CMP_STRIDE = 16
CMP_HIDDEN = 256
SLC_BLOCK = 64
SLC_TOPK = 16
WINDOW = 512
REL_BUCKETS = 32
REL_MAX_DIST = 128
D_FF = 8192
EPS = 1e-6
NEG = -1e30
FORCED_SCORE = 1e4
D_IN = SSM_WIDTH + NSA_WIDTH + 6 * KV_WIDTH + 3 * NSA_HEADS
D_IN_PAD = 3712
COL_Q = SSM_WIDTH
COL_KC = COL_Q + NSA_WIDTH
COL_VC = COL_KC + KV_WIDTH
COL_KS = COL_VC + KV_WIDTH
COL_VS = COL_KS + KV_WIDTH
COL_KW = COL_VS + KV_WIDTH
COL_VW = COL_KW + KV_WIDTH
COL_GATE = COL_VW + KV_WIDTH

LANES = 128
S5_CHUNK = 16
SLAB_GROUPS = LANES // SSM_GROUP
N_SLABS = SSM_GROUPS // SLAB_GROUPS
QT = 128
N_SLC_PAD = 64
RANK_SEG = 16
NSA_GROUPS = 4
FAR_TILES = 2
UNSELECTED = -1e9
LOG2E = math.log2(math.e)
TABLE_W = 1024
VMEM_LIMIT = 56 * 1024 * 1024
HI = lax.Precision.HIGHEST


def _cparams(*sem):
    return pltpu.CompilerParams(dimension_semantics=sem, vmem_limit_bytes=VMEM_LIMIT)


def _rms(x, w):
    ms = jnp.mean(x * x, axis=-1, keepdims=True)
    return x * lax.rsqrt(ms + EPS) * w


def _lane_cat(parts):
    return jnp.concatenate(parts, axis=1)


def _norm_matmul_kernel(x_ref, nw_ref, w_ref, o_ref):
    h = _rms(x_ref[...], nw_ref[...]).astype(BF16)
    o_ref[...] = jnp.dot(h, w_ref[...], preferred_element_type=F32)


def _norm_matmul(x2, nw, w, tm=512):
    t, d = x2.shape
    n = w.shape[1]
    return pl.pallas_call(
        _norm_matmul_kernel,
        grid=(t // tm,),
        in_specs=[pl.BlockSpec((tm, d), lambda i: (i, 0)),
                  pl.BlockSpec((1, d), lambda i: (0, 0)),
                  pl.BlockSpec((d, n), lambda i: (0, 0), pl.Buffered(1))],
        out_specs=pl.BlockSpec((tm, n), lambda i: (i, 0)),
        out_shape=jax.ShapeDtypeStruct((t, n), F32),
        compiler_params=_cparams("parallel"),
        name="norm_in_proj",
    )(x2, nw, w)


def _s5_operators(a_re, a_im, log_dt, b_re, b_im, c_re, c_im, d):
    q = S5_CHUNK
    g, p = a_re.shape
    h = SSM_GROUP
    dt = jnp.exp(log_dt)[:, None]
    lam_re, lam_im = dt * a_re, dt * a_im
    mag1 = jnp.exp(lam_re)
    abar_re, abar_im = mag1 * jnp.cos(lam_im), mag1 * jnp.sin(lam_im)
    den = a_re * a_re + a_im * a_im
    zr, zi = abar_re - 1.0, abar_im
    coef_re = (zr * a_re + zi * a_im) / den
    coef_im = (zi * a_re - zr * a_im) / den
    bb_re = coef_re[..., None] * b_re - coef_im[..., None] * b_im
    bb_im = coef_re[..., None] * b_im + coef_im[..., None] * b_re
    k = jnp.arange(q + 1, dtype=F32)[:, None, None]
    mag = jnp.exp(k * lam_re)
    pw_re, pw_im = mag * jnp.cos(k * lam_im), mag * jnp.sin(k * lam_im)
    m_re = pw_re[..., None] * bb_re - pw_im[..., None] * bb_im
    m_im = pw_re[..., None] * bb_im + pw_im[..., None] * bb_re
    kern = (jnp.einsum('ghp,kgpi->gkhi', c_re, m_re[:q], precision=HI)
            - jnp.einsum('ghp,kgpi->gkhi', c_im, m_im[:q], precision=HI))
    kern = kern.at[:, 0].add(jax.vmap(jnp.diag)(d))
    rev = np.arange(q - 1, -1, -1)
    s_end = jnp.stack([m_re[rev], m_im[rev]], axis=0).transpose(2, 1, 4, 0, 3)
    pr, pi = pw_re[1:], pw_im[1:]
    o_re = c_re[None] * pr[:, :, None, :] - c_im[None] * pi[:, :, None, :]
    o_im = -(c_re[None] * pi[:, :, None, :] + c_im[None] * pr[:, :, None, :])
    o_carry = jnp.stack([o_re, o_im], axis=0).transpose(2, 0, 4, 1, 3)
    a_q = jnp.stack([pw_re[q], pw_im[q]], axis=0)
    return kern, s_end, o_carry, a_q


def _s5_slab_weights(kern, s_end, o_carry, a_q):
    q, h, p = S5_CHUNK, SSM_GROUP, SSM_STATE
    s_dense = s_end.reshape(SSM_GROUPS, q, h, 2 * p).astype(BF16)
    k_dense = kern.transpose(0, 1, 3, 2).astype(BF16)
    o_dense = o_carry.reshape(SSM_GROUPS, 2, p, q * h).astype(BF16)
    a_re = a_q[0].reshape(N_SLABS, SLAB_GROUPS * p)
    a_im = a_q[1].reshape(N_SLABS, SLAB_GROUPS * p)
    return s_dense, k_dense, o_dense, a_re, a_im


def _chunk_rows(u_ref, nc):
    return _lane_cat([u_ref[pl.ds(s, nc, stride=S5_CHUNK), :] for s in range(S5_CHUNK)]).astype(BF16)


def _s5_state_kernel(u_ref, sd_ref, o_ref, ws_scr):
    nc = u_ref.shape[0] // S5_CHUNK
    q, h, p, sg = S5_CHUNK, SSM_GROUP, SSM_STATE, SLAB_GROUPS

    @pl.when(pl.program_id(1) == 0)
    def _():
        ws_scr[...] = jnp.zeros_like(ws_scr)
        for s in range(q):
            for a in range(sg):
                for c in range(2):
                    ws_scr[s * LANES + a * h: s * LANES + (a + 1) * h,
                           (c * sg + a) * p: (c * sg + a + 1) * p] = sd_ref[a, s, :, c * p:(c + 1) * p]

    o_ref[0] = jnp.dot(_chunk_rows(u_ref, nc), ws_scr[...], preferred_element_type=F32)


def _s5_scan_kernel(h_ref, ar_ref, ai_ref, o_ref):
    half = ar_ref.shape[1]
    ar = ar_ref[...]
    ai = ai_ref[...]

    def body(c, carry):
        hr, hi = carry
        o_ref[0, c, :, :half] = hr
        o_ref[0, c, :, half:] = hi
        xr = h_ref[0, c, :, :half]
        xi = h_ref[0, c, :, half:]
        return ar * hr - ai * hi + xr, ar * hi + ai * hr + xi

    zero = jnp.zeros(ar.shape, F32)
    lax.fori_loop(0, h_ref.shape[1], body, (zero, zero))


def _s5_out_kernel(u_ref, kd_ref, h_ref, od_ref, y_ref, kl_scr, wi_scr, wc_scr):
    nc = u_ref.shape[0] // S5_CHUNK
    q, h, p, sg = S5_CHUNK, SSM_GROUP, SSM_STATE, SLAB_GROUPS

    @pl.when(pl.program_id(1) == 0)
    def _():
        kl_scr[...] = jnp.zeros_like(kl_scr)
        for k in range(q):
            for a in range(sg):
                kl_scr[k, a * h:(a + 1) * h, a * h:(a + 1) * h] = kd_ref[a, k]
        wi_scr[...] = jnp.zeros_like(wi_scr)
        for s in range(q):
            for t in range(s, q):
                wi_scr[s * LANES:(s + 1) * LANES, t * LANES:(t + 1) * LANES] = kl_scr[t - s]
        wc_scr[...] = jnp.zeros_like(wc_scr)
        for a in range(sg):
            for c in range(2):
                for t in range(q):
                    wc_scr[(c * sg + a) * p:(c * sg + a + 1) * p,
                           t * LANES + a * h: t * LANES + (a + 1) * h] = od_ref[a, c, :, t * h:(t + 1) * h]

    u_rows = _chunk_rows(u_ref, nc)
    h_rows = h_ref[0].astype(BF16)
    pair = 2 * LANES
    for tp in range(q // 2):
        cols = slice(tp * pair, (tp + 1) * pair)
        k_used = (tp + 1) * pair
        y = (jnp.dot(u_rows[:, :k_used], wi_scr[:k_used, cols], preferred_element_type=F32)
             + jnp.dot(h_rows, wc_scr[:, cols], preferred_element_type=F32))
        for tt in range(2):
            y_ref[pl.ds(2 * tp + tt, nc, stride=q), :] = y[:, tt * LANES:(tt + 1) * LANES]


def _glu_kernel(y_ref, w_ref, b_ref, o_ref):
    z = jax.nn.gelu(y_ref[...])
    gate = jnp.dot(z.astype(BF16), w_ref[...], preferred_element_type=F32) + b_ref[...]
    o_ref[...] = (z * jax.nn.sigmoid(gate)).astype(o_ref.dtype)


def _s5_mixer(proj, bsz, L, ops, w_glu, b_glu):
    s_dense, k_dense, o_dense, a_re, a_im = _s5_slab_weights(*ops)
    q, h, p = S5_CHUNK, SSM_GROUP, SSM_STATE
    nc = L // q
    ns, sg = N_SLABS, SLAB_GROUPS
    sw = sg * 2 * p
    kq = q * LANES
    hend = pl.pallas_call(
        _s5_state_kernel,
        grid=(ns, bsz),
        in_specs=[pl.BlockSpec((L, LANES), lambda j, b: (b, j)),
                  pl.BlockSpec((sg, q, h, 2 * p), lambda j, b: (j, 0, 0, 0))],
        out_specs=pl.BlockSpec((1, nc, sw), lambda j, b: (b, 0, j)),
        out_shape=jax.ShapeDtypeStruct((bsz, nc, ns * sw), F32),
        scratch_shapes=[pltpu.VMEM((kq, sw), BF16)],
        compiler_params=_cparams("parallel", "arbitrary"),
        name="s5_chunk_state",
    )(proj, s_dense)
    hprev4 = pl.pallas_call(
        _s5_scan_kernel,
        grid=(bsz,),
        in_specs=[pl.BlockSpec((1, nc, ns, sw), lambda b: (b, 0, 0, 0)),
                  pl.BlockSpec((ns, sw // 2), lambda b: (0, 0)),
                  pl.BlockSpec((ns, sw // 2), lambda b: (0, 0))],
        out_specs=pl.BlockSpec((1, nc, ns, sw), lambda b: (b, 0, 0, 0)),
        out_shape=jax.ShapeDtypeStruct((bsz, nc, ns, sw), F32),
        compiler_params=_cparams("parallel"),
        name="s5_chunk_scan",
    )(hend.reshape(bsz, nc, ns, sw), a_re, a_im)
    y = pl.pallas_call(
        _s5_out_kernel,
        grid=(ns, bsz),
        in_specs=[pl.BlockSpec((L, LANES), lambda j, b: (b, j)),
                  pl.BlockSpec((sg, q, h, h), lambda j, b: (j, 0, 0, 0)),
                  pl.BlockSpec((1, nc, sw), lambda j, b: (b, 0, j)),
                  pl.BlockSpec((sg, 2, p, q * h), lambda j, b: (j, 0, 0, 0))],
        out_specs=pl.BlockSpec((L, LANES), lambda j, b: (b, j)),
        out_shape=jax.ShapeDtypeStruct((bsz * L, SSM_WIDTH), F32),
        scratch_shapes=[pltpu.VMEM((q, LANES, LANES), BF16), pltpu.VMEM((kq, kq), BF16), pltpu.VMEM((sw, kq), BF16)],
        compiler_params=_cparams("parallel", "arbitrary"),
        name="s5_chunk_out",
    )(proj, k_dense, hprev4.reshape(bsz, nc, ns * sw), o_dense)
    tm = 512
    return pl.pallas_call(
        _glu_kernel,
        grid=(bsz * L // tm,),
        in_specs=[pl.BlockSpec((tm, SSM_WIDTH), lambda i: (i, 0)),
                  pl.BlockSpec((SSM_WIDTH, SSM_WIDTH), lambda i: (0, 0)),
                  pl.BlockSpec((1, SSM_WIDTH), lambda i: (0, 0))],
        out_specs=pl.BlockSpec((tm, SSM_WIDTH), lambda i: (i, 0)),
        out_shape=jax.ShapeDtypeStruct((bsz * L, SSM_WIDTH), BF16),
        compiler_params=_cparams("parallel"),
        name="s5_gelu_glu",
    )(y, w_glu.astype(BF16), b_glu.reshape(1, SSM_WIDTH))


def _compress_kernel(k_ref, w1_ref, pe_ref, w1full_ref, w2_ref, o_ref, *, transpose_out):
    nc = k_ref.shape[0] // CMP_STRIDE
    prod = jnp.dot(_chunk_rows(k_ref, nc), w1_ref[...], preferred_element_type=F32)
    pe_h = jnp.dot(jnp.broadcast_to(pe_ref[...], (8, pe_ref.shape[1])), w1full_ref[...],
                   precision=HI, preferred_element_type=F32)[0:1]
    two_h = 2 * CMP_HIDDEN
    for gl in range(LANES // HEAD_DIM):
        first = prod[:, gl * two_h: gl * two_h + CMP_HIDDEN]
        second = prod[:, gl * two_h + CMP_HIDDEN: (gl + 1) * two_h]
        hid = first + pltpu.roll(second, nc - 1, axis=0) + pe_h
        out = jnp.dot(jax.nn.gelu(hid).astype(BF16), w2_ref[...], preferred_element_type=F32)
        if transpose_out:
            o_ref[0, gl] = _lane_cat([out, jnp.zeros_like(out)]).T[:HEAD_DIM].astype(BF16)
        else:
            o_ref[0, gl] = out.astype(BF16)


def _compress(proj, col, bsz, L, pe, w1, w2, transpose_out):
    nc = L // CMP_STRIDE
    half = CMP_STRIDE * HEAD_DIM
    gpl = LANES // HEAD_DIM
    w1cat = jnp.concatenate([w1[:half], w1[half:]], axis=1)
    eye = jnp.eye(gpl, dtype=F32)
    w1slab = jnp.einsum('ldh,ag->ladgh', w1cat.reshape(CMP_STRIDE, HEAD_DIM, 2 * CMP_HIDDEN), eye)
    w1slab = w1slab.reshape(CMP_STRIDE * LANES, gpl * 2 * CMP_HIDDEN).astype(BF16)
    out_block = (1, gpl, HEAD_DIM, nc) if transpose_out else (1, gpl, nc, HEAD_DIM)
    out_full = (bsz, KV_HEADS) + out_block[2:]
    return pl.pallas_call(
        functools.partial(_compress_kernel, transpose_out=transpose_out),
        grid=(bsz, KV_HEADS // gpl),
        in_specs=[pl.BlockSpec((L, LANES), lambda b, j: (b, col // LANES + j)),
                  pl.BlockSpec(w1slab.shape, lambda b, j: (0, 0)),
                  pl.BlockSpec((1, 2 * half), lambda b, j: (0, 0)),
                  pl.BlockSpec((2 * half, CMP_HIDDEN), lambda b, j: (0, 0)),
                  pl.BlockSpec((CMP_HIDDEN, HEAD_DIM), lambda b, j: (0, 0))],
        out_specs=pl.BlockSpec(out_block, lambda b, j: (b, j, 0, 0)),
        out_shape=jax.ShapeDtypeStruct(out_full, BF16),
        compiler_params=_cparams("parallel", "parallel"),
        name="nsa_compress",
    )(proj, w1slab, pe.reshape(1, 2 * half), w1, w2.astype(BF16))


KS_COLS = 2 * LANES
KW_COLS = LANES
KS_FLAG = HEAD_DIM + N_SLC_PAD
KW_FLAG = HEAD_DIM
PAD_TILES = 8
V_ROWS = HEAD_DIM + 16
V_ONES = HEAD_DIM


def _kv_tiles_kernel(ks_ref, vs_ref, kw_ref, vw_ref, kso_ref, vso_ref, kwo_ref, vwo_ref):
    tm = ks_ref.shape[0]
    step = pl.program_id(1)
    row0 = step * tm
    kso_ref[:, :, :, :, KS_FLAG:] = jnp.zeros(kso_ref.shape[:4] + (KS_COLS - KS_FLAG,), BF16)
    kwo_ref[:, :, :, :, KW_FLAG:] = jnp.zeros(kwo_ref.shape[:4] + (KW_COLS - KW_FLAG,), BF16)
    is_real = step < pl.num_programs(1) - 1
    tail_rows = lax.broadcasted_iota(jnp.int32, vso_ref.shape[:3] + (V_ROWS - HEAD_DIM, QT), 3)
    v_tail = jnp.where(tail_rows == 0, jnp.where(is_real, 1.0, 0.0), 0.0).astype(BF16)
    vso_ref[:, :, :, HEAD_DIM:, :] = v_tail
    vwo_ref[:, :, :, HEAD_DIM:, :] = v_tail

    @pl.when(step < pl.num_programs(1) - 1)
    def _():
        lane_blk = lax.broadcasted_iota(jnp.int32, (QT, N_SLC_PAD), 1)
        for k in range(tm // QT):
            rows = slice(k * QT, (k + 1) * QT)
            tok = lax.broadcasted_iota(jnp.int32, (QT, N_SLC_PAD), 0) + (row0 + k * QT)
            onehot = jnp.where((tok >> int(math.log2(SLC_BLOCK))) == lane_blk, 1.0, 0.0).astype(BF16)
            vs_t = vs_ref[rows, :].T
            vw_t = vw_ref[rows, :].T
            for g in range(KV_HEADS):
                cols = slice(g * HEAD_DIM, (g + 1) * HEAD_DIM)
                kso_ref[0, g, k, :, :HEAD_DIM] = ks_ref[rows, cols].astype(BF16)
                kso_ref[0, g, k, :, HEAD_DIM:KS_FLAG] = onehot
                kwo_ref[0, g, k, :, :HEAD_DIM] = kw_ref[rows, cols].astype(BF16)
                vso_ref[0, g, k, :HEAD_DIM, :] = vs_t[cols].astype(BF16)
                vwo_ref[0, g, k, :HEAD_DIM, :] = vw_t[cols].astype(BF16)

    @pl.when(step == pl.num_programs(1) - 1)
    def _():
        ones = jnp.ones(kso_ref.shape[:4] + (1,), BF16)
        kso_ref[:, :, :, :, :KS_FLAG] = jnp.zeros(kso_ref.shape[:4] + (KS_FLAG,), BF16)
        kwo_ref[:, :, :, :, :KW_FLAG] = jnp.zeros(kwo_ref.shape[:4] + (KW_FLAG,), BF16)
        kso_ref[:, :, :, :, KS_FLAG:KS_FLAG + 1] = ones
        kwo_ref[:, :, :, :, KW_FLAG:KW_FLAG + 1] = ones
        vso_ref[:, :, :, :HEAD_DIM, :] = jnp.zeros(vso_ref.shape[:3] + (HEAD_DIM, QT), BF16)
        vwo_ref[:, :, :, :HEAD_DIM, :] = jnp.zeros(vwo_ref.shape[:3] + (HEAD_DIM, QT), BF16)


def _kv_tiles(proj, bsz, L):
    nt = L // QT
    g = KV_HEADS
    kt = PAD_TILES
    tm = kt * QT
    steps = L // tm
    in_spec = lambda col: pl.BlockSpec((tm, KV_WIDTH),
                                       lambda b, i: (b * steps + jnp.minimum(i, steps - 1), col // KV_WIDTH))
    out_spec = lambda a, c: pl.BlockSpec((1, g, kt, a, c), lambda b, i: (b, 0, i, 0, 0))
    shape = lambda a, c: jax.ShapeDtypeStruct((bsz, g, nt + kt, a, c), BF16)
    return pl.pallas_call(
        _kv_tiles_kernel,
        grid=(bsz, steps + 1),
        in_specs=[in_spec(COL_KS), in_spec(COL_VS), in_spec(COL_KW), in_spec(COL_VW)],
        out_specs=[out_spec(QT, KS_COLS), out_spec(V_ROWS, QT), out_spec(QT, KW_COLS), out_spec(V_ROWS, QT)],
        out_shape=[shape(QT, KS_COLS), shape(V_ROWS, QT), shape(QT, KW_COLS), shape(V_ROWS, QT)],
        compiler_params=_cparams("parallel", "arbitrary"),
        name="nsa_kv_tiles",
    )(proj, proj, proj, proj)


def _rel_bucket(dist):
    dist = jnp.maximum(dist, 0)
    max_exact = REL_BUCKETS // 2
    large = max_exact + (jnp.log(jnp.maximum(dist, 1).astype(F32) / max_exact)
                         / math.log(REL_MAX_DIST / max_exact) * (REL_BUCKETS - max_exact)).astype(jnp.int32)
    large = jnp.minimum(large, REL_BUCKETS - 1)
    return jnp.where(dist < max_exact, dist, large)


def _toeplitz_rows(base, nrows, step):
    w = base.shape[1]
    x = jnp.broadcast_to(base, (nrows, w))
    k_ix = lax.broadcasted_iota(jnp.int32, (nrows, w), 0)
    bit = 1
    while bit < nrows:
        x = jnp.where((k_ix & bit) != 0, pltpu.roll(x, (bit * step) % w, axis=1), x)
        bit *= 2
    return x


def _bias_tables_kernel(bw_ref, bc_ref, far_ref, wb_ref, nb_ref, strip_ref, *, ncp):
    slab = WINDOW + QT
    y = _toeplitz_rows(bw_ref[0], QT, 1)
    for kh in range(slab // QT):
        tile = y[:, WINDOW - kh * QT: WINDOW - kh * QT + QT]
        wb_ref[0, kh * QT:(kh + 1) * QT, :] = tile
        if kh >= slab // QT - 2:
            row = kh - (slab // QT - 2)
            nb_ref[0, row * QT:(row + 1) * QT, :] = tile - far_ref[0]
    band = 2 * CMP_STRIDE
    z = _toeplitz_rows(bc_ref[0], band, CMP_STRIDE)[:, 2 * QT: 3 * QT]
    strip_ref[0, :ncp - CMP_STRIDE, :] = jnp.broadcast_to(far_ref[0], (ncp - CMP_STRIDE, QT))
    strip_ref[0, ncp - CMP_STRIDE: ncp + CMP_STRIDE, :] = z
    strip_ref[0, ncp + CMP_STRIDE:, :] = jnp.full((ncp - CMP_STRIDE, QT), NEG, F32)


def _nsa_tables(rel_table, L):
    ncp = L // CMP_STRIDE
    slab = WINDOW + QT
    dist = jnp.arange(TABLE_W)
    rel_table = rel_table * LOG2E
    per_dist = rel_table[_rel_bucket(dist)].T
    base_w = jnp.where(dist < WINDOW, per_dist, NEG)
    shift = CMP_BLOCK - 1
    shifted = jnp.concatenate([jnp.full((NSA_HEADS, shift), NEG, F32), per_dist[:, :TABLE_W - shift]], axis=1)
    base_c = jnp.where(dist < TABLE_W // 2, shifted, NEG)
    far = jnp.broadcast_to(rel_table[REL_BUCKETS - 1][:, None, None], (NSA_HEADS, 1, QT))
    wb, nb, strip = pl.pallas_call(
        functools.partial(_bias_tables_kernel, ncp=ncp),
        grid=(NSA_HEADS,),
        in_specs=[pl.BlockSpec((1, 1, TABLE_W), lambda h: (h, 0, 0)),
                  pl.BlockSpec((1, 1, TABLE_W), lambda h: (h, 0, 0)),
                  pl.BlockSpec((1, 1, QT), lambda h: (h, 0, 0))],
        out_specs=[pl.BlockSpec((1, slab, QT), lambda h: (h, 0, 0)),
                   pl.BlockSpec((1, 2 * QT, QT), lambda h: (h, 0, 0)),
                   pl.BlockSpec((1, 2 * ncp, QT), lambda h: (h, 0, 0))],
        out_shape=[jax.ShapeDtypeStruct((NSA_HEADS, slab, QT), F32),
                   jax.ShapeDtypeStruct((NSA_HEADS, 2 * QT, QT), F32),
                   jax.ShapeDtypeStruct((NSA_HEADS, 2 * ncp, QT), F32)],
        compiler_params=_cparams("parallel"),
        name="nsa_bias_tables",
    )(base_w.reshape(NSA_HEADS, 1, TABLE_W), base_c.reshape(NSA_HEADS, 1, TABLE_W), far)
    g, r = KV_HEADS, Q_PER_KV
    return wb.reshape(g, r, slab, QT), nb.reshape(g, r, 2 * QT, QT), strip.reshape(g, r, 2 * ncp, QT)


def _overlap_t(L):
    n_cmp = (L - CMP_BLOCK) // CMP_STRIDE + 1
    n_slc = L // SLC_BLOCK
    cmp_idx = np.arange(n_cmp)[:, None] * CMP_STRIDE + np.arange(CMP_BLOCK)[None, :]
    overlap = ((cmp_idx[:, :, None] // SLC_BLOCK) == np.arange(n_slc)[None, None, :]).sum(1) / CMP_BLOCK
    out = np.zeros((N_SLC_PAD, L // CMP_STRIDE), np.float32)
    out[:n_slc, :n_cmp] = overlap.T
    return jnp.asarray(out, dtype=BF16)


def _split3(x):
    hi = x.astype(BF16)
    r1 = x - hi.astype(F32)
    mid = r1.astype(BF16)
    lo = (r1 - mid.astype(F32)).astype(BF16)
    return hi, mid, lo


def _nsa_kernel(q_ref, gl_ref, kc_ref, vcT_ref, strip_ref, ovT_ref, ks_ref, vsT_ref, kw_ref, vwT_ref, wb_ref,
                nb_ref, o_ref, gate_ref, *, n_slc):
    i = pl.program_id(2)
    R = Q_PER_KV
    RQ = R * QT
    RW = R * HEAD_DIM
    groups = range(kc_ref.shape[1])
    first_group = pl.program_id(1) * len(groups)
    pad_tile = ks_ref.shape[2] - PAD_TILES
    t_pos = lax.broadcasted_iota(jnp.int32, (1, QT), 1) + i * QT

    def flag_rows(n):
        return jnp.where(lax.broadcasted_iota(jnp.int32, (n, QT), 0) == 0, NEG, 0.0).astype(BF16)

    q_heads = []
    for gg in groups:
        qT = (q_ref[:, gg * RW:(gg + 1) * RW].T * (0.125 * LOG2E)).astype(BF16)
        q_heads.append([qT[r * HEAD_DIM:(r + 1) * HEAD_DIM] for r in range(R)])

    slab = WINDOW + QT
    n_win = slab // QT
    win_idx = [jnp.where(i + jj >= n_win - 1, i + jj - (n_win - 1), pad_tile) for jj in range(n_win)]

    def window_scores(gg):
        kw = jnp.concatenate([kw_ref[0, gg, j] for j in win_idx], axis=0)
        qwin = _lane_cat([jnp.concatenate([q_heads[gg][r], flag_rows(KW_COLS - HEAD_DIM)], axis=0)
                          for r in range(R)])
        wb = _lane_cat([wb_ref[gg, r] for r in range(R)])
        return jnp.dot(kw, qwin, preferred_element_type=F32) + wb

    def window_branch(gg, sw):
        m_w = jnp.max(sw, axis=0, keepdims=True)
        e_wb = jnp.exp2(sw - m_w).astype(BF16)
        acc = None
        for jj, j in enumerate(win_idx):
            term = jnp.dot(vwT_ref[0, gg, j], e_wb[jj * QT:(jj + 1) * QT], preferred_element_type=F32)
            acc = term if acc is None else acc + term
        return acc[:HEAD_DIM] * (1.0 / acc[V_ONES:V_ONES + 1])

    sw_all = [window_scores(gg) for gg in groups]

    ncp = kc_ref.shape[2]
    strip_row = pl.multiple_of(ncp - i * (QT // CMP_STRIDE), QT // CMP_STRIDE)
    ov = ovT_ref[...]
    s_ix = lax.broadcasted_iota(jnp.int32, (N_SLC_PAD, QT), 0)
    cur = t_pos >> int(math.log2(SLC_BLOCK))
    forced = (s_ix == 0) | (s_ix == cur) | (s_ix == cur - 1)

    def compressed_scores(gg):
        sc = jnp.dot(kc_ref[0, gg], _lane_cat(q_heads[gg]), preferred_element_type=F32)
        return sc + _lane_cat([strip_ref[gg, r, pl.ds(strip_row, ncp), :] for r in range(R)])

    def compressed_branch(gg, sm):
        m_c = jnp.max(sm, axis=0, keepdims=True)
        e_c = jnp.exp2(sm - m_c)
        l_c = jnp.sum(e_c, axis=0, keepdims=True)
        inv_c = jnp.where(m_c > 0.5 * NEG, 1.0 / l_c, 0.0)
        o_cmp = jnp.dot(vcT_ref[0, gg], e_c.astype(BF16), preferred_element_type=F32) * inv_c
        p_c = e_c * inv_c
        p_sum = p_c[:, 0:QT]
        for r in range(1, R):
            p_sum = p_sum + p_c[:, r * QT:(r + 1) * QT]
        imp = None
        for part in _split3(p_sum):
            term = jnp.dot(ov, part, preferred_element_type=F32)
            imp = term if imp is None else imp + term
        score = jnp.where(forced, FORCED_SCORE, jnp.where(s_ix <= cur, imp, -1.0))
        return o_cmp, jnp.where(s_ix < n_slc, score, -2.0)

    sm_all = [compressed_scores(gg) for gg in groups]
    o_win = [window_branch(gg, sw_all[gg]) for gg in groups]
    cmp_out = [compressed_branch(gg, sm_all[gg]) for gg in groups]
    o_cmp = [c[0] for c in cmp_out]
    scores = [c[1] for c in cmp_out]

    SUB = 8
    n_sub = N_SLC_PAD // SUB
    score_rows = [[score[v * SUB:(v + 1) * SUB] for v in range(n_sub)] for score in scores]
    sub_ix = lax.broadcasted_iota(jnp.int32, (SUB, QT), 0)

    vregs_per_seg = RANK_SEG // SUB
    n_seg = -(-n_slc // RANK_SEG)

    def rank_levels(levels, ranks):
        out = []
        for gg in groups:
            rk = list(ranks[gg])
            for k in levels:
                for seg in range(k + 1):
                    rows = range((k + 1) * vregs_per_seg) if seg == k else range(k * vregs_per_seg,
                                                                                 (k + 1) * vregs_per_seg)
                    for sp in range(seg * RANK_SEG, min((seg + 1) * RANK_SEG, n_slc)):
                        row = scores[gg][sp:sp + 1, :]
                        for v in rows:
                            blk = score_rows[gg][v]
                            if v * SUB > sp:
                                beats = row >= blk
                            elif v * SUB + SUB - 1 <= sp:
                                beats = row > blk
                            else:
                                beats = (row > blk) | ((row == blk) & (sub_ix > sp - v * SUB))
                            rk[v] = rk[v] + jnp.where(beats, 1.0, 0.0)
            out.append(tuple(rk))
        return tuple(out)

    ranks = tuple(tuple(jnp.zeros((SUB, QT), F32) for _ in range(n_sub)) for _ in groups)
    last_block = (i + 1) * (QT // SLC_BLOCK) - 1
    for levels in [(0, 1)] + [(k,) for k in range(2, n_seg)]:
        first = max(levels) * RANK_SEG if max(levels) > 1 else SLC_TOPK
        ranks = lax.cond(last_block >= first, functools.partial(rank_levels, levels), lambda rk: rk, ranks)

    qsel = []
    for gg in groups:
        sel_neg = jnp.where(jnp.concatenate(ranks[gg], axis=0) < float(SLC_TOPK), 0.0, UNSELECTED).astype(BF16)
        qsel.append(_lane_cat([jnp.concatenate([q_heads[gg][r], sel_neg, flag_rows(KS_COLS - KS_FLAG)], axis=0)
                               for r in range(R)]))

    n_far = jnp.maximum(i - 1, 0)

    def far_tiles(j):
        return [jnp.where(j + n < n_far, j + n, pad_tile) for n in range(FAR_TILES)]

    def tile_scores(gg, tiles):
        keys = jnp.concatenate([ks_ref[0, gg, t] for t in tiles], axis=0)
        return jnp.dot(keys, qsel[gg], preferred_element_type=F32)

    def online_step(gg, carry, s, tiles):
        m, acc = carry
        m_new = jnp.maximum(m, jnp.max(s, axis=0, keepdims=True))
        p = jnp.exp2(s - m_new).astype(BF16)
        acc = jnp.exp2(m - m_new) * acc
        for n, t in enumerate(tiles):
            acc = acc + jnp.dot(vsT_ref[0, gg, t], p[n * QT:(n + 1) * QT], preferred_element_type=F32)
        return m_new, acc

    def far_trip(u, carries):
        tiles = far_tiles(FAR_TILES * u)
        s_all = [tile_scores(gg, tiles) for gg in groups]
        return tuple(online_step(gg, carries[gg], s_all[gg], tiles) for gg in groups)

    init = (jnp.full((1, RQ), NEG, F32), jnp.zeros((V_ROWS, RQ), F32))
    carries = lax.fori_loop(0, (n_far + FAR_TILES - 1) // FAR_TILES, far_trip, tuple(init for _ in groups))
    near = [jnp.where(i >= 1, i - 1, pad_tile), i]

    gate_ref[...] = jax.nn.sigmoid(gl_ref[...].T)
    s_near = [tile_scores(gg, near) + _lane_cat([nb_ref[gg, r] for r in range(R)]) for gg in groups]
    for gg in groups:
        _, acc_s = online_step(gg, carries[gg], s_near[gg], near)
        o_slc = acc_s[:HEAD_DIM] * (1.0 / acc_s[V_ONES:V_ONES + 1])
        head0 = (first_group + gg) * R
        gates = [_lane_cat([gate_ref[pl.ds((head0 + r) * 3 + br, 1), :] for r in range(R)]) for br in range(3)]
        total = gates[0] * o_cmp[gg] + gates[1] * o_slc + gates[2] * o_win[gg]
        o_ref[:, gg * RW:(gg + 1) * RW] = jnp.concatenate(
            [total[:, r * QT:(r + 1) * QT] for r in range(R)], axis=0).T.astype(o_ref.dtype)


def _nsa_mixer(proj, bsz, L, pe_ck, w_ck1, w_ck2, pe_cv, w_cv1, w_cv2, rel_table):
    g, r = KV_HEADS, Q_PER_KV
    ng = NSA_GROUPS
    ni = L // QT
    nt = L // QT + PAD_TILES
    ncp = L // CMP_STRIDE
    n_slc = L // SLC_BLOCK
    slab = WINDOW + QT
    k_cmp = _compress(proj, COL_KC, bsz, L, pe_ck, w_ck1, w_ck2, transpose_out=False)
    v_cmp_t = _compress(proj, COL_VC, bsz, L, pe_cv, w_cv1, w_cv2, transpose_out=True)
    ks_t, vs_t, kw_t, vw_t = _kv_tiles(proj, bsz, L)
    wb, nb, strip = _nsa_tables(rel_table, L)
    qw = ng * r * HEAD_DIM
    once = pl.Buffered(1)
    whole = lambda *shape: pl.BlockSpec((1, ng) + shape, lambda b, h, i: (b, h) + (0,) * len(shape), once)
    per_group = lambda *shape: pl.BlockSpec((ng,) + shape, lambda b, h, i: (h,) + (0,) * len(shape), once)
    return pl.pallas_call(
        functools.partial(_nsa_kernel, n_slc=n_slc),
        grid=(bsz, g // ng, ni),
        in_specs=[
            pl.BlockSpec((QT, qw), lambda b, h, i: (b * ni + i, COL_Q // qw + h)),
            pl.BlockSpec((QT, LANES), lambda b, h, i: (b * ni + i, COL_GATE // LANES)),
            whole(ncp, HEAD_DIM),
            whole(HEAD_DIM, ncp),
            per_group(r, 2 * ncp, QT),
            pl.BlockSpec((N_SLC_PAD, ncp), lambda b, h, i: (0, 0), once),
            whole(nt, QT, KS_COLS),
            whole(nt, V_ROWS, QT),
            whole(nt, QT, KW_COLS),
            whole(nt, V_ROWS, QT),
            per_group(r, slab, QT),
            per_group(r, 2 * QT, QT),
        ],
        out_specs=pl.BlockSpec((QT, qw), lambda b, h, i: (b * ni + i, h)),
        out_shape=jax.ShapeDtypeStruct((bsz * L, NSA_WIDTH), BF16),
        scratch_shapes=[pltpu.VMEM((LANES, QT), F32)],
        compiler_params=_cparams("parallel", "parallel", "arbitrary"),
        name="nsa_attention",
    )(proj, proj, k_cmp, v_cmp_t, strip, _overlap_t(L), ks_t, vs_t, kw_t, vw_t, wb, nb)


def _out_proj_kernel(x_ref, ys_ref, yn_ref, w_ref, o_ref):
    half = ys_ref.shape[1]
    acc = jnp.dot(ys_ref[...], w_ref[:half, :], preferred_element_type=F32)
    acc = acc + jnp.dot(yn_ref[...], w_ref[half:, :], preferred_element_type=F32)
    o_ref[...] = x_ref[...] + acc


def _out_proj(x2, y_ssm, y_nsa, w, tm=512):
    t, d = x2.shape
    return pl.pallas_call(
        _out_proj_kernel,
        grid=(t // tm,),
        in_specs=[pl.BlockSpec((tm, d), lambda i: (i, 0)),
                  pl.BlockSpec((tm, SSM_WIDTH), lambda i: (i, 0)),
                  pl.BlockSpec((tm, NSA_WIDTH), lambda i: (i, 0)),
                  pl.BlockSpec((SSM_WIDTH + NSA_WIDTH, d), lambda i: (0, 0), pl.Buffered(1))],
        out_specs=pl.BlockSpec((tm, d), lambda i: (i, 0)),
        out_shape=jax.ShapeDtypeStruct((t, d), F32),
        compiler_params=_cparams("parallel"),
        name="out_proj",
    )(x2, y_ssm, y_nsa, w)


def _mlp_kernel(x_ref, n2_ref, wu_ref, wd_ref, nf_ref, o_ref, h_ref, acc_ref):
    f = pl.program_id(1)

    @pl.when(f == 0)
    def _():
        h_ref[...] = _rms(x_ref[...], n2_ref[...]).astype(BF16)
        acc_ref[...] = jnp.zeros_like(acc_ref)

    a = jnp.dot(h_ref[...], wu_ref[...], preferred_element_type=F32)
    a = jnp.square(jnp.maximum(a, 0.0))
    acc_ref[...] += jnp.dot(a.astype(BF16), wd_ref[...], preferred_element_type=F32)

    @pl.when(f == pl.num_programs(1) - 1)
    def _():
        o_ref[...] = _rms(x_ref[...] + acc_ref[...], nf_ref[...])


def _mlp(x2, n2, wu, wd, nf, tm=512, tf=1024):
    t, d = x2.shape
    ff = wu.shape[1]
    return pl.pallas_call(
        _mlp_kernel,
        grid=(t // tm, ff // tf),
        in_specs=[pl.BlockSpec((tm, d), lambda i, f: (i, 0)),
                  pl.BlockSpec((1, d), lambda i, f: (0, 0)),
                  pl.BlockSpec((d, tf), lambda i, f: (0, f)),
                  pl.BlockSpec((tf, d), lambda i, f: (f, 0)),
                  pl.BlockSpec((1, d), lambda i, f: (0, 0))],
        out_specs=pl.BlockSpec((tm, d), lambda i, f: (i, 0)),
        out_shape=jax.ShapeDtypeStruct((t, d), F32),
        scratch_shapes=[pltpu.VMEM((tm, d), BF16), pltpu.VMEM((tm, d), F32)],
        compiler_params=_cparams("parallel", "arbitrary"),
        name="mlp_final_norm",
    )(x2, n2, wu, wd, nf)


def kernel(x, norm1_w, w_in, ssm_a_re, ssm_a_im, ssm_log_dt, ssm_b_re, ssm_b_im, ssm_c_re, ssm_c_im, ssm_d,
           w_glu, b_glu, pe_ck, w_ck1, w_ck2, pe_cv, w_cv1, w_cv2, w_out, norm2_w, w_up, w_down, rel_table,
           norm_f_w):
    bsz, L, d = x.shape
    assert w_in.shape[0] == 1, "the closing rmsnorm is fused into the (single) layer's MLP kernel"
    x2 = x.reshape(bsz * L, d)
    w_in_p = jnp.pad(w_in[0].astype(BF16), ((0, 0), (0, D_IN_PAD - D_IN)))
    proj = _norm_matmul(x2, norm1_w[0].reshape(1, d), w_in_p)
    ops = _s5_operators(ssm_a_re[0], ssm_a_im[0], ssm_log_dt[0], ssm_b_re[0], ssm_b_im[0], ssm_c_re[0], ssm_c_im[0],
                        ssm_d[0])
    y_ssm = _s5_mixer(proj, bsz, L, ops, w_glu[0], b_glu[0])
    y_nsa = _nsa_mixer(proj, bsz, L, pe_ck[0], w_ck1[0], w_ck2[0], pe_cv[0], w_cv1[0], w_cv2[0], rel_table)
    x2 = _out_proj(x2, y_ssm, y_nsa, w_out[0].astype(BF16))
    x2 = _mlp(x2, norm2_w[0].reshape(1, d), w_up[0].astype(BF16), w_down[0].astype(BF16), norm_f_w.reshape(1, d))
    return x2.reshape(bsz, L, d)
```

```python
import functools
import math

import numpy as np
import jax
import jax.numpy as jnp
from jax import lax
from jax.experimental import pallas as pl
from jax.experimental.pallas import tpu as pltpu

F32 = jnp.float32
BF16 = jnp.bfloat16

SSM_WIDTH = 1024
SSM_GROUP = 16
SSM_GROUPS = 64
SSM_STATE = 64
NSA_WIDTH = 1024
HEAD_DIM = 64
NSA_HEADS = 16
KV_HEADS = 4
Q_PER_KV = 4
KV_WIDTH = 256
CMP_BLOCK = 32
CMP_STRIDE = 16
CMP_HIDDEN = 256
SLC_BLOCK = 64
SLC_TOPK = 16
WINDOW = 512
REL_BUCKETS = 32
REL_MAX_DIST = 128
EPS = 1e-6
NEG = -1e30
FORCED_SCORE = 1e4
D_IN = SSM_WIDTH + NSA_WIDTH + 6 * KV_WIDTH + 3 * NSA_HEADS
D_IN_PAD = 3712
COL_Q = SSM_WIDTH
COL_KC = COL_Q + NSA_WIDTH
COL_VC = COL_KC + KV_WIDTH
COL_KS = COL_VC + KV_WIDTH
COL_VS = COL_KS + KV_WIDTH
COL_KW = COL_VS + KV_WIDTH
COL_VW = COL_KW + KV_WIDTH
COL_GATE = COL_VW + KV_WIDTH

LANES = 128
S5_CHUNK = 16
SLAB_GROUPS = LANES // SSM_GROUP
N_SLABS = SSM_GROUPS // SLAB_GROUPS
QT = 128
N_SLC_PAD = 64
RANK_SEG = 16
NSA_GROUPS = 4
FAR_TILES = 4
UNSELECTED = -1e9
LOG2E = math.log2(math.e)
TABLE_W = 1024
VMEM_LIMIT = 56 * 1024 * 1024
HI = lax.Precision.HIGHEST


def _cparams(*sem):
    return pltpu.CompilerParams(dimension_semantics=sem, vmem_limit_bytes=VMEM_LIMIT)


def _rms(x, w):
    ms = jnp.mean(x * x, axis=-1, keepdims=True)
    return x * lax.rsqrt(ms + EPS) * w


def _lane_cat(parts):
    return jnp.concatenate(parts, axis=1)


def _norm_matmul_kernel(x_ref, nw_ref, w_ref, o_ref):
    h = _rms(x_ref[...], nw_ref[...]).astype(BF16)
    o_ref[...] = jnp.dot(h, w_ref[...], preferred_element_type=F32)


def _norm_matmul(x2, nw, w, tm=512):
    t, d = x2.shape
    n = w.shape[1]
    return pl.pallas_call(
        _norm_matmul_kernel,
        grid=(t // tm,),
        in_specs=[pl.BlockSpec((tm, d), lambda i: (i, 0)),
                  pl.BlockSpec((1, d), lambda i: (0, 0)),
                  pl.BlockSpec((d, n), lambda i: (0, 0), pl.Buffered(1))],
        out_specs=pl.BlockSpec((tm, n), lambda i: (i, 0)),
        out_shape=jax.ShapeDtypeStruct((t, n), F32),
        compiler_params=_cparams("parallel"),
        name="norm_in_proj",
    )(x2, nw, w)


def _s5_operators(a_re, a_im, log_dt, b_re, b_im, c_re, c_im, d):
    q = S5_CHUNK
    g, p = a_re.shape
    h = SSM_GROUP
    dt = jnp.exp(log_dt)[:, None]
    lam_re, lam_im = dt * a_re, dt * a_im
    mag1 = jnp.exp(lam_re)
    abar_re, abar_im = mag1 * jnp.cos(lam_im), mag1 * jnp.sin(lam_im)
    den = a_re * a_re + a_im * a_im
    zr, zi = abar_re - 1.0, abar_im
    coef_re = (zr * a_re + zi * a_im) / den
    coef_im = (zi * a_re - zr * a_im) / den
    bb_re = coef_re[..., None] * b_re - coef_im[..., None] * b_im
    bb_im = coef_re[..., None] * b_im + coef_im[..., None] * b_re
    k = jnp.arange(q + 1, dtype=F32)[:, None, None]
    mag = jnp.exp(k * lam_re)
    pw_re, pw_im = mag * jnp.cos(k * lam_im), mag * jnp.sin(k * lam_im)
    m_re = pw_re[..., None] * bb_re - pw_im[..., None] * bb_im
    m_im = pw_re[..., None] * bb_im + pw_im[..., None] * bb_re
    kern = (jnp.einsum('ghp,kgpi->gkhi', c_re, m_re[:q], precision=HI)
            - jnp.einsum('ghp,kgpi->gkhi', c_im, m_im[:q], precision=HI))
    kern = kern.at[:, 0].add(jax.vmap(jnp.diag)(d))
    rev = np.arange(q - 1, -1, -1)
    s_end = jnp.stack([m_re[rev], m_im[rev]], axis=0).transpose(2, 1, 4, 0, 3)
    pr, pi = pw_re[1:], pw_im[1:]
    o_re = c_re[None] * pr[:, :, None, :] - c_im[None] * pi[:, :, None, :]
    o_im = -(c_re[None] * pi[:, :, None, :] + c_im[None] * pr[:, :, None, :])
    o_carry = jnp.stack([o_re, o_im], axis=0).transpose(2, 0, 4, 1, 3)
    a_q = jnp.stack([pw_re[q], pw_im[q]], axis=0)
    return kern, s_end, o_carry, a_q


def _s5_slab_weights(kern, s_end, o_carry, a_q):
    q, h, p = S5_CHUNK, SSM_GROUP, SSM_STATE
    s_dense = s_end.reshape(SSM_GROUPS, q, h, 2 * p).astype(BF16)
    k_dense = kern.transpose(0, 1, 3, 2).astype(BF16)
    o_dense = o_carry.reshape(SSM_GROUPS, 2, p, q * h).astype(BF16)
    a_re = a_q[0].reshape(N_SLABS, SLAB_GROUPS * p)
    a_im = a_q[1].reshape(N_SLABS, SLAB_GROUPS * p)
    return s_dense, k_dense, o_dense, a_re, a_im


def _chunk_rows(u_ref, nc):
    return _lane_cat([u_ref[pl.ds(s, nc, stride=S5_CHUNK), :] for s in range(S5_CHUNK)]).astype(BF16)


def _s5_state_kernel(u_ref, sd_ref, o_ref, ws_scr):
    nc = u_ref.shape[0] // S5_CHUNK
    q, h, p, sg = S5_CHUNK, SSM_GROUP, SSM_STATE, SLAB_GROUPS

    @pl.when(pl.program_id(1) == 0)
    def _():
        ws_scr[...] = jnp.zeros_like(ws_scr)
        for s in range(q):
            for a in range(sg):
                for c in range(2):
                    ws_scr[s * LANES + a * h: s * LANES + (a + 1) * h,
                           (c * sg + a) * p: (c * sg + a + 1) * p] = sd_ref[a, s, :, c * p:(c + 1) * p]

    o_ref[0] = jnp.dot(_chunk_rows(u_ref, nc), ws_scr[...], preferred_element_type=F32)


def _s5_scan_kernel(h_ref, ar_ref, ai_ref, o_ref):
    half = ar_ref.shape[1]
    ar = ar_ref[...]
    ai = ai_ref[...]

    def body(c, carry):
        hr, hi = carry
        o_ref[0, c, :, :half] = hr
        o_ref[0, c, :, half:] = hi
        xr = h_ref[0, c, :, :half]
        xi = h_ref[0, c, :, half:]
        return ar * hr - ai * hi + xr, ar * hi + ai * hr + xi

    zero = jnp.zeros(ar.shape, F32)
    lax.fori_loop(0, h_ref.shape[1], body, (zero, zero))


def _s5_out_kernel(u_ref, kd_ref, h_ref, od_ref, y_ref, kl_scr, wi_scr, wc_scr):
    nc = u_ref.shape[0] // S5_CHUNK
    q, h, p, sg = S5_CHUNK, SSM_GROUP, SSM_STATE, SLAB_GROUPS

    @pl.when(pl.program_id(1) == 0)
    def _():
        kl_scr[...] = jnp.zeros_like(kl_scr)
        for k in range(q):
            for a in range(sg):
                kl_scr[k, a * h:(a + 1) * h, a * h:(a + 1) * h] = kd_ref[a, k]
        for s in range(q):
            for t in range(s, q):
                wi_scr[s * LANES:(s + 1) * LANES, t * LANES:(t + 1) * LANES] = kl_scr[t - s]
        for t in range(0, q, 2):
            wi_scr[(t + 1) * LANES:(t + 2) * LANES, t * LANES:(t + 1) * LANES] = jnp.zeros((LANES, LANES), BF16)
        wc_scr[...] = jnp.zeros_like(wc_scr)
        for a in range(sg):
            for c in range(2):
                for t in range(q):
                    wc_scr[(c * sg + a) * p:(c * sg + a + 1) * p,
                           t * LANES + a * h: t * LANES + (a + 1) * h] = od_ref[a, c, :, t * h:(t + 1) * h]

    u_rows = _chunk_rows(u_ref, nc)
    h_rows = h_ref[0].astype(BF16)
    pair = 2 * LANES
    for tp in range(q // 2):
        cols = slice(tp * pair, (tp + 1) * pair)
        k_used = (tp + 1) * pair
        y = (jnp.dot(u_rows[:, :k_used], wi_scr[:k_used, cols], preferred_element_type=F32)
             + jnp.dot(h_rows, wc_scr[:, cols], preferred_element_type=F32))
        for tt in range(2):
            y_ref[pl.ds(2 * tp + tt, nc, stride=q), :] = y[:, tt * LANES:(tt + 1) * LANES]


def _glu_kernel(y_ref, w_ref, b_ref, o_ref):
    z = jax.nn.gelu(y_ref[...])
    gate = jnp.dot(z.astype(BF16), w_ref[...], preferred_element_type=F32) + b_ref[...]
    o_ref[...] = (z * jax.nn.sigmoid(gate)).astype(o_ref.dtype)


def _s5_mixer(proj, bsz, L, ops, w_glu, b_glu):
    s_dense, k_dense, o_dense, a_re, a_im = _s5_slab_weights(*ops)
    q, h, p = S5_CHUNK, SSM_GROUP, SSM_STATE
    nc = L // q
    ns, sg = N_SLABS, SLAB_GROUPS
    sw = sg * 2 * p
    kq = q * LANES
    hend = pl.pallas_call(
        _s5_state_kernel,
        grid=(ns, bsz),
        in_specs=[pl.BlockSpec((L, LANES), lambda j, b: (b, j)),
                  pl.BlockSpec((sg, q, h, 2 * p), lambda j, b: (j, 0, 0, 0))],
        out_specs=pl.BlockSpec((1, nc, sw), lambda j, b: (b, 0, j)),
        out_shape=jax.ShapeDtypeStruct((bsz, nc, ns * sw), F32),
        scratch_shapes=[pltpu.VMEM((kq, sw), BF16)],
        compiler_params=_cparams("parallel", "arbitrary"),
        name="s5_chunk_state",
    )(proj, s_dense)
    hprev4 = pl.pallas_call(
        _s5_scan_kernel,
        grid=(bsz,),
        in_specs=[pl.BlockSpec((1, nc, ns, sw), lambda b: (b, 0, 0, 0)),
                  pl.BlockSpec((ns, sw // 2), lambda b: (0, 0)),
                  pl.BlockSpec((ns, sw // 2), lambda b: (0, 0))],
        out_specs=pl.BlockSpec((1, nc, ns, sw), lambda b: (b, 0, 0, 0)),
        out_shape=jax.ShapeDtypeStruct((bsz, nc, ns, sw), F32),
        compiler_params=_cparams("parallel"),
        name="s5_chunk_scan",
    )(hend.reshape(bsz, nc, ns, sw), a_re, a_im)
    y = pl.pallas_call(
        _s5_out_kernel,
        grid=(ns, bsz),
        in_specs=[pl.BlockSpec((L, LANES), lambda j, b: (b, j)),
                  pl.BlockSpec((sg, q, h, h), lambda j, b: (j, 0, 0, 0)),
                  pl.BlockSpec((1, nc, sw), lambda j, b: (b, 0, j)),
                  pl.BlockSpec((sg, 2, p, q * h), lambda j, b: (j, 0, 0, 0))],
        out_specs=pl.BlockSpec((L, LANES), lambda j, b: (b, j)),
        out_shape=jax.ShapeDtypeStruct((bsz * L, SSM_WIDTH), F32),
        scratch_shapes=[pltpu.VMEM((q, LANES, LANES), BF16), pltpu.VMEM((kq, kq), BF16), pltpu.VMEM((sw, kq), BF16)],
        compiler_params=_cparams("parallel", "arbitrary"),
        name="s5_chunk_out",
    )(proj, k_dense, hprev4.reshape(bsz, nc, ns * sw), o_dense)
    tm = 512
    return pl.pallas_call(
        _glu_kernel,
        grid=(bsz * L // tm,),
        in_specs=[pl.BlockSpec((tm, SSM_WIDTH), lambda i: (i, 0)),
                  pl.BlockSpec((SSM_WIDTH, SSM_WIDTH), lambda i: (0, 0)),
                  pl.BlockSpec((1, SSM_WIDTH), lambda i: (0, 0))],
        out_specs=pl.BlockSpec((tm, SSM_WIDTH), lambda i: (i, 0)),
        out_shape=jax.ShapeDtypeStruct((bsz * L, SSM_WIDTH), BF16),
        compiler_params=_cparams("parallel"),
        name="s5_gelu_glu",
    )(y, w_glu.astype(BF16), b_glu.reshape(1, SSM_WIDTH))


def _compress_kernel(k_ref, w1_ref, pe_ref, w1full_ref, w2_ref, o_ref, *, transpose_out):
    nc = k_ref.shape[0] // CMP_STRIDE
    prod = jnp.dot(_chunk_rows(k_ref, nc), w1_ref[...], preferred_element_type=F32)
    pe_h = jnp.dot(jnp.broadcast_to(pe_ref[...], (8, pe_ref.shape[1])), w1full_ref[...],
                   precision=HI, preferred_element_type=F32)[0:1]
    two_h = 2 * CMP_HIDDEN
    for gl in range(LANES // HEAD_DIM):
        first = prod[:, gl * two_h: gl * two_h + CMP_HIDDEN]
        second = prod[:, gl * two_h + CMP_HIDDEN: (gl + 1) * two_h]
        hid = first + pltpu.roll(second, nc - 1, axis=0) + pe_h
        out = jnp.dot(jax.nn.gelu(hid).astype(BF16), w2_ref[...], preferred_element_type=F32)
        if transpose_out:
            o_ref[0, gl] = _lane_cat([out, jnp.zeros_like(out)]).T[:HEAD_DIM].astype(BF16)
        else:
            o_ref[0, gl] = out.astype(BF16)


def _compress(proj, col, bsz, L, pe, w1, w2, transpose_out):
    nc = L // CMP_STRIDE
    half = CMP_STRIDE * HEAD_DIM
    gpl = LANES // HEAD_DIM
    w1cat = jnp.concatenate([w1[:half], w1[half:]], axis=1)
    eye = jnp.eye(gpl, dtype=F32)
    w1slab = jnp.einsum('ldh,ag->ladgh', w1cat.reshape(CMP_STRIDE, HEAD_DIM, 2 * CMP_HIDDEN), eye)
    w1slab = w1slab.reshape(CMP_STRIDE * LANES, gpl * 2 * CMP_HIDDEN).astype(BF16)
    out_block = (1, gpl, HEAD_DIM, nc) if transpose_out else (1, gpl, nc, HEAD_DIM)
    out_full = (bsz, KV_HEADS) + out_block[2:]
    return pl.pallas_call(
        functools.partial(_compress_kernel, transpose_out=transpose_out),
        grid=(bsz, KV_HEADS // gpl),
        in_specs=[pl.BlockSpec((L, LANES), lambda b, j: (b, col // LANES + j)),
                  pl.BlockSpec(w1slab.shape, lambda b, j: (0, 0)),
                  pl.BlockSpec((1, 2 * half), lambda b, j: (0, 0)),
                  pl.BlockSpec((2 * half, CMP_HIDDEN), lambda b, j: (0, 0)),
                  pl.BlockSpec((CMP_HIDDEN, HEAD_DIM), lambda b, j: (0, 0))],
        out_specs=pl.BlockSpec(out_block, lambda b, j: (b, j, 0, 0)),
        out_shape=jax.ShapeDtypeStruct(out_full, BF16),
        compiler_params=_cparams("parallel", "parallel"),
        name="nsa_compress",
    )(proj, w1slab, pe.reshape(1, 2 * half), w1, w2.astype(BF16))


KS_COLS = 2 * LANES
KW_COLS = LANES
KS_FLAG = HEAD_DIM + N_SLC_PAD
KW_FLAG = HEAD_DIM
PAD_TILES = 8
V_ROWS = HEAD_DIM + 16
V_ONES = HEAD_DIM


def _kv_tiles_kernel(ks_ref, vs_ref, kw_ref, vw_ref, kso_ref, vso_ref, kwo_ref, vwo_ref):
    tm = ks_ref.shape[0]
    step = pl.program_id(1)
    row0 = step * tm
    kso_ref[:, :, :, :, KS_FLAG:] = jnp.zeros(kso_ref.shape[:4] + (KS_COLS - KS_FLAG,), BF16)
    kwo_ref[:, :, :, :, KW_FLAG:] = jnp.zeros(kwo_ref.shape[:4] + (KW_COLS - KW_FLAG,), BF16)
    is_real = step < pl.num_programs(1) - 1
    tail_rows = lax.broadcasted_iota(jnp.int32, vso_ref.shape[:3] + (V_ROWS - HEAD_DIM, QT), 3)
    v_tail = jnp.where(tail_rows == 0, jnp.where(is_real, 1.0, 0.0), 0.0).astype(BF16)
    vso_ref[:, :, :, HEAD_DIM:, :] = v_tail
    vwo_ref[:, :, :, HEAD_DIM:, :] = v_tail

    @pl.when(step < pl.num_programs(1) - 1)
    def _():
        lane_blk = lax.broadcasted_iota(jnp.int32, (QT, N_SLC_PAD), 1)
        for k in range(tm // QT):
            rows = slice(k * QT, (k + 1) * QT)
            tok = lax.broadcasted_iota(jnp.int32, (QT, N_SLC_PAD), 0) + (row0 + k * QT)
            onehot = jnp.where((tok >> int(math.log2(SLC_BLOCK))) == lane_blk, 1.0, 0.0).astype(BF16)
            vs_t = vs_ref[rows, :].T
            vw_t = vw_ref[rows, :].T
            for g in range(KV_HEADS):
                cols = slice(g * HEAD_DIM, (g + 1) * HEAD_DIM)
                kso_ref[0, g, k, :, :HEAD_DIM] = ks_ref[rows, cols].astype(BF16)
                kso_ref[0, g, k, :, HEAD_DIM:KS_FLAG] = onehot
                kwo_ref[0, g, k, :, :HEAD_DIM] = kw_ref[rows, cols].astype(BF16)
                vso_ref[0, g, k, :HEAD_DIM, :] = vs_t[cols].astype(BF16)
                vwo_ref[0, g, k, :HEAD_DIM, :] = vw_t[cols].astype(BF16)

    @pl.when(step == pl.num_programs(1) - 1)
    def _():
        ones = jnp.ones(kso_ref.shape[:4] + (1,), BF16)
        kso_ref[:, :, :, :, :KS_FLAG] = jnp.zeros(kso_ref.shape[:4] + (KS_FLAG,), BF16)
        kwo_ref[:, :, :, :, :KW_FLAG] = jnp.zeros(kwo_ref.shape[:4] + (KW_FLAG,), BF16)
        kso_ref[:, :, :, :, KS_FLAG:KS_FLAG + 1] = ones
        kwo_ref[:, :, :, :, KW_FLAG:KW_FLAG + 1] = ones
        vso_ref[:, :, :, :HEAD_DIM, :] = jnp.zeros(vso_ref.shape[:3] + (HEAD_DIM, QT), BF16)
        vwo_ref[:, :, :, :HEAD_DIM, :] = jnp.zeros(vwo_ref.shape[:3] + (HEAD_DIM, QT), BF16)


def _kv_tiles(proj, bsz, L):
    nt = L // QT
    g = KV_HEADS
    kt = PAD_TILES
    tm = kt * QT
    steps = L // tm
    in_spec = lambda col: pl.BlockSpec((tm, KV_WIDTH),
                                       lambda b, i: (b * steps + jnp.minimum(i, steps - 1), col // KV_WIDTH))
    out_spec = lambda a, c: pl.BlockSpec((1, g, kt, a, c), lambda b, i: (b, 0, i, 0, 0))
    shape = lambda a, c: jax.ShapeDtypeStruct((bsz, g, nt + kt, a, c), BF16)
    return pl.pallas_call(
        _kv_tiles_kernel,
        grid=(bsz, steps + 1),
        in_specs=[in_spec(COL_KS), in_spec(COL_VS), in_spec(COL_KW), in_spec(COL_VW)],
        out_specs=[out_spec(QT, KS_COLS), out_spec(V_ROWS, QT), out_spec(QT, KW_COLS), out_spec(V_ROWS, QT)],
        out_shape=[shape(QT, KS_COLS), shape(V_ROWS, QT), shape(QT, KW_COLS), shape(V_ROWS, QT)],
        compiler_params=_cparams("parallel", "arbitrary"),
        name="nsa_kv_tiles",
    )(proj, proj, proj, proj)


def _rel_bucket(dist):
    dist = jnp.maximum(dist, 0)
    max_exact = REL_BUCKETS // 2
    large = max_exact + (jnp.log(jnp.maximum(dist, 1).astype(F32) / max_exact)
                         / math.log(REL_MAX_DIST / max_exact) * (REL_BUCKETS - max_exact)).astype(jnp.int32)
    large = jnp.minimum(large, REL_BUCKETS - 1)
    return jnp.where(dist < max_exact, dist, large)


def _toeplitz_rows(base, nrows, step):
    w = base.shape[1]
    x = jnp.broadcast_to(base, (nrows, w))
    k_ix = lax.broadcasted_iota(jnp.int32, (nrows, w), 0)
    bit = 1
    while bit < nrows:
        x = jnp.where((k_ix & bit) != 0, pltpu.roll(x, (bit * step) % w, axis=1), x)
        bit *= 2
    return x


def _bias_tables_kernel(bw_ref, bc_ref, far_ref, wb_ref, nb_ref, strip_ref, *, ncp):
    slab = WINDOW + QT
    y = _toeplitz_rows(bw_ref[0], QT, 1)
    for kh in range(slab // QT):
        tile = y[:, WINDOW - kh * QT: WINDOW - kh * QT + QT]
        wb_ref[0, kh * QT:(kh + 1) * QT, :] = tile
        if kh >= slab // QT - 2:
            row = kh - (slab // QT - 2)
            nb_ref[0, row * QT:(row + 1) * QT, :] = tile - far_ref[0]
    band = 2 * CMP_STRIDE
    z = _toeplitz_rows(bc_ref[0], band, CMP_STRIDE)[:, 2 * QT: 3 * QT]
    strip_ref[0, :ncp - CMP_STRIDE, :] = jnp.broadcast_to(far_ref[0], (ncp - CMP_STRIDE, QT))
    strip_ref[0, ncp - CMP_STRIDE: ncp + CMP_STRIDE, :] = z
    strip_ref[0, ncp + CMP_STRIDE:, :] = jnp.full((ncp - CMP_STRIDE, QT), NEG, F32)


def _nsa_tables(rel_table, L):
    ncp = L // CMP_STRIDE
    slab = WINDOW + QT
    dist = jnp.arange(TABLE_W)
    rel_table = rel_table * LOG2E
    per_dist = rel_table[_rel_bucket(dist)].T
    base_w = jnp.where(dist < WINDOW, per_dist, NEG)
    shift = CMP_BLOCK - 1
    shifted = jnp.concatenate([jnp.full((NSA_HEADS, shift), NEG, F32), per_dist[:, :TABLE_W - shift]], axis=1)
    base_c = jnp.where(dist < TABLE_W // 2, shifted, NEG)
    far = jnp.broadcast_to(rel_table[REL_BUCKETS - 1][:, None, None], (NSA_HEADS, 1, QT))
    wb, nb, strip = pl.pallas_call(
        functools.partial(_bias_tables_kernel, ncp=ncp),
        grid=(NSA_HEADS,),
        in_specs=[pl.BlockSpec((1, 1, TABLE_W), lambda h: (h, 0, 0)),
                  pl.BlockSpec((1, 1, TABLE_W), lambda h: (h, 0, 0)),
                  pl.BlockSpec((1, 1, QT), lambda h: (h, 0, 0))],
        out_specs=[pl.BlockSpec((1, slab, QT), lambda h: (h, 0, 0)),
                   pl.BlockSpec((1, 2 * QT, QT), lambda h: (h, 0, 0)),
                   pl.BlockSpec((1, 2 * ncp, QT), lambda h: (h, 0, 0))],
        out_shape=[jax.ShapeDtypeStruct((NSA_HEADS, slab, QT), F32),
                   jax.ShapeDtypeStruct((NSA_HEADS, 2 * QT, QT), F32),
                   jax.ShapeDtypeStruct((NSA_HEADS, 2 * ncp, QT), F32)],
        compiler_params=_cparams("parallel"),
        name="nsa_bias_tables",
    )(base_w.reshape(NSA_HEADS, 1, TABLE_W), base_c.reshape(NSA_HEADS, 1, TABLE_W), far)
    g, r = KV_HEADS, Q_PER_KV
    return wb.reshape(g, r, slab, QT), nb.reshape(g, r, 2 * QT, QT), strip.reshape(g, r, 2 * ncp, QT)


def _overlap_t(L):
    n_cmp = (L - CMP_BLOCK) // CMP_STRIDE + 1
    n_slc = L // SLC_BLOCK
    cmp_idx = np.arange(n_cmp)[:, None] * CMP_STRIDE + np.arange(CMP_BLOCK)[None, :]
    overlap = ((cmp_idx[:, :, None] // SLC_BLOCK) == np.arange(n_slc)[None, None, :]).sum(1) / CMP_BLOCK
    out = np.zeros((N_SLC_PAD, L // CMP_STRIDE), np.float32)
    out[:n_slc, :n_cmp] = overlap.T
    return jnp.asarray(out, dtype=BF16)


def _split3(x):
    hi = x.astype(BF16)
    r1 = x - hi.astype(F32)
    mid = r1.astype(BF16)
    lo = (r1 - mid.astype(F32)).astype(BF16)
    return hi, mid, lo


def _nsa_kernel(q_ref, gl_ref, kc_ref, vcT_ref, strip_ref, ovT_ref, ks_ref, vsT_ref, kw_ref, vwT_ref, wb_ref,
                nb_ref, o_ref, gate_ref, *, n_slc):
    i = pl.program_id(2)
    R = Q_PER_KV
    RQ = R * QT
    RW = R * HEAD_DIM
    groups = range(kc_ref.shape[1])
    first_group = pl.program_id(1) * len(groups)
    pad_tile = ks_ref.shape[2] - PAD_TILES
    t_pos = lax.broadcasted_iota(jnp.int32, (1, QT), 1) + i * QT

    def flag_rows(n):
        return jnp.where(lax.broadcasted_iota(jnp.int32, (n, QT), 0) == 0, NEG, 0.0).astype(BF16)

    q_heads = []
    for gg in groups:
        qT = (q_ref[:, gg * RW:(gg + 1) * RW].T * (0.125 * LOG2E)).astype(BF16)
        q_heads.append([qT[r * HEAD_DIM:(r + 1) * HEAD_DIM] for r in range(R)])

    slab = WINDOW + QT
    n_win = slab // QT
    win_idx = [jnp.where(i + jj >= n_win - 1, i + jj - (n_win - 1), pad_tile) for jj in range(n_win)]

    def window_scores(gg):
        kw = jnp.concatenate([kw_ref[0, gg, j] for j in win_idx], axis=0)
        qwin = _lane_cat([jnp.concatenate([q_heads[gg][r], flag_rows(KW_COLS - HEAD_DIM)], axis=0)
                          for r in range(R)])
        wb = _lane_cat([wb_ref[gg, r] for r in range(R)])
        return jnp.dot(kw, qwin, preferred_element_type=F32) + wb

    def window_branch(gg, sw):
        m_w = jnp.max(sw, axis=0, keepdims=True)
        e_wb = jnp.exp2(sw - m_w).astype(BF16)
        acc = None
        for jj, j in enumerate(win_idx):
            term = jnp.dot(vwT_ref[0, gg, j], e_wb[jj * QT:(jj + 1) * QT], preferred_element_type=F32)
            acc = term if acc is None else acc + term
        return acc[:HEAD_DIM] * (1.0 / acc[V_ONES:V_ONES + 1])

    sw_all = [window_scores(gg) for gg in groups]

    ncp = kc_ref.shape[2]
    strip_row = pl.multiple_of(ncp - i * (QT // CMP_STRIDE), QT // CMP_STRIDE)
    ov = ovT_ref[...]
    s_ix = lax.broadcasted_iota(jnp.int32, (N_SLC_PAD, QT), 0)
    cur = t_pos >> int(math.log2(SLC_BLOCK))
    forced = (s_ix == 0) | (s_ix == cur) | (s_ix == cur - 1)

    def compressed_scores(gg):
        sc = jnp.dot(kc_ref[0, gg], _lane_cat(q_heads[gg]), preferred_element_type=F32)
        return sc + _lane_cat([strip_ref[gg, r, pl.ds(strip_row, ncp), :] for r in range(R)])

    def compressed_branch(gg, sm):
        m_c = jnp.max(sm, axis=0, keepdims=True)
        e_c = jnp.exp2(sm - m_c)
        l_c = jnp.sum(e_c, axis=0, keepdims=True)
        inv_c = jnp.where(m_c > 0.5 * NEG, 1.0 / l_c, 0.0)
        o_cmp = jnp.dot(vcT_ref[0, gg], e_c.astype(BF16), preferred_element_type=F32) * inv_c
        p_c = e_c * inv_c
        p_sum = p_c[:, 0:QT]
        for r in range(1, R):
            p_sum = p_sum + p_c[:, r * QT:(r + 1) * QT]
        imp = None
        for part in _split3(p_sum):
            term = jnp.dot(ov, part, preferred_element_type=F32)
            imp = term if imp is None else imp + term
        score = jnp.where(forced, FORCED_SCORE, jnp.where(s_ix <= cur, imp, -1.0))
        return o_cmp, jnp.where(s_ix < n_slc, score, -2.0)

    sm_all = [compressed_scores(gg) for gg in groups]
    o_win = [window_branch(gg, sw_all[gg]) for gg in groups]
    cmp_out = [compressed_branch(gg, sm_all[gg]) for gg in groups]
    o_cmp = [c[0] for c in cmp_out]
    scores = [c[1] for c in cmp_out]

    SUB = 8
    n_sub = N_SLC_PAD // SUB
    score_rows = [[score[v * SUB:(v + 1) * SUB] for v in range(n_sub)] for score in scores]
    sub_ix = lax.broadcasted_iota(jnp.int32, (SUB, QT), 0)

    vregs_per_seg = RANK_SEG // SUB
    n_seg = -(-n_slc // RANK_SEG)

    def rank_levels(levels, ranks):
        out = []
        for gg in groups:
            rk = list(ranks[gg])
            for k in levels:
                for seg in range(k + 1):
                    rows = range((k + 1) * vregs_per_seg) if seg == k else range(k * vregs_per_seg,
                                                                                 (k + 1) * vregs_per_seg)
                    for sp in range(seg * RANK_SEG, min((seg + 1) * RANK_SEG, n_slc)):
                        row = scores[gg][sp:sp + 1, :]
                        for v in rows:
                            blk = score_rows[gg][v]
                            if v * SUB > sp:
                                beats = row >= blk
                            elif v * SUB + SUB - 1 <= sp:
                                beats = row > blk
                            else:
                                beats = (row > blk) | ((row == blk) & (sub_ix > sp - v * SUB))
                            rk[v] = rk[v] + jnp.where(beats, 1.0, 0.0)
            out.append(tuple(rk))
        return tuple(out)

    ranks = tuple(tuple(jnp.zeros((SUB, QT), F32) for _ in range(n_sub)) for _ in groups)
    last_block = (i + 1) * (QT // SLC_BLOCK) - 1
    for levels in [(0, 1)] + [(k,) for k in range(2, n_seg)]:
        first = max(levels) * RANK_SEG if max(levels) > 1 else SLC_TOPK
        ranks = lax.cond(last_block >= first, functools.partial(rank_levels, levels), lambda rk: rk, ranks)

    qsel = []
    for gg in groups:
        sel_neg = jnp.where(jnp.concatenate(ranks[gg], axis=0) < float(SLC_TOPK), 0.0, UNSELECTED).astype(BF16)
        qsel.append(_lane_cat([jnp.concatenate([q_heads[gg][r], sel_neg, flag_rows(KS_COLS - KS_FLAG)], axis=0)
                               for r in range(R)]))

    n_far = jnp.maximum(i - 1, 0)

    def far_tiles(j):
        return [jnp.where(j + n < n_far, j + n, pad_tile) for n in range(FAR_TILES)]

    def tile_scores(gg, tiles):
        keys = jnp.concatenate([ks_ref[0, gg, t] for t in tiles], axis=0)
        return jnp.dot(keys, qsel[gg], preferred_element_type=F32)

    def online_step(gg, carry, s, tiles):
        m, acc = carry
        m_new = jnp.maximum(m, jnp.max(s, axis=0, keepdims=True))
        p = jnp.exp2(s - m_new).astype(BF16)
        acc = jnp.exp2(m - m_new) * acc
        for n, t in enumerate(tiles):
            acc = acc + jnp.dot(vsT_ref[0, gg, t], p[n * QT:(n + 1) * QT], preferred_element_type=F32)
        return m_new, acc

    def far_trip(u, carries):
        tiles = far_tiles(FAR_TILES * u)
        s_all = [tile_scores(gg, tiles) for gg in groups]
        return tuple(online_step(gg, carries[gg], s_all[gg], tiles) for gg in groups)

    init = (jnp.full((1, RQ), NEG, F32), jnp.zeros((V_ROWS, RQ), F32))
    carries = lax.fori_loop(0, (n_far + FAR_TILES - 1) // FAR_TILES, far_trip, tuple(init for _ in groups))
    near = [jnp.where(i >= 1, i - 1, pad_tile), i]

    gate_ref[...] = jax.nn.sigmoid(gl_ref[...].T)
    s_near = [tile_scores(gg, near) + _lane_cat([nb_ref[gg, r] for r in range(R)]) for gg in groups]
    for gg in groups:
        _, acc_s = online_step(gg, carries[gg], s_near[gg], near)
        o_slc = acc_s[:HEAD_DIM] * (1.0 / acc_s[V_ONES:V_ONES + 1])
        head0 = (first_group + gg) * R
        gates = [_lane_cat([gate_ref[pl.ds((head0 + r) * 3 + br, 1), :] for r in range(R)]) for br in range(3)]
        total = gates[0] * o_cmp[gg] + gates[1] * o_slc + gates[2] * o_win[gg]
        o_ref[:, gg * RW:(gg + 1) * RW] = jnp.concatenate(
            [total[:, r * QT:(r + 1) * QT] for r in range(R)], axis=0).T.astype(o_ref.dtype)


def _nsa_mixer(proj, bsz, L, pe_ck, w_ck1, w_ck2, pe_cv, w_cv1, w_cv2, rel_table):
    g, r = KV_HEADS, Q_PER_KV
    ng = NSA_GROUPS
    ni = L // QT
    nt = L // QT + PAD_TILES
    ncp = L // CMP_STRIDE
    n_slc = L // SLC_BLOCK
    slab = WINDOW + QT
    k_cmp = _compress(proj, COL_KC, bsz, L, pe_ck, w_ck1, w_ck2, transpose_out=False)
    v_cmp_t = _compress(proj, COL_VC, bsz, L, pe_cv, w_cv1, w_cv2, transpose_out=True)
    ks_t, vs_t, kw_t, vw_t = _kv_tiles(proj, bsz, L)
    wb, nb, strip = _nsa_tables(rel_table, L)
    qw = ng * r * HEAD_DIM
    once = pl.Buffered(1)
    whole = lambda *shape: pl.BlockSpec((1, ng) + shape, lambda b, h, i: (b, h) + (0,) * len(shape), once)
    per_group = lambda *shape: pl.BlockSpec((ng,) + shape, lambda b, h, i: (h,) + (0,) * len(shape), once)
    return pl.pallas_call(
        functools.partial(_nsa_kernel, n_slc=n_slc),
        grid=(bsz, g // ng, ni),
        in_specs=[
            pl.BlockSpec((QT, qw), lambda b, h, i: (b * ni + i, COL_Q // qw + h)),
            pl.BlockSpec((QT, LANES), lambda b, h, i: (b * ni + i, COL_GATE // LANES)),
            whole(ncp, HEAD_DIM),
            whole(HEAD_DIM, ncp),
            per_group(r, 2 * ncp, QT),
            pl.BlockSpec((N_SLC_PAD, ncp), lambda b, h, i: (0, 0), once),
            whole(nt, QT, KS_COLS),
            whole(nt, V_ROWS, QT),
            whole(nt, QT, KW_COLS),
            whole(nt, V_ROWS, QT),
            per_group(r, slab, QT),
            per_group(r, 2 * QT, QT),
        ],
        out_specs=pl.BlockSpec((QT, qw), lambda b, h, i: (b * ni + i, h)),
        out_shape=jax.ShapeDtypeStruct((bsz * L, NSA_WIDTH), BF16),
        scratch_shapes=[pltpu.VMEM((LANES, QT), F32)],
        compiler_params=_cparams("parallel", "parallel", "arbitrary"),
        name="nsa_attention",
    )(proj, proj, k_cmp, v_cmp_t, strip, _overlap_t(L), ks_t, vs_t, kw_t, vw_t, wb, nb)


def _out_proj_kernel(x_ref, ys_ref, yn_ref, w_ref, o_ref):
    half = ys_ref.shape[1]
    acc = jnp.dot(ys_ref[...], w_ref[:half, :], preferred_element_type=F32)
    acc = acc + jnp.dot(yn_ref[...], w_ref[half:, :], preferred_element_type=F32)
    o_ref[...] = x_ref[...] + acc


def _out_proj(x2, y_ssm, y_nsa, w, tm=512):
    t, d = x2.shape
    return pl.pallas_call(
        _out_proj_kernel,
        grid=(t // tm,),
        in_specs=[pl.BlockSpec((tm, d), lambda i: (i, 0)),
                  pl.BlockSpec((tm, SSM_WIDTH), lambda i: (i, 0)),
                  pl.BlockSpec((tm, NSA_WIDTH), lambda i: (i, 0)),
                  pl.BlockSpec((SSM_WIDTH + NSA_WIDTH, d), lambda i: (0, 0), pl.Buffered(1))],
        out_specs=pl.BlockSpec((tm, d), lambda i: (i, 0)),
        out_shape=jax.ShapeDtypeStruct((t, d), F32),
        compiler_params=_cparams("parallel"),
        name="out_proj",
    )(x2, y_ssm, y_nsa, w)


def _mlp_kernel(x_ref, n2_ref, wu_ref, wd_ref, nf_ref, o_ref, h_ref, acc_ref):
    f = pl.program_id(1)

    @pl.when(f == 0)
    def _():
        h_ref[...] = _rms(x_ref[...], n2_ref[...]).astype(BF16)
        acc_ref[...] = jnp.zeros_like(acc_ref)

    a = jnp.dot(h_ref[...], wu_ref[...], preferred_element_type=F32)
    a = jnp.square(jnp.maximum(a, 0.0))
    acc_ref[...] += jnp.dot(a.astype(BF16), wd_ref[...], preferred_element_type=F32)

    @pl.when(f == pl.num_programs(1) - 1)
    def _():
        o_ref[...] = _rms(x_ref[...] + acc_ref[...], nf_ref[...])


def _mlp(x2, n2, wu, wd, nf, tm=512, tf=1024):
    t, d = x2.shape
    ff = wu.shape[1]
    return pl.pallas_call(
        _mlp_kernel,
        grid=(t // tm, ff // tf),
        in_specs=[pl.BlockSpec((tm, d), lambda i, f: (i, 0)),
                  pl.BlockSpec((1, d), lambda i, f: (0, 0)),
                  pl.BlockSpec((d, tf), lambda i, f: (0, f)),
                  pl.BlockSpec((tf, d), lambda i, f: (f, 0)),
                  pl.BlockSpec((1, d), lambda i, f: (0, 0))],
        out_specs=pl.BlockSpec((tm, d), lambda i, f: (i, 0)),
        out_shape=jax.ShapeDtypeStruct((t, d), F32),
        scratch_shapes=[pltpu.VMEM((tm, d), BF16), pltpu.VMEM((tm, d), F32)],
        compiler_params=_cparams("parallel", "arbitrary"),
        name="mlp_final_norm",
    )(x2, n2, wu, wd, nf)


def kernel(x, norm1_w, w_in, ssm_a_re, ssm_a_im, ssm_log_dt, ssm_b_re, ssm_b_im, ssm_c_re, ssm_c_im, ssm_d,
           w_glu, b_glu, pe_ck, w_ck1, w_ck2, pe_cv, w_cv1, w_cv2, w_out, norm2_w, w_up, w_down, rel_table,
           norm_f_w):
    bsz, L, d = x.shape
    assert w_in.shape[0] == 1, "the closing rmsnorm is fused into the (single) layer's MLP kernel"
    x2 = x.reshape(bsz * L, d)
    w_in_p = jnp.pad(w_in[0].astype(BF16), ((0, 0), (0, D_IN_PAD - D_IN)))
    proj = _norm_matmul(x2, norm1_w[0].reshape(1, d), w_in_p)
    ops = _s5_operators(ssm_a_re[0], ssm_a_im[0], ssm_log_dt[0], ssm_b_re[0], ssm_b_im[0], ssm_c_re[0], ssm_c_im[0],
                        ssm_d[0])
    y_ssm = _s5_mixer(proj, bsz, L, ops, w_glu[0], b_glu[0])
    y_nsa = _nsa_mixer(proj, bsz, L, pe_ck[0], w_ck1[0], w_ck2[0], pe_cv[0], w_cv1[0], w_cv2[0], rel_table)
    x2 = _out_proj(x2, y_ssm, y_nsa, w_out[0].astype(BF16))
    x2 = _mlp(x2, norm2_w[0].reshape(1, d), w_up[0].astype(BF16), w_down[0].astype(BF16), norm_f_w.reshape(1, d))
    return x2.reshape(bsz, L, d)
```

```python
import functools
import math

import numpy as np
import jax
import jax.numpy as jnp
from jax import lax
from jax.experimental import pallas as pl
from jax.experimental.pallas import tpu as pltpu

F32 = jnp.float32
BF16 = jnp.bfloat16

SSM_WIDTH = 1024
SSM_GROUP = 16
SSM_GROUPS = 64
SSM_STATE = 64
NSA_WIDTH = 1024
HEAD_DIM = 64
NSA_HEADS = 16
KV_HEADS = 4
Q_PER_KV = 4
KV_WIDTH = 256
CMP_BLOCK = 32
CMP_STRIDE = 16
CMP_HIDDEN = 256
SLC_BLOCK = 64
SLC_TOPK = 16
WINDOW = 512
REL_BUCKETS = 32
REL_MAX_DIST = 128
EPS = 1e-6
NEG = -1e30
FORCED_SCORE = 1e4
D_IN = SSM_WIDTH + NSA_WIDTH + 6 * KV_WIDTH + 3 * NSA_HEADS
D_IN_PAD = 3712
COL_Q = SSM_WIDTH
COL_KC = COL_Q + NSA_WIDTH
COL_VC = COL_KC + KV_WIDTH
COL_KS = COL_VC + KV_WIDTH
COL_VS = COL_KS + KV_WIDTH
COL_KW = COL_VS + KV_WIDTH
COL_VW = COL_KW + KV_WIDTH
COL_GATE = COL_VW + KV_WIDTH

LANES = 128
S5_CHUNK = 16
SLAB_GROUPS = LANES // SSM_GROUP
N_SLABS = SSM_GROUPS // SLAB_GROUPS
QT = 128
N_SLC_PAD = 64
RANK_SEG = 16
NSA_GROUPS = 4
FAR_TILES = 2
UNSELECTED = -1e9
LOG2E = math.log2(math.e)
TABLE_W = 1024
VMEM_LIMIT = 56 * 1024 * 1024
MLP_VMEM_LIMIT = 62 * 1024 * 1024
HI = lax.Precision.HIGHEST


def _cparams(*sem):
    return pltpu.CompilerParams(dimension_semantics=sem, vmem_limit_bytes=VMEM_LIMIT)


def _rms(x, w):
    ms = jnp.mean(x * x, axis=-1, keepdims=True)
    return x * lax.rsqrt(ms + EPS) * w


def _lane_cat(parts):
    return jnp.concatenate(parts, axis=1)


def _norm_matmul_kernel(x_ref, nw_ref, w_ref, o_ref):
    h = _rms(x_ref[...], nw_ref[...]).astype(BF16)
    o_ref[...] = jnp.dot(h, w_ref[...], preferred_element_type=F32)


def _norm_matmul(x2, nw, w, tm=512):
    t, d = x2.shape
    n = w.shape[1]
    return pl.pallas_call(
        _norm_matmul_kernel,
        grid=(t // tm,),
        in_specs=[pl.BlockSpec((tm, d), lambda i: (i, 0)),
                  pl.BlockSpec((1, d), lambda i: (0, 0)),
                  pl.BlockSpec((d, n), lambda i: (0, 0), pl.Buffered(1))],
        out_specs=pl.BlockSpec((tm, n), lambda i: (i, 0)),
        out_shape=jax.ShapeDtypeStruct((t, n), F32),
        compiler_params=_cparams("parallel"),
        name="norm_in_proj",
    )(x2, nw, w)


def _s5_operators(a_re, a_im, log_dt, b_re, b_im, c_re, c_im, d):
    q = S5_CHUNK
    g, p = a_re.shape
    h = SSM_GROUP
    dt = jnp.exp(log_dt)[:, None]
    lam_re, lam_im = dt * a_re, dt * a_im
    mag1 = jnp.exp(lam_re)
    abar_re, abar_im = mag1 * jnp.cos(lam_im), mag1 * jnp.sin(lam_im)
    den = a_re * a_re + a_im * a_im
    zr, zi = abar_re - 1.0, abar_im
    coef_re = (zr * a_re + zi * a_im) / den
    coef_im = (zi * a_re - zr * a_im) / den
    bb_re = coef_re[..., None] * b_re - coef_im[..., None] * b_im
    bb_im = coef_re[..., None] * b_im + coef_im[..., None] * b_re
    k = jnp.arange(q + 1, dtype=F32)[:, None, None]
    mag = jnp.exp(k * lam_re)
    pw_re, pw_im = mag * jnp.cos(k * lam_im), mag * jnp.sin(k * lam_im)
    m_re = pw_re[..., None] * bb_re - pw_im[..., None] * bb_im
    m_im = pw_re[..., None] * bb_im + pw_im[..., None] * bb_re
    kern = (jnp.einsum('ghp,kgpi->gkhi', c_re, m_re[:q], precision=HI)
            - jnp.einsum('ghp,kgpi->gkhi', c_im, m_im[:q], precision=HI))
    kern = kern.at[:, 0].add(jax.vmap(jnp.diag)(d))
    rev = np.arange(q - 1, -1, -1)
    s_end = jnp.stack([m_re[rev], m_im[rev]], axis=0).transpose(2, 1, 4, 0, 3)
    pr, pi = pw_re[1:], pw_im[1:]
    o_re = c_re[None] * pr[:, :, None, :] - c_im[None] * pi[:, :, None, :]
    o_im = -(c_re[None] * pi[:, :, None, :] + c_im[None] * pr[:, :, None, :])
    o_carry = jnp.stack([o_re, o_im], axis=0).transpose(2, 0, 4, 1, 3)
    a_q = jnp.stack([pw_re[q], pw_im[q]], axis=0)
    return kern, s_end, o_carry, a_q


def _s5_slab_weights(kern, s_end, o_carry, a_q):
    q, h, p = S5_CHUNK, SSM_GROUP, SSM_STATE
    s_dense = s_end.reshape(SSM_GROUPS, q, h, 2 * p).astype(BF16)
    k_dense = kern.transpose(0, 1, 3, 2).astype(BF16)
    o_dense = o_carry.reshape(SSM_GROUPS, 2, p, q * h).astype(BF16)
    a_re = a_q[0].reshape(N_SLABS, SLAB_GROUPS * p)
    a_im = a_q[1].reshape(N_SLABS, SLAB_GROUPS * p)
    return s_dense, k_dense, o_dense, a_re, a_im


def _chunk_rows(u_ref, nc):
    return _lane_cat([u_ref[pl.ds(s, nc, stride=S5_CHUNK), :] for s in range(S5_CHUNK)]).astype(BF16)


def _s5_state_kernel(u_ref, sd_ref, o_ref, ws_scr):
    nc = u_ref.shape[0] // S5_CHUNK
    q, h, p, sg = S5_CHUNK, SSM_GROUP, SSM_STATE, SLAB_GROUPS

    @pl.when(pl.program_id(1) == 0)
    def _():
        ws_scr[...] = jnp.zeros_like(ws_scr)
        for s in range(q):
            for a in range(sg):
                for c in range(2):
                    ws_scr[s * LANES + a * h: s * LANES + (a + 1) * h,
                           (c * sg + a) * p: (c * sg + a + 1) * p] = sd_ref[a, s, :, c * p:(c + 1) * p]

    o_ref[0] = jnp.dot(_chunk_rows(u_ref, nc), ws_scr[...], preferred_element_type=F32)


def _s5_scan_kernel(h_ref, ar_ref, ai_ref, o_ref):
    half = ar_ref.shape[1]
    ar = ar_ref[...]
    ai = ai_ref[...]

    def body(c, carry):
        hr, hi = carry
        o_ref[0, c, :, :half] = hr
        o_ref[0, c, :, half:] = hi
        xr = h_ref[0, c, :, :half]
        xi = h_ref[0, c, :, half:]
        return ar * hr - ai * hi + xr, ar * hi + ai * hr + xi

    zero = jnp.zeros(ar.shape, F32)
    lax.fori_loop(0, h_ref.shape[1], body, (zero, zero))


def _s5_out_kernel(u_ref, kd_ref, h_ref, od_ref, y_ref, kl_scr, wi_scr, wc_scr):
    nc = u_ref.shape[0] // S5_CHUNK
    q, h, p, sg = S5_CHUNK, SSM_GROUP, SSM_STATE, SLAB_GROUPS

    @pl.when(pl.program_id(1) == 0)
    def _():
        kl_scr[...] = jnp.zeros_like(kl_scr)
        for k in range(q):
            for a in range(sg):
                kl_scr[k, a * h:(a + 1) * h, a * h:(a + 1) * h] = kd_ref[a, k]
        for s in range(q):
            for t in range(s, q):
                wi_scr[s * LANES:(s + 1) * LANES, t * LANES:(t + 1) * LANES] = kl_scr[t - s]
        for t in range(0, q, 2):
            wi_scr[(t + 1) * LANES:(t + 2) * LANES, t * LANES:(t + 1) * LANES] = jnp.zeros((LANES, LANES), BF16)
        wc_scr[...] = jnp.zeros_like(wc_scr)
        for a in range(sg):
            for c in range(2):
                for t in range(q):
                    wc_scr[(c * sg + a) * p:(c * sg + a + 1) * p,
                           t * LANES + a * h: t * LANES + (a + 1) * h] = od_ref[a, c, :, t * h:(t + 1) * h]

    u_rows = _chunk_rows(u_ref, nc)
    h_rows = h_ref[0].astype(BF16)
    pair = 2 * LANES
    for tp in range(q // 2):
        cols = slice(tp * pair, (tp + 1) * pair)
        k_used = (tp + 1) * pair
        y = (jnp.dot(u_rows[:, :k_used], wi_scr[:k_used, cols], preferred_element_type=F32)
             + jnp.dot(h_rows, wc_scr[:, cols], preferred_element_type=F32))
        for tt in range(2):
            y_ref[pl.ds(2 * tp + tt, nc, stride=q), :] = y[:, tt * LANES:(tt + 1) * LANES]


def _glu_kernel(y_ref, w_ref, b_ref, o_ref):
    z = jax.nn.gelu(y_ref[...])
    gate = jnp.dot(z.astype(BF16), w_ref[...], preferred_element_type=F32) + b_ref[...]
    o_ref[...] = (z * jax.nn.sigmoid(gate)).astype(o_ref.dtype)


def _s5_mixer(proj, bsz, L, ops, w_glu, b_glu):
    s_dense, k_dense, o_dense, a_re, a_im = _s5_slab_weights(*ops)
    q, h, p = S5_CHUNK, SSM_GROUP, SSM_STATE
    nc = L // q
    ns, sg = N_SLABS, SLAB_GROUPS
    sw = sg * 2 * p
    kq = q * LANES
    hend = pl.pallas_call(
        _s5_state_kernel,
        grid=(ns, bsz),
        in_specs=[pl.BlockSpec((L, LANES), lambda j, b: (b, j)),
                  pl.BlockSpec((sg, q, h, 2 * p), lambda j, b: (j, 0, 0, 0))],
        out_specs=pl.BlockSpec((1, nc, sw), lambda j, b: (b, 0, j)),
        out_shape=jax.ShapeDtypeStruct((bsz, nc, ns * sw), F32),
        scratch_shapes=[pltpu.VMEM((kq, sw), BF16)],
        compiler_params=_cparams("parallel", "arbitrary"),
        name="s5_chunk_state",
    )(proj, s_dense)
    hprev4 = pl.pallas_call(
        _s5_scan_kernel,
        grid=(bsz,),
        in_specs=[pl.BlockSpec((1, nc, ns, sw), lambda b: (b, 0, 0, 0)),
                  pl.BlockSpec((ns, sw // 2), lambda b: (0, 0)),
                  pl.BlockSpec((ns, sw // 2), lambda b: (0, 0))],
        out_specs=pl.BlockSpec((1, nc, ns, sw), lambda b: (b, 0, 0, 0)),
        out_shape=jax.ShapeDtypeStruct((bsz, nc, ns, sw), F32),
        compiler_params=_cparams("parallel"),
        name="s5_chunk_scan",
    )(hend.reshape(bsz, nc, ns, sw), a_re, a_im)
    y = pl.pallas_call(
        _s5_out_kernel,
        grid=(ns, bsz),
        in_specs=[pl.BlockSpec((L, LANES), lambda j, b: (b, j)),
                  pl.BlockSpec((sg, q, h, h), lambda j, b: (j, 0, 0, 0)),
                  pl.BlockSpec((1, nc, sw), lambda j, b: (b, 0, j)),
                  pl.BlockSpec((sg, 2, p, q * h), lambda j, b: (j, 0, 0, 0))],
        out_specs=pl.BlockSpec((L, LANES), lambda j, b: (b, j)),
        out_shape=jax.ShapeDtypeStruct((bsz * L, SSM_WIDTH), F32),
        scratch_shapes=[pltpu.VMEM((q, LANES, LANES), BF16), pltpu.VMEM((kq, kq), BF16), pltpu.VMEM((sw, kq), BF16)],
        compiler_params=_cparams("parallel", "arbitrary"),
        name="s5_chunk_out",
    )(proj, k_dense, hprev4.reshape(bsz, nc, ns * sw), o_dense)
    tm = 512
    return pl.pallas_call(
        _glu_kernel,
        grid=(bsz * L // tm,),
        in_specs=[pl.BlockSpec((tm, SSM_WIDTH), lambda i: (i, 0)),
                  pl.BlockSpec((SSM_WIDTH, SSM_WIDTH), lambda i: (0, 0)),
                  pl.BlockSpec((1, SSM_WIDTH), lambda i: (0, 0))],
        out_specs=pl.BlockSpec((tm, SSM_WIDTH), lambda i: (i, 0)),
        out_shape=jax.ShapeDtypeStruct((bsz * L, SSM_WIDTH), BF16),
        compiler_params=_cparams("parallel"),
        name="s5_gelu_glu",
    )(y, w_glu.astype(BF16), b_glu.reshape(1, SSM_WIDTH))


def _compress_kernel(k_ref, w1_ref, pe_ref, w1full_ref, w2_ref, o_ref, *, transpose_out):
    nc = k_ref.shape[0] // CMP_STRIDE
    prod = jnp.dot(_chunk_rows(k_ref, nc), w1_ref[...], preferred_element_type=F32)
    pe_h = jnp.dot(jnp.broadcast_to(pe_ref[...], (8, pe_ref.shape[1])), w1full_ref[...],
                   precision=HI, preferred_element_type=F32)[0:1]
    two_h = 2 * CMP_HIDDEN
    for gl in range(LANES // HEAD_DIM):
        first = prod[:, gl * two_h: gl * two_h + CMP_HIDDEN]
        second = prod[:, gl * two_h + CMP_HIDDEN: (gl + 1) * two_h]
        hid = first + pltpu.roll(second, nc - 1, axis=0) + pe_h
        out = jnp.dot(jax.nn.gelu(hid).astype(BF16), w2_ref[...], preferred_element_type=F32)
        if transpose_out:
            o_ref[0, gl] = _lane_cat([out, jnp.zeros_like(out)]).T[:HEAD_DIM].astype(BF16)
        else:
            o_ref[0, gl] = out.astype(BF16)


def _compress(proj, col, bsz, L, pe, w1, w2, transpose_out):
    nc = L // CMP_STRIDE
    half = CMP_STRIDE * HEAD_DIM
    gpl = LANES // HEAD_DIM
    w1cat = jnp.concatenate([w1[:half], w1[half:]], axis=1)
    eye = jnp.eye(gpl, dtype=F32)
    w1slab = jnp.einsum('ldh,ag->ladgh', w1cat.reshape(CMP_STRIDE, HEAD_DIM, 2 * CMP_HIDDEN), eye)
    w1slab = w1slab.reshape(CMP_STRIDE * LANES, gpl * 2 * CMP_HIDDEN).astype(BF16)
    out_block = (1, gpl, HEAD_DIM, nc) if transpose_out else (1, gpl, nc, HEAD_DIM)
    out_full = (bsz, KV_HEADS) + out_block[2:]
    return pl.pallas_call(
        functools.partial(_compress_kernel, transpose_out=transpose_out),
        grid=(bsz, KV_HEADS // gpl),
        in_specs=[pl.BlockSpec((L, LANES), lambda b, j: (b, col // LANES + j)),
                  pl.BlockSpec(w1slab.shape, lambda b, j: (0, 0)),
                  pl.BlockSpec((1, 2 * half), lambda b, j: (0, 0)),
                  pl.BlockSpec((2 * half, CMP_HIDDEN), lambda b, j: (0, 0)),
                  pl.BlockSpec((CMP_HIDDEN, HEAD_DIM), lambda b, j: (0, 0))],
        out_specs=pl.BlockSpec(out_block, lambda b, j: (b, j, 0, 0)),
        out_shape=jax.ShapeDtypeStruct(out_full, BF16),
        compiler_params=_cparams("parallel", "parallel"),
        name="nsa_compress",
    )(proj, w1slab, pe.reshape(1, 2 * half), w1, w2.astype(BF16))


KS_COLS = 2 * LANES
KW_COLS = LANES
KS_FLAG = HEAD_DIM + N_SLC_PAD
KW_FLAG = HEAD_DIM
PAD_TILES = 8
V_ROWS = HEAD_DIM + 16
V_ONES = HEAD_DIM


def _kv_tiles_kernel(ks_ref, vs_ref, kw_ref, vw_ref, kso_ref, vso_ref, kwo_ref, vwo_ref):
    tm = ks_ref.shape[0]
    step = pl.program_id(1)
    row0 = step * tm
    kso_ref[:, :, :, :, KS_FLAG:] = jnp.zeros(kso_ref.shape[:4] + (KS_COLS - KS_FLAG,), BF16)
    kwo_ref[:, :, :, :, KW_FLAG:] = jnp.zeros(kwo_ref.shape[:4] + (KW_COLS - KW_FLAG,), BF16)
    is_real = step < pl.num_programs(1) - 1
    tail_rows = lax.broadcasted_iota(jnp.int32, vso_ref.shape[:3] + (V_ROWS - HEAD_DIM, QT), 3)
    v_tail = jnp.where(tail_rows == 0, jnp.where(is_real, 1.0, 0.0), 0.0).astype(BF16)
    vso_ref[:, :, :, HEAD_DIM:, :] = v_tail
    vwo_ref[:, :, :, HEAD_DIM:, :] = v_tail

    @pl.when(step < pl.num_programs(1) - 1)
    def _():
        lane_blk = lax.broadcasted_iota(jnp.int32, (QT, N_SLC_PAD), 1)
        for k in range(tm // QT):
            rows = slice(k * QT, (k + 1) * QT)
            tok = lax.broadcasted_iota(jnp.int32, (QT, N_SLC_PAD), 0) + (row0 + k * QT)
            onehot = jnp.where((tok >> int(math.log2(SLC_BLOCK))) == lane_blk, 1.0, 0.0).astype(BF16)
            vs_t = vs_ref[rows, :].T
            vw_t = vw_ref[rows, :].T
            for g in range(KV_HEADS):
                cols = slice(g * HEAD_DIM, (g + 1) * HEAD_DIM)
                kso_ref[0, g, k, :, :HEAD_DIM] = ks_ref[rows, cols].astype(BF16)
                kso_ref[0, g, k, :, HEAD_DIM:KS_FLAG] = onehot
                kwo_ref[0, g, k, :, :HEAD_DIM] = kw_ref[rows, cols].astype(BF16)
                vso_ref[0, g, k, :HEAD_DIM, :] = vs_t[cols].astype(BF16)
                vwo_ref[0, g, k, :HEAD_DIM, :] = vw_t[cols].astype(BF16)

    @pl.when(step == pl.num_programs(1) - 1)
    def _():
        ones = jnp.ones(kso_ref.shape[:4] + (1,), BF16)
        kso_ref[:, :, :, :, :KS_FLAG] = jnp.zeros(kso_ref.shape[:4] + (KS_FLAG,), BF16)
        kwo_ref[:, :, :, :, :KW_FLAG] = jnp.zeros(kwo_ref.shape[:4] + (KW_FLAG,), BF16)
        kso_ref[:, :, :, :, KS_FLAG:KS_FLAG + 1] = ones
        kwo_ref[:, :, :, :, KW_FLAG:KW_FLAG + 1] = ones
        vso_ref[:, :, :, :HEAD_DIM, :] = jnp.zeros(vso_ref.shape[:3] + (HEAD_DIM, QT), BF16)
        vwo_ref[:, :, :, :HEAD_DIM, :] = jnp.zeros(vwo_ref.shape[:3] + (HEAD_DIM, QT), BF16)


def _kv_tiles(proj, bsz, L):
    nt = L // QT
    g = KV_HEADS
    kt = PAD_TILES
    tm = kt * QT
    steps = L // tm
    in_spec = lambda col: pl.BlockSpec((tm, KV_WIDTH),
                                       lambda b, i: (b * steps + jnp.minimum(i, steps - 1), col // KV_WIDTH))
    out_spec = lambda a, c: pl.BlockSpec((1, g, kt, a, c), lambda b, i: (b, 0, i, 0, 0))
    shape = lambda a, c: jax.ShapeDtypeStruct((bsz, g, nt + kt, a, c), BF16)
    return pl.pallas_call(
        _kv_tiles_kernel,
        grid=(bsz, steps + 1),
        in_specs=[in_spec(COL_KS), in_spec(COL_VS), in_spec(COL_KW), in_spec(COL_VW)],
        out_specs=[out_spec(QT, KS_COLS), out_spec(V_ROWS, QT), out_spec(QT, KW_COLS), out_spec(V_ROWS, QT)],
        out_shape=[shape(QT, KS_COLS), shape(V_ROWS, QT), shape(QT, KW_COLS), shape(V_ROWS, QT)],
        compiler_params=_cparams("parallel", "arbitrary"),
        name="nsa_kv_tiles",
    )(proj, proj, proj, proj)


def _rel_bucket(dist):
    dist = jnp.maximum(dist, 0)
    max_exact = REL_BUCKETS // 2
    large = max_exact + (jnp.log(jnp.maximum(dist, 1).astype(F32) / max_exact)
                         / math.log(REL_MAX_DIST / max_exact) * (REL_BUCKETS - max_exact)).astype(jnp.int32)
    large = jnp.minimum(large, REL_BUCKETS - 1)
    return jnp.where(dist < max_exact, dist, large)


def _toeplitz_rows(base, nrows, step):
    w = base.shape[1]
    x = jnp.broadcast_to(base, (nrows, w))
    k_ix = lax.broadcasted_iota(jnp.int32, (nrows, w), 0)
    bit = 1
    while bit < nrows:
        x = jnp.where((k_ix & bit) != 0, pltpu.roll(x, (bit * step) % w, axis=1), x)
        bit *= 2
    return x


def _bias_tables_kernel(bw_ref, bc_ref, far_ref, wb_ref, nb_ref, strip_ref, *, ncp):
    slab = WINDOW + QT
    y = _toeplitz_rows(bw_ref[0], QT, 1)
    for kh in range(slab // QT):
        tile = y[:, WINDOW - kh * QT: WINDOW - kh * QT + QT]
        wb_ref[0, kh * QT:(kh + 1) * QT, :] = tile
        if kh >= slab // QT - 2:
            row = kh - (slab // QT - 2)
            nb_ref[0, row * QT:(row + 1) * QT, :] = tile - far_ref[0]
    band = 2 * CMP_STRIDE
    z = _toeplitz_rows(bc_ref[0], band, CMP_STRIDE)[:, 2 * QT: 3 * QT]
    strip_ref[0, :ncp - CMP_STRIDE, :] = jnp.broadcast_to(far_ref[0], (ncp - CMP_STRIDE, QT))
    strip_ref[0, ncp - CMP_STRIDE: ncp + CMP_STRIDE, :] = z
    strip_ref[0, ncp + CMP_STRIDE:, :] = jnp.full((ncp - CMP_STRIDE, QT), NEG, F32)


def _nsa_tables(rel_table, L):
    ncp = L // CMP_STRIDE
    slab = WINDOW + QT
    dist = jnp.arange(TABLE_W)
    rel_table = rel_table * LOG2E
    per_dist = rel_table[_rel_bucket(dist)].T
    base_w = jnp.where(dist < WINDOW, per_dist, NEG)
    shift = CMP_BLOCK - 1
    shifted = jnp.concatenate([jnp.full((NSA_HEADS, shift), NEG, F32), per_dist[:, :TABLE_W - shift]], axis=1)
    base_c = jnp.where(dist < TABLE_W // 2, shifted, NEG)
    far = jnp.broadcast_to(rel_table[REL_BUCKETS - 1][:, None, None], (NSA_HEADS, 1, QT))
    wb, nb, strip = pl.pallas_call(
        functools.partial(_bias_tables_kernel, ncp=ncp),
        grid=(NSA_HEADS,),
        in_specs=[pl.BlockSpec((1, 1, TABLE_W), lambda h: (h, 0, 0)),
                  pl.BlockSpec((1, 1, TABLE_W), lambda h: (h, 0, 0)),
                  pl.BlockSpec((1, 1, QT), lambda h: (h, 0, 0))],
        out_specs=[pl.BlockSpec((1, slab, QT), lambda h: (h, 0, 0)),
                   pl.BlockSpec((1, 2 * QT, QT), lambda h: (h, 0, 0)),
                   pl.BlockSpec((1, 2 * ncp, QT), lambda h: (h, 0, 0))],
        out_shape=[jax.ShapeDtypeStruct((NSA_HEADS, slab, QT), F32),
                   jax.ShapeDtypeStruct((NSA_HEADS, 2 * QT, QT), F32),
                   jax.ShapeDtypeStruct((NSA_HEADS, 2 * ncp, QT), F32)],
        compiler_params=_cparams("parallel"),
        name="nsa_bias_tables",
    )(base_w.reshape(NSA_HEADS, 1, TABLE_W), base_c.reshape(NSA_HEADS, 1, TABLE_W), far)
    g, r = KV_HEADS, Q_PER_KV
    return wb.reshape(g, r, slab, QT), nb.reshape(g, r, 2 * QT, QT), strip.reshape(g, r, 2 * ncp, QT)


def _overlap_t(L):
    n_cmp = (L - CMP_BLOCK) // CMP_STRIDE + 1
    n_slc = L // SLC_BLOCK
    cmp_idx = np.arange(n_cmp)[:, None] * CMP_STRIDE + np.arange(CMP_BLOCK)[None, :]
    overlap = ((cmp_idx[:, :, None] // SLC_BLOCK) == np.arange(n_slc)[None, None, :]).sum(1) / CMP_BLOCK
    out = np.zeros((N_SLC_PAD, L // CMP_STRIDE), np.float32)
    out[:n_slc, :n_cmp] = overlap.T
    return jnp.asarray(out, dtype=BF16)


def _split3(x):
    hi = x.astype(BF16)
    r1 = x - hi.astype(F32)
    mid = r1.astype(BF16)
    lo = (r1 - mid.astype(F32)).astype(BF16)
    return hi, mid, lo


def _nsa_kernel(q_ref, gl_ref, kc_ref, vcT_ref, strip_ref, ovT_ref, ks_ref, vsT_ref, kw_ref, vwT_ref, wb_ref,
                nb_ref, o_ref, gate_ref, *, n_slc):
    i = pl.program_id(2)
    R = Q_PER_KV
    RQ = R * QT
    RW = R * HEAD_DIM
    groups = range(kc_ref.shape[1])
    first_group = pl.program_id(1) * len(groups)
    pad_tile = ks_ref.shape[2] - PAD_TILES
    t_pos = lax.broadcasted_iota(jnp.int32, (1, QT), 1) + i * QT

    def flag_rows(n):
        return jnp.where(lax.broadcasted_iota(jnp.int32, (n, QT), 0) == 0, NEG, 0.0).astype(BF16)

    q_heads = []
    for gg in groups:
        qT = (q_ref[:, gg * RW:(gg + 1) * RW].T * (0.125 * LOG2E)).astype(BF16)
        q_heads.append([qT[r * HEAD_DIM:(r + 1) * HEAD_DIM] for r in range(R)])

    slab = WINDOW + QT
    n_win = slab // QT
    win_idx = [jnp.where(i + jj >= n_win - 1, i + jj - (n_win - 1), pad_tile) for jj in range(n_win)]

    def window_scores(gg):
        kw = jnp.concatenate([kw_ref[0, gg, j] for j in win_idx], axis=0)
        qwin = _lane_cat([jnp.concatenate([q_heads[gg][r], flag_rows(KW_COLS - HEAD_DIM)], axis=0)
                          for r in range(R)])
        wb = _lane_cat([wb_ref[gg, r] for r in range(R)])
        return jnp.dot(kw, qwin, preferred_element_type=F32) + wb

    def window_branch(gg, sw):
        m_w = jnp.max(sw, axis=0, keepdims=True)
        e_wb = jnp.exp2(sw - m_w).astype(BF16)
        acc = None
        for jj, j in enumerate(win_idx):
            term = jnp.dot(vwT_ref[0, gg, j], e_wb[jj * QT:(jj + 1) * QT], preferred_element_type=F32)
            acc = term if acc is None else acc + term
        return acc[:HEAD_DIM] * (1.0 / acc[V_ONES:V_ONES + 1])

    sw_all = [window_scores(gg) for gg in groups]

    ncp = kc_ref.shape[2]
    strip_row = pl.multiple_of(ncp - i * (QT // CMP_STRIDE), QT // CMP_STRIDE)
    ov = ovT_ref[...]
    s_ix = lax.broadcasted_iota(jnp.int32, (N_SLC_PAD, QT), 0)
    cur = t_pos >> int(math.log2(SLC_BLOCK))
    forced = (s_ix == 0) | (s_ix == cur) | (s_ix == cur - 1)

    def compressed_scores(gg):
        sc = jnp.dot(kc_ref[0, gg], _lane_cat(q_heads[gg]), preferred_element_type=F32)
        return sc + _lane_cat([strip_ref[gg, r, pl.ds(strip_row, ncp), :] for r in range(R)])

    def compressed_branch(gg, sm):
        m_c = jnp.max(sm, axis=0, keepdims=True)
        e_c = jnp.exp2(sm - m_c)
        l_c = jnp.sum(e_c, axis=0, keepdims=True)
        inv_c = jnp.where(m_c > 0.5 * NEG, 1.0 / l_c, 0.0)
        o_cmp = jnp.dot(vcT_ref[0, gg], e_c.astype(BF16), preferred_element_type=F32) * inv_c
        p_c = e_c * inv_c
        p_sum = p_c[:, 0:QT]
        for r in range(1, R):
            p_sum = p_sum + p_c[:, r * QT:(r + 1) * QT]
        imp = None
        for part in _split3(p_sum):
            term = jnp.dot(ov, part, preferred_element_type=F32)
            imp = term if imp is None else imp + term
        score = jnp.where(forced, FORCED_SCORE, jnp.where(s_ix <= cur, imp, -1.0))
        return o_cmp, jnp.where(s_ix < n_slc, score, -2.0)

    sm_all = [compressed_scores(gg) for gg in groups]
    o_win = [window_branch(gg, sw_all[gg]) for gg in groups]
    cmp_out = [compressed_branch(gg, sm_all[gg]) for gg in groups]
    o_cmp = [c[0] for c in cmp_out]
    scores = [c[1] for c in cmp_out]

    SUB = 8
    n_sub = N_SLC_PAD // SUB
    score_rows = [[score[v * SUB:(v + 1) * SUB] for v in range(n_sub)] for score in scores]
    sub_ix = lax.broadcasted_iota(jnp.int32, (SUB, QT), 0)

    vregs_per_seg = RANK_SEG // SUB
    n_seg = -(-n_slc // RANK_SEG)

    def rank_levels(levels, ranks):
        out = []
        for gg in groups:
            rk = list(ranks[gg])
            for k in levels:
                for seg in range(k + 1):
                    rows = range((k + 1) * vregs_per_seg) if seg == k else range(k * vregs_per_seg,
                                                                                 (k + 1) * vregs_per_seg)
                    for sp in range(seg * RANK_SEG, min((seg + 1) * RANK_SEG, n_slc)):
                        row = scores[gg][sp:sp + 1, :]
                        for v in rows:
                            blk = score_rows[gg][v]
                            if v * SUB > sp:
                                beats = row >= blk
                            elif v * SUB + SUB - 1 <= sp:
                                beats = row > blk
                            else:
                                beats = (row > blk) | ((row == blk) & (sub_ix > sp - v * SUB))
                            rk[v] = rk[v] + jnp.where(beats, 1.0, 0.0)
            out.append(tuple(rk))
        return tuple(out)

    ranks = tuple(tuple(jnp.zeros((SUB, QT), F32) for _ in range(n_sub)) for _ in groups)
    last_block = (i + 1) * (QT // SLC_BLOCK) - 1
    for levels in [(0, 1)] + [(k,) for k in range(2, n_seg)]:
        first = max(levels) * RANK_SEG if max(levels) > 1 else SLC_TOPK
        ranks = lax.cond(last_block >= first, functools.partial(rank_levels, levels), lambda rk: rk, ranks)

    qsel = []
    for gg in groups:
        sel_neg = jnp.where(jnp.concatenate(ranks[gg], axis=0) < float(SLC_TOPK), 0.0, UNSELECTED).astype(BF16)
        qsel.append(_lane_cat([jnp.concatenate([q_heads[gg][r], sel_neg, flag_rows(KS_COLS - KS_FLAG)], axis=0)
                               for r in range(R)]))

    n_far = jnp.maximum(i - 1, 0)

    def far_tiles(j):
        return [jnp.where(j + n < n_far, j + n, pad_tile) for n in range(FAR_TILES)]

    def tile_scores(gg, tiles):
        keys = jnp.concatenate([ks_ref[0, gg, t] for t in tiles], axis=0)
        return jnp.dot(keys, qsel[gg], preferred_element_type=F32)

    def online_step(gg, carry, s, tiles):
        m, acc = carry
        m_new = jnp.maximum(m, jnp.max(s, axis=0, keepdims=True))
        p = jnp.exp2(s - m_new).astype(BF16)
        acc = jnp.exp2(m - m_new) * acc
        for n, t in enumerate(tiles):
            acc = acc + jnp.dot(vsT_ref[0, gg, t], p[n * QT:(n + 1) * QT], preferred_element_type=F32)
        return m_new, acc

    def far_trip(u, carries):
        tiles = far_tiles(FAR_TILES * u)
        s_all = [tile_scores(gg, tiles) for gg in groups]
        return tuple(online_step(gg, carries[gg], s_all[gg], tiles) for gg in groups)

    init = (jnp.full((1, RQ), NEG, F32), jnp.zeros((V_ROWS, RQ), F32))
    carries = lax.fori_loop(0, (n_far + FAR_TILES - 1) // FAR_TILES, far_trip, tuple(init for _ in groups))
    near = [jnp.where(i >= 1, i - 1, pad_tile), i]

    gate_ref[...] = jax.nn.sigmoid(gl_ref[...].T)
    s_near = [tile_scores(gg, near) + _lane_cat([nb_ref[gg, r] for r in range(R)]) for gg in groups]
    for gg in groups:
        _, acc_s = online_step(gg, carries[gg], s_near[gg], near)
        o_slc = acc_s[:HEAD_DIM] * (1.0 / acc_s[V_ONES:V_ONES + 1])
        head0 = (first_group + gg) * R
        gates = [_lane_cat([gate_ref[pl.ds((head0 + r) * 3 + br, 1), :] for r in range(R)]) for br in range(3)]
        total = gates[0] * o_cmp[gg] + gates[1] * o_slc + gates[2] * o_win[gg]
        o_ref[:, gg * RW:(gg + 1) * RW] = jnp.concatenate(
            [total[:, r * QT:(r + 1) * QT] for r in range(R)], axis=0).T.astype(o_ref.dtype)


def _nsa_mixer(proj, bsz, L, pe_ck, w_ck1, w_ck2, pe_cv, w_cv1, w_cv2, rel_table):
    g, r = KV_HEADS, Q_PER_KV
    ng = NSA_GROUPS
    ni = L // QT
    nt = L // QT + PAD_TILES
    ncp = L // CMP_STRIDE
    n_slc = L // SLC_BLOCK
    slab = WINDOW + QT
    k_cmp = _compress(proj, COL_KC, bsz, L, pe_ck, w_ck1, w_ck2, transpose_out=False)
    v_cmp_t = _compress(proj, COL_VC, bsz, L, pe_cv, w_cv1, w_cv2, transpose_out=True)
    ks_t, vs_t, kw_t, vw_t = _kv_tiles(proj, bsz, L)
    wb, nb, strip = _nsa_tables(rel_table, L)
    qw = ng * r * HEAD_DIM
    once = pl.Buffered(1)
    whole = lambda *shape: pl.BlockSpec((1, ng) + shape, lambda b, h, i: (b, h) + (0,) * len(shape), once)
    per_group = lambda *shape: pl.BlockSpec((ng,) + shape, lambda b, h, i: (h,) + (0,) * len(shape), once)
    return pl.pallas_call(
        functools.partial(_nsa_kernel, n_slc=n_slc),
        grid=(bsz, g // ng, ni),
        in_specs=[
            pl.BlockSpec((QT, qw), lambda b, h, i: (b * ni + i, COL_Q // qw + h)),
            pl.BlockSpec((QT, LANES), lambda b, h, i: (b * ni + i, COL_GATE // LANES)),
            whole(ncp, HEAD_DIM),
            whole(HEAD_DIM, ncp),
            per_group(r, 2 * ncp, QT),
            pl.BlockSpec((N_SLC_PAD, ncp), lambda b, h, i: (0, 0), once),
            whole(nt, QT, KS_COLS),
            whole(nt, V_ROWS, QT),
            whole(nt, QT, KW_COLS),
            whole(nt, V_ROWS, QT),
            per_group(r, slab, QT),
            per_group(r, 2 * QT, QT),
        ],
        out_specs=pl.BlockSpec((QT, qw), lambda b, h, i: (b * ni + i, h)),
        out_shape=jax.ShapeDtypeStruct((bsz * L, NSA_WIDTH), BF16),
        scratch_shapes=[pltpu.VMEM((LANES, QT), F32)],
        compiler_params=_cparams("parallel", "parallel", "arbitrary"),
        name="nsa_attention",
    )(proj, proj, k_cmp, v_cmp_t, strip, _overlap_t(L), ks_t, vs_t, kw_t, vw_t, wb, nb)


def _out_proj_kernel(x_ref, ys_ref, yn_ref, w_ref, o_ref):
    half = ys_ref.shape[1]
    acc = jnp.dot(ys_ref[...], w_ref[:half, :], preferred_element_type=F32)
    acc = acc + jnp.dot(yn_ref[...], w_ref[half:, :], preferred_element_type=F32)
    o_ref[...] = x_ref[...] + acc


def _out_proj(x2, y_ssm, y_nsa, w, tm=512):
    t, d = x2.shape
    return pl.pallas_call(
        _out_proj_kernel,
        grid=(t // tm,),
        in_specs=[pl.BlockSpec((tm, d), lambda i: (i, 0)),
                  pl.BlockSpec((tm, SSM_WIDTH), lambda i: (i, 0)),
                  pl.BlockSpec((tm, NSA_WIDTH), lambda i: (i, 0)),
                  pl.BlockSpec((SSM_WIDTH + NSA_WIDTH, d), lambda i: (0, 0), pl.Buffered(1))],
        out_specs=pl.BlockSpec((tm, d), lambda i: (i, 0)),
        out_shape=jax.ShapeDtypeStruct((t, d), F32),
        compiler_params=_cparams("parallel"),
        name="out_proj",
    )(x2, y_ssm, y_nsa, w)


def _mlp_kernel(x_ref, n2_ref, wu_ref, wd_ref, nf_ref, o_ref, h_ref):
    f = pl.program_id(1)

    @pl.when(f == 0)
    def _():
        h_ref[...] = _rms(x_ref[...], n2_ref[...]).astype(BF16)

    a = jnp.dot(h_ref[...], wu_ref[...], preferred_element_type=F32)
    a = jnp.square(jnp.maximum(a, 0.0))
    part = jnp.dot(a.astype(BF16), wd_ref[...], preferred_element_type=F32)

    @pl.when(f == 0)
    def _():
        o_ref[...] = part

    @pl.when(f > 0)
    def _():
        o_ref[...] += part

    @pl.when(f == pl.num_programs(1) - 1)
    def _():
        o_ref[...] = _rms(x_ref[...] + o_ref[...], nf_ref[...])


def _mlp(x2, n2, wu, wd, nf, tm=512, tf=2048):
    t, d = x2.shape
    ff = wu.shape[1]
    return pl.pallas_call(
        _mlp_kernel,
        grid=(t // tm, ff // tf),
        in_specs=[pl.BlockSpec((tm, d), lambda i, f: (i, 0)),
                  pl.BlockSpec((1, d), lambda i, f: (0, 0)),
                  pl.BlockSpec((d, tf), lambda i, f: (0, f)),
                  pl.BlockSpec((tf, d), lambda i, f: (f, 0)),
                  pl.BlockSpec((1, d), lambda i, f: (0, 0))],
        out_specs=pl.BlockSpec((tm, d), lambda i, f: (i, 0)),
        out_shape=jax.ShapeDtypeStruct((t, d), F32),
        scratch_shapes=[pltpu.VMEM((tm, d), BF16)],
        compiler_params=pltpu.CompilerParams(dimension_semantics=("parallel", "arbitrary"),
                                             vmem_limit_bytes=MLP_VMEM_LIMIT),
        name="mlp_final_norm",
    )(x2, n2, wu, wd, nf)


def kernel(x, norm1_w, w_in, ssm_a_re, ssm_a_im, ssm_log_dt, ssm_b_re, ssm_b_im, ssm_c_re, ssm_c_im, ssm_d,
           w_glu, b_glu, pe_ck, w_ck1, w_ck2, pe_cv, w_cv1, w_cv2, w_out, norm2_w, w_up, w_down, rel_table,
           norm_f_w):
    bsz, L, d = x.shape
    assert w_in.shape[0] == 1, "the closing rmsnorm is fused into the (single) layer's MLP kernel"
    x2 = x.reshape(bsz * L, d)
    w_in_p = jnp.pad(w_in[0].astype(BF16), ((0, 0), (0, D_IN_PAD - D_IN)))
    proj = _norm_matmul(x2, norm1_w[0].reshape(1, d), w_in_p)
    ops = _s5_operators(ssm_a_re[0], ssm_a_im[0], ssm_log_dt[0], ssm_b_re[0], ssm_b_im[0], ssm_c_re[0], ssm_c_im[0],
                        ssm_d[0])
    y_ssm = _s5_mixer(proj, bsz, L, ops, w_glu[0], b_glu[0])
    y_nsa = _nsa_mixer(proj, bsz, L, pe_ck[0], w_ck1[0], w_ck2[0], pe_cv[0], w_cv1[0], w_cv2[0], rel_table)
    x2 = _out_proj(x2, y_ssm, y_nsa, w_out[0].astype(BF16))
    x2 = _mlp(x2, norm2_w[0].reshape(1, d), w_up[0].astype(BF16), w_down[0].astype(BF16), norm_f_w.reshape(1, d))
    return x2.reshape(bsz, L, d)
```

```python
import functools
import math

import numpy as np
import jax
import jax.numpy as jnp
from jax import lax
from jax.experimental import pallas as pl
from jax.experimental.pallas import tpu as pltpu

F32 = jnp.float32
BF16 = jnp.bfloat16

SSM_WIDTH = 1024
SSM_GROUP = 16
SSM_GROUPS = 64
SSM_STATE = 64
NSA_WIDTH = 1024
HEAD_DIM = 64
NSA_HEADS = 16
KV_HEADS = 4
Q_PER_KV = 4
KV_WIDTH = 256
CMP_BLOCK = 32
CMP_STRIDE = 16
CMP_HIDDEN = 256
SLC_BLOCK = 64
SLC_TOPK = 16
WINDOW = 512
REL_BUCKETS = 32
REL_MAX_DIST = 128
EPS = 1e-6
NEG = -1e30
FORCED_SCORE = 1e4
D_IN = SSM_WIDTH + NSA_WIDTH + 6 * KV_WIDTH + 3 * NSA_HEADS
D_IN_PAD = 3712
COL_Q = SSM_WIDTH
COL_KC = COL_Q + NSA_WIDTH
COL_VC = COL_KC + KV_WIDTH
COL_KS = COL_VC + KV_WIDTH
COL_VS = COL_KS + KV_WIDTH
COL_KW = COL_VS + KV_WIDTH
COL_VW = COL_KW + KV_WIDTH
COL_GATE = COL_VW + KV_WIDTH

LANES = 128
S5_CHUNK = 16
SLAB_GROUPS = LANES // SSM_GROUP
N_SLABS = SSM_GROUPS // SLAB_GROUPS
QT = 128
N_SLC_PAD = 64
RANK_SEG = 16
NSA_GROUPS = 4
FAR_TILES = 2
UNSELECTED = -1e9
LOG2E = math.log2(math.e)
TABLE_W = 1024
VMEM_LIMIT = 56 * 1024 * 1024
MLP_VMEM_LIMIT = 62 * 1024 * 1024
HI = lax.Precision.HIGHEST


def _cparams(*sem):
    return pltpu.CompilerParams(dimension_semantics=sem, vmem_limit_bytes=VMEM_LIMIT)


def _rms(x, w):
    ms = jnp.mean(x * x, axis=-1, keepdims=True)
    return x * lax.rsqrt(ms + EPS) * w


def _lane_cat(parts):
    return jnp.concatenate(parts, axis=1)


def _norm_matmul_kernel(x_ref, nw_ref, w_ref, o_ref):
    h = _rms(x_ref[...], nw_ref[...]).astype(BF16)
    o_ref[...] = jnp.dot(h, w_ref[...], preferred_element_type=F32)


def _norm_matmul(x2, nw, w, tm=512):
    t, d = x2.shape
    n = w.shape[1]
    return pl.pallas_call(
        _norm_matmul_kernel,
        grid=(t // tm,),
        in_specs=[pl.BlockSpec((tm, d), lambda i: (i, 0)),
                  pl.BlockSpec((1, d), lambda i: (0, 0)),
                  pl.BlockSpec((d, n), lambda i: (0, 0), pl.Buffered(1))],
        out_specs=pl.BlockSpec((tm, n), lambda i: (i, 0)),
        out_shape=jax.ShapeDtypeStruct((t, n), F32),
        compiler_params=_cparams("parallel"),
        name="norm_in_proj",
    )(x2, nw, w)


def _s5_operators(a_re, a_im, log_dt, b_re, b_im, c_re, c_im, d):
    q = S5_CHUNK
    g, p = a_re.shape
    h = SSM_GROUP
    dt = jnp.exp(log_dt)[:, None]
    lam_re, lam_im = dt * a_re, dt * a_im
    mag1 = jnp.exp(lam_re)
    abar_re, abar_im = mag1 * jnp.cos(lam_im), mag1 * jnp.sin(lam_im)
    den = a_re * a_re + a_im * a_im
    zr, zi = abar_re - 1.0, abar_im
    coef_re = (zr * a_re + zi * a_im) / den
    coef_im = (zi * a_re - zr * a_im) / den
    bb_re = coef_re[..., None] * b_re - coef_im[..., None] * b_im
    bb_im = coef_re[..., None] * b_im + coef_im[..., None] * b_re
    k = jnp.arange(q + 1, dtype=F32)[:, None, None]
    mag = jnp.exp(k * lam_re)
    pw_re, pw_im = mag * jnp.cos(k * lam_im), mag * jnp.sin(k * lam_im)
    m_re = pw_re[..., None] * bb_re - pw_im[..., None] * bb_im
    m_im = pw_re[..., None] * bb_im + pw_im[..., None] * bb_re
    kern = (jnp.einsum('ghp,kgpi->gkhi', c_re, m_re[:q], precision=HI)
            - jnp.einsum('ghp,kgpi->gkhi', c_im, m_im[:q], precision=HI))
    kern = kern.at[:, 0].add(jax.vmap(jnp.diag)(d))
    rev = np.arange(q - 1, -1, -1)
    s_end = jnp.stack([m_re[rev], m_im[rev]], axis=0).transpose(2, 1, 4, 0, 3)
    pr, pi = pw_re[1:], pw_im[1:]
    o_re = c_re[None] * pr[:, :, None, :] - c_im[None] * pi[:, :, None, :]
    o_im = -(c_re[None] * pi[:, :, None, :] + c_im[None] * pr[:, :, None, :])
    o_carry = jnp.stack([o_re, o_im], axis=0).transpose(2, 0, 4, 1, 3)
    a_q = jnp.stack([pw_re[q], pw_im[q]], axis=0)
    return kern, s_end, o_carry, a_q


def _s5_slab_weights(kern, s_end, o_carry, a_q):
    q, h, p = S5_CHUNK, SSM_GROUP, SSM_STATE
    s_dense = s_end.reshape(SSM_GROUPS, q, h, 2 * p).astype(BF16)
    k_dense = kern.transpose(0, 1, 3, 2).astype(BF16)
    o_dense = o_carry.reshape(SSM_GROUPS, 2, p, q * h).astype(BF16)
    a_re = a_q[0].reshape(N_SLABS, SLAB_GROUPS * p)
    a_im = a_q[1].reshape(N_SLABS, SLAB_GROUPS * p)
    return s_dense, k_dense, o_dense, a_re, a_im


def _chunk_rows(u_ref, nc):
    return _lane_cat([u_ref[pl.ds(s, nc, stride=S5_CHUNK), :] for s in range(S5_CHUNK)]).astype(BF16)


def _s5_state_kernel(u_ref, sd_ref, o_ref, ws_scr):
    nc = u_ref.shape[0] // S5_CHUNK
    q, h, p, sg = S5_CHUNK, SSM_GROUP, SSM_STATE, SLAB_GROUPS

    @pl.when(pl.program_id(1) == 0)
    def _():
        ws_scr[...] = jnp.zeros_like(ws_scr)
        for s in range(q):
            for a in range(sg):
                for c in range(2):
                    ws_scr[s * LANES + a * h: s * LANES + (a + 1) * h,
                           (c * sg + a) * p: (c * sg + a + 1) * p] = sd_ref[a, s, :, c * p:(c + 1) * p]

    o_ref[0] = jnp.dot(_chunk_rows(u_ref, nc), ws_scr[...], preferred_element_type=F32)


def _s5_scan_kernel(h_ref, ar_ref, ai_ref, o_ref):
    half = ar_ref.shape[1]
    ar = ar_ref[...]
    ai = ai_ref[...]

    def body(c, carry):
        hr, hi = carry
        o_ref[0, c, :, :half] = hr
        o_ref[0, c, :, half:] = hi
        xr = h_ref[0, c, :, :half]
        xi = h_ref[0, c, :, half:]
        return ar * hr - ai * hi + xr, ar * hi + ai * hr + xi

    zero = jnp.zeros(ar.shape, F32)
    lax.fori_loop(0, h_ref.shape[1], body, (zero, zero))


def _s5_out_kernel(u_ref, kd_ref, h_ref, od_ref, y_ref, kl_scr, wi_scr, wc_scr):
    nc = u_ref.shape[0] // S5_CHUNK
    q, h, p, sg = S5_CHUNK, SSM_GROUP, SSM_STATE, SLAB_GROUPS

    @pl.when(pl.program_id(1) == 0)
    def _():
        kl_scr[...] = jnp.zeros_like(kl_scr)
        for k in range(q):
            for a in range(sg):
                kl_scr[k, a * h:(a + 1) * h, a * h:(a + 1) * h] = kd_ref[a, k]
        for s in range(q):
            for t in range(s, q):
                wi_scr[s * LANES:(s + 1) * LANES, t * LANES:(t + 1) * LANES] = kl_scr[t - s]
        for t in range(0, q, 2):
            wi_scr[(t + 1) * LANES:(t + 2) * LANES, t * LANES:(t + 1) * LANES] = jnp.zeros((LANES, LANES), BF16)
        wc_scr[...] = jnp.zeros_like(wc_scr)
        for a in range(sg):
            for c in range(2):
                for t in range(q):
                    wc_scr[(c * sg + a) * p:(c * sg + a + 1) * p,
                           t * LANES + a * h: t * LANES + (a + 1) * h] = od_ref[a, c, :, t * h:(t + 1) * h]

    u_rows = _chunk_rows(u_ref, nc)
    h_rows = h_ref[0].astype(BF16)
    pair = 2 * LANES
    for tp in range(q // 2):
        cols = slice(tp * pair, (tp + 1) * pair)
        k_used = (tp + 1) * pair
        y = (jnp.dot(u_rows[:, :k_used], wi_scr[:k_used, cols], preferred_element_type=F32)
             + jnp.dot(h_rows, wc_scr[:, cols], preferred_element_type=F32))
        for tt in range(2):
            y_ref[pl.ds(2 * tp + tt, nc, stride=q), :] = y[:, tt * LANES:(tt + 1) * LANES]


def _glu_kernel(y_ref, w_ref, b_ref, o_ref):
    z = jax.nn.gelu(y_ref[...])
    gate = jnp.dot(z.astype(BF16), w_ref[...], preferred_element_type=F32) + b_ref[...]
    o_ref[...] = (z * jax.nn.sigmoid(gate)).astype(o_ref.dtype)


def _s5_mixer(proj, bsz, L, ops, w_glu, b_glu):
    s_dense, k_dense, o_dense, a_re, a_im = _s5_slab_weights(*ops)
    q, h, p = S5_CHUNK, SSM_GROUP, SSM_STATE
    nc = L // q
    ns, sg = N_SLABS, SLAB_GROUPS
    sw = sg * 2 * p
    kq = q * LANES
    hend = pl.pallas_call(
        _s5_state_kernel,
        grid=(ns, bsz),
        in_specs=[pl.BlockSpec((L, LANES), lambda j, b: (b, j)),
                  pl.BlockSpec((sg, q, h, 2 * p), lambda j, b: (j, 0, 0, 0))],
        out_specs=pl.BlockSpec((1, nc, sw), lambda j, b: (b, 0, j)),
        out_shape=jax.ShapeDtypeStruct((bsz, nc, ns * sw), F32),
        scratch_shapes=[pltpu.VMEM((kq, sw), BF16)],
        compiler_params=_cparams("parallel", "arbitrary"),
        name="s5_chunk_state",
    )(proj, s_dense)
    hprev4 = pl.pallas_call(
        _s5_scan_kernel,
        grid=(bsz,),
        in_specs=[pl.BlockSpec((1, nc, ns, sw), lambda b: (b, 0, 0, 0)),
                  pl.BlockSpec((ns, sw // 2), lambda b: (0, 0)),
                  pl.BlockSpec((ns, sw // 2), lambda b: (0, 0))],
        out_specs=pl.BlockSpec((1, nc, ns, sw), lambda b: (b, 0, 0, 0)),
        out_shape=jax.ShapeDtypeStruct((bsz, nc, ns, sw), F32),
        compiler_params=_cparams("parallel"),
        name="s5_chunk_scan",
    )(hend.reshape(bsz, nc, ns, sw), a_re, a_im)
    y = pl.pallas_call(
        _s5_out_kernel,
        grid=(ns, bsz),
        in_specs=[pl.BlockSpec((L, LANES), lambda j, b: (b, j)),
                  pl.BlockSpec((sg, q, h, h), lambda j, b: (j, 0, 0, 0)),
                  pl.BlockSpec((1, nc, sw), lambda j, b: (b, 0, j)),
                  pl.BlockSpec((sg, 2, p, q * h), lambda j, b: (j, 0, 0, 0))],
        out_specs=pl.BlockSpec((L, LANES), lambda j, b: (b, j)),
        out_shape=jax.ShapeDtypeStruct((bsz * L, SSM_WIDTH), F32),
        scratch_shapes=[pltpu.VMEM((q, LANES, LANES), BF16), pltpu.VMEM((kq, kq), BF16), pltpu.VMEM((sw, kq), BF16)],
        compiler_params=_cparams("parallel", "arbitrary"),
        name="s5_chunk_out",
    )(proj, k_dense, hprev4.reshape(bsz, nc, ns * sw), o_dense)
    tm = 512
    return pl.pallas_call(
        _glu_kernel,
        grid=(bsz * L // tm,),
        in_specs=[pl.BlockSpec((tm, SSM_WIDTH), lambda i: (i, 0)),
                  pl.BlockSpec((SSM_WIDTH, SSM_WIDTH), lambda i: (0, 0)),
                  pl.BlockSpec((1, SSM_WIDTH), lambda i: (0, 0))],
        out_specs=pl.BlockSpec((tm, SSM_WIDTH), lambda i: (i, 0)),
        out_shape=jax.ShapeDtypeStruct((bsz * L, SSM_WIDTH), BF16),
        compiler_params=_cparams("parallel"),
        name="s5_gelu_glu",
    )(y, w_glu.astype(BF16), b_glu.reshape(1, SSM_WIDTH))


def _compress_kernel(k_ref, w1_ref, pe_ref, w1full_ref, w2_ref, o_ref, *, transpose_out):
    nc = k_ref.shape[0] // CMP_STRIDE
    prod = jnp.dot(_chunk_rows(k_ref, nc), w1_ref[...], preferred_element_type=F32)
    pe_h = jnp.dot(jnp.broadcast_to(pe_ref[...], (8, pe_ref.shape[1])), w1full_ref[...],
                   precision=HI, preferred_element_type=F32)[0:1]
    two_h = 2 * CMP_HIDDEN
    for gl in range(LANES // HEAD_DIM):
        first = prod[:, gl * two_h: gl * two_h + CMP_HIDDEN]
        second = prod[:, gl * two_h + CMP_HIDDEN: (gl + 1) * two_h]
        hid = first + pltpu.roll(second, nc - 1, axis=0) + pe_h
        out = jnp.dot(jax.nn.gelu(hid).astype(BF16), w2_ref[...], preferred_element_type=F32)
        if transpose_out:
            o_ref[0, gl] = _lane_cat([out, jnp.zeros_like(out)]).T[:HEAD_DIM].astype(BF16)
        else:
            o_ref[0, gl] = out.astype(BF16)


def _compress(proj, col, bsz, L, pe, w1, w2, transpose_out):
    nc = L // CMP_STRIDE
    half = CMP_STRIDE * HEAD_DIM
    gpl = LANES // HEAD_DIM
    w1cat = jnp.concatenate([w1[:half], w1[half:]], axis=1)
    eye = jnp.eye(gpl, dtype=F32)
    w1slab = jnp.einsum('ldh,ag->ladgh', w1cat.reshape(CMP_STRIDE, HEAD_DIM, 2 * CMP_HIDDEN), eye)
    w1slab = w1slab.reshape(CMP_STRIDE * LANES, gpl * 2 * CMP_HIDDEN).astype(BF16)
    out_block = (1, gpl, HEAD_DIM, nc) if transpose_out else (1, gpl, nc, HEAD_DIM)
    out_full = (bsz, KV_HEADS) + out_block[2:]
    return pl.pallas_call(
        functools.partial(_compress_kernel, transpose_out=transpose_out),
        grid=(bsz, KV_HEADS // gpl),
        in_specs=[pl.BlockSpec((L, LANES), lambda b, j: (b, col // LANES + j)),
                  pl.BlockSpec(w1slab.shape, lambda b, j: (0, 0)),
                  pl.BlockSpec((1, 2 * half), lambda b, j: (0, 0)),
                  pl.BlockSpec((2 * half, CMP_HIDDEN), lambda b, j: (0, 0)),
                  pl.BlockSpec((CMP_HIDDEN, HEAD_DIM), lambda b, j: (0, 0))],
        out_specs=pl.BlockSpec(out_block, lambda b, j: (b, j, 0, 0)),
        out_shape=jax.ShapeDtypeStruct(out_full, BF16),
        compiler_params=_cparams("parallel", "parallel"),
        name="nsa_compress",
    )(proj, w1slab, pe.reshape(1, 2 * half), w1, w2.astype(BF16))


KS_COLS = 2 * LANES
KW_COLS = LANES
KS_FLAG = HEAD_DIM + N_SLC_PAD
KW_FLAG = HEAD_DIM
PAD_TILES = 8
V_ROWS = HEAD_DIM + 16
V_ONES = HEAD_DIM


def _kv_tiles_kernel(ks_ref, vs_ref, kw_ref, vw_ref, kso_ref, vso_ref, kwo_ref, vwo_ref):
    tm = ks_ref.shape[0]
    step = pl.program_id(1)
    row0 = step * tm
    kso_ref[:, :, :, :, KS_FLAG:] = jnp.zeros(kso_ref.shape[:4] + (KS_COLS - KS_FLAG,), BF16)
    kwo_ref[:, :, :, :, KW_FLAG:] = jnp.zeros(kwo_ref.shape[:4] + (KW_COLS - KW_FLAG,), BF16)
    is_real = step < pl.num_programs(1) - 1
    tail_rows = lax.broadcasted_iota(jnp.int32, vso_ref.shape[:3] + (V_ROWS - HEAD_DIM, QT), 3)
    v_tail = jnp.where(tail_rows == 0, jnp.where(is_real, 1.0, 0.0), 0.0).astype(BF16)
    vso_ref[:, :, :, HEAD_DIM:, :] = v_tail
    vwo_ref[:, :, :, HEAD_DIM:, :] = v_tail

    @pl.when(step < pl.num_programs(1) - 1)
    def _():
        lane_blk = lax.broadcasted_iota(jnp.int32, (QT, N_SLC_PAD), 1)
        for k in range(tm // QT):
            rows = slice(k * QT, (k + 1) * QT)
            tok = lax.broadcasted_iota(jnp.int32, (QT, N_SLC_PAD), 0) + (row0 + k * QT)
            onehot = jnp.where((tok >> int(math.log2(SLC_BLOCK))) == lane_blk, 1.0, 0.0).astype(BF16)
            vs_t = vs_ref[rows, :].T
            vw_t = vw_ref[rows, :].T
            for g in range(KV_HEADS):
                cols = slice(g * HEAD_DIM, (g + 1) * HEAD_DIM)
                kso_ref[0, g, k, :, :HEAD_DIM] = ks_ref[rows, cols].astype(BF16)
                kso_ref[0, g, k, :, HEAD_DIM:KS_FLAG] = onehot
                kwo_ref[0, g, k, :, :HEAD_DIM] = kw_ref[rows, cols].astype(BF16)
                vso_ref[0, g, k, :HEAD_DIM, :] = vs_t[cols].astype(BF16)
                vwo_ref[0, g, k, :HEAD_DIM, :] = vw_t[cols].astype(BF16)

    @pl.when(step == pl.num_programs(1) - 1)
    def _():
        ones = jnp.ones(kso_ref.shape[:4] + (1,), BF16)
        kso_ref[:, :, :, :, :KS_FLAG] = jnp.zeros(kso_ref.shape[:4] + (KS_FLAG,), BF16)
        kwo_ref[:, :, :, :, :KW_FLAG] = jnp.zeros(kwo_ref.shape[:4] + (KW_FLAG,), BF16)
        kso_ref[:, :, :, :, KS_FLAG:KS_FLAG + 1] = ones
        kwo_ref[:, :, :, :, KW_FLAG:KW_FLAG + 1] = ones
        vso_ref[:, :, :, :HEAD_DIM, :] = jnp.zeros(vso_ref.shape[:3] + (HEAD_DIM, QT), BF16)
        vwo_ref[:, :, :, :HEAD_DIM, :] = jnp.zeros(vwo_ref.shape[:3] + (HEAD_DIM, QT), BF16)


def _kv_tiles(proj, bsz, L):
    nt = L // QT
    g = KV_HEADS
    kt = PAD_TILES
    tm = kt * QT
    steps = L // tm
    in_spec = lambda col: pl.BlockSpec((tm, KV_WIDTH),
                                       lambda b, i: (b * steps + jnp.minimum(i, steps - 1), col // KV_WIDTH))
    out_spec = lambda a, c: pl.BlockSpec((1, g, kt, a, c), lambda b, i: (b, 0, i, 0, 0))
    shape = lambda a, c: jax.ShapeDtypeStruct((bsz, g, nt + kt, a, c), BF16)
    return pl.pallas_call(
        _kv_tiles_kernel,
        grid=(bsz, steps + 1),
        in_specs=[in_spec(COL_KS), in_spec(COL_VS), in_spec(COL_KW), in_spec(COL_VW)],
        out_specs=[out_spec(QT, KS_COLS), out_spec(V_ROWS, QT), out_spec(QT, KW_COLS), out_spec(V_ROWS, QT)],
        out_shape=[shape(QT, KS_COLS), shape(V_ROWS, QT), shape(QT, KW_COLS), shape(V_ROWS, QT)],
        compiler_params=_cparams("parallel", "arbitrary"),
        name="nsa_kv_tiles",
    )(proj, proj, proj, proj)


def _rel_bucket(dist):
    dist = jnp.maximum(dist, 0)
    max_exact = REL_BUCKETS // 2
    large = max_exact + (jnp.log(jnp.maximum(dist, 1).astype(F32) / max_exact)
                         / math.log(REL_MAX_DIST / max_exact) * (REL_BUCKETS - max_exact)).astype(jnp.int32)
    large = jnp.minimum(large, REL_BUCKETS - 1)
    return jnp.where(dist < max_exact, dist, large)


def _toeplitz_rows(base, nrows, step):
    w = base.shape[1]
    x = jnp.broadcast_to(base, (nrows, w))
    k_ix = lax.broadcasted_iota(jnp.int32, (nrows, w), 0)
    bit = 1
    while bit < nrows:
        x = jnp.where((k_ix & bit) != 0, pltpu.roll(x, (bit * step) % w, axis=1), x)
        bit *= 2
    return x


def _bias_tables_kernel(bw_ref, bc_ref, far_ref, wb_ref, nb_ref, strip_ref, *, ncp):
    slab = WINDOW + QT
    y = _toeplitz_rows(bw_ref[0], QT, 1)
    for kh in range(slab // QT):
        tile = y[:, WINDOW - kh * QT: WINDOW - kh * QT + QT]
        wb_ref[0, kh * QT:(kh + 1) * QT, :] = tile
        if kh >= slab // QT - 2:
            row = kh - (slab // QT - 2)
            nb_ref[0, row * QT:(row + 1) * QT, :] = tile - far_ref[0]
    band = 2 * CMP_STRIDE
    z = _toeplitz_rows(bc_ref[0], band, CMP_STRIDE)[:, 2 * QT: 3 * QT]
    strip_ref[0, :ncp - CMP_STRIDE, :] = jnp.broadcast_to(far_ref[0], (ncp - CMP_STRIDE, QT))
    strip_ref[0, ncp - CMP_STRIDE: ncp + CMP_STRIDE, :] = z
    strip_ref[0, ncp + CMP_STRIDE:, :] = jnp.full((ncp - CMP_STRIDE, QT), NEG, F32)


def _nsa_tables(rel_table, L):
    ncp = L // CMP_STRIDE
    slab = WINDOW + QT
    dist = jnp.arange(TABLE_W)
    rel_table = rel_table * LOG2E
    per_dist = rel_table[_rel_bucket(dist)].T
    base_w = jnp.where(dist < WINDOW, per_dist, NEG)
    shift = CMP_BLOCK - 1
    shifted = jnp.concatenate([jnp.full((NSA_HEADS, shift), NEG, F32), per_dist[:, :TABLE_W - shift]], axis=1)
    base_c = jnp.where(dist < TABLE_W // 2, shifted, NEG)
    far = jnp.broadcast_to(rel_table[REL_BUCKETS - 1][:, None, None], (NSA_HEADS, 1, QT))
    wb, nb, strip = pl.pallas_call(
        functools.partial(_bias_tables_kernel, ncp=ncp),
        grid=(NSA_HEADS,),
        in_specs=[pl.BlockSpec((1, 1, TABLE_W), lambda h: (h, 0, 0)),
                  pl.BlockSpec((1, 1, TABLE_W), lambda h: (h, 0, 0)),
                  pl.BlockSpec((1, 1, QT), lambda h: (h, 0, 0))],
        out_specs=[pl.BlockSpec((1, slab, QT), lambda h: (h, 0, 0)),
                   pl.BlockSpec((1, 2 * QT, QT), lambda h: (h, 0, 0)),
                   pl.BlockSpec((1, 2 * ncp, QT), lambda h: (h, 0, 0))],
        out_shape=[jax.ShapeDtypeStruct((NSA_HEADS, slab, QT), F32),
                   jax.ShapeDtypeStruct((NSA_HEADS, 2 * QT, QT), F32),
                   jax.ShapeDtypeStruct((NSA_HEADS, 2 * ncp, QT), F32)],
        compiler_params=_cparams("parallel"),
        name="nsa_bias_tables",
    )(base_w.reshape(NSA_HEADS, 1, TABLE_W), base_c.reshape(NSA_HEADS, 1, TABLE_W), far)
    g, r = KV_HEADS, Q_PER_KV
    return wb.reshape(g, r, slab, QT), nb.reshape(g, r, 2 * QT, QT), strip.reshape(g, r, 2 * ncp, QT)


def _overlap_t(L):
    n_cmp = (L - CMP_BLOCK) // CMP_STRIDE + 1
    n_slc = L // SLC_BLOCK
    cmp_idx = np.arange(n_cmp)[:, None] * CMP_STRIDE + np.arange(CMP_BLOCK)[None, :]
    overlap = ((cmp_idx[:, :, None] // SLC_BLOCK) == np.arange(n_slc)[None, None, :]).sum(1) / CMP_BLOCK
    out = np.zeros((N_SLC_PAD, L // CMP_STRIDE), np.float32)
    out[:n_slc, :n_cmp] = overlap.T
    return jnp.asarray(out, dtype=BF16)


def _split3(x):
    hi = x.astype(BF16)
    r1 = x - hi.astype(F32)
    mid = r1.astype(BF16)
    lo = (r1 - mid.astype(F32)).astype(BF16)
    return hi, mid, lo


def _nsa_kernel(q_ref, gl_ref, kc_ref, vcT_ref, strip_ref, ovT_ref, ks_ref, vsT_ref, kw_ref, vwT_ref, wb_ref,
                nb_ref, o_ref, gate_ref, *, n_slc):
    i = pl.program_id(2)
    R = Q_PER_KV
    RQ = R * QT
    RW = R * HEAD_DIM
    groups = range(kc_ref.shape[1])
    first_group = pl.program_id(1) * len(groups)
    pad_tile = ks_ref.shape[2] - PAD_TILES
    t_pos = lax.broadcasted_iota(jnp.int32, (1, QT), 1) + i * QT

    def flag_rows(n):
        return jnp.where(lax.broadcasted_iota(jnp.int32, (n, QT), 0) == 0, NEG, 0.0).astype(BF16)

    q_heads = []
    for gg in groups:
        qT = (q_ref[:, gg * RW:(gg + 1) * RW].T * (0.125 * LOG2E)).astype(BF16)
        q_heads.append([qT[r * HEAD_DIM:(r + 1) * HEAD_DIM] for r in range(R)])

    slab = WINDOW + QT
    n_win = slab // QT
    win_idx = [jnp.where(i + jj >= n_win - 1, i + jj - (n_win - 1), pad_tile) for jj in range(n_win)]

    def window_scores(gg):
        kw = jnp.concatenate([kw_ref[0, gg, j] for j in win_idx], axis=0)
        qwin = _lane_cat([jnp.concatenate([q_heads[gg][r], flag_rows(KW_COLS - HEAD_DIM)], axis=0)
                          for r in range(R)])
        wb = _lane_cat([wb_ref[gg, r] for r in range(R)])
        return jnp.dot(kw, qwin, preferred_element_type=F32) + wb

    def window_branch(gg, sw):
        m_w = jnp.max(sw, axis=0, keepdims=True)
        e_wb = jnp.exp2(sw - m_w).astype(BF16)
        acc = None
        for jj, j in enumerate(win_idx):
            term = jnp.dot(vwT_ref[0, gg, j], e_wb[jj * QT:(jj + 1) * QT], preferred_element_type=F32)
            acc = term if acc is None else acc + term
        return acc[:HEAD_DIM] * (1.0 / acc[V_ONES:V_ONES + 1])

    sw_all = [window_scores(gg) for gg in groups]

    ncp = kc_ref.shape[2]
    strip_row = pl.multiple_of(ncp - i * (QT // CMP_STRIDE), QT // CMP_STRIDE)
    ov = ovT_ref[...]
    s_ix = lax.broadcasted_iota(jnp.int32, (N_SLC_PAD, QT), 0)
    cur = t_pos >> int(math.log2(SLC_BLOCK))
    forced = (s_ix == 0) | (s_ix == cur) | (s_ix == cur - 1)

    def compressed_scores(gg):
        sc = jnp.dot(kc_ref[0, gg], _lane_cat(q_heads[gg]), preferred_element_type=F32)
        return sc + _lane_cat([strip_ref[gg, r, pl.ds(strip_row, ncp), :] for r in range(R)])

    def compressed_branch(gg, sm):
        m_c = jnp.max(sm, axis=0, keepdims=True)
        e_c = jnp.exp2(sm - m_c)
        l_c = jnp.sum(e_c, axis=0, keepdims=True)
        inv_c = jnp.where(m_c > 0.5 * NEG, 1.0 / l_c, 0.0)
        o_cmp = jnp.dot(vcT_ref[0, gg], e_c.astype(BF16), preferred_element_type=F32) * inv_c
        p_c = e_c * inv_c
        p_sum = p_c[:, 0:QT]
        for r in range(1, R):
            p_sum = p_sum + p_c[:, r * QT:(r + 1) * QT]
        imp = None
        for part in _split3(p_sum):
            term = jnp.dot(ov, part, preferred_element_type=F32)
            imp = term if imp is None else imp + term
        score = jnp.where(forced, FORCED_SCORE, jnp.where(s_ix <= cur, imp, -1.0))
        return o_cmp, jnp.where(s_ix < n_slc, score, -2.0)

    sm_all = [compressed_scores(gg) for gg in groups]
    o_win = [window_branch(gg, sw_all[gg]) for gg in groups]
    cmp_out = [compressed_branch(gg, sm_all[gg]) for gg in groups]
    o_cmp = [c[0] for c in cmp_out]
    scores = [c[1] for c in cmp_out]

    SUB = 8
    n_sub = N_SLC_PAD // SUB
    score_rows = [[score[v * SUB:(v + 1) * SUB] for v in range(n_sub)] for score in scores]
    sub_ix = lax.broadcasted_iota(jnp.int32, (SUB, QT), 0)

    vregs_per_seg = RANK_SEG // SUB
    n_seg = -(-n_slc // RANK_SEG)

    def rank_levels(levels, ranks):
        out = []
        for gg in groups:
            rk = list(ranks[gg])
            for k in levels:
                for seg in range(k + 1):
                    rows = range((k + 1) * vregs_per_seg) if seg == k else range(k * vregs_per_seg,
                                                                                 (k + 1) * vregs_per_seg)
                    for sp in range(seg * RANK_SEG, min((seg + 1) * RANK_SEG, n_slc)):
                        row = scores[gg][sp:sp + 1, :]
                        for v in rows:
                            blk = score_rows[gg][v]
                            if v * SUB > sp:
                                beats = row >= blk
                            elif v * SUB + SUB - 1 <= sp:
                                beats = row > blk
                            else:
                                beats = (row > blk) | ((row == blk) & (sub_ix > sp - v * SUB))
                            rk[v] = rk[v] + jnp.where(beats, 1.0, 0.0)
            out.append(tuple(rk))
        return tuple(out)

    ranks = tuple(tuple(jnp.zeros((SUB, QT), F32) for _ in range(n_sub)) for _ in groups)
    last_block = (i + 1) * (QT // SLC_BLOCK) - 1
    for levels in [(0, 1)] + [(k,) for k in range(2, n_seg)]:
        first = max(levels) * RANK_SEG if max(levels) > 1 else SLC_TOPK
        ranks = lax.cond(last_block >= first, functools.partial(rank_levels, levels), lambda rk: rk, ranks)

    qsel = []
    for gg in groups:
        sel_neg = jnp.where(jnp.concatenate(ranks[gg], axis=0) < float(SLC_TOPK), 0.0, UNSELECTED).astype(BF16)
        qsel.append(_lane_cat([jnp.concatenate([q_heads[gg][r], sel_neg, flag_rows(KS_COLS - KS_FLAG)], axis=0)
                               for r in range(R)]))

    n_far = jnp.maximum(i - 1, 0)

    def far_tiles(j):
        return [jnp.where(j + n < n_far, j + n, pad_tile) for n in range(FAR_TILES)]

    def tile_scores(gg, tiles):
        keys = jnp.concatenate([ks_ref[0, gg, t] for t in tiles], axis=0)
        return jnp.dot(keys, qsel[gg], preferred_element_type=F32)

    def online_step(gg, carry, s, tiles):
        m, acc = carry
        m_new = jnp.maximum(m, jnp.max(s, axis=0, keepdims=True))
        p = jnp.exp2(s - m_new).astype(BF16)
        acc = jnp.exp2(m - m_new) * acc
        for n, t in enumerate(tiles):
            acc = acc + jnp.dot(vsT_ref[0, gg, t], p[n * QT:(n + 1) * QT], preferred_element_type=F32)
        return m_new, acc

    def far_trip(u, carries):
        tiles = far_tiles(FAR_TILES * u)
        s_all = [tile_scores(gg, tiles) for gg in groups]
        return tuple(online_step(gg, carries[gg], s_all[gg], tiles) for gg in groups)

    init = (jnp.full((1, RQ), NEG, F32), jnp.zeros((V_ROWS, RQ), F32))
    carries = lax.fori_loop(0, (n_far + FAR_TILES - 1) // FAR_TILES, far_trip, tuple(init for _ in groups))
    near = [jnp.where(i >= 1, i - 1, pad_tile), i]

    gate_ref[...] = jax.nn.sigmoid(gl_ref[...].T)
    s_near = [tile_scores(gg, near) + _lane_cat([nb_ref[gg, r] for r in range(R)]) for gg in groups]
    for gg in groups:
        _, acc_s = online_step(gg, carries[gg], s_near[gg], near)
        o_slc = acc_s[:HEAD_DIM] * (1.0 / acc_s[V_ONES:V_ONES + 1])
        head0 = (first_group + gg) * R
        gates = [_lane_cat([gate_ref[pl.ds((head0 + r) * 3 + br, 1), :] for r in range(R)]) for br in range(3)]
        total = gates[0] * o_cmp[gg] + gates[1] * o_slc + gates[2] * o_win[gg]
        o_ref[:, gg * RW:(gg + 1) * RW] = jnp.concatenate(
            [total[:, r * QT:(r + 1) * QT] for r in range(R)], axis=0).T.astype(o_ref.dtype)


def _nsa_mixer(proj, bsz, L, pe_ck, w_ck1, w_ck2, pe_cv, w_cv1, w_cv2, rel_table):
    g, r = KV_HEADS, Q_PER_KV
    ng = NSA_GROUPS
    ni = L // QT
    nt = L // QT + PAD_TILES
    ncp = L // CMP_STRIDE
    n_slc = L // SLC_BLOCK
    slab = WINDOW + QT
    k_cmp = _compress(proj, COL_KC, bsz, L, pe_ck, w_ck1, w_ck2, transpose_out=False)
    v_cmp_t = _compress(proj, COL_VC, bsz, L, pe_cv, w_cv1, w_cv2, transpose_out=True)
    ks_t, vs_t, kw_t, vw_t = _kv_tiles(proj, bsz, L)
    wb, nb, strip = _nsa_tables(rel_table, L)
    qw = ng * r * HEAD_DIM
    once = pl.Buffered(1)
    whole = lambda *shape: pl.BlockSpec((1, ng) + shape, lambda b, h, i: (b, h) + (0,) * len(shape), once)
    per_group = lambda *shape: pl.BlockSpec((ng,) + shape, lambda b, h, i: (h,) + (0,) * len(shape), once)
    return pl.pallas_call(
        functools.partial(_nsa_kernel, n_slc=n_slc),
        grid=(bsz, g // ng, ni),
        in_specs=[
            pl.BlockSpec((QT, qw), lambda b, h, i: (b * ni + i, COL_Q // qw + h)),
            pl.BlockSpec((QT, LANES), lambda b, h, i: (b * ni + i, COL_GATE // LANES)),
            whole(ncp, HEAD_DIM),
            whole(HEAD_DIM, ncp),
            per_group(r, 2 * ncp, QT),
            pl.BlockSpec((N_SLC_PAD, ncp), lambda b, h, i: (0, 0), once),
            whole(nt, QT, KS_COLS),
            whole(nt, V_ROWS, QT),
            whole(nt, QT, KW_COLS),
            whole(nt, V_ROWS, QT),
            per_group(r, slab, QT),
            per_group(r, 2 * QT, QT),
        ],
        out_specs=pl.BlockSpec((QT, qw), lambda b, h, i: (b * ni + i, h)),
        out_shape=jax.ShapeDtypeStruct((bsz * L, NSA_WIDTH), BF16),
        scratch_shapes=[pltpu.VMEM((LANES, QT), F32)],
        compiler_params=_cparams("parallel", "parallel", "arbitrary"),
        name="nsa_attention",
    )(proj, proj, k_cmp, v_cmp_t, strip, _overlap_t(L), ks_t, vs_t, kw_t, vw_t, wb, nb)


def _out_proj_kernel(x_ref, ys_ref, yn_ref, w_ref, o_ref):
    half = ys_ref.shape[1]
    acc = jnp.dot(ys_ref[...], w_ref[:half, :], preferred_element_type=F32)
    acc = acc + jnp.dot(yn_ref[...], w_ref[half:, :], preferred_element_type=F32)
    o_ref[...] = x_ref[...] + acc


def _out_proj(x2, y_ssm, y_nsa, w, tm=512):
    t, d = x2.shape
    return pl.pallas_call(
        _out_proj_kernel,
        grid=(t // tm,),
        in_specs=[pl.BlockSpec((tm, d), lambda i: (i, 0)),
                  pl.BlockSpec((tm, SSM_WIDTH), lambda i: (i, 0)),
                  pl.BlockSpec((tm, NSA_WIDTH), lambda i: (i, 0)),
                  pl.BlockSpec((SSM_WIDTH + NSA_WIDTH, d), lambda i: (0, 0), pl.Buffered(1))],
        out_specs=pl.BlockSpec((tm, d), lambda i: (i, 0)),
        out_shape=jax.ShapeDtypeStruct((t, d), F32),
        compiler_params=_cparams("parallel"),
        name="out_proj",
    )(x2, y_ssm, y_nsa, w)


def _mlp_kernel(x_ref, n2_ref, wu_ref, wd_ref, nf_ref, o_ref, h_ref):
    f = pl.program_id(1)

    @pl.when(f == 0)
    def _():
        h_ref[...] = _rms(x_ref[...], n2_ref[...]).astype(BF16)

    a = jnp.dot(h_ref[...], wu_ref[...].astype(BF16), preferred_element_type=F32)
    a = jnp.square(jnp.maximum(a, 0.0))
    part = jnp.dot(a.astype(BF16), wd_ref[...].astype(BF16), preferred_element_type=F32)

    @pl.when(f == 0)
    def _():
        o_ref[...] = part

    @pl.when(f > 0)
    def _():
        o_ref[...] += part

    @pl.when(f == pl.num_programs(1) - 1)
    def _():
        o_ref[...] = _rms(x_ref[...] + o_ref[...], nf_ref[...])


def _mlp(x2, n2, wu, wd, nf, tm=1024, tf=256):
    t, d = x2.shape
    ff = wu.shape[1]
    return pl.pallas_call(
        _mlp_kernel,
        grid=(t // tm, ff // tf),
        in_specs=[pl.BlockSpec((tm, d), lambda i, f: (i, 0)),
                  pl.BlockSpec((1, d), lambda i, f: (0, 0)),
                  pl.BlockSpec((d, tf), lambda i, f: (0, f)),
                  pl.BlockSpec((tf, d), lambda i, f: (f, 0)),
                  pl.BlockSpec((1, d), lambda i, f: (0, 0))],
        out_specs=pl.BlockSpec((tm, d), lambda i, f: (i, 0)),
        out_shape=jax.ShapeDtypeStruct((t, d), F32),
        scratch_shapes=[pltpu.VMEM((tm, d), BF16)],
        compiler_params=pltpu.CompilerParams(dimension_semantics=("parallel", "arbitrary"),
                                             vmem_limit_bytes=MLP_VMEM_LIMIT),
        name="mlp_final_norm",
    )(x2, n2, wu, wd, nf)


def kernel(x, norm1_w, w_in, ssm_a_re, ssm_a_im, ssm_log_dt, ssm_b_re, ssm_b_im, ssm_c_re, ssm_c_im, ssm_d,
           w_glu, b_glu, pe_ck, w_ck1, w_ck2, pe_cv, w_cv1, w_cv2, w_out, norm2_w, w_up, w_down, rel_table,
           norm_f_w):
    bsz, L, d = x.shape
    assert w_in.shape[0] == 1, "the closing rmsnorm is fused into the (single) layer's MLP kernel"
    x2 = x.reshape(bsz * L, d)
    w_in_p = jnp.pad(w_in[0].astype(BF16), ((0, 0), (0, D_IN_PAD - D_IN)))
    proj = _norm_matmul(x2, norm1_w[0].reshape(1, d), w_in_p)
    ops = _s5_operators(ssm_a_re[0], ssm_a_im[0], ssm_log_dt[0], ssm_b_re[0], ssm_b_im[0], ssm_c_re[0], ssm_c_im[0],
                        ssm_d[0])
    y_ssm = _s5_mixer(proj, bsz, L, ops, w_glu[0], b_glu[0])
    y_nsa = _nsa_mixer(proj, bsz, L, pe_ck[0], w_ck1[0], w_ck2[0], pe_cv[0], w_cv1[0], w_cv2[0], rel_table)
    x2 = _out_proj(x2, y_ssm, y_nsa, w_out[0].astype(BF16))
    x2 = _mlp(x2, norm2_w[0].reshape(1, d), w_up[0], w_down[0], norm_f_w.reshape(1, d))
    return x2.reshape(bsz, L, d)
```

```python
import functools
import math

import numpy as np
import jax
import jax.numpy as jnp
from jax import lax
from jax.experimental import pallas as pl
from jax.experimental.pallas import tpu as pltpu

F32 = jnp.float32
BF16 = jnp.bfloat16

SSM_WIDTH = 1024
SSM_GROUP = 16
SSM_GROUPS = 64
SSM_STATE = 64
NSA_WIDTH = 1024
HEAD_DIM = 64
NSA_HEADS = 16
KV_HEADS = 4
Q_PER_KV = 4
KV_WIDTH = 256
CMP_BLOCK = 32
CMP_STRIDE = 16
CMP_HIDDEN = 256
SLC_BLOCK = 64
SLC_TOPK = 16
WINDOW = 512
REL_BUCKETS = 32
REL_MAX_DIST = 128
EPS = 1e-6
NEG = -1e30
FORCED_SCORE = 1e4
D_IN = SSM_WIDTH + NSA_WIDTH + 6 * KV_WIDTH + 3 * NSA_HEADS
D_IN_PAD = 3712
COL_Q = SSM_WIDTH
COL_KC = COL_Q + NSA_WIDTH
COL_VC = COL_KC + KV_WIDTH
COL_KS = COL_VC + KV_WIDTH
COL_VS = COL_KS + KV_WIDTH
COL_KW = COL_VS + KV_WIDTH
COL_VW = COL_KW + KV_WIDTH
COL_GATE = COL_VW + KV_WIDTH

LANES = 128
S5_CHUNK = 16
SLAB_GROUPS = LANES // SSM_GROUP
N_SLABS = SSM_GROUPS // SLAB_GROUPS
QT = 128
N_SLC_PAD = 64
RANK_SEG = 16
NSA_GROUPS = 4
FAR_TILES = 2
UNSELECTED = -1e9
LOG2E = math.log2(math.e)
TABLE_W = 1024
VMEM_LIMIT = 56 * 1024 * 1024
HI = lax.Precision.HIGHEST


def _cparams(*sem):
    return pltpu.CompilerParams(dimension_semantics=sem, vmem_limit_bytes=VMEM_LIMIT)


def _rms(x, w):
    ms = jnp.mean(x * x, axis=-1, keepdims=True)
    return x * lax.rsqrt(ms + EPS) * w


def _lane_cat(parts):
    return jnp.concatenate(parts, axis=1)


def _norm_matmul_kernel(x_ref, nw_ref, w_ref, o_ref):
    h = _rms(x_ref[...], nw_ref[...]).astype(BF16)
    o_ref[...] = jnp.dot(h, w_ref[...], preferred_element_type=F32)


def _norm_matmul(x2, nw, w, tm=512):
    t, d = x2.shape
    n = w.shape[1]
    return pl.pallas_call(
        _norm_matmul_kernel,
        grid=(t // tm,),
        in_specs=[pl.BlockSpec((tm, d), lambda i: (i, 0)),
                  pl.BlockSpec((1, d), lambda i: (0, 0)),
                  pl.BlockSpec((d, n), lambda i: (0, 0), pl.Buffered(1))],
        out_specs=pl.BlockSpec((tm, n), lambda i: (i, 0)),
        out_shape=jax.ShapeDtypeStruct((t, n), F32),
        compiler_params=_cparams("parallel"),
        name="norm_in_proj",
    )(x2, nw, w)


def _s5_operators(a_re, a_im, log_dt, b_re, b_im, c_re, c_im, d):
    q = S5_CHUNK
    g, p = a_re.shape
    h = SSM_GROUP
    dt = jnp.exp(log_dt)[:, None]
    lam_re, lam_im = dt * a_re, dt * a_im
    mag1 = jnp.exp(lam_re)
    abar_re, abar_im = mag1 * jnp.cos(lam_im), mag1 * jnp.sin(lam_im)
    den = a_re * a_re + a_im * a_im
    zr, zi = abar_re - 1.0, abar_im
    coef_re = (zr * a_re + zi * a_im) / den
    coef_im = (zi * a_re - zr * a_im) / den
    bb_re = coef_re[..., None] * b_re - coef_im[..., None] * b_im
    bb_im = coef_re[..., None] * b_im + coef_im[..., None] * b_re
    k = jnp.arange(q + 1, dtype=F32)[:, None, None]
    mag = jnp.exp(k * lam_re)
    pw_re, pw_im = mag * jnp.cos(k * lam_im), mag * jnp.sin(k * lam_im)
    m_re = pw_re[..., None] * bb_re - pw_im[..., None] * bb_im
    m_im = pw_re[..., None] * bb_im + pw_im[..., None] * bb_re
    kern = (jnp.einsum('ghp,kgpi->gkhi', c_re, m_re[:q], precision=HI)
            - jnp.einsum('ghp,kgpi->gkhi', c_im, m_im[:q], precision=HI))
    kern = kern.at[:, 0].add(jax.vmap(jnp.diag)(d))
    rev = np.arange(q - 1, -1, -1)
    s_end = jnp.stack([m_re[rev], m_im[rev]], axis=0).transpose(2, 1, 4, 0, 3)
    pr, pi = pw_re[1:], pw_im[1:]
    o_re = c_re[None] * pr[:, :, None, :] - c_im[None] * pi[:, :, None, :]
    o_im = -(c_re[None] * pi[:, :, None, :] + c_im[None] * pr[:, :, None, :])
    o_carry = jnp.stack([o_re, o_im], axis=0).transpose(2, 0, 4, 1, 3)
    a_q = jnp.stack([pw_re[q], pw_im[q]], axis=0)
    return kern, s_end, o_carry, a_q


def _s5_slab_weights(kern, s_end, o_carry, a_q):
    q, h, p = S5_CHUNK, SSM_GROUP, SSM_STATE
    s_dense = s_end.reshape(SSM_GROUPS, q, h, 2 * p).astype(BF16)
    k_dense = kern.transpose(0, 1, 3, 2).astype(BF16)
    o_dense = o_carry.reshape(SSM_GROUPS, 2, p, q * h).astype(BF16)
    a_re = a_q[0].reshape(N_SLABS, SLAB_GROUPS * p)
    a_im = a_q[1].reshape(N_SLABS, SLAB_GROUPS * p)
    return s_dense, k_dense, o_dense, a_re, a_im


def _chunk_rows(u_ref, nc):
    return _lane_cat([u_ref[pl.ds(s, nc, stride=S5_CHUNK), :] for s in range(S5_CHUNK)]).astype(BF16)


def _s5_state_kernel(u_ref, sd_ref, o_ref, ws_scr):
    nc = u_ref.shape[0] // S5_CHUNK
    q, h, p, sg = S5_CHUNK, SSM_GROUP, SSM_STATE, SLAB_GROUPS

    @pl.when(pl.program_id(1) == 0)
    def _():
        ws_scr[...] = jnp.zeros_like(ws_scr)
        for s in range(q):
            for a in range(sg):
                for c in range(2):
                    ws_scr[s * LANES + a * h: s * LANES + (a + 1) * h,
                           (c * sg + a) * p: (c * sg + a + 1) * p] = sd_ref[a, s, :, c * p:(c + 1) * p]

    o_ref[0] = jnp.dot(_chunk_rows(u_ref, nc), ws_scr[...], preferred_element_type=F32)


def _s5_scan_kernel(h_ref, ar_ref, ai_ref, o_ref):
    half = ar_ref.shape[1]
    ar = ar_ref[...]
    ai = ai_ref[...]

    def body(c, carry):
        hr, hi = carry
        o_ref[0, c, :, :half] = hr
        o_ref[0, c, :, half:] = hi
        xr = h_ref[0, c, :, :half]
        xi = h_ref[0, c, :, half:]
        return ar * hr - ai * hi + xr, ar * hi + ai * hr + xi

    zero = jnp.zeros(ar.shape, F32)
    lax.fori_loop(0, h_ref.shape[1], body, (zero, zero))


def _s5_out_kernel(u_ref, kd_ref, h_ref, od_ref, y_ref, kl_scr, wi_scr, wc_scr):
    nc = u_ref.shape[0] // S5_CHUNK
    q, h, p, sg = S5_CHUNK, SSM_GROUP, SSM_STATE, SLAB_GROUPS

    @pl.when(pl.program_id(1) == 0)
    def _():
        kl_scr[...] = jnp.zeros_like(kl_scr)
        for k in range(q):
            for a in range(sg):
                kl_scr[k, a * h:(a + 1) * h, a * h:(a + 1) * h] = kd_ref[a, k]
        for s in range(q):
            for t in range(s, q):
                wi_scr[s * LANES:(s + 1) * LANES, t * LANES:(t + 1) * LANES] = kl_scr[t - s]
        for t in range(0, q, 2):
            wi_scr[(t + 1) * LANES:(t + 2) * LANES, t * LANES:(t + 1) * LANES] = jnp.zeros((LANES, LANES), BF16)
        wc_scr[...] = jnp.zeros_like(wc_scr)
        for a in range(sg):
            for c in range(2):
                for t in range(q):
                    wc_scr[(c * sg + a) * p:(c * sg + a + 1) * p,
                           t * LANES + a * h: t * LANES + (a + 1) * h] = od_ref[a, c, :, t * h:(t + 1) * h]

    u_rows = _chunk_rows(u_ref, nc)
    h_rows = h_ref[0].astype(BF16)
    pair = 2 * LANES
    for tp in range(q // 2):
        cols = slice(tp * pair, (tp + 1) * pair)
        k_used = (tp + 1) * pair
        y = (jnp.dot(u_rows[:, :k_used], wi_scr[:k_used, cols], preferred_element_type=F32)
             + jnp.dot(h_rows, wc_scr[:, cols], preferred_element_type=F32))
        for tt in range(2):
            y_ref[pl.ds(2 * tp + tt, nc, stride=q), :] = y[:, tt * LANES:(tt + 1) * LANES]


def _glu_kernel(y_ref, w_ref, b_ref, o_ref):
    z = jax.nn.gelu(y_ref[...])
    gate = jnp.dot(z.astype(BF16), w_ref[...], preferred_element_type=F32) + b_ref[...]
    o_ref[...] = (z * jax.nn.sigmoid(gate)).astype(o_ref.dtype)


def _s5_mixer(proj, bsz, L, ops, w_glu, b_glu):
    s_dense, k_dense, o_dense, a_re, a_im = _s5_slab_weights(*ops)
    q, h, p = S5_CHUNK, SSM_GROUP, SSM_STATE
    nc = L // q
    ns, sg = N_SLABS, SLAB_GROUPS
    sw = sg * 2 * p
    kq = q * LANES
    hend = pl.pallas_call(
        _s5_state_kernel,
        grid=(ns, bsz),
        in_specs=[pl.BlockSpec((L, LANES), lambda j, b: (b, j)),
                  pl.BlockSpec((sg, q, h, 2 * p), lambda j, b: (j, 0, 0, 0))],
        out_specs=pl.BlockSpec((1, nc, sw), lambda j, b: (b, 0, j)),
        out_shape=jax.ShapeDtypeStruct((bsz, nc, ns * sw), F32),
        scratch_shapes=[pltpu.VMEM((kq, sw), BF16)],
        compiler_params=_cparams("parallel", "arbitrary"),
        name="s5_chunk_state",
    )(proj, s_dense)
    hprev4 = pl.pallas_call(
        _s5_scan_kernel,
        grid=(bsz,),
        in_specs=[pl.BlockSpec((1, nc, ns, sw), lambda b: (b, 0, 0, 0)),
                  pl.BlockSpec((ns, sw // 2), lambda b: (0, 0)),
                  pl.BlockSpec((ns, sw // 2), lambda b: (0, 0))],
        out_specs=pl.BlockSpec((1, nc, ns, sw), lambda b: (b, 0, 0, 0)),
        out_shape=jax.ShapeDtypeStruct((bsz, nc, ns, sw), F32),
        compiler_params=_cparams("parallel"),
        name="s5_chunk_scan",
    )(hend.reshape(bsz, nc, ns, sw), a_re, a_im)
    y = pl.pallas_call(
        _s5_out_kernel,
        grid=(ns, bsz),
        in_specs=[pl.BlockSpec((L, LANES), lambda j, b: (b, j)),
                  pl.BlockSpec((sg, q, h, h), lambda j, b: (j, 0, 0, 0)),
                  pl.BlockSpec((1, nc, sw), lambda j, b: (b, 0, j)),
                  pl.BlockSpec((sg, 2, p, q * h), lambda j, b: (j, 0, 0, 0))],
        out_specs=pl.BlockSpec((L, LANES), lambda j, b: (b, j)),
        out_shape=jax.ShapeDtypeStruct((bsz * L, SSM_WIDTH), F32),
        scratch_shapes=[pltpu.VMEM((q, LANES, LANES), BF16), pltpu.VMEM((kq, kq), BF16), pltpu.VMEM((sw, kq), BF16)],
        compiler_params=_cparams("parallel", "arbitrary"),
        name="s5_chunk_out",
    )(proj, k_dense, hprev4.reshape(bsz, nc, ns * sw), o_dense)
    tm = 512
    return pl.pallas_call(
        _glu_kernel,
        grid=(bsz * L // tm,),
        in_specs=[pl.BlockSpec((tm, SSM_WIDTH), lambda i: (i, 0)),
                  pl.BlockSpec((SSM_WIDTH, SSM_WIDTH), lambda i: (0, 0)),
                  pl.BlockSpec((1, SSM_WIDTH), lambda i: (0, 0))],
        out_specs=pl.BlockSpec((tm, SSM_WIDTH), lambda i: (i, 0)),
        out_shape=jax.ShapeDtypeStruct((bsz * L, SSM_WIDTH), BF16),
        compiler_params=_cparams("parallel"),
        name="s5_gelu_glu",
    )(y, w_glu.astype(BF16), b_glu.reshape(1, SSM_WIDTH))


def _compress_kernel(k_ref, w1_ref, pe_ref, w1full_ref, w2_ref, o_ref, *, transpose_out):
    nc = k_ref.shape[0] // CMP_STRIDE
    prod = jnp.dot(_chunk_rows(k_ref, nc), w1_ref[...], preferred_element_type=F32)
    pe_h = jnp.dot(jnp.broadcast_to(pe_ref[...], (8, pe_ref.shape[1])), w1full_ref[...],
                   precision=HI, preferred_element_type=F32)[0:1]
    two_h = 2 * CMP_HIDDEN
    for gl in range(LANES // HEAD_DIM):
        first = prod[:, gl * two_h: gl * two_h + CMP_HIDDEN]
        second = prod[:, gl * two_h + CMP_HIDDEN: (gl + 1) * two_h]
        hid = first + pltpu.roll(second, nc - 1, axis=0) + pe_h
        out = jnp.dot(jax.nn.gelu(hid).astype(BF16), w2_ref[...], preferred_element_type=F32)
        if transpose_out:
            o_ref[0, gl] = _lane_cat([out, jnp.zeros_like(out)]).T[:HEAD_DIM].astype(BF16)
        else:
            o_ref[0, gl] = out.astype(BF16)


def _compress(proj, col, bsz, L, pe, w1, w2, transpose_out):
    nc = L // CMP_STRIDE
    half = CMP_STRIDE * HEAD_DIM
    gpl = LANES // HEAD_DIM
    w1cat = jnp.concatenate([w1[:half], w1[half:]], axis=1)
    eye = jnp.eye(gpl, dtype=F32)
    w1slab = jnp.einsum('ldh,ag->ladgh', w1cat.reshape(CMP_STRIDE, HEAD_DIM, 2 * CMP_HIDDEN), eye)
    w1slab = w1slab.reshape(CMP_STRIDE * LANES, gpl * 2 * CMP_HIDDEN).astype(BF16)
    out_block = (1, gpl, HEAD_DIM, nc) if transpose_out else (1, gpl, nc, HEAD_DIM)
    out_full = (bsz, KV_HEADS) + out_block[2:]
    return pl.pallas_call(
        functools.partial(_compress_kernel, transpose_out=transpose_out),
        grid=(bsz, KV_HEADS // gpl),
        in_specs=[pl.BlockSpec((L, LANES), lambda b, j: (b, col // LANES + j)),
                  pl.BlockSpec(w1slab.shape, lambda b, j: (0, 0)),
                  pl.BlockSpec((1, 2 * half), lambda b, j: (0, 0)),
                  pl.BlockSpec((2 * half, CMP_HIDDEN), lambda b, j: (0, 0)),
                  pl.BlockSpec((CMP_HIDDEN, HEAD_DIM), lambda b, j: (0, 0))],
        out_specs=pl.BlockSpec(out_block, lambda b, j: (b, j, 0, 0)),
        out_shape=jax.ShapeDtypeStruct(out_full, BF16),
        compiler_params=_cparams("parallel", "parallel"),
        name="nsa_compress",
    )(proj, w1slab, pe.reshape(1, 2 * half), w1, w2.astype(BF16))


KS_COLS = 2 * LANES
KW_COLS = LANES
KS_FLAG = HEAD_DIM + N_SLC_PAD
KW_FLAG = HEAD_DIM
PAD_TILES = 8
V_ROWS = HEAD_DIM + 16
V_ONES = HEAD_DIM


def _kv_tiles_kernel(ks_ref, vs_ref, kw_ref, vw_ref, kso_ref, vso_ref, kwo_ref, vwo_ref):
    tm = ks_ref.shape[0]
    step = pl.program_id(1)
    row0 = step * tm
    kso_ref[:, :, :, :, KS_FLAG:] = jnp.zeros(kso_ref.shape[:4] + (KS_COLS - KS_FLAG,), BF16)
    kwo_ref[:, :, :, :, KW_FLAG:] = jnp.zeros(kwo_ref.shape[:4] + (KW_COLS - KW_FLAG,), BF16)
    is_real = step < pl.num_programs(1) - 1
    tail_rows = lax.broadcasted_iota(jnp.int32, vso_ref.shape[:3] + (V_ROWS - HEAD_DIM, QT), 3)
    v_tail = jnp.where(tail_rows == 0, jnp.where(is_real, 1.0, 0.0), 0.0).astype(BF16)
    vso_ref[:, :, :, HEAD_DIM:, :] = v_tail
    vwo_ref[:, :, :, HEAD_DIM:, :] = v_tail

    @pl.when(step < pl.num_programs(1) - 1)
    def _():
        lane_blk = lax.broadcasted_iota(jnp.int32, (QT, N_SLC_PAD), 1)
        for k in range(tm // QT):
            rows = slice(k * QT, (k + 1) * QT)
            tok = lax.broadcasted_iota(jnp.int32, (QT, N_SLC_PAD), 0) + (row0 + k * QT)
            onehot = jnp.where((tok >> int(math.log2(SLC_BLOCK))) == lane_blk, 1.0, 0.0).astype(BF16)
            vs_t = vs_ref[rows, :].T
            vw_t = vw_ref[rows, :].T
            for g in range(KV_HEADS):
                cols = slice(g * HEAD_DIM, (g + 1) * HEAD_DIM)
                kso_ref[0, g, k, :, :HEAD_DIM] = ks_ref[rows, cols].astype(BF16)
                kso_ref[0, g, k, :, HEAD_DIM:KS_FLAG] = onehot
                kwo_ref[0, g, k, :, :HEAD_DIM] = kw_ref[rows, cols].astype(BF16)
                vso_ref[0, g, k, :HEAD_DIM, :] = vs_t[cols].astype(BF16)
                vwo_ref[0, g, k, :HEAD_DIM, :] = vw_t[cols].astype(BF16)

    @pl.when(step == pl.num_programs(1) - 1)
    def _():
        ones = jnp.ones(kso_ref.shape[:4] + (1,), BF16)
        kso_ref[:, :, :, :, :KS_FLAG] = jnp.zeros(kso_ref.shape[:4] + (KS_FLAG,), BF16)
        kwo_ref[:, :, :, :, :KW_FLAG] = jnp.zeros(kwo_ref.shape[:4] + (KW_FLAG,), BF16)
        kso_ref[:, :, :, :, KS_FLAG:KS_FLAG + 1] = ones
        kwo_ref[:, :, :, :, KW_FLAG:KW_FLAG + 1] = ones
        vso_ref[:, :, :, :HEAD_DIM, :] = jnp.zeros(vso_ref.shape[:3] + (HEAD_DIM, QT), BF16)
        vwo_ref[:, :, :, :HEAD_DIM, :] = jnp.zeros(vwo_ref.shape[:3] + (HEAD_DIM, QT), BF16)


def _kv_tiles(proj, bsz, L):
    nt = L // QT
    g = KV_HEADS
    kt = PAD_TILES
    tm = kt * QT
    steps = L // tm
    in_spec = lambda col: pl.BlockSpec((tm, KV_WIDTH),
                                       lambda b, i: (b * steps + jnp.minimum(i, steps - 1), col // KV_WIDTH))
    out_spec = lambda a, c: pl.BlockSpec((1, g, kt, a, c), lambda b, i: (b, 0, i, 0, 0))
    shape = lambda a, c: jax.ShapeDtypeStruct((bsz, g, nt + kt, a, c), BF16)
    return pl.pallas_call(
        _kv_tiles_kernel,
        grid=(bsz, steps + 1),
        in_specs=[in_spec(COL_KS), in_spec(COL_VS), in_spec(COL_KW), in_spec(COL_VW)],
        out_specs=[out_spec(QT, KS_COLS), out_spec(V_ROWS, QT), out_spec(QT, KW_COLS), out_spec(V_ROWS, QT)],
        out_shape=[shape(QT, KS_COLS), shape(V_ROWS, QT), shape(QT, KW_COLS), shape(V_ROWS, QT)],
        compiler_params=_cparams("parallel", "arbitrary"),
        name="nsa_kv_tiles",
    )(proj, proj, proj, proj)


def _rel_bucket(dist):
    dist = jnp.maximum(dist, 0)
    max_exact = REL_BUCKETS // 2
    large = max_exact + (jnp.log(jnp.maximum(dist, 1).astype(F32) / max_exact)
                         / math.log(REL_MAX_DIST / max_exact) * (REL_BUCKETS - max_exact)).astype(jnp.int32)
    large = jnp.minimum(large, REL_BUCKETS - 1)
    return jnp.where(dist < max_exact, dist, large)


def _toeplitz_rows(base, nrows, step):
    w = base.shape[1]
    x = jnp.broadcast_to(base, (nrows, w))
    k_ix = lax.broadcasted_iota(jnp.int32, (nrows, w), 0)
    bit = 1
    while bit < nrows:
        x = jnp.where((k_ix & bit) != 0, pltpu.roll(x, (bit * step) % w, axis=1), x)
        bit *= 2
    return x


def _bias_tables_kernel(bw_ref, bc_ref, far_ref, wb_ref, nb_ref, strip_ref, *, ncp):
    slab = WINDOW + QT
    y = _toeplitz_rows(bw_ref[0], QT, 1)
    for kh in range(slab // QT):
        tile = y[:, WINDOW - kh * QT: WINDOW - kh * QT + QT]
        wb_ref[0, kh * QT:(kh + 1) * QT, :] = tile
        if kh >= slab // QT - 2:
            row = kh - (slab // QT - 2)
            nb_ref[0, row * QT:(row + 1) * QT, :] = tile - far_ref[0]
    band = 2 * CMP_STRIDE
    z = _toeplitz_rows(bc_ref[0], band, CMP_STRIDE)[:, 2 * QT: 3 * QT]
    strip_ref[0, :ncp - CMP_STRIDE, :] = jnp.broadcast_to(far_ref[0], (ncp - CMP_STRIDE, QT))
    strip_ref[0, ncp - CMP_STRIDE: ncp + CMP_STRIDE, :] = z
    strip_ref[0, ncp + CMP_STRIDE:, :] = jnp.full((ncp - CMP_STRIDE, QT), NEG, F32)


def _nsa_tables(rel_table, L):
    ncp = L // CMP_STRIDE
    slab = WINDOW + QT
    dist = jnp.arange(TABLE_W)
    rel_table = rel_table * LOG2E
    per_dist = rel_table[_rel_bucket(dist)].T
    base_w = jnp.where(dist < WINDOW, per_dist, NEG)
    shift = CMP_BLOCK - 1
    shifted = jnp.concatenate([jnp.full((NSA_HEADS, shift), NEG, F32), per_dist[:, :TABLE_W - shift]], axis=1)
    base_c = jnp.where(dist < TABLE_W // 2, shifted, NEG)
    far = jnp.broadcast_to(rel_table[REL_BUCKETS - 1][:, None, None], (NSA_HEADS, 1, QT))
    wb, nb, strip = pl.pallas_call(
        functools.partial(_bias_tables_kernel, ncp=ncp),
        grid=(NSA_HEADS,),
        in_specs=[pl.BlockSpec((1, 1, TABLE_W), lambda h: (h, 0, 0)),
                  pl.BlockSpec((1, 1, TABLE_W), lambda h: (h, 0, 0)),
                  pl.BlockSpec((1, 1, QT), lambda h: (h, 0, 0))],
        out_specs=[pl.BlockSpec((1, slab, QT), lambda h: (h, 0, 0)),
                   pl.BlockSpec((1, 2 * QT, QT), lambda h: (h, 0, 0)),
                   pl.BlockSpec((1, 2 * ncp, QT), lambda h: (h, 0, 0))],
        out_shape=[jax.ShapeDtypeStruct((NSA_HEADS, slab, QT), F32),
                   jax.ShapeDtypeStruct((NSA_HEADS, 2 * QT, QT), F32),
                   jax.ShapeDtypeStruct((NSA_HEADS, 2 * ncp, QT), F32)],
        compiler_params=_cparams("parallel"),
        name="nsa_bias_tables",
    )(base_w.reshape(NSA_HEADS, 1, TABLE_W), base_c.reshape(NSA_HEADS, 1, TABLE_W), far)
    g, r = KV_HEADS, Q_PER_KV
    return wb.reshape(g, r, slab, QT), nb.reshape(g, r, 2 * QT, QT), strip.reshape(g, r, 2 * ncp, QT)


def _overlap_t(L):
    n_cmp = (L - CMP_BLOCK) // CMP_STRIDE + 1
    n_slc = L // SLC_BLOCK
    cmp_idx = np.arange(n_cmp)[:, None] * CMP_STRIDE + np.arange(CMP_BLOCK)[None, :]
    overlap = ((cmp_idx[:, :, None] // SLC_BLOCK) == np.arange(n_slc)[None, None, :]).sum(1) / CMP_BLOCK
    out = np.zeros((N_SLC_PAD, L // CMP_STRIDE), np.float32)
    out[:n_slc, :n_cmp] = overlap.T
    return jnp.asarray(out, dtype=BF16)


def _split3(x):
    hi = x.astype(BF16)
    r1 = x - hi.astype(F32)
    mid = r1.astype(BF16)
    lo = (r1 - mid.astype(F32)).astype(BF16)
    return hi, mid, lo


def _nsa_kernel(q_ref, gl_ref, kc_ref, vcT_ref, strip_ref, ovT_ref, ks_ref, vsT_ref, kw_ref, vwT_ref, wb_ref,
                nb_ref, o_ref, gate_ref, *, n_slc):
    i = pl.program_id(2)
    R = Q_PER_KV
    RQ = R * QT
    RW = R * HEAD_DIM
    groups = range(kc_ref.shape[1])
    first_group = pl.program_id(1) * len(groups)
    pad_tile = ks_ref.shape[2] - PAD_TILES
    t_pos = lax.broadcasted_iota(jnp.int32, (1, QT), 1) + i * QT

    def flag_rows(n):
        return jnp.where(lax.broadcasted_iota(jnp.int32, (n, QT), 0) == 0, NEG, 0.0).astype(BF16)

    q_heads = []
    for gg in groups:
        qT = (q_ref[:, gg * RW:(gg + 1) * RW].T * (0.125 * LOG2E)).astype(BF16)
        q_heads.append([qT[r * HEAD_DIM:(r + 1) * HEAD_DIM] for r in range(R)])

    slab = WINDOW + QT
    n_win = slab // QT
    win_idx = [jnp.where(i + jj >= n_win - 1, i + jj - (n_win - 1), pad_tile) for jj in range(n_win)]

    def window_scores(gg):
        kw = jnp.concatenate([kw_ref[0, gg, j] for j in win_idx], axis=0)
        qwin = _lane_cat([jnp.concatenate([q_heads[gg][r], flag_rows(KW_COLS - HEAD_DIM)], axis=0)
                          for r in range(R)])
        wb = _lane_cat([wb_ref[gg, r] for r in range(R)])
        return jnp.dot(kw, qwin, preferred_element_type=F32) + wb

    def window_branch(gg, sw):
        m_w = jnp.max(sw, axis=0, keepdims=True)
        e_wb = jnp.exp2(sw - m_w).astype(BF16)
        acc = None
        for jj, j in enumerate(win_idx):
            term = jnp.dot(vwT_ref[0, gg, j], e_wb[jj * QT:(jj + 1) * QT], preferred_element_type=F32)
            acc = term if acc is None else acc + term
        return acc[:HEAD_DIM] * (1.0 / acc[V_ONES:V_ONES + 1])

    sw_all = [window_scores(gg) for gg in groups]

    ncp = kc_ref.shape[2]
    strip_row = pl.multiple_of(ncp - i * (QT // CMP_STRIDE), QT // CMP_STRIDE)
    ov = ovT_ref[...]
    s_ix = lax.broadcasted_iota(jnp.int32, (N_SLC_PAD, QT), 0)
    cur = t_pos >> int(math.log2(SLC_BLOCK))
    forced = (s_ix == 0) | (s_ix == cur) | (s_ix == cur - 1)

    def compressed_scores(gg):
        sc = jnp.dot(kc_ref[0, gg], _lane_cat(q_heads[gg]), preferred_element_type=F32)
        return sc + _lane_cat([strip_ref[gg, r, pl.ds(strip_row, ncp), :] for r in range(R)])

    def compressed_branch(gg, sm):
        m_c = jnp.max(sm, axis=0, keepdims=True)
        e_c = jnp.exp2(sm - m_c)
        l_c = jnp.sum(e_c, axis=0, keepdims=True)
        inv_c = jnp.where(m_c > 0.5 * NEG, 1.0 / l_c, 0.0)
        o_cmp = jnp.dot(vcT_ref[0, gg], e_c.astype(BF16), preferred_element_type=F32) * inv_c
        p_c = e_c * inv_c
        p_sum = p_c[:, 0:QT]
        for r in range(1, R):
            p_sum = p_sum + p_c[:, r * QT:(r + 1) * QT]
        imp = None
        for part in _split3(p_sum):
            term = jnp.dot(ov, part, preferred_element_type=F32)
            imp = term if imp is None else imp + term
        score = jnp.where(forced, FORCED_SCORE, jnp.where(s_ix <= cur, imp, -1.0))
        return o_cmp, jnp.where(s_ix < n_slc, score, -2.0)

    sm_all = [compressed_scores(gg) for gg in groups]
    o_win = [window_branch(gg, sw_all[gg]) for gg in groups]
    cmp_out = [compressed_branch(gg, sm_all[gg]) for gg in groups]
    o_cmp = [c[0] for c in cmp_out]
    scores = [c[1] for c in cmp_out]

    SUB = 8
    n_sub = N_SLC_PAD // SUB
    score_rows = [[score[v * SUB:(v + 1) * SUB] for v in range(n_sub)] for score in scores]
    sub_ix = lax.broadcasted_iota(jnp.int32, (SUB, QT), 0)

    vregs_per_seg = RANK_SEG // SUB
    n_seg = -(-n_slc // RANK_SEG)

    def rank_levels(levels, ranks):
        out = []
        for gg in groups:
            rk = list(ranks[gg])
            for k in levels:
                for seg in range(k + 1):
                    rows = range((k + 1) * vregs_per_seg) if seg == k else range(k * vregs_per_seg,
                                                                                 (k + 1) * vregs_per_seg)
                    for sp in range(seg * RANK_SEG, min((seg + 1) * RANK_SEG, n_slc)):
                        row = scores[gg][sp:sp + 1, :]
                        for v in rows:
                            blk = score_rows[gg][v]
                            if v * SUB > sp:
                                beats = row >= blk
                            elif v * SUB + SUB - 1 <= sp:
                                beats = row > blk
                            else:
                                beats = (row > blk) | ((row == blk) & (sub_ix > sp - v * SUB))
                            rk[v] = rk[v] + jnp.where(beats, 1.0, 0.0)
            out.append(tuple(rk))
        return tuple(out)

    ranks = tuple(tuple(jnp.zeros((SUB, QT), F32) for _ in range(n_sub)) for _ in groups)
    last_block = (i + 1) * (QT // SLC_BLOCK) - 1
    for levels in [(0, 1)] + [(k,) for k in range(2, n_seg)]:
        first = max(levels) * RANK_SEG if max(levels) > 1 else SLC_TOPK
        ranks = lax.cond(last_block >= first, functools.partial(rank_levels, levels), lambda rk: rk, ranks)

    qsel = []
    for gg in groups:
        sel_neg = jnp.where(jnp.concatenate(ranks[gg], axis=0) < float(SLC_TOPK), 0.0, UNSELECTED).astype(BF16)
        qsel.append(_lane_cat([jnp.concatenate([q_heads[gg][r], sel_neg, flag_rows(KS_COLS - KS_FLAG)], axis=0)
                               for r in range(R)]))

    n_far = jnp.maximum(i - 1, 0)

    def far_tiles(j):
        return [jnp.where(j + n < n_far, j + n, pad_tile) for n in range(FAR_TILES)]

    def tile_scores(gg, tiles):
        keys = jnp.concatenate([ks_ref[0, gg, t] for t in tiles], axis=0)
        return jnp.dot(keys, qsel[gg], preferred_element_type=F32)

    def online_step(gg, carry, s, tiles):
        m, acc = carry
        m_new = jnp.maximum(m, jnp.max(s, axis=0, keepdims=True))
        p = jnp.exp2(s - m_new).astype(BF16)
        acc = jnp.exp2(m - m_new) * acc
        for n, t in enumerate(tiles):
            acc = acc + jnp.dot(vsT_ref[0, gg, t], p[n * QT:(n + 1) * QT], preferred_element_type=F32)
        return m_new, acc

    def far_trip(u, carries):
        tiles = far_tiles(FAR_TILES * u)
        s_all = [tile_scores(gg, tiles) for gg in groups]
        return tuple(online_step(gg, carries[gg], s_all[gg], tiles) for gg in groups)

    init = (jnp.full((1, RQ), NEG, F32), jnp.zeros((V_ROWS, RQ), F32))
    carries = lax.fori_loop(0, (n_far + FAR_TILES - 1) // FAR_TILES, far_trip, tuple(init for _ in groups))
    near = [jnp.where(i >= 1, i - 1, pad_tile), i]

    gate_ref[...] = jax.nn.sigmoid(gl_ref[...].T)
    s_near = [tile_scores(gg, near) + _lane_cat([nb_ref[gg, r] for r in range(R)]) for gg in groups]
    for gg in groups:
        _, acc_s = online_step(gg, carries[gg], s_near[gg], near)
        o_slc = acc_s[:HEAD_DIM] * (1.0 / acc_s[V_ONES:V_ONES + 1])
        head0 = (first_group + gg) * R
        gates = [_lane_cat([gate_ref[pl.ds((head0 + r) * 3 + br, 1), :] for r in range(R)]) for br in range(3)]
        total = gates[0] * o_cmp[gg] + gates[1] * o_slc + gates[2] * o_win[gg]
        o_ref[:, gg * RW:(gg + 1) * RW] = jnp.concatenate(
            [total[:, r * QT:(r + 1) * QT] for r in range(R)], axis=0).T.astype(o_ref.dtype)


def _nsa_mixer(proj, bsz, L, pe_ck, w_ck1, w_ck2, pe_cv, w_cv1, w_cv2, rel_table):
    g, r = KV_HEADS, Q_PER_KV
    ng = NSA_GROUPS
    ni = L // QT
    nt = L // QT + PAD_TILES
    ncp = L // CMP_STRIDE
    n_slc = L // SLC_BLOCK
    slab = WINDOW + QT
    k_cmp = _compress(proj, COL_KC, bsz, L, pe_ck, w_ck1, w_ck2, transpose_out=False)
    v_cmp_t = _compress(proj, COL_VC, bsz, L, pe_cv, w_cv1, w_cv2, transpose_out=True)
    ks_t, vs_t, kw_t, vw_t = _kv_tiles(proj, bsz, L)
    wb, nb, strip = _nsa_tables(rel_table, L)
    qw = ng * r * HEAD_DIM
    once = pl.Buffered(1)
    whole = lambda *shape: pl.BlockSpec((1, ng) + shape, lambda b, h, i: (b, h) + (0,) * len(shape), once)
    per_group = lambda *shape: pl.BlockSpec((ng,) + shape, lambda b, h, i: (h,) + (0,) * len(shape), once)
    return pl.pallas_call(
        functools.partial(_nsa_kernel, n_slc=n_slc),
        grid=(bsz, g // ng, ni),
        in_specs=[
            pl.BlockSpec((QT, qw), lambda b, h, i: (b * ni + i, COL_Q // qw + h)),
            pl.BlockSpec((QT, LANES), lambda b, h, i: (b * ni + i, COL_GATE // LANES)),
            whole(ncp, HEAD_DIM),
            whole(HEAD_DIM, ncp),
            per_group(r, 2 * ncp, QT),
            pl.BlockSpec((N_SLC_PAD, ncp), lambda b, h, i: (0, 0), once),
            whole(nt, QT, KS_COLS),
            whole(nt, V_ROWS, QT),
            whole(nt, QT, KW_COLS),
            whole(nt, V_ROWS, QT),
            per_group(r, slab, QT),
            per_group(r, 2 * QT, QT),
        ],
        out_specs=pl.BlockSpec((QT, qw), lambda b, h, i: (b * ni + i, h)),
        out_shape=jax.ShapeDtypeStruct((bsz * L, NSA_WIDTH), BF16),
        scratch_shapes=[pltpu.VMEM((LANES, QT), F32)],
        compiler_params=_cparams("parallel", "parallel", "arbitrary"),
        name="nsa_attention",
    )(proj, proj, k_cmp, v_cmp_t, strip, _overlap_t(L), ks_t, vs_t, kw_t, vw_t, wb, nb)


def _out_proj_kernel(x_ref, ys_ref, yn_ref, w_ref, o_ref):
    half = ys_ref.shape[1]
    acc = jnp.dot(ys_ref[...], w_ref[:half, :], preferred_element_type=F32)
    acc = acc + jnp.dot(yn_ref[...], w_ref[half:, :], preferred_element_type=F32)
    o_ref[...] = x_ref[...] + acc


def _out_proj(x2, y_ssm, y_nsa, w, tm=512):
    t, d = x2.shape
    return pl.pallas_call(
        _out_proj_kernel,
        grid=(t // tm,),
        in_specs=[pl.BlockSpec((tm, d), lambda i: (i, 0)),
                  pl.BlockSpec((tm, SSM_WIDTH), lambda i: (i, 0)),
                  pl.BlockSpec((tm, NSA_WIDTH), lambda i: (i, 0)),
                  pl.BlockSpec((SSM_WIDTH + NSA_WIDTH, d), lambda i: (0, 0), pl.Buffered(1))],
        out_specs=pl.BlockSpec((tm, d), lambda i: (i, 0)),
        out_shape=jax.ShapeDtypeStruct((t, d), F32),
        compiler_params=_cparams("parallel"),
        name="out_proj",
    )(x2, y_ssm, y_nsa, w)


def _mlp_kernel(x_ref, n2_ref, wu_ref, wd_ref, nf_ref, o_ref, h_ref, acc_ref):
    f = pl.program_id(1)

    @pl.when(f == 0)
    def _():
        h_ref[...] = _rms(x_ref[...], n2_ref[...]).astype(BF16)
        acc_ref[...] = jnp.zeros_like(acc_ref)

    a = jnp.dot(h_ref[...], wu_ref[...], preferred_element_type=F32)
    a = jnp.square(jnp.maximum(a, 0.0))
    acc_ref[...] += jnp.dot(a.astype(BF16), wd_ref[...], preferred_element_type=F32)

    @pl.when(f == pl.num_programs(1) - 1)
    def _():
        o_ref[...] = _rms(x_ref[...] + acc_ref[...], nf_ref[...])


def _mlp(x2, n2, wu, wd, nf, tm=512, tf=1024):
    t, d = x2.shape
    ff = wu.shape[1]
    return pl.pallas_call(
        _mlp_kernel,
        grid=(t // tm, ff // tf),
        in_specs=[pl.BlockSpec((tm, d), lambda i, f: (i, 0)),
                  pl.BlockSpec((1, d), lambda i, f: (0, 0)),
                  pl.BlockSpec((d, tf), lambda i, f: (0, f)),
                  pl.BlockSpec((tf, d), lambda i, f: (f, 0)),
                  pl.BlockSpec((1, d), lambda i, f: (0, 0))],
        out_specs=pl.BlockSpec((tm, d), lambda i, f: (i, 0)),
        out_shape=jax.ShapeDtypeStruct((t, d), F32),
        scratch_shapes=[pltpu.VMEM((tm, d), BF16), pltpu.VMEM((tm, d), F32)],
        compiler_params=_cparams("parallel", "arbitrary"),
        name="mlp_final_norm",
    )(x2, n2, wu, wd, nf)


def kernel(x, norm1_w, w_in, ssm_a_re, ssm_a_im, ssm_log_dt, ssm_b_re, ssm_b_im, ssm_c_re, ssm_c_im, ssm_d,
           w_glu, b_glu, pe_ck, w_ck1, w_ck2, pe_cv, w_cv1, w_cv2, w_out, norm2_w, w_up, w_down, rel_table,
           norm_f_w):
    bsz, L, d = x.shape
    assert w_in.shape[0] == 1, "the closing rmsnorm is fused into the (single) layer's MLP kernel"
    x2 = x.reshape(bsz * L, d)
    w_in_p = jnp.pad(w_in[0].astype(BF16), ((0, 0), (0, D_IN_PAD - D_IN)))
    proj = _norm_matmul(x2, norm1_w[0].reshape(1, d), w_in_p)
    ops = _s5_operators(ssm_a_re[0], ssm_a_im[0], ssm_log_dt[0], ssm_b_re[0], ssm_b_im[0], ssm_c_re[0], ssm_c_im[0],
                        ssm_d[0])
    y_ssm = _s5_mixer(proj, bsz, L, ops, w_glu[0], b_glu[0])
    y_nsa = _nsa_mixer(proj, bsz, L, pe_ck[0], w_ck1[0], w_ck2[0], pe_cv[0], w_cv1[0], w_cv2[0], rel_table)
    x2 = _out_proj(x2, y_ssm, y_nsa, w_out[0].astype(BF16))
    x2 = _mlp(x2, norm2_w[0].reshape(1, d), w_up[0].astype(BF16), w_down[0].astype(BF16), norm_f_w.reshape(1, d))
    return x2.reshape(bsz, L, d)
```

```python
import functools
import math

import numpy as np
import jax
import jax.numpy as jnp
from jax import lax
from jax.experimental import pallas as pl
from jax.experimental.pallas import tpu as pltpu

F32 = jnp.float32
BF16 = jnp.bfloat16

SSM_WIDTH = 1024
SSM_GROUP = 16
SSM_GROUPS = 64
SSM_STATE = 64
NSA_WIDTH = 1024
HEAD_DIM = 64
NSA_HEADS = 16
KV_HEADS = 4
Q_PER_KV = 4
KV_WIDTH = 256
CMP_BLOCK = 32
CMP_STRIDE = 16
CMP_HIDDEN = 256
SLC_BLOCK = 64
SLC_TOPK = 16
WINDOW = 512
REL_BUCKETS = 32
REL_MAX_DIST = 128
EPS = 1e-6
NEG = -1e30
FORCED_SCORE = 1e4
D_IN = SSM_WIDTH + NSA_WIDTH + 6 * KV_WIDTH + 3 * NSA_HEADS
D_IN_PAD = 3712
COL_Q = SSM_WIDTH
COL_KC = COL_Q + NSA_WIDTH
COL_VC = COL_KC + KV_WIDTH
COL_KS = COL_VC + KV_WIDTH
COL_VS = COL_KS + KV_WIDTH
COL_KW = COL_VS + KV_WIDTH
COL_VW = COL_KW + KV_WIDTH
COL_GATE = COL_VW + KV_WIDTH

LANES = 128
S5_CHUNK = 16
SLAB_GROUPS = LANES // SSM_GROUP
N_SLABS = SSM_GROUPS // SLAB_GROUPS
QT = 128
N_SLC_PAD = 64
RANK_SEG = 16
NSA_GROUPS = 2
FAR_TILES = 2
UNSELECTED = -1e9
LOG2E = math.log2(math.e)
TABLE_W = 1024
VMEM_LIMIT = 56 * 1024 * 1024
HI = lax.Precision.HIGHEST


def _cparams(*sem):
    return pltpu.CompilerParams(dimension_semantics=sem, vmem_limit_bytes=VMEM_LIMIT)


def _rms(x, w):
    ms = jnp.mean(x * x, axis=-1, keepdims=True)
    return x * lax.rsqrt(ms + EPS) * w


def _lane_cat(parts):
    return jnp.concatenate(parts, axis=1)


def _norm_matmul_kernel(x_ref, nw_ref, w_ref, o_ref):
    h = _rms(x_ref[...], nw_ref[...]).astype(BF16)
    o_ref[...] = jnp.dot(h, w_ref[...], preferred_element_type=F32)


def _norm_matmul(x2, nw, w, tm=512):
    t, d = x2.shape
    n = w.shape[1]
    return pl.pallas_call(
        _norm_matmul_kernel,
        grid=(t // tm,),
        in_specs=[pl.BlockSpec((tm, d), lambda i: (i, 0)),
                  pl.BlockSpec((1, d), lambda i: (0, 0)),
                  pl.BlockSpec((d, n), lambda i: (0, 0), pl.Buffered(1))],
        out_specs=pl.BlockSpec((tm, n), lambda i: (i, 0)),
        out_shape=jax.ShapeDtypeStruct((t, n), F32),
        compiler_params=_cparams("parallel"),
        name="norm_in_proj",
    )(x2, nw, w)


def _s5_operators(a_re, a_im, log_dt, b_re, b_im, c_re, c_im, d):
    q = S5_CHUNK
    g, p = a_re.shape
    h = SSM_GROUP
    dt = jnp.exp(log_dt)[:, None]
    lam_re, lam_im = dt * a_re, dt * a_im
    mag1 = jnp.exp(lam_re)
    abar_re, abar_im = mag1 * jnp.cos(lam_im), mag1 * jnp.sin(lam_im)
    den = a_re * a_re + a_im * a_im
    zr, zi = abar_re - 1.0, abar_im
    coef_re = (zr * a_re + zi * a_im) / den
    coef_im = (zi * a_re - zr * a_im) / den
    bb_re = coef_re[..., None] * b_re - coef_im[..., None] * b_im
    bb_im = coef_re[..., None] * b_im + coef_im[..., None] * b_re
    k = jnp.arange(q + 1, dtype=F32)[:, None, None]
    mag = jnp.exp(k * lam_re)
    pw_re, pw_im = mag * jnp.cos(k * lam_im), mag * jnp.sin(k * lam_im)
    m_re = pw_re[..., None] * bb_re - pw_im[..., None] * bb_im
    m_im = pw_re[..., None] * bb_im + pw_im[..., None] * bb_re
    kern = (jnp.einsum('ghp,kgpi->gkhi', c_re, m_re[:q], precision=HI)
            - jnp.einsum('ghp,kgpi->gkhi', c_im, m_im[:q], precision=HI))
    kern = kern.at[:, 0].add(jax.vmap(jnp.diag)(d))
    rev = np.arange(q - 1, -1, -1)
    s_end = jnp.stack([m_re[rev], m_im[rev]], axis=0).transpose(2, 1, 4, 0, 3)
    pr, pi = pw_re[1:], pw_im[1:]
    o_re = c_re[None] * pr[:, :, None, :] - c_im[None] * pi[:, :, None, :]
    o_im = -(c_re[None] * pi[:, :, None, :] + c_im[None] * pr[:, :, None, :])
    o_carry = jnp.stack([o_re, o_im], axis=0).transpose(2, 0, 4, 1, 3)
    a_q = jnp.stack([pw_re[q], pw_im[q]], axis=0)
    return kern, s_end, o_carry, a_q


def _s5_slab_weights(kern, s_end, o_carry, a_q):
    q, h, p = S5_CHUNK, SSM_GROUP, SSM_STATE
    s_dense = s_end.reshape(SSM_GROUPS, q, h, 2 * p).astype(BF16)
    k_dense = kern.transpose(0, 1, 3, 2).astype(BF16)
    o_dense = o_carry.reshape(SSM_GROUPS, 2, p, q * h).astype(BF16)
    a_re = a_q[0].reshape(N_SLABS, SLAB_GROUPS * p)
    a_im = a_q[1].reshape(N_SLABS, SLAB_GROUPS * p)
    return s_dense, k_dense, o_dense, a_re, a_im


def _chunk_rows(u_ref, nc):
    return _lane_cat([u_ref[pl.ds(s, nc, stride=S5_CHUNK), :] for s in range(S5_CHUNK)]).astype(BF16)


def _s5_state_kernel(u_ref, sd_ref, o_ref, ws_scr):
    nc = u_ref.shape[0] // S5_CHUNK
    q, h, p, sg = S5_CHUNK, SSM_GROUP, SSM_STATE, SLAB_GROUPS

    @pl.when(pl.program_id(1) == 0)
    def _():
        ws_scr[...] = jnp.zeros_like(ws_scr)
        for s in range(q):
            for a in range(sg):
                for c in range(2):
                    ws_scr[s * LANES + a * h: s * LANES + (a + 1) * h,
                           (c * sg + a) * p: (c * sg + a + 1) * p] = sd_ref[a, s, :, c * p:(c + 1) * p]

    o_ref[0] = jnp.dot(_chunk_rows(u_ref, nc), ws_scr[...], preferred_element_type=F32)


def _s5_scan_kernel(h_ref, ar_ref, ai_ref, o_ref):
    half = ar_ref.shape[1]
    ar = ar_ref[...]
    ai = ai_ref[...]

    def body(c, carry):
        hr, hi = carry
        o_ref[0, c, :, :half] = hr
        o_ref[0, c, :, half:] = hi
        xr = h_ref[0, c, :, :half]
        xi = h_ref[0, c, :, half:]
        return ar * hr - ai * hi + xr, ar * hi + ai * hr + xi

    zero = jnp.zeros(ar.shape, F32)
    lax.fori_loop(0, h_ref.shape[1], body, (zero, zero))


def _s5_out_kernel(u_ref, kd_ref, h_ref, od_ref, y_ref, kl_scr, wi_scr, wc_scr):
    nc = u_ref.shape[0] // S5_CHUNK
    q, h, p, sg = S5_CHUNK, SSM_GROUP, SSM_STATE, SLAB_GROUPS

    @pl.when(pl.program_id(1) == 0)
    def _():
        kl_scr[...] = jnp.zeros_like(kl_scr)
        for k in range(q):
            for a in range(sg):
                kl_scr[k, a * h:(a + 1) * h, a * h:(a + 1) * h] = kd_ref[a, k]
        for s in range(q):
            for t in range(s, q):
                wi_scr[s * LANES:(s + 1) * LANES, t * LANES:(t + 1) * LANES] = kl_scr[t - s]
        for t in range(0, q, 2):
            wi_scr[(t + 1) * LANES:(t + 2) * LANES, t * LANES:(t + 1) * LANES] = jnp.zeros((LANES, LANES), BF16)
        wc_scr[...] = jnp.zeros_like(wc_scr)
        for a in range(sg):
            for c in range(2):
                for t in range(q):
                    wc_scr[(c * sg + a) * p:(c * sg + a + 1) * p,
                           t * LANES + a * h: t * LANES + (a + 1) * h] = od_ref[a, c, :, t * h:(t + 1) * h]

    u_rows = _chunk_rows(u_ref, nc)
    h_rows = h_ref[0].astype(BF16)
    pair = 2 * LANES
    for tp in range(q // 2):
        cols = slice(tp * pair, (tp + 1) * pair)
        k_used = (tp + 1) * pair
        y = (jnp.dot(u_rows[:, :k_used], wi_scr[:k_used, cols], preferred_element_type=F32)
             + jnp.dot(h_rows, wc_scr[:, cols], preferred_element_type=F32))
        for tt in range(2):
            y_ref[pl.ds(2 * tp + tt, nc, stride=q), :] = y[:, tt * LANES:(tt + 1) * LANES]


def _glu_kernel(y_ref, w_ref, b_ref, o_ref):
    z = jax.nn.gelu(y_ref[...])
    gate = jnp.dot(z.astype(BF16), w_ref[...], preferred_element_type=F32) + b_ref[...]
    o_ref[...] = (z * jax.nn.sigmoid(gate)).astype(o_ref.dtype)


def _s5_mixer(proj, bsz, L, ops, w_glu, b_glu):
    s_dense, k_dense, o_dense, a_re, a_im = _s5_slab_weights(*ops)
    q, h, p = S5_CHUNK, SSM_GROUP, SSM_STATE
    nc = L // q
    ns, sg = N_SLABS, SLAB_GROUPS
    sw = sg * 2 * p
    kq = q * LANES
    hend = pl.pallas_call(
        _s5_state_kernel,
        grid=(ns, bsz),
        in_specs=[pl.BlockSpec((L, LANES), lambda j, b: (b, j)),
                  pl.BlockSpec((sg, q, h, 2 * p), lambda j, b: (j, 0, 0, 0))],
        out_specs=pl.BlockSpec((1, nc, sw), lambda j, b: (b, 0, j)),
        out_shape=jax.ShapeDtypeStruct((bsz, nc, ns * sw), F32),
        scratch_shapes=[pltpu.VMEM((kq, sw), BF16)],
        compiler_params=_cparams("parallel", "arbitrary"),
        name="s5_chunk_state",
    )(proj, s_dense)
    hprev4 = pl.pallas_call(
        _s5_scan_kernel,
        grid=(bsz,),
        in_specs=[pl.BlockSpec((1, nc, ns, sw), lambda b: (b, 0, 0, 0)),
                  pl.BlockSpec((ns, sw // 2), lambda b: (0, 0)),
                  pl.BlockSpec((ns, sw // 2), lambda b: (0, 0))],
        out_specs=pl.BlockSpec((1, nc, ns, sw), lambda b: (b, 0, 0, 0)),
        out_shape=jax.ShapeDtypeStruct((bsz, nc, ns, sw), F32),
        compiler_params=_cparams("parallel"),
        name="s5_chunk_scan",
    )(hend.reshape(bsz, nc, ns, sw), a_re, a_im)
    y = pl.pallas_call(
        _s5_out_kernel,
        grid=(ns, bsz),
        in_specs=[pl.BlockSpec((L, LANES), lambda j, b: (b, j)),
                  pl.BlockSpec((sg, q, h, h), lambda j, b: (j, 0, 0, 0)),
                  pl.BlockSpec((1, nc, sw), lambda j, b: (b, 0, j)),
                  pl.BlockSpec((sg, 2, p, q * h), lambda j, b: (j, 0, 0, 0))],
        out_specs=pl.BlockSpec((L, LANES), lambda j, b: (b, j)),
        out_shape=jax.ShapeDtypeStruct((bsz * L, SSM_WIDTH), F32),
        scratch_shapes=[pltpu.VMEM((q, LANES, LANES), BF16), pltpu.VMEM((kq, kq), BF16), pltpu.VMEM((sw, kq), BF16)],
        compiler_params=_cparams("parallel", "arbitrary"),
        name="s5_chunk_out",
    )(proj, k_dense, hprev4.reshape(bsz, nc, ns * sw), o_dense)
    tm = 512
    return pl.pallas_call(
        _glu_kernel,
        grid=(bsz * L // tm,),
        in_specs=[pl.BlockSpec((tm, SSM_WIDTH), lambda i: (i, 0)),
                  pl.BlockSpec((SSM_WIDTH, SSM_WIDTH), lambda i: (0, 0)),
                  pl.BlockSpec((1, SSM_WIDTH), lambda i: (0, 0))],
        out_specs=pl.BlockSpec((tm, SSM_WIDTH), lambda i: (i, 0)),
        out_shape=jax.ShapeDtypeStruct((bsz * L, SSM_WIDTH), BF16),
        compiler_params=_cparams("parallel"),
        name="s5_gelu_glu",
    )(y, w_glu.astype(BF16), b_glu.reshape(1, SSM_WIDTH))


def _compress_kernel(k_ref, w1_ref, pe_ref, w1full_ref, w2_ref, o_ref, *, transpose_out):
    nc = k_ref.shape[0] // CMP_STRIDE
    prod = jnp.dot(_chunk_rows(k_ref, nc), w1_ref[...], preferred_element_type=F32)
    pe_h = jnp.dot(jnp.broadcast_to(pe_ref[...], (8, pe_ref.shape[1])), w1full_ref[...],
                   precision=HI, preferred_element_type=F32)[0:1]
    two_h = 2 * CMP_HIDDEN
    for gl in range(LANES // HEAD_DIM):
        first = prod[:, gl * two_h: gl * two_h + CMP_HIDDEN]
        second = prod[:, gl * two_h + CMP_HIDDEN: (gl + 1) * two_h]
        hid = first + pltpu.roll(second, nc - 1, axis=0) + pe_h
        out = jnp.dot(jax.nn.gelu(hid).astype(BF16), w2_ref[...], preferred_element_type=F32)
        if transpose_out:
            o_ref[0, gl] = _lane_cat([out, jnp.zeros_like(out)]).T[:HEAD_DIM].astype(BF16)
        else:
            o_ref[0, gl] = out.astype(BF16)


def _compress(proj, col, bsz, L, pe, w1, w2, transpose_out):
    nc = L // CMP_STRIDE
    half = CMP_STRIDE * HEAD_DIM
    gpl = LANES // HEAD_DIM
    w1cat = jnp.concatenate([w1[:half], w1[half:]], axis=1)
    eye = jnp.eye(gpl, dtype=F32)
    w1slab = jnp.einsum('ldh,ag->ladgh', w1cat.reshape(CMP_STRIDE, HEAD_DIM, 2 * CMP_HIDDEN), eye)
    w1slab = w1slab.reshape(CMP_STRIDE * LANES, gpl * 2 * CMP_HIDDEN).astype(BF16)
    out_block = (1, gpl, HEAD_DIM, nc) if transpose_out else (1, gpl, nc, HEAD_DIM)
    out_full = (bsz, KV_HEADS) + out_block[2:]
    return pl.pallas_call(
        functools.partial(_compress_kernel, transpose_out=transpose_out),
        grid=(bsz, KV_HEADS // gpl),
        in_specs=[pl.BlockSpec((L, LANES), lambda b, j: (b, col // LANES + j)),
                  pl.BlockSpec(w1slab.shape, lambda b, j: (0, 0)),
                  pl.BlockSpec((1, 2 * half), lambda b, j: (0, 0)),
                  pl.BlockSpec((2 * half, CMP_HIDDEN), lambda b, j: (0, 0)),
                  pl.BlockSpec((CMP_HIDDEN, HEAD_DIM), lambda b, j: (0, 0))],
        out_specs=pl.BlockSpec(out_block, lambda b, j: (b, j, 0, 0)),
        out_shape=jax.ShapeDtypeStruct(out_full, BF16),
        compiler_params=_cparams("parallel", "parallel"),
        name="nsa_compress",
    )(proj, w1slab, pe.reshape(1, 2 * half), w1, w2.astype(BF16))


KS_COLS = 2 * LANES
KW_COLS = LANES
KS_FLAG = HEAD_DIM + N_SLC_PAD
KW_FLAG = HEAD_DIM
PAD_TILES = 8
V_ROWS = HEAD_DIM + 16
V_ONES = HEAD_DIM


def _kv_tiles_kernel(ks_ref, vs_ref, kw_ref, vw_ref, kso_ref, vso_ref, kwo_ref, vwo_ref):
    tm = ks_ref.shape[0]
    step = pl.program_id(1)
    row0 = step * tm
    kso_ref[:, :, :, :, KS_FLAG:] = jnp.zeros(kso_ref.shape[:4] + (KS_COLS - KS_FLAG,), BF16)
    kwo_ref[:, :, :, :, KW_FLAG:] = jnp.zeros(kwo_ref.shape[:4] + (KW_COLS - KW_FLAG,), BF16)
    is_real = step < pl.num_programs(1) - 1
    tail_rows = lax.broadcasted_iota(jnp.int32, vso_ref.shape[:3] + (V_ROWS - HEAD_DIM, QT), 3)
    v_tail = jnp.where(tail_rows == 0, jnp.where(is_real, 1.0, 0.0), 0.0).astype(BF16)
    vso_ref[:, :, :, HEAD_DIM:, :] = v_tail
    vwo_ref[:, :, :, HEAD_DIM:, :] = v_tail

    @pl.when(step < pl.num_programs(1) - 1)
    def _():
        lane_blk = lax.broadcasted_iota(jnp.int32, (QT, N_SLC_PAD), 1)
        for k in range(tm // QT):
            rows = slice(k * QT, (k + 1) * QT)
            tok = lax.broadcasted_iota(jnp.int32, (QT, N_SLC_PAD), 0) + (row0 + k * QT)
            onehot = jnp.where((tok >> int(math.log2(SLC_BLOCK))) == lane_blk, 1.0, 0.0).astype(BF16)
            vs_t = vs_ref[rows, :].T
            vw_t = vw_ref[rows, :].T
            for g in range(KV_HEADS):
                cols = slice(g * HEAD_DIM, (g + 1) * HEAD_DIM)
                kso_ref[0, g, k, :, :HEAD_DIM] = ks_ref[rows, cols].astype(BF16)
                kso_ref[0, g, k, :, HEAD_DIM:KS_FLAG] = onehot
                kwo_ref[0, g, k, :, :HEAD_DIM] = kw_ref[rows, cols].astype(BF16)
                vso_ref[0, g, k, :HEAD_DIM, :] = vs_t[cols].astype(BF16)
                vwo_ref[0, g, k, :HEAD_DIM, :] = vw_t[cols].astype(BF16)

    @pl.when(step == pl.num_programs(1) - 1)
    def _():
        ones = jnp.ones(kso_ref.shape[:4] + (1,), BF16)
        kso_ref[:, :, :, :, :KS_FLAG] = jnp.zeros(kso_ref.shape[:4] + (KS_FLAG,), BF16)
        kwo_ref[:, :, :, :, :KW_FLAG] = jnp.zeros(kwo_ref.shape[:4] + (KW_FLAG,), BF16)
        kso_ref[:, :, :, :, KS_FLAG:KS_FLAG + 1] = ones
        kwo_ref[:, :, :, :, KW_FLAG:KW_FLAG + 1] = ones
        vso_ref[:, :, :, :HEAD_DIM, :] = jnp.zeros(vso_ref.shape[:3] + (HEAD_DIM, QT), BF16)
        vwo_ref[:, :, :, :HEAD_DIM, :] = jnp.zeros(vwo_ref.shape[:3] + (HEAD_DIM, QT), BF16)


def _kv_tiles(proj, bsz, L):
    nt = L // QT
    g = KV_HEADS
    kt = PAD_TILES
    tm = kt * QT
    steps = L // tm
    in_spec = lambda col: pl.BlockSpec((tm, KV_WIDTH),
                                       lambda b, i: (b * steps + jnp.minimum(i, steps - 1), col // KV_WIDTH))
    out_spec = lambda a, c: pl.BlockSpec((1, g, kt, a, c), lambda b, i: (b, 0, i, 0, 0))
    shape = lambda a, c: jax.ShapeDtypeStruct((bsz, g, nt + kt, a, c), BF16)
    return pl.pallas_call(
        _kv_tiles_kernel,
        grid=(bsz, steps + 1),
        in_specs=[in_spec(COL_KS), in_spec(COL_VS), in_spec(COL_KW), in_spec(COL_VW)],
        out_specs=[out_spec(QT, KS_COLS), out_spec(V_ROWS, QT), out_spec(QT, KW_COLS), out_spec(V_ROWS, QT)],
        out_shape=[shape(QT, KS_COLS), shape(V_ROWS, QT), shape(QT, KW_COLS), shape(V_ROWS, QT)],
        compiler_params=_cparams("parallel", "arbitrary"),
        name="nsa_kv_tiles",
    )(proj, proj, proj, proj)


def _rel_bucket(dist):
    dist = jnp.maximum(dist, 0)
    max_exact = REL_BUCKETS // 2
    large = max_exact + (jnp.log(jnp.maximum(dist, 1).astype(F32) / max_exact)
                         / math.log(REL_MAX_DIST / max_exact) * (REL_BUCKETS - max_exact)).astype(jnp.int32)
    large = jnp.minimum(large, REL_BUCKETS - 1)
    return jnp.where(dist < max_exact, dist, large)


def _toeplitz_rows(base, nrows, step):
    w = base.shape[1]
    x = jnp.broadcast_to(base, (nrows, w))
    k_ix = lax.broadcasted_iota(jnp.int32, (nrows, w), 0)
    bit = 1
    while bit < nrows:
        x = jnp.where((k_ix & bit) != 0, pltpu.roll(x, (bit * step) % w, axis=1), x)
        bit *= 2
    return x


def _bias_tables_kernel(bw_ref, bc_ref, far_ref, wb_ref, nb_ref, strip_ref, *, ncp):
    slab = WINDOW + QT
    y = _toeplitz_rows(bw_ref[0], QT, 1)
    for kh in range(slab // QT):
        tile = y[:, WINDOW - kh * QT: WINDOW - kh * QT + QT]
        wb_ref[0, kh * QT:(kh + 1) * QT, :] = tile
        if kh >= slab // QT - 2:
            row = kh - (slab // QT - 2)
            nb_ref[0, row * QT:(row + 1) * QT, :] = tile - far_ref[0]
    band = 2 * CMP_STRIDE
    z = _toeplitz_rows(bc_ref[0], band, CMP_STRIDE)[:, 2 * QT: 3 * QT]
    strip_ref[0, :ncp - CMP_STRIDE, :] = jnp.broadcast_to(far_ref[0], (ncp - CMP_STRIDE, QT))
    strip_ref[0, ncp - CMP_STRIDE: ncp + CMP_STRIDE, :] = z
    strip_ref[0, ncp + CMP_STRIDE:, :] = jnp.full((ncp - CMP_STRIDE, QT), NEG, F32)


def _nsa_tables(rel_table, L):
    ncp = L // CMP_STRIDE
    slab = WINDOW + QT
    dist = jnp.arange(TABLE_W)
    rel_table = rel_table * LOG2E
    per_dist = rel_table[_rel_bucket(dist)].T
    base_w = jnp.where(dist < WINDOW, per_dist, NEG)
    shift = CMP_BLOCK - 1
    shifted = jnp.concatenate([jnp.full((NSA_HEADS, shift), NEG, F32), per_dist[:, :TABLE_W - shift]], axis=1)
    base_c = jnp.where(dist < TABLE_W // 2, shifted, NEG)
    far = jnp.broadcast_to(rel_table[REL_BUCKETS - 1][:, None, None], (NSA_HEADS, 1, QT))
    wb, nb, strip = pl.pallas_call(
        functools.partial(_bias_tables_kernel, ncp=ncp),
        grid=(NSA_HEADS,),
        in_specs=[pl.BlockSpec((1, 1, TABLE_W), lambda h: (h, 0, 0)),
                  pl.BlockSpec((1, 1, TABLE_W), lambda h: (h, 0, 0)),
                  pl.BlockSpec((1, 1, QT), lambda h: (h, 0, 0))],
        out_specs=[pl.BlockSpec((1, slab, QT), lambda h: (h, 0, 0)),
                   pl.BlockSpec((1, 2 * QT, QT), lambda h: (h, 0, 0)),
                   pl.BlockSpec((1, 2 * ncp, QT), lambda h: (h, 0, 0))],
        out_shape=[jax.ShapeDtypeStruct((NSA_HEADS, slab, QT), F32),
                   jax.ShapeDtypeStruct((NSA_HEADS, 2 * QT, QT), F32),
                   jax.ShapeDtypeStruct((NSA_HEADS, 2 * ncp, QT), F32)],
        compiler_params=_cparams("parallel"),
        name="nsa_bias_tables",
    )(base_w.reshape(NSA_HEADS, 1, TABLE_W), base_c.reshape(NSA_HEADS, 1, TABLE_W), far)
    g, r = KV_HEADS, Q_PER_KV
    return wb.reshape(g, r, slab, QT), nb.reshape(g, r, 2 * QT, QT), strip.reshape(g, r, 2 * ncp, QT)


def _overlap_t(L):
    n_cmp = (L - CMP_BLOCK) // CMP_STRIDE + 1
    n_slc = L // SLC_BLOCK
    cmp_idx = np.arange(n_cmp)[:, None] * CMP_STRIDE + np.arange(CMP_BLOCK)[None, :]
    overlap = ((cmp_idx[:, :, None] // SLC_BLOCK) == np.arange(n_slc)[None, None, :]).sum(1) / CMP_BLOCK
    out = np.zeros((N_SLC_PAD, L // CMP_STRIDE), np.float32)
    out[:n_slc, :n_cmp] = overlap.T
    return jnp.asarray(out, dtype=BF16)


def _split3(x):
    hi = x.astype(BF16)
    r1 = x - hi.astype(F32)
    mid = r1.astype(BF16)
    lo = (r1 - mid.astype(F32)).astype(BF16)
    return hi, mid, lo


def _nsa_kernel(q_ref, gl_ref, kc_ref, vcT_ref, strip_ref, ovT_ref, ks_ref, vsT_ref, kw_ref, vwT_ref, wb_ref,
                nb_ref, o_ref, gate_ref, *, n_slc):
    i = pl.program_id(2)
    R = Q_PER_KV
    RQ = R * QT
    RW = R * HEAD_DIM
    groups = range(kc_ref.shape[1])
    first_group = pl.program_id(1) * len(groups)
    pad_tile = ks_ref.shape[2] - PAD_TILES
    t_pos = lax.broadcasted_iota(jnp.int32, (1, QT), 1) + i * QT

    def flag_rows(n):
        return jnp.where(lax.broadcasted_iota(jnp.int32, (n, QT), 0) == 0, NEG, 0.0).astype(BF16)

    q_heads = []
    for gg in groups:
        qT = (q_ref[:, gg * RW:(gg + 1) * RW].T * (0.125 * LOG2E)).astype(BF16)
        q_heads.append([qT[r * HEAD_DIM:(r + 1) * HEAD_DIM] for r in range(R)])

    slab = WINDOW + QT
    n_win = slab // QT
    win_idx = [jnp.where(i + jj >= n_win - 1, i + jj - (n_win - 1), pad_tile) for jj in range(n_win)]

    def window_scores(gg):
        kw = jnp.concatenate([kw_ref[0, gg, j] for j in win_idx], axis=0)
        qwin = _lane_cat([jnp.concatenate([q_heads[gg][r], flag_rows(KW_COLS - HEAD_DIM)], axis=0)
                          for r in range(R)])
        wb = _lane_cat([wb_ref[gg, r] for r in range(R)])
        return jnp.dot(kw, qwin, preferred_element_type=F32) + wb

    def window_branch(gg, sw):
        m_w = jnp.max(sw, axis=0, keepdims=True)
        e_wb = jnp.exp2(sw - m_w).astype(BF16)
        acc = None
        for jj, j in enumerate(win_idx):
            term = jnp.dot(vwT_ref[0, gg, j], e_wb[jj * QT:(jj + 1) * QT], preferred_element_type=F32)
            acc = term if acc is None else acc + term
        return acc[:HEAD_DIM] * (1.0 / acc[V_ONES:V_ONES + 1])

    sw_all = [window_scores(gg) for gg in groups]

    ncp = kc_ref.shape[2]
    strip_row = pl.multiple_of(ncp - i * (QT // CMP_STRIDE), QT // CMP_STRIDE)
    ov = ovT_ref[...]
    s_ix = lax.broadcasted_iota(jnp.int32, (N_SLC_PAD, QT), 0)
    cur = t_pos >> int(math.log2(SLC_BLOCK))
    forced = (s_ix == 0) | (s_ix == cur) | (s_ix == cur - 1)

    def compressed_scores(gg):
        sc = jnp.dot(kc_ref[0, gg], _lane_cat(q_heads[gg]), preferred_element_type=F32)
        return sc + _lane_cat([strip_ref[gg, r, pl.ds(strip_row, ncp), :] for r in range(R)])

    def compressed_branch(gg, sm):
        m_c = jnp.max(sm, axis=0, keepdims=True)
        e_c = jnp.exp2(sm - m_c)
        l_c = jnp.sum(e_c, axis=0, keepdims=True)
        inv_c = jnp.where(m_c > 0.5 * NEG, 1.0 / l_c, 0.0)
        o_cmp = jnp.dot(vcT_ref[0, gg], e_c.astype(BF16), preferred_element_type=F32) * inv_c
        p_c = e_c * inv_c
        p_sum = p_c[:, 0:QT]
        for r in range(1, R):
            p_sum = p_sum + p_c[:, r * QT:(r + 1) * QT]
        imp = None
        for part in _split3(p_sum):
            term = jnp.dot(ov, part, preferred_element_type=F32)
            imp = term if imp is None else imp + term
        score = jnp.where(forced, FORCED_SCORE, jnp.where(s_ix <= cur, imp, -1.0))
        return o_cmp, jnp.where(s_ix < n_slc, score, -2.0)

    sm_all = [compressed_scores(gg) for gg in groups]
    o_win = [window_branch(gg, sw_all[gg]) for gg in groups]
    cmp_out = [compressed_branch(gg, sm_all[gg]) for gg in groups]
    o_cmp = [c[0] for c in cmp_out]
    scores = [c[1] for c in cmp_out]

    SUB = 8
    n_sub = N_SLC_PAD // SUB
    score_rows = [[score[v * SUB:(v + 1) * SUB] for v in range(n_sub)] for score in scores]
    sub_ix = lax.broadcasted_iota(jnp.int32, (SUB, QT), 0)

    vregs_per_seg = RANK_SEG // SUB
    n_seg = -(-n_slc // RANK_SEG)

    def rank_levels(levels, ranks):
        out = []
        for gg in groups:
            rk = list(ranks[gg])
            for k in levels:
                for seg in range(k + 1):
                    rows = range((k + 1) * vregs_per_seg) if seg == k else range(k * vregs_per_seg,
                                                                                 (k + 1) * vregs_per_seg)
                    for sp in range(seg * RANK_SEG, min((seg + 1) * RANK_SEG, n_slc)):
                        row = scores[gg][sp:sp + 1, :]
                        for v in rows:
                            blk = score_rows[gg][v]
                            if v * SUB > sp:
                                beats = row >= blk
                            elif v * SUB + SUB - 1 <= sp:
                                beats = row > blk
                            else:
                                beats = (row > blk) | ((row == blk) & (sub_ix > sp - v * SUB))
                            rk[v] = rk[v] + jnp.where(beats, 1.0, 0.0)
            out.append(tuple(rk))
        return tuple(out)

    ranks = tuple(tuple(jnp.zeros((SUB, QT), F32) for _ in range(n_sub)) for _ in groups)
    last_block = (i + 1) * (QT // SLC_BLOCK) - 1
    for levels in [(0, 1)] + [(k,) for k in range(2, n_seg)]:
        first = max(levels) * RANK_SEG if max(levels) > 1 else SLC_TOPK
        ranks = lax.cond(last_block >= first, functools.partial(rank_levels, levels), lambda rk: rk, ranks)

    qsel = []
    for gg in groups:
        sel_neg = jnp.where(jnp.concatenate(ranks[gg], axis=0) < float(SLC_TOPK), 0.0, UNSELECTED).astype(BF16)
        qsel.append(_lane_cat([jnp.concatenate([q_heads[gg][r], sel_neg, flag_rows(KS_COLS - KS_FLAG)], axis=0)
                               for r in range(R)]))

    n_far = jnp.maximum(i - 1, 0)

    def far_tiles(j):
        return [jnp.where(j + n < n_far, j + n, pad_tile) for n in range(FAR_TILES)]

    def tile_scores(gg, tiles):
        keys = jnp.concatenate([ks_ref[0, gg, t] for t in tiles], axis=0)
        return jnp.dot(keys, qsel[gg], preferred_element_type=F32)

    def online_step(gg, carry, s, tiles):
        m, acc = carry
        m_new = jnp.maximum(m, jnp.max(s, axis=0, keepdims=True))
        p = jnp.exp2(s - m_new).astype(BF16)
        acc = jnp.exp2(m - m_new) * acc
        for n, t in enumerate(tiles):
            acc = acc + jnp.dot(vsT_ref[0, gg, t], p[n * QT:(n + 1) * QT], preferred_element_type=F32)
        return m_new, acc

    def far_trip(u, carries):
        tiles = far_tiles(FAR_TILES * u)
        s_all = [tile_scores(gg, tiles) for gg in groups]
        return tuple(online_step(gg, carries[gg], s_all[gg], tiles) for gg in groups)

    init = (jnp.full((1, RQ), NEG, F32), jnp.zeros((V_ROWS, RQ), F32))
    carries = lax.fori_loop(0, (n_far + FAR_TILES - 1) // FAR_TILES, far_trip, tuple(init for _ in groups))
    near = [jnp.where(i >= 1, i - 1, pad_tile), i]

    gate_ref[...] = jax.nn.sigmoid(gl_ref[...].T)
    s_near = [tile_scores(gg, near) + _lane_cat([nb_ref[gg, r] for r in range(R)]) for gg in groups]
    for gg in groups:
        _, acc_s = online_step(gg, carries[gg], s_near[gg], near)
        o_slc = acc_s[:HEAD_DIM] * (1.0 / acc_s[V_ONES:V_ONES + 1])
        head0 = (first_group + gg) * R
        gates = [_lane_cat([gate_ref[pl.ds((head0 + r) * 3 + br, 1), :] for r in range(R)]) for br in range(3)]
        total = gates[0] * o_cmp[gg] + gates[1] * o_slc + gates[2] * o_win[gg]
        o_ref[:, gg * RW:(gg + 1) * RW] = jnp.concatenate(
            [total[:, r * QT:(r + 1) * QT] for r in range(R)], axis=0).T.astype(o_ref.dtype)


def _nsa_mixer(proj, bsz, L, pe_ck, w_ck1, w_ck2, pe_cv, w_cv1, w_cv2, rel_table):
    g, r = KV_HEADS, Q_PER_KV
    ng = NSA_GROUPS
    ni = L // QT
    nt = L // QT + PAD_TILES
    ncp = L // CMP_STRIDE
    n_slc = L // SLC_BLOCK
    slab = WINDOW + QT
    k_cmp = _compress(proj, COL_KC, bsz, L, pe_ck, w_ck1, w_ck2, transpose_out=False)
    v_cmp_t = _compress(proj, COL_VC, bsz, L, pe_cv, w_cv1, w_cv2, transpose_out=True)
    ks_t, vs_t, kw_t, vw_t = _kv_tiles(proj, bsz, L)
    wb, nb, strip = _nsa_tables(rel_table, L)
    qw = ng * r * HEAD_DIM
    once = pl.Buffered(1)
    whole = lambda *shape: pl.BlockSpec((1, ng) + shape, lambda b, h, i: (b, h) + (0,) * len(shape), once)
    per_group = lambda *shape: pl.BlockSpec((ng,) + shape, lambda b, h, i: (h,) + (0,) * len(shape), once)
    return pl.pallas_call(
        functools.partial(_nsa_kernel, n_slc=n_slc),
        grid=(bsz, g // ng, ni),
        in_specs=[
            pl.BlockSpec((QT, qw), lambda b, h, i: (b * ni + i, COL_Q // qw + h)),
            pl.BlockSpec((QT, LANES), lambda b, h, i: (b * ni + i, COL_GATE // LANES)),
            whole(ncp, HEAD_DIM),
            whole(HEAD_DIM, ncp),
            per_group(r, 2 * ncp, QT),
            pl.BlockSpec((N_SLC_PAD, ncp), lambda b, h, i: (0, 0), once),
            whole(nt, QT, KS_COLS),
            whole(nt, V_ROWS, QT),
            whole(nt, QT, KW_COLS),
            whole(nt, V_ROWS, QT),
            per_group(r, slab, QT),
            per_group(r, 2 * QT, QT),
        ],
        out_specs=pl.BlockSpec((QT, qw), lambda b, h, i: (b * ni + i, h)),
        out_shape=jax.ShapeDtypeStruct((bsz * L, NSA_WIDTH), BF16),
        scratch_shapes=[pltpu.VMEM((LANES, QT), F32)],
        compiler_params=_cparams("parallel", "parallel", "arbitrary"),
        name="nsa_attention",
    )(proj, proj, k_cmp, v_cmp_t, strip, _overlap_t(L), ks_t, vs_t, kw_t, vw_t, wb, nb)


def _out_proj_kernel(x_ref, ys_ref, yn_ref, w_ref, o_ref):
    half = ys_ref.shape[1]
    acc = jnp.dot(ys_ref[...], w_ref[:half, :], preferred_element_type=F32)
    acc = acc + jnp.dot(yn_ref[...], w_ref[half:, :], preferred_element_type=F32)
    o_ref[...] = x_ref[...] + acc


def _out_proj(x2, y_ssm, y_nsa, w, tm=512):
    t, d = x2.shape
    return pl.pallas_call(
        _out_proj_kernel,
        grid=(t // tm,),
        in_specs=[pl.BlockSpec((tm, d), lambda i: (i, 0)),
                  pl.BlockSpec((tm, SSM_WIDTH), lambda i: (i, 0)),
                  pl.BlockSpec((tm, NSA_WIDTH), lambda i: (i, 0)),
                  pl.BlockSpec((SSM_WIDTH + NSA_WIDTH, d), lambda i: (0, 0), pl.Buffered(1))],
        out_specs=pl.BlockSpec((tm, d), lambda i: (i, 0)),
        out_shape=jax.ShapeDtypeStruct((t, d), F32),
        compiler_params=_cparams("parallel"),
        name="out_proj",
    )(x2, y_ssm, y_nsa, w)


def _mlp_kernel(x_ref, n2_ref, wu_ref, wd_ref, nf_ref, o_ref, h_ref, acc_ref):
    f = pl.program_id(1)

    @pl.when(f == 0)
    def _():
        h_ref[...] = _rms(x_ref[...], n2_ref[...]).astype(BF16)
        acc_ref[...] = jnp.zeros_like(acc_ref)

    a = jnp.dot(h_ref[...], wu_ref[...], preferred_element_type=F32)
    a = jnp.square(jnp.maximum(a, 0.0))
    acc_ref[...] += jnp.dot(a.astype(BF16), wd_ref[...], preferred_element_type=F32)

    @pl.when(f == pl.num_programs(1) - 1)
    def _():
        o_ref[...] = _rms(x_ref[...] + acc_ref[...], nf_ref[...])


def _mlp(x2, n2, wu, wd, nf, tm=512, tf=1024):
    t, d = x2.shape
    ff = wu.shape[1]
    return pl.pallas_call(
        _mlp_kernel,
        grid=(t // tm, ff // tf),
        in_specs=[pl.BlockSpec((tm, d), lambda i, f: (i, 0)),
                  pl.BlockSpec((1, d), lambda i, f: (0, 0)),
                  pl.BlockSpec((d, tf), lambda i, f: (0, f)),
                  pl.BlockSpec((tf, d), lambda i, f: (f, 0)),
                  pl.BlockSpec((1, d), lambda i, f: (0, 0))],
        out_specs=pl.BlockSpec((tm, d), lambda i, f: (i, 0)),
        out_shape=jax.ShapeDtypeStruct((t, d), F32),
        scratch_shapes=[pltpu.VMEM((tm, d), BF16), pltpu.VMEM((tm, d), F32)],
        compiler_params=_cparams("parallel", "arbitrary"),
        name="mlp_final_norm",
    )(x2, n2, wu, wd, nf)


def kernel(x, norm1_w, w_in, ssm_a_re, ssm_a_im, ssm_log_dt, ssm_b_re, ssm_b_im, ssm_c_re, ssm_c_im, ssm_d,
           w_glu, b_glu, pe_ck, w_ck1, w_ck2, pe_cv, w_cv1, w_cv2, w_out, norm2_w, w_up, w_down, rel_table,
           norm_f_w):
    bsz, L, d = x.shape
    assert w_in.shape[0] == 1, "the closing rmsnorm is fused into the (single) layer's MLP kernel"
    x2 = x.reshape(bsz * L, d)
    w_in_p = jnp.pad(w_in[0].astype(BF16), ((0, 0), (0, D_IN_PAD - D_IN)))
    proj = _norm_matmul(x2, norm1_w[0].reshape(1, d), w_in_p)
    ops = _s5_operators(ssm_a_re[0], ssm_a_im[0], ssm_log_dt[0], ssm_b_re[0], ssm_b_im[0], ssm_c_re[0], ssm_c_im[0],
                        ssm_d[0])
    y_ssm = _s5_mixer(proj, bsz, L, ops, w_glu[0], b_glu[0])
    y_nsa = _nsa_mixer(proj, bsz, L, pe_ck[0], w_ck1[0], w_ck2[0], pe_cv[0], w_cv1[0], w_cv2[0], rel_table)
    x2 = _out_proj(x2, y_ssm, y_nsa, w_out[0].astype(BF16))
    x2 = _mlp(x2, norm2_w[0].reshape(1, d), w_up[0].astype(BF16), w_down[0].astype(BF16), norm_f_w.reshape(1, d))
    return x2.reshape(bsz, L, d)
```

```python
import functools
import math

import numpy as np
import jax
import jax.numpy as jnp
from jax import lax
from jax.experimental import pallas as pl
from jax.experimental.pallas import tpu as pltpu

F32 = jnp.float32
BF16 = jnp.bfloat16

SSM_WIDTH = 1024
SSM_GROUP = 16
SSM_GROUPS = 64
SSM_STATE = 64
NSA_WIDTH = 1024
HEAD_DIM = 64
NSA_HEADS = 16
KV_HEADS = 4
Q_PER_KV = 4
KV_WIDTH = 256
CMP_BLOCK = 32
CMP_STRIDE = 16
CMP_HIDDEN = 256
SLC_BLOCK = 64
SLC_TOPK = 16
WINDOW = 512
REL_BUCKETS = 32
REL_MAX_DIST = 128
EPS = 1e-6
NEG = -1e30
FORCED_SCORE = 1e4
D_IN = SSM_WIDTH + NSA_WIDTH + 6 * KV_WIDTH + 3 * NSA_HEADS
D_IN_PAD = 3712
COL_Q = SSM_WIDTH
COL_KC = COL_Q + NSA_WIDTH
COL_VC = COL_KC + KV_WIDTH
COL_KS = COL_VC + KV_WIDTH
COL_VS = COL_KS + KV_WIDTH
COL_KW = COL_VS + KV_WIDTH
COL_VW = COL_KW + KV_WIDTH
COL_GATE = COL_VW + KV_WIDTH

LANES = 128
S5_CHUNK = 16
S5_BATCH = 2
SLAB_GROUPS = LANES // SSM_GROUP
N_SLABS = SSM_GROUPS // SLAB_GROUPS
QT = 128
N_SLC_PAD = 64
RANK_SEG = 16
NSA_GROUPS = 4
FAR_TILES = 2
UNSELECTED = -1e9
LOG2E = math.log2(math.e)
TABLE_W = 1024
VMEM_LIMIT = 56 * 1024 * 1024
HI = lax.Precision.HIGHEST


def _cparams(*sem):
    return pltpu.CompilerParams(dimension_semantics=sem, vmem_limit_bytes=VMEM_LIMIT)


def _rms(x, w):
    ms = jnp.mean(x * x, axis=-1, keepdims=True)
    return x * lax.rsqrt(ms + EPS) * w


def _lane_cat(parts):
    return jnp.concatenate(parts, axis=1)


def _norm_matmul_kernel(x_ref, nw_ref, w_ref, o_ref):
    h = _rms(x_ref[...], nw_ref[...]).astype(BF16)
    o_ref[...] = jnp.dot(h, w_ref[...], preferred_element_type=F32)


def _norm_matmul(x2, nw, w, tm=512):
    t, d = x2.shape
    n = w.shape[1]
    return pl.pallas_call(
        _norm_matmul_kernel,
        grid=(t // tm,),
        in_specs=[pl.BlockSpec((tm, d), lambda i: (i, 0)),
                  pl.BlockSpec((1, d), lambda i: (0, 0)),
                  pl.BlockSpec((d, n), lambda i: (0, 0), pl.Buffered(1))],
        out_specs=pl.BlockSpec((tm, n), lambda i: (i, 0)),
        out_shape=jax.ShapeDtypeStruct((t, n), F32),
        compiler_params=_cparams("parallel"),
        name="norm_in_proj",
    )(x2, nw, w)


def _s5_operators(a_re, a_im, log_dt, b_re, b_im, c_re, c_im, d):
    q = S5_CHUNK
    g, p = a_re.shape
    h = SSM_GROUP
    dt = jnp.exp(log_dt)[:, None]
    lam_re, lam_im = dt * a_re, dt * a_im
    mag1 = jnp.exp(lam_re)
    abar_re, abar_im = mag1 * jnp.cos(lam_im), mag1 * jnp.sin(lam_im)
    den = a_re * a_re + a_im * a_im
    zr, zi = abar_re - 1.0, abar_im
    coef_re = (zr * a_re + zi * a_im) / den
    coef_im = (zi * a_re - zr * a_im) / den
    bb_re = coef_re[..., None] * b_re - coef_im[..., None] * b_im
    bb_im = coef_re[..., None] * b_im + coef_im[..., None] * b_re
    k = jnp.arange(q + 1, dtype=F32)[:, None, None]
    mag = jnp.exp(k * lam_re)
    pw_re, pw_im = mag * jnp.cos(k * lam_im), mag * jnp.sin(k * lam_im)
    m_re = pw_re[..., None] * bb_re - pw_im[..., None] * bb_im
    m_im = pw_re[..., None] * bb_im + pw_im[..., None] * bb_re
    kern = (jnp.einsum('ghp,kgpi->gkhi', c_re, m_re[:q], precision=HI)
            - jnp.einsum('ghp,kgpi->gkhi', c_im, m_im[:q], precision=HI))
    kern = kern.at[:, 0].add(jax.vmap(jnp.diag)(d))
    rev = np.arange(q - 1, -1, -1)
    s_end = jnp.stack([m_re[rev], m_im[rev]], axis=0).transpose(2, 1, 4, 0, 3)
    pr, pi = pw_re[1:], pw_im[1:]
    o_re = c_re[None] * pr[:, :, None, :] - c_im[None] * pi[:, :, None, :]
    o_im = -(c_re[None] * pi[:, :, None, :] + c_im[None] * pr[:, :, None, :])
    o_carry = jnp.stack([o_re, o_im], axis=0).transpose(2, 0, 4, 1, 3)
    a_q = jnp.stack([pw_re[q], pw_im[q]], axis=0)
    return kern, s_end, o_carry, a_q


def _s5_slab_weights(kern, s_end, o_carry, a_q):
    q, h, p = S5_CHUNK, SSM_GROUP, SSM_STATE
    s_dense = s_end.reshape(SSM_GROUPS, q, h, 2 * p).astype(BF16)
    k_dense = kern.transpose(0, 1, 3, 2).astype(BF16)
    o_dense = o_carry.reshape(SSM_GROUPS, 2, p, q * h).astype(BF16)
    a_re = a_q[0].reshape(N_SLABS, SLAB_GROUPS * p)
    a_im = a_q[1].reshape(N_SLABS, SLAB_GROUPS * p)
    return s_dense, k_dense, o_dense, a_re, a_im


def _chunk_rows(u_ref, nc):
    return _lane_cat([u_ref[pl.ds(s, nc, stride=S5_CHUNK), :] for s in range(S5_CHUNK)]).astype(BF16)


def _s5_state_kernel(u_ref, sd_ref, o_ref, ws_scr):
    nc = u_ref.shape[0] // S5_CHUNK
    q, h, p, sg = S5_CHUNK, SSM_GROUP, SSM_STATE, SLAB_GROUPS

    @pl.when(pl.program_id(1) == 0)
    def _():
        ws_scr[...] = jnp.zeros_like(ws_scr)
        for s in range(q):
            for a in range(sg):
                for c in range(2):
                    ws_scr[s * LANES + a * h: s * LANES + (a + 1) * h,
                           (c * sg + a) * p: (c * sg + a + 1) * p] = sd_ref[a, s, :, c * p:(c + 1) * p]

    o_ref[...] = jnp.dot(_chunk_rows(u_ref, nc), ws_scr[...], preferred_element_type=F32).reshape(o_ref.shape)


def _s5_scan_kernel(h_ref, ar_ref, ai_ref, o_ref):
    half = ar_ref.shape[1]
    ar = ar_ref[...]
    ai = ai_ref[...]

    def body(c, carry):
        hr, hi = carry
        o_ref[0, c, :, :half] = hr
        o_ref[0, c, :, half:] = hi
        xr = h_ref[0, c, :, :half]
        xi = h_ref[0, c, :, half:]
        return ar * hr - ai * hi + xr, ar * hi + ai * hr + xi

    zero = jnp.zeros(ar.shape, F32)
    lax.fori_loop(0, h_ref.shape[1], body, (zero, zero))


def _s5_out_kernel(u_ref, kd_ref, h_ref, od_ref, y_ref, kl_scr, wi_scr, wc_scr):
    nc = u_ref.shape[0] // S5_CHUNK
    q, h, p, sg = S5_CHUNK, SSM_GROUP, SSM_STATE, SLAB_GROUPS

    @pl.when(pl.program_id(1) == 0)
    def _():
        kl_scr[...] = jnp.zeros_like(kl_scr)
        for k in range(q):
            for a in range(sg):
                kl_scr[k, a * h:(a + 1) * h, a * h:(a + 1) * h] = kd_ref[a, k]
        for s in range(q):
            for t in range(s, q):
                wi_scr[s * LANES:(s + 1) * LANES, t * LANES:(t + 1) * LANES] = kl_scr[t - s]
        for t in range(0, q, 2):
            wi_scr[(t + 1) * LANES:(t + 2) * LANES, t * LANES:(t + 1) * LANES] = jnp.zeros((LANES, LANES), BF16)
        wc_scr[...] = jnp.zeros_like(wc_scr)
        for a in range(sg):
            for c in range(2):
                for t in range(q):
                    wc_scr[(c * sg + a) * p:(c * sg + a + 1) * p,
                           t * LANES + a * h: t * LANES + (a + 1) * h] = od_ref[a, c, :, t * h:(t + 1) * h]

    u_rows = _chunk_rows(u_ref, nc)
    h_rows = h_ref[...].reshape(nc, h_ref.shape[2]).astype(BF16)
    pair = 2 * LANES
    for tp in range(q // 2):
        cols = slice(tp * pair, (tp + 1) * pair)
        k_used = (tp + 1) * pair
        y = (jnp.dot(u_rows[:, :k_used], wi_scr[:k_used, cols], preferred_element_type=F32)
             + jnp.dot(h_rows, wc_scr[:, cols], preferred_element_type=F32))
        for tt in range(2):
            y_ref[pl.ds(2 * tp + tt, nc, stride=q), :] = y[:, tt * LANES:(tt + 1) * LANES]


def _glu_kernel(y_ref, w_ref, b_ref, o_ref):
    z = jax.nn.gelu(y_ref[...])
    gate = jnp.dot(z.astype(BF16), w_ref[...], preferred_element_type=F32) + b_ref[...]
    o_ref[...] = (z * jax.nn.sigmoid(gate)).astype(o_ref.dtype)


def _s5_mixer(proj, bsz, L, ops, w_glu, b_glu):
    s_dense, k_dense, o_dense, a_re, a_im = _s5_slab_weights(*ops)
    q, h, p = S5_CHUNK, SSM_GROUP, SSM_STATE
    nc = L // q
    ns, sg = N_SLABS, SLAB_GROUPS
    sw = sg * 2 * p
    kq = q * LANES
    nb = S5_BATCH
    hend = pl.pallas_call(
        _s5_state_kernel,
        grid=(ns, bsz // nb),
        in_specs=[pl.BlockSpec((nb * L, LANES), lambda j, b: (b, j)),
                  pl.BlockSpec((sg, q, h, 2 * p), lambda j, b: (j, 0, 0, 0))],
        out_specs=pl.BlockSpec((nb, nc, sw), lambda j, b: (b, 0, j)),
        out_shape=jax.ShapeDtypeStruct((bsz, nc, ns * sw), F32),
        scratch_shapes=[pltpu.VMEM((kq, sw), BF16)],
        compiler_params=_cparams("parallel", "arbitrary"),
        name="s5_chunk_state",
    )(proj, s_dense)
    hprev4 = pl.pallas_call(
        _s5_scan_kernel,
        grid=(bsz,),
        in_specs=[pl.BlockSpec((1, nc, ns, sw), lambda b: (b, 0, 0, 0)),
                  pl.BlockSpec((ns, sw // 2), lambda b: (0, 0)),
                  pl.BlockSpec((ns, sw // 2), lambda b: (0, 0))],
        out_specs=pl.BlockSpec((1, nc, ns, sw), lambda b: (b, 0, 0, 0)),
        out_shape=jax.ShapeDtypeStruct((bsz, nc, ns, sw), F32),
        compiler_params=_cparams("parallel"),
        name="s5_chunk_scan",
    )(hend.reshape(bsz, nc, ns, sw), a_re, a_im)
    y = pl.pallas_call(
        _s5_out_kernel,
        grid=(ns, bsz // nb),
        in_specs=[pl.BlockSpec((nb * L, LANES), lambda j, b: (b, j)),
                  pl.BlockSpec((sg, q, h, h), lambda j, b: (j, 0, 0, 0)),
                  pl.BlockSpec((nb, nc, sw), lambda j, b: (b, 0, j)),
                  pl.BlockSpec((sg, 2, p, q * h), lambda j, b: (j, 0, 0, 0))],
        out_specs=pl.BlockSpec((nb * L, LANES), lambda j, b: (b, j)),
        out_shape=jax.ShapeDtypeStruct((bsz * L, SSM_WIDTH), F32),
        scratch_shapes=[pltpu.VMEM((q, LANES, LANES), BF16), pltpu.VMEM((kq, kq), BF16), pltpu.VMEM((sw, kq), BF16)],
        compiler_params=_cparams("parallel", "arbitrary"),
        name="s5_chunk_out",
    )(proj, k_dense, hprev4.reshape(bsz, nc, ns * sw), o_dense)
    tm = 512
    return pl.pallas_call(
        _glu_kernel,
        grid=(bsz * L // tm,),
        in_specs=[pl.BlockSpec((tm, SSM_WIDTH), lambda i: (i, 0)),
                  pl.BlockSpec((SSM_WIDTH, SSM_WIDTH), lambda i: (0, 0)),
                  pl.BlockSpec((1, SSM_WIDTH), lambda i: (0, 0))],
        out_specs=pl.BlockSpec((tm, SSM_WIDTH), lambda i: (i, 0)),
        out_shape=jax.ShapeDtypeStruct((bsz * L, SSM_WIDTH), BF16),
        compiler_params=_cparams("parallel"),
        name="s5_gelu_glu",
    )(y, w_glu.astype(BF16), b_glu.reshape(1, SSM_WIDTH))


def _compress_kernel(k_ref, w1_ref, pe_ref, w1full_ref, w2_ref, o_ref, *, transpose_out):
    nc = k_ref.shape[0] // CMP_STRIDE
    prod = jnp.dot(_chunk_rows(k_ref, nc), w1_ref[...], preferred_element_type=F32)
    pe_h = jnp.dot(jnp.broadcast_to(pe_ref[...], (8, pe_ref.shape[1])), w1full_ref[...],
                   precision=HI, preferred_element_type=F32)[0:1]
    two_h = 2 * CMP_HIDDEN
    for gl in range(LANES // HEAD_DIM):
        first = prod[:, gl * two_h: gl * two_h + CMP_HIDDEN]
        second = prod[:, gl * two_h + CMP_HIDDEN: (gl + 1) * two_h]
        hid = first + pltpu.roll(second, nc - 1, axis=0) + pe_h
        out = jnp.dot(jax.nn.gelu(hid).astype(BF16), w2_ref[...], preferred_element_type=F32)
        if transpose_out:
            o_ref[0, gl] = _lane_cat([out, jnp.zeros_like(out)]).T[:HEAD_DIM].astype(BF16)
        else:
            o_ref[0, gl] = out.astype(BF16)


def _compress(proj, col, bsz, L, pe, w1, w2, transpose_out):
    nc = L // CMP_STRIDE
    half = CMP_STRIDE * HEAD_DIM
    gpl = LANES // HEAD_DIM
    w1cat = jnp.concatenate([w1[:half], w1[half:]], axis=1)
    eye = jnp.eye(gpl, dtype=F32)
    w1slab = jnp.einsum('ldh,ag->ladgh', w1cat.reshape(CMP_STRIDE, HEAD_DIM, 2 * CMP_HIDDEN), eye)
    w1slab = w1slab.reshape(CMP_STRIDE * LANES, gpl * 2 * CMP_HIDDEN).astype(BF16)
    out_block = (1, gpl, HEAD_DIM, nc) if transpose_out else (1, gpl, nc, HEAD_DIM)
    out_full = (bsz, KV_HEADS) + out_block[2:]
    return pl.pallas_call(
        functools.partial(_compress_kernel, transpose_out=transpose_out),
        grid=(bsz, KV_HEADS // gpl),
        in_specs=[pl.BlockSpec((L, LANES), lambda b, j: (b, col // LANES + j)),
                  pl.BlockSpec(w1slab.shape, lambda b, j: (0, 0)),
                  pl.BlockSpec((1, 2 * half), lambda b, j: (0, 0)),
                  pl.BlockSpec((2 * half, CMP_HIDDEN), lambda b, j: (0, 0)),
                  pl.BlockSpec((CMP_HIDDEN, HEAD_DIM), lambda b, j: (0, 0))],
        out_specs=pl.BlockSpec(out_block, lambda b, j: (b, j, 0, 0)),
        out_shape=jax.ShapeDtypeStruct(out_full, BF16),
        compiler_params=_cparams("parallel", "parallel"),
        name="nsa_compress",
    )(proj, w1slab, pe.reshape(1, 2 * half), w1, w2.astype(BF16))


KS_COLS = 2 * LANES
KW_COLS = LANES
KS_FLAG = HEAD_DIM + N_SLC_PAD
KW_FLAG = HEAD_DIM
PAD_TILES = 8
V_ROWS = HEAD_DIM + 16
V_ONES = HEAD_DIM


def _kv_tiles_kernel(ks_ref, vs_ref, kw_ref, vw_ref, kso_ref, vso_ref, kwo_ref, vwo_ref):
    tm = ks_ref.shape[0]
    step = pl.program_id(1)
    row0 = step * tm
    kso_ref[:, :, :, :, KS_FLAG:] = jnp.zeros(kso_ref.shape[:4] + (KS_COLS - KS_FLAG,), BF16)
    kwo_ref[:, :, :, :, KW_FLAG:] = jnp.zeros(kwo_ref.shape[:4] + (KW_COLS - KW_FLAG,), BF16)
    is_real = step < pl.num_programs(1) - 1
    tail_rows = lax.broadcasted_iota(jnp.int32, vso_ref.shape[:3] + (V_ROWS - HEAD_DIM, QT), 3)
    v_tail = jnp.where(tail_rows == 0, jnp.where(is_real, 1.0, 0.0), 0.0).astype(BF16)
    vso_ref[:, :, :, HEAD_DIM:, :] = v_tail
    vwo_ref[:, :, :, HEAD_DIM:, :] = v_tail

    @pl.when(step < pl.num_programs(1) - 1)
    def _():
        lane_blk = lax.broadcasted_iota(jnp.int32, (QT, N_SLC_PAD), 1)
        for k in range(tm // QT):
            rows = slice(k * QT, (k + 1) * QT)
            tok = lax.broadcasted_iota(jnp.int32, (QT, N_SLC_PAD), 0) + (row0 + k * QT)
            onehot = jnp.where((tok >> int(math.log2(SLC_BLOCK))) == lane_blk, 1.0, 0.0).astype(BF16)
            vs_t = vs_ref[rows, :].T
            vw_t = vw_ref[rows, :].T
            for g in range(KV_HEADS):
                cols = slice(g * HEAD_DIM, (g + 1) * HEAD_DIM)
                kso_ref[0, g, k, :, :HEAD_DIM] = ks_ref[rows, cols].astype(BF16)
                kso_ref[0, g, k, :, HEAD_DIM:KS_FLAG] = onehot
                kwo_ref[0, g, k, :, :HEAD_DIM] = kw_ref[rows, cols].astype(BF16)
                vso_ref[0, g, k, :HEAD_DIM, :] = vs_t[cols].astype(BF16)
                vwo_ref[0, g, k, :HEAD_DIM, :] = vw_t[cols].astype(BF16)

    @pl.when(step == pl.num_programs(1) - 1)
    def _():
        ones = jnp.ones(kso_ref.shape[:4] + (1,), BF16)
        kso_ref[:, :, :, :, :KS_FLAG] = jnp.zeros(kso_ref.shape[:4] + (KS_FLAG,), BF16)
        kwo_ref[:, :, :, :, :KW_FLAG] = jnp.zeros(kwo_ref.shape[:4] + (KW_FLAG,), BF16)
        kso_ref[:, :, :, :, KS_FLAG:KS_FLAG + 1] = ones
        kwo_ref[:, :, :, :, KW_FLAG:KW_FLAG + 1] = ones
        vso_ref[:, :, :, :HEAD_DIM, :] = jnp.zeros(vso_ref.shape[:3] + (HEAD_DIM, QT), BF16)
        vwo_ref[:, :, :, :HEAD_DIM, :] = jnp.zeros(vwo_ref.shape[:3] + (HEAD_DIM, QT), BF16)


def _kv_tiles(proj, bsz, L):
    nt = L // QT
    g = KV_HEADS
    kt = PAD_TILES
    tm = kt * QT
    steps = L // tm
    in_spec = lambda col: pl.BlockSpec((tm, KV_WIDTH),
                                       lambda b, i: (b * steps + jnp.minimum(i, steps - 1), col // KV_WIDTH))
    out_spec = lambda a, c: pl.BlockSpec((1, g, kt, a, c), lambda b, i: (b, 0, i, 0, 0))
    shape = lambda a, c: jax.ShapeDtypeStruct((bsz, g, nt + kt, a, c), BF16)
    return pl.pallas_call(
        _kv_tiles_kernel,
        grid=(bsz, steps + 1),
        in_specs=[in_spec(COL_KS), in_spec(COL_VS), in_spec(COL_KW), in_spec(COL_VW)],
        out_specs=[out_spec(QT, KS_COLS), out_spec(V_ROWS, QT), out_spec(QT, KW_COLS), out_spec(V_ROWS, QT)],
        out_shape=[shape(QT, KS_COLS), shape(V_ROWS, QT), shape(QT, KW_COLS), shape(V_ROWS, QT)],
        compiler_params=_cparams("parallel", "arbitrary"),
        name="nsa_kv_tiles",
    )(proj, proj, proj, proj)


def _rel_bucket(dist):
    dist = jnp.maximum(dist, 0)
    max_exact = REL_BUCKETS // 2
    large = max_exact + (jnp.log(jnp.maximum(dist, 1).astype(F32) / max_exact)
                         / math.log(REL_MAX_DIST / max_exact) * (REL_BUCKETS - max_exact)).astype(jnp.int32)
    large = jnp.minimum(large, REL_BUCKETS - 1)
    return jnp.where(dist < max_exact, dist, large)


def _toeplitz_rows(base, nrows, step):
    w = base.shape[1]
    x = jnp.broadcast_to(base, (nrows, w))
    k_ix = lax.broadcasted_iota(jnp.int32, (nrows, w), 0)
    bit = 1
    while bit < nrows:
        x = jnp.where((k_ix & bit) != 0, pltpu.roll(x, (bit * step) % w, axis=1), x)
        bit *= 2
    return x


def _bias_tables_kernel(bw_ref, bc_ref, far_ref, wb_ref, nb_ref, strip_ref, *, ncp):
    slab = WINDOW + QT
    y = _toeplitz_rows(bw_ref[0], QT, 1)
    for kh in range(slab // QT):
        tile = y[:, WINDOW - kh * QT: WINDOW - kh * QT + QT]
        wb_ref[0, kh * QT:(kh + 1) * QT, :] = tile
        if kh >= slab // QT - 2:
            row = kh - (slab // QT - 2)
            nb_ref[0, row * QT:(row + 1) * QT, :] = tile - far_ref[0]
    band = 2 * CMP_STRIDE
    z = _toeplitz_rows(bc_ref[0], band, CMP_STRIDE)[:, 2 * QT: 3 * QT]
    strip_ref[0, :ncp - CMP_STRIDE, :] = jnp.broadcast_to(far_ref[0], (ncp - CMP_STRIDE, QT))
    strip_ref[0, ncp - CMP_STRIDE: ncp + CMP_STRIDE, :] = z
    strip_ref[0, ncp + CMP_STRIDE:, :] = jnp.full((ncp - CMP_STRIDE, QT), NEG, F32)


def _nsa_tables(rel_table, L):
    ncp = L // CMP_STRIDE
    slab = WINDOW + QT
    dist = jnp.arange(TABLE_W)
    rel_table = rel_table * LOG2E
    per_dist = rel_table[_rel_bucket(dist)].T
    base_w = jnp.where(dist < WINDOW, per_dist, NEG)
    shift = CMP_BLOCK - 1
    shifted = jnp.concatenate([jnp.full((NSA_HEADS, shift), NEG, F32), per_dist[:, :TABLE_W - shift]], axis=1)
    base_c = jnp.where(dist < TABLE_W // 2, shifted, NEG)
    far = jnp.broadcast_to(rel_table[REL_BUCKETS - 1][:, None, None], (NSA_HEADS, 1, QT))
    wb, nb, strip = pl.pallas_call(
        functools.partial(_bias_tables_kernel, ncp=ncp),
        grid=(NSA_HEADS,),
        in_specs=[pl.BlockSpec((1, 1, TABLE_W), lambda h: (h, 0, 0)),
                  pl.BlockSpec((1, 1, TABLE_W), lambda h: (h, 0, 0)),
                  pl.BlockSpec((1, 1, QT), lambda h: (h, 0, 0))],
        out_specs=[pl.BlockSpec((1, slab, QT), lambda h: (h, 0, 0)),
                   pl.BlockSpec((1, 2 * QT, QT), lambda h: (h, 0, 0)),
                   pl.BlockSpec((1, 2 * ncp, QT), lambda h: (h, 0, 0))],
        out_shape=[jax.ShapeDtypeStruct((NSA_HEADS, slab, QT), F32),
                   jax.ShapeDtypeStruct((NSA_HEADS, 2 * QT, QT), F32),
                   jax.ShapeDtypeStruct((NSA_HEADS, 2 * ncp, QT), F32)],
        compiler_params=_cparams("parallel"),
        name="nsa_bias_tables",
    )(base_w.reshape(NSA_HEADS, 1, TABLE_W), base_c.reshape(NSA_HEADS, 1, TABLE_W), far)
    g, r = KV_HEADS, Q_PER_KV
    return wb.reshape(g, r, slab, QT), nb.reshape(g, r, 2 * QT, QT), strip.reshape(g, r, 2 * ncp, QT)


def _overlap_t(L):
    n_cmp = (L - CMP_BLOCK) // CMP_STRIDE + 1
    n_slc = L // SLC_BLOCK
    cmp_idx = np.arange(n_cmp)[:, None] * CMP_STRIDE + np.arange(CMP_BLOCK)[None, :]
    overlap = ((cmp_idx[:, :, None] // SLC_BLOCK) == np.arange(n_slc)[None, None, :]).sum(1) / CMP_BLOCK
    out = np.zeros((N_SLC_PAD, L // CMP_STRIDE), np.float32)
    out[:n_slc, :n_cmp] = overlap.T
    return jnp.asarray(out, dtype=BF16)


def _split3(x):
    hi = x.astype(BF16)
    r1 = x - hi.astype(F32)
    mid = r1.astype(BF16)
    lo = (r1 - mid.astype(F32)).astype(BF16)
    return hi, mid, lo


def _nsa_kernel(q_ref, gl_ref, kc_ref, vcT_ref, strip_ref, ovT_ref, ks_ref, vsT_ref, kw_ref, vwT_ref, wb_ref,
                nb_ref, o_ref, gate_ref, *, n_slc):
    i = pl.program_id(2)
    R = Q_PER_KV
    RQ = R * QT
    RW = R * HEAD_DIM
    groups = range(kc_ref.shape[1])
    first_group = pl.program_id(1) * len(groups)
    pad_tile = ks_ref.shape[2] - PAD_TILES
    t_pos = lax.broadcasted_iota(jnp.int32, (1, QT), 1) + i * QT

    def flag_rows(n):
        return jnp.where(lax.broadcasted_iota(jnp.int32, (n, QT), 0) == 0, NEG, 0.0).astype(BF16)

    q_heads = []
    for gg in groups:
        qT = (q_ref[:, gg * RW:(gg + 1) * RW].T * (0.125 * LOG2E)).astype(BF16)
        q_heads.append([qT[r * HEAD_DIM:(r + 1) * HEAD_DIM] for r in range(R)])

    slab = WINDOW + QT
    n_win = slab // QT
    win_idx = [jnp.where(i + jj >= n_win - 1, i + jj - (n_win - 1), pad_tile) for jj in range(n_win)]

    def window_scores(gg):
        kw = jnp.concatenate([kw_ref[0, gg, j] for j in win_idx], axis=0)
        qwin = _lane_cat([jnp.concatenate([q_heads[gg][r], flag_rows(KW_COLS - HEAD_DIM)], axis=0)
                          for r in range(R)])
        wb = _lane_cat([wb_ref[gg, r] for r in range(R)])
        return jnp.dot(kw, qwin, preferred_element_type=F32) + wb

    def window_branch(gg, sw):
        m_w = jnp.max(sw, axis=0, keepdims=True)
        e_wb = jnp.exp2(sw - m_w).astype(BF16)
        acc = None
        for jj, j in enumerate(win_idx):
            term = jnp.dot(vwT_ref[0, gg, j], e_wb[jj * QT:(jj + 1) * QT], preferred_element_type=F32)
            acc = term if acc is None else acc + term
        return acc[:HEAD_DIM] * (1.0 / acc[V_ONES:V_ONES + 1])

    sw_all = [window_scores(gg) for gg in groups]

    ncp = kc_ref.shape[2]
    strip_row = pl.multiple_of(ncp - i * (QT // CMP_STRIDE), QT // CMP_STRIDE)
    ov = ovT_ref[...]
    s_ix = lax.broadcasted_iota(jnp.int32, (N_SLC_PAD, QT), 0)
    cur = t_pos >> int(math.log2(SLC_BLOCK))
    forced = (s_ix == 0) | (s_ix == cur) | (s_ix == cur - 1)

    def compressed_scores(gg):
        sc = jnp.dot(kc_ref[0, gg], _lane_cat(q_heads[gg]), preferred_element_type=F32)
        return sc + _lane_cat([strip_ref[gg, r, pl.ds(strip_row, ncp), :] for r in range(R)])

    def compressed_branch(gg, sm):
        m_c = jnp.max(sm, axis=0, keepdims=True)
        e_c = jnp.exp2(sm - m_c)
        l_c = jnp.sum(e_c, axis=0, keepdims=True)
        inv_c = jnp.where(m_c > 0.5 * NEG, 1.0 / l_c, 0.0)
        o_cmp = jnp.dot(vcT_ref[0, gg], e_c.astype(BF16), preferred_element_type=F32) * inv_c
        p_c = e_c * inv_c
        p_sum = p_c[:, 0:QT]
        for r in range(1, R):
            p_sum = p_sum + p_c[:, r * QT:(r + 1) * QT]
        imp = None
        for part in _split3(p_sum):
            term = jnp.dot(ov, part, preferred_element_type=F32)
            imp = term if imp is None else imp + term
        score = jnp.where(forced, FORCED_SCORE, jnp.where(s_ix <= cur, imp, -1.0))
        return o_cmp, jnp.where(s_ix < n_slc, score, -2.0)

    sm_all = [compressed_scores(gg) for gg in groups]
    o_win = [window_branch(gg, sw_all[gg]) for gg in groups]
    cmp_out = [compressed_branch(gg, sm_all[gg]) for gg in groups]
    o_cmp = [c[0] for c in cmp_out]
    scores = [c[1] for c in cmp_out]

    SUB = 8
    n_sub = N_SLC_PAD // SUB
    score_rows = [[score[v * SUB:(v + 1) * SUB] for v in range(n_sub)] for score in scores]
    sub_ix = lax.broadcasted_iota(jnp.int32, (SUB, QT), 0)

    vregs_per_seg = RANK_SEG // SUB
    n_seg = -(-n_slc // RANK_SEG)

    def rank_levels(levels, ranks):
        out = []
        for gg in groups:
            rk = list(ranks[gg])
            for k in levels:
                for seg in range(k + 1):
                    rows = range((k + 1) * vregs_per_seg) if seg == k else range(k * vregs_per_seg,
                                                                                 (k + 1) * vregs_per_seg)
                    for sp in range(seg * RANK_SEG, min((seg + 1) * RANK_SEG, n_slc)):
                        row = scores[gg][sp:sp + 1, :]
                        for v in rows:
                            blk = score_rows[gg][v]
                            if v * SUB > sp:
                                beats = row >= blk
                            elif v * SUB + SUB - 1 <= sp:
                                beats = row > blk
                            else:
                                beats = (row > blk) | ((row == blk) & (sub_ix > sp - v * SUB))
                            rk[v] = rk[v] + jnp.where(beats, 1.0, 0.0)
            out.append(tuple(rk))
        return tuple(out)

    ranks = tuple(tuple(jnp.zeros((SUB, QT), F32) for _ in range(n_sub)) for _ in groups)
    last_block = (i + 1) * (QT // SLC_BLOCK) - 1
    for levels in [(0, 1)] + [(k,) for k in range(2, n_seg)]:
        first = max(levels) * RANK_SEG if max(levels) > 1 else SLC_TOPK
        ranks = lax.cond(last_block >= first, functools.partial(rank_levels, levels), lambda rk: rk, ranks)

    qsel = []
    for gg in groups:
        sel_neg = jnp.where(jnp.concatenate(ranks[gg], axis=0) < float(SLC_TOPK), 0.0, UNSELECTED).astype(BF16)
        qsel.append(_lane_cat([jnp.concatenate([q_heads[gg][r], sel_neg, flag_rows(KS_COLS - KS_FLAG)], axis=0)
                               for r in range(R)]))

    n_far = jnp.maximum(i - 1, 0)

    def far_tiles(j):
        return [jnp.where(j + n < n_far, j + n, pad_tile) for n in range(FAR_TILES)]

    def tile_scores(gg, tiles):
        keys = jnp.concatenate([ks_ref[0, gg, t] for t in tiles], axis=0)
        return jnp.dot(keys, qsel[gg], preferred_element_type=F32)

    def online_step(gg, carry, s, tiles):
        m, acc = carry
        m_new = jnp.maximum(m, jnp.max(s, axis=0, keepdims=True))
        p = jnp.exp2(s - m_new).astype(BF16)
        acc = jnp.exp2(m - m_new) * acc
        for n, t in enumerate(tiles):
            acc = acc + jnp.dot(vsT_ref[0, gg, t], p[n * QT:(n + 1) * QT], preferred_element_type=F32)
        return m_new, acc

    def far_trip(u, carries):
        tiles = far_tiles(FAR_TILES * u)
        s_all = [tile_scores(gg, tiles) for gg in groups]
        return tuple(online_step(gg, carries[gg], s_all[gg], tiles) for gg in groups)

    init = (jnp.full((1, RQ), NEG, F32), jnp.zeros((V_ROWS, RQ), F32))
    carries = lax.fori_loop(0, (n_far + FAR_TILES - 1) // FAR_TILES, far_trip, tuple(init for _ in groups))
    near = [jnp.where(i >= 1, i - 1, pad_tile), i]

    gate_ref[...] = jax.nn.sigmoid(gl_ref[...].T)
    s_near = [tile_scores(gg, near) + _lane_cat([nb_ref[gg, r] for r in range(R)]) for gg in groups]
    for gg in groups:
        _, acc_s = online_step(gg, carries[gg], s_near[gg], near)
        o_slc = acc_s[:HEAD_DIM] * (1.0 / acc_s[V_ONES:V_ONES + 1])
        head0 = (first_group + gg) * R
        gates = [_lane_cat([gate_ref[pl.ds((head0 + r) * 3 + br, 1), :] for r in range(R)]) for br in range(3)]
        total = gates[0] * o_cmp[gg] + gates[1] * o_slc + gates[2] * o_win[gg]
        o_ref[:, gg * RW:(gg + 1) * RW] = jnp.concatenate(
            [total[:, r * QT:(r + 1) * QT] for r in range(R)], axis=0).T.astype(o_ref.dtype)


def _nsa_mixer(proj, bsz, L, pe_ck, w_ck1, w_ck2, pe_cv, w_cv1, w_cv2, rel_table):
    g, r = KV_HEADS, Q_PER_KV
    ng = NSA_GROUPS
    ni = L // QT
    nt = L // QT + PAD_TILES
    ncp = L // CMP_STRIDE
    n_slc = L // SLC_BLOCK
    slab = WINDOW + QT
    k_cmp = _compress(proj, COL_KC, bsz, L, pe_ck, w_ck1, w_ck2, transpose_out=False)
    v_cmp_t = _compress(proj, COL_VC, bsz, L, pe_cv, w_cv1, w_cv2, transpose_out=True)
    ks_t, vs_t, kw_t, vw_t = _kv_tiles(proj, bsz, L)
    wb, nb, strip = _nsa_tables(rel_table, L)
    qw = ng * r * HEAD_DIM
    once = pl.Buffered(1)
    whole = lambda *shape: pl.BlockSpec((1, ng) + shape, lambda b, h, i: (b, h) + (0,) * len(shape), once)
    per_group = lambda *shape: pl.BlockSpec((ng,) + shape, lambda b, h, i: (h,) + (0,) * len(shape), once)
    return pl.pallas_call(
        functools.partial(_nsa_kernel, n_slc=n_slc),
        grid=(bsz, g // ng, ni),
        in_specs=[
            pl.BlockSpec((QT, qw), lambda b, h, i: (b * ni + i, COL_Q // qw + h)),
            pl.BlockSpec((QT, LANES), lambda b, h, i: (b * ni + i, COL_GATE // LANES)),
            whole(ncp, HEAD_DIM),
            whole(HEAD_DIM, ncp),
            per_group(r, 2 * ncp, QT),
            pl.BlockSpec((N_SLC_PAD, ncp), lambda b, h, i: (0, 0), once),
            whole(nt, QT, KS_COLS),
            whole(nt, V_ROWS, QT),
            whole(nt, QT, KW_COLS),
            whole(nt, V_ROWS, QT),
            per_group(r, slab, QT),
            per_group(r, 2 * QT, QT),
        ],
        out_specs=pl.BlockSpec((QT, qw), lambda b, h, i: (b * ni + i, h)),
        out_shape=jax.ShapeDtypeStruct((bsz * L, NSA_WIDTH), BF16),
        scratch_shapes=[pltpu.VMEM((LANES, QT), F32)],
        compiler_params=_cparams("parallel", "parallel", "arbitrary"),
        name="nsa_attention",
    )(proj, proj, k_cmp, v_cmp_t, strip, _overlap_t(L), ks_t, vs_t, kw_t, vw_t, wb, nb)


def _out_proj_kernel(x_ref, ys_ref, yn_ref, w_ref, o_ref):
    half = ys_ref.shape[1]
    acc = jnp.dot(ys_ref[...], w_ref[:half, :], preferred_element_type=F32)
    acc = acc + jnp.dot(yn_ref[...], w_ref[half:, :], preferred_element_type=F32)
    o_ref[...] = x_ref[...] + acc


def _out_proj(x2, y_ssm, y_nsa, w, tm=512):
    t, d = x2.shape
    return pl.pallas_call(
        _out_proj_kernel,
        grid=(t // tm,),
        in_specs=[pl.BlockSpec((tm, d), lambda i: (i, 0)),
                  pl.BlockSpec((tm, SSM_WIDTH), lambda i: (i, 0)),
                  pl.BlockSpec((tm, NSA_WIDTH), lambda i: (i, 0)),
                  pl.BlockSpec((SSM_WIDTH + NSA_WIDTH, d), lambda i: (0, 0), pl.Buffered(1))],
        out_specs=pl.BlockSpec((tm, d), lambda i: (i, 0)),
        out_shape=jax.ShapeDtypeStruct((t, d), F32),
        compiler_params=_cparams("parallel"),
        name="out_proj",
    )(x2, y_ssm, y_nsa, w)


def _mlp_kernel(x_ref, n2_ref, wu_ref, wd_ref, nf_ref, o_ref, h_ref, acc_ref):
    f = pl.program_id(1)

    @pl.when(f == 0)
    def _():
        h_ref[...] = _rms(x_ref[...], n2_ref[...]).astype(BF16)
        acc_ref[...] = jnp.zeros_like(acc_ref)

    a = jnp.dot(h_ref[...], wu_ref[...], preferred_element_type=F32)
    a = jnp.square(jnp.maximum(a, 0.0))
    acc_ref[...] += jnp.dot(a.astype(BF16), wd_ref[...], preferred_element_type=F32)

    @pl.when(f == pl.num_programs(1) - 1)
    def _():
        o_ref[...] = _rms(x_ref[...] + acc_ref[...], nf_ref[...])


def _mlp(x2, n2, wu, wd, nf, tm=512, tf=1024):
    t, d = x2.shape
    ff = wu.shape[1]
    return pl.pallas_call(
        _mlp_kernel,
        grid=(t // tm, ff // tf),
        in_specs=[pl.BlockSpec((tm, d), lambda i, f: (i, 0)),
                  pl.BlockSpec((1, d), lambda i, f: (0, 0)),
                  pl.BlockSpec((d, tf), lambda i, f: (0, f)),
                  pl.BlockSpec((tf, d), lambda i, f: (f, 0)),
                  pl.BlockSpec((1, d), lambda i, f: (0, 0))],
        out_specs=pl.BlockSpec((tm, d), lambda i, f: (i, 0)),
        out_shape=jax.ShapeDtypeStruct((t, d), F32),
        scratch_shapes=[pltpu.VMEM((tm, d), BF16), pltpu.VMEM((tm, d), F32)],
        compiler_params=_cparams("parallel", "arbitrary"),
        name="mlp_final_norm",
    )(x2, n2, wu, wd, nf)


def kernel(x, norm1_w, w_in, ssm_a_re, ssm_a_im, ssm_log_dt, ssm_b_re, ssm_b_im, ssm_c_re, ssm_c_im, ssm_d,
           w_glu, b_glu, pe_ck, w_ck1, w_ck2, pe_cv, w_cv1, w_cv2, w_out, norm2_w, w_up, w_down, rel_table,
           norm_f_w):
    bsz, L, d = x.shape
    assert w_in.shape[0] == 1, "the closing rmsnorm is fused into the (single) layer's MLP kernel"
    x2 = x.reshape(bsz * L, d)
    w_in_p = jnp.pad(w_in[0].astype(BF16), ((0, 0), (0, D_IN_PAD - D_IN)))
    proj = _norm_matmul(x2, norm1_w[0].reshape(1, d), w_in_p)
    ops = _s5_operators(ssm_a_re[0], ssm_a_im[0], ssm_log_dt[0], ssm_b_re[0], ssm_b_im[0], ssm_c_re[0], ssm_c_im[0],
                        ssm_d[0])
    y_ssm = _s5_mixer(proj, bsz, L, ops, w_glu[0], b_glu[0])
    y_nsa = _nsa_mixer(proj, bsz, L, pe_ck[0], w_ck1[0], w_ck2[0], pe_cv[0], w_cv1[0], w_cv2[0], rel_table)
    x2 = _out_proj(x2, y_ssm, y_nsa, w_out[0].astype(BF16))
    x2 = _mlp(x2, norm2_w[0].reshape(1, d), w_up[0].astype(BF16), w_down[0].astype(BF16), norm_f_w.reshape(1, d))
    return x2.reshape(bsz, L, d)
```

```python
import functools
import math

import numpy as np
import jax
import jax.numpy as jnp
from jax import lax
from jax.experimental import pallas as pl
from jax.experimental.pallas import tpu as pltpu

F32 = jnp.float32
BF16 = jnp.bfloat16

SSM_WIDTH = 1024
SSM_GROUP = 16
SSM_GROUPS = 64
SSM_STATE = 64
NSA_WIDTH = 1024
HEAD_DIM = 64
NSA_HEADS = 16
KV_HEADS = 4
Q_PER_KV = 4
KV_WIDTH = 256
CMP_BLOCK = 32
CMP_STRIDE = 16
CMP_HIDDEN = 256
SLC_BLOCK = 64
SLC_TOPK = 16
WINDOW = 512
REL_BUCKETS = 32
REL_MAX_DIST = 128
EPS = 1e-6
NEG = -1e30
FORCED_SCORE = 1e4
D_IN = SSM_WIDTH + NSA_WIDTH + 6 * KV_WIDTH + 3 * NSA_HEADS
D_IN_PAD = 3712
COL_Q = SSM_WIDTH
COL_KC = COL_Q + NSA_WIDTH
COL_VC = COL_KC + KV_WIDTH
COL_KS = COL_VC + KV_WIDTH
COL_VS = COL_KS + KV_WIDTH
COL_KW = COL_VS + KV_WIDTH
COL_VW = COL_KW + KV_WIDTH
COL_GATE = COL_VW + KV_WIDTH

LANES = 128
S5_CHUNK = 16
S5_BATCH = 2
SLAB_GROUPS = LANES // SSM_GROUP
N_SLABS = SSM_GROUPS // SLAB_GROUPS
QT = 128
N_SLC_PAD = 64
RANK_SEG = 16
NSA_GROUPS = 4
FAR_TILES = 2
UNSELECTED = -1e9
LOG2E = math.log2(math.e)
TABLE_W = 1024
VMEM_LIMIT = 56 * 1024 * 1024
HI = lax.Precision.HIGHEST


def _cparams(*sem):
    return pltpu.CompilerParams(dimension_semantics=sem, vmem_limit_bytes=VMEM_LIMIT)


def _rms(x, w):
    ms = jnp.mean(x * x, axis=-1, keepdims=True)
    return x * lax.rsqrt(ms + EPS) * w


def _lane_cat(parts):
    return jnp.concatenate(parts, axis=1)


def _norm_matmul_kernel(x_ref, nw_ref, w_ref, o_ref):
    h = _rms(x_ref[...], nw_ref[...]).astype(BF16)
    o_ref[...] = jnp.dot(h, w_ref[...], preferred_element_type=F32)


def _norm_matmul(x2, nw, w, tm=512):
    t, d = x2.shape
    n = w.shape[1]
    return pl.pallas_call(
        _norm_matmul_kernel,
        grid=(t // tm,),
        in_specs=[pl.BlockSpec((tm, d), lambda i: (i, 0)),
                  pl.BlockSpec((1, d), lambda i: (0, 0)),
                  pl.BlockSpec((d, n), lambda i: (0, 0), pl.Buffered(1))],
        out_specs=pl.BlockSpec((tm, n), lambda i: (i, 0)),
        out_shape=jax.ShapeDtypeStruct((t, n), F32),
        compiler_params=_cparams("parallel"),
        name="norm_in_proj",
    )(x2, nw, w)


def _s5_operators(a_re, a_im, log_dt, b_re, b_im, c_re, c_im, d):
    q = S5_CHUNK
    g, p = a_re.shape
    h = SSM_GROUP
    dt = jnp.exp(log_dt)[:, None]
    lam_re, lam_im = dt * a_re, dt * a_im
    mag1 = jnp.exp(lam_re)
    abar_re, abar_im = mag1 * jnp.cos(lam_im), mag1 * jnp.sin(lam_im)
    den = a_re * a_re + a_im * a_im
    zr, zi = abar_re - 1.0, abar_im
    coef_re = (zr * a_re + zi * a_im) / den
    coef_im = (zi * a_re - zr * a_im) / den
    bb_re = coef_re[..., None] * b_re - coef_im[..., None] * b_im
    bb_im = coef_re[..., None] * b_im + coef_im[..., None] * b_re
    k = jnp.arange(q + 1, dtype=F32)[:, None, None]
    mag = jnp.exp(k * lam_re)
    pw_re, pw_im = mag * jnp.cos(k * lam_im), mag * jnp.sin(k * lam_im)
    m_re = pw_re[..., None] * bb_re - pw_im[..., None] * bb_im
    m_im = pw_re[..., None] * bb_im + pw_im[..., None] * bb_re
    kern = (jnp.einsum('ghp,kgpi->gkhi', c_re, m_re[:q], precision=HI)
            - jnp.einsum('ghp,kgpi->gkhi', c_im, m_im[:q], precision=HI))
    kern = kern.at[:, 0].add(jax.vmap(jnp.diag)(d))
    rev = np.arange(q - 1, -1, -1)
    s_end = jnp.stack([m_re[rev], m_im[rev]], axis=0).transpose(2, 1, 4, 0, 3)
    pr, pi = pw_re[1:], pw_im[1:]
    o_re = c_re[None] * pr[:, :, None, :] - c_im[None] * pi[:, :, None, :]
    o_im = -(c_re[None] * pi[:, :, None, :] + c_im[None] * pr[:, :, None, :])
    o_carry = jnp.stack([o_re, o_im], axis=0).transpose(2, 0, 4, 1, 3)
    a_q = jnp.stack([pw_re[q], pw_im[q]], axis=0)
    return kern, s_end, o_carry, a_q


def _s5_slab_weights(kern, s_end, o_carry, a_q):
    q, h, p = S5_CHUNK, SSM_GROUP, SSM_STATE
    s_dense = s_end.reshape(SSM_GROUPS, q, h, 2 * p).astype(BF16)
    k_dense = kern.transpose(0, 1, 3, 2).astype(BF16)
    o_dense = o_carry.reshape(SSM_GROUPS, 2, p, q * h).astype(BF16)
    a_re = a_q[0].reshape(N_SLABS, SLAB_GROUPS * p)
    a_im = a_q[1].reshape(N_SLABS, SLAB_GROUPS * p)
    return s_dense, k_dense, o_dense, a_re, a_im


def _chunk_rows(u_ref, nc):
    return _lane_cat([u_ref[pl.ds(s, nc, stride=S5_CHUNK), :] for s in range(S5_CHUNK)]).astype(BF16)


def _s5_state_kernel(u_ref, sd_ref, o_ref, ws_scr):
    nc = u_ref.shape[0] // S5_CHUNK
    q, h, p, sg = S5_CHUNK, SSM_GROUP, SSM_STATE, SLAB_GROUPS

    @pl.when(pl.program_id(1) == 0)
    def _():
        ws_scr[...] = jnp.zeros_like(ws_scr)
        for s in range(q):
            for a in range(sg):
                for c in range(2):
                    ws_scr[s * LANES + a * h: s * LANES + (a + 1) * h,
                           (c * sg + a) * p: (c * sg + a + 1) * p] = sd_ref[a, s, :, c * p:(c + 1) * p]

    o_ref[...] = jnp.dot(_chunk_rows(u_ref, nc), ws_scr[...], preferred_element_type=F32).reshape(o_ref.shape)


def _s5_scan_kernel(h_ref, ar_ref, ai_ref, o_ref):
    half = ar_ref.shape[1]
    ar = ar_ref[...]
    ai = ai_ref[...]

    def body(c, carry):
        hr, hi = carry
        o_ref[0, c, :, :half] = hr
        o_ref[0, c, :, half:] = hi
        xr = h_ref[0, c, :, :half]
        xi = h_ref[0, c, :, half:]
        return ar * hr - ai * hi + xr, ar * hi + ai * hr + xi

    zero = jnp.zeros(ar.shape, F32)
    lax.fori_loop(0, h_ref.shape[1], body, (zero, zero))


def _s5_out_kernel(u_ref, kd_ref, h_ref, od_ref, y_ref, kl_scr, wi_scr, wc_scr):
    nc = u_ref.shape[0] // S5_CHUNK
    q, h, p, sg = S5_CHUNK, SSM_GROUP, SSM_STATE, SLAB_GROUPS

    @pl.when(pl.program_id(1) == 0)
    def _():
        kl_scr[...] = jnp.zeros_like(kl_scr)
        for k in range(q):
            for a in range(sg):
                kl_scr[k, a * h:(a + 1) * h, a * h:(a + 1) * h] = kd_ref[a, k]
        for s in range(q):
            for t in range(s, q):
                wi_scr[s * LANES:(s + 1) * LANES, t * LANES:(t + 1) * LANES] = kl_scr[t - s]
        for t in range(0, q, 2):
            wi_scr[(t + 1) * LANES:(t + 2) * LANES, t * LANES:(t + 1) * LANES] = jnp.zeros((LANES, LANES), BF16)
        wc_scr[...] = jnp.zeros_like(wc_scr)
        for a in range(sg):
            for c in range(2):
                for t in range(q):
                    wc_scr[(c * sg + a) * p:(c * sg + a + 1) * p,
                           t * LANES + a * h: t * LANES + (a + 1) * h] = od_ref[a, c, :, t * h:(t + 1) * h]

    u_rows = _chunk_rows(u_ref, nc)
    h_rows = h_ref[...].reshape(nc, h_ref.shape[2]).astype(BF16)
    pair = 2 * LANES
    for tp in range(q // 2):
        cols = slice(tp * pair, (tp + 1) * pair)
        k_used = (tp + 1) * pair
        y = (jnp.dot(u_rows[:, :k_used], wi_scr[:k_used, cols], preferred_element_type=F32)
             + jnp.dot(h_rows, wc_scr[:, cols], preferred_element_type=F32))
        for tt in range(2):
            y_ref[pl.ds(2 * tp + tt, nc, stride=q), :] = y[:, tt * LANES:(tt + 1) * LANES]


def _s5_mixer(proj, bsz, L, ops):
    s_dense, k_dense, o_dense, a_re, a_im = _s5_slab_weights(*ops)
    q, h, p = S5_CHUNK, SSM_GROUP, SSM_STATE
    nc = L // q
    ns, sg = N_SLABS, SLAB_GROUPS
    sw = sg * 2 * p
    kq = q * LANES
    nb = S5_BATCH
    hend = pl.pallas_call(
        _s5_state_kernel,
        grid=(ns, bsz // nb),
        in_specs=[pl.BlockSpec((nb * L, LANES), lambda j, b: (b, j)),
                  pl.BlockSpec((sg, q, h, 2 * p), lambda j, b: (j, 0, 0, 0))],
        out_specs=pl.BlockSpec((nb, nc, sw), lambda j, b: (b, 0, j)),
        out_shape=jax.ShapeDtypeStruct((bsz, nc, ns * sw), F32),
        scratch_shapes=[pltpu.VMEM((kq, sw), BF16)],
        compiler_params=_cparams("parallel", "arbitrary"),
        name="s5_chunk_state",
    )(proj, s_dense)
    hprev4 = pl.pallas_call(
        _s5_scan_kernel,
        grid=(bsz,),
        in_specs=[pl.BlockSpec((1, nc, ns, sw), lambda b: (b, 0, 0, 0)),
                  pl.BlockSpec((ns, sw // 2), lambda b: (0, 0)),
                  pl.BlockSpec((ns, sw // 2), lambda b: (0, 0))],
        out_specs=pl.BlockSpec((1, nc, ns, sw), lambda b: (b, 0, 0, 0)),
        out_shape=jax.ShapeDtypeStruct((bsz, nc, ns, sw), F32),
        compiler_params=_cparams("parallel"),
        name="s5_chunk_scan",
    )(hend.reshape(bsz, nc, ns, sw), a_re, a_im)
    y = pl.pallas_call(
        _s5_out_kernel,
        grid=(ns, bsz // nb),
        in_specs=[pl.BlockSpec((nb * L, LANES), lambda j, b: (b, j)),
                  pl.BlockSpec((sg, q, h, h), lambda j, b: (j, 0, 0, 0)),
                  pl.BlockSpec((nb, nc, sw), lambda j, b: (b, 0, j)),
                  pl.BlockSpec((sg, 2, p, q * h), lambda j, b: (j, 0, 0, 0))],
        out_specs=pl.BlockSpec((nb * L, LANES), lambda j, b: (b, j)),
        out_shape=jax.ShapeDtypeStruct((bsz * L, SSM_WIDTH), F32),
        scratch_shapes=[pltpu.VMEM((q, LANES, LANES), BF16), pltpu.VMEM((kq, kq), BF16), pltpu.VMEM((sw, kq), BF16)],
        compiler_params=_cparams("parallel", "arbitrary"),
        name="s5_chunk_out",
    )(proj, k_dense, hprev4.reshape(bsz, nc, ns * sw), o_dense)
    return y


def _compress_kernel(k_ref, w1_ref, pe_ref, w1full_ref, w2_ref, o_ref, *, transpose_out):
    nc = k_ref.shape[0] // CMP_STRIDE
    prod = jnp.dot(_chunk_rows(k_ref, nc), w1_ref[...], preferred_element_type=F32)
    pe_h = jnp.dot(jnp.broadcast_to(pe_ref[...], (8, pe_ref.shape[1])), w1full_ref[...],
                   precision=HI, preferred_element_type=F32)[0:1]
    two_h = 2 * CMP_HIDDEN
    for gl in range(LANES // HEAD_DIM):
        first = prod[:, gl * two_h: gl * two_h + CMP_HIDDEN]
        second = prod[:, gl * two_h + CMP_HIDDEN: (gl + 1) * two_h]
        hid = first + pltpu.roll(second, nc - 1, axis=0) + pe_h
        out = jnp.dot(jax.nn.gelu(hid).astype(BF16), w2_ref[...], preferred_element_type=F32)
        if transpose_out:
            o_ref[0, gl] = _lane_cat([out, jnp.zeros_like(out)]).T[:HEAD_DIM].astype(BF16)
        else:
            o_ref[0, gl] = out.astype(BF16)


def _compress(proj, col, bsz, L, pe, w1, w2, transpose_out):
    nc = L // CMP_STRIDE
    half = CMP_STRIDE * HEAD_DIM
    gpl = LANES // HEAD_DIM
    w1cat = jnp.concatenate([w1[:half], w1[half:]], axis=1)
    eye = jnp.eye(gpl, dtype=F32)
    w1slab = jnp.einsum('ldh,ag->ladgh', w1cat.reshape(CMP_STRIDE, HEAD_DIM, 2 * CMP_HIDDEN), eye)
    w1slab = w1slab.reshape(CMP_STRIDE * LANES, gpl * 2 * CMP_HIDDEN).astype(BF16)
    out_block = (1, gpl, HEAD_DIM, nc) if transpose_out else (1, gpl, nc, HEAD_DIM)
    out_full = (bsz, KV_HEADS) + out_block[2:]
    return pl.pallas_call(
        functools.partial(_compress_kernel, transpose_out=transpose_out),
        grid=(bsz, KV_HEADS // gpl),
        in_specs=[pl.BlockSpec((L, LANES), lambda b, j: (b, col // LANES + j)),
                  pl.BlockSpec(w1slab.shape, lambda b, j: (0, 0)),
                  pl.BlockSpec((1, 2 * half), lambda b, j: (0, 0)),
                  pl.BlockSpec((2 * half, CMP_HIDDEN), lambda b, j: (0, 0)),
                  pl.BlockSpec((CMP_HIDDEN, HEAD_DIM), lambda b, j: (0, 0))],
        out_specs=pl.BlockSpec(out_block, lambda b, j: (b, j, 0, 0)),
        out_shape=jax.ShapeDtypeStruct(out_full, BF16),
        compiler_params=_cparams("parallel", "parallel"),
        name="nsa_compress",
    )(proj, w1slab, pe.reshape(1, 2 * half), w1, w2.astype(BF16))


KS_COLS = 2 * LANES
KW_COLS = LANES
KS_FLAG = HEAD_DIM + N_SLC_PAD
KW_FLAG = HEAD_DIM
PAD_TILES = 8
V_ROWS = HEAD_DIM + 16
V_ONES = HEAD_DIM


def _kv_tiles_kernel(ks_ref, vs_ref, kw_ref, vw_ref, kso_ref, vso_ref, kwo_ref, vwo_ref):
    tm = ks_ref.shape[0]
    step = pl.program_id(1)
    row0 = step * tm
    kso_ref[:, :, :, :, KS_FLAG:] = jnp.zeros(kso_ref.shape[:4] + (KS_COLS - KS_FLAG,), BF16)
    kwo_ref[:, :, :, :, KW_FLAG:] = jnp.zeros(kwo_ref.shape[:4] + (KW_COLS - KW_FLAG,), BF16)
    is_real = step < pl.num_programs(1) - 1
    tail_rows = lax.broadcasted_iota(jnp.int32, vso_ref.shape[:3] + (V_ROWS - HEAD_DIM, QT), 3)
    v_tail = jnp.where(tail_rows == 0, jnp.where(is_real, 1.0, 0.0), 0.0).astype(BF16)
    vso_ref[:, :, :, HEAD_DIM:, :] = v_tail
    vwo_ref[:, :, :, HEAD_DIM:, :] = v_tail

    @pl.when(step < pl.num_programs(1) - 1)
    def _():
        lane_blk = lax.broadcasted_iota(jnp.int32, (QT, N_SLC_PAD), 1)
        for k in range(tm // QT):
            rows = slice(k * QT, (k + 1) * QT)
            tok = lax.broadcasted_iota(jnp.int32, (QT, N_SLC_PAD), 0) + (row0 + k * QT)
            onehot = jnp.where((tok >> int(math.log2(SLC_BLOCK))) == lane_blk, 1.0, 0.0).astype(BF16)
            vs_t = vs_ref[rows, :].T
            vw_t = vw_ref[rows, :].T
            for g in range(KV_HEADS):
                cols = slice(g * HEAD_DIM, (g + 1) * HEAD_DIM)
                kso_ref[0, g, k, :, :HEAD_DIM] = ks_ref[rows, cols].astype(BF16)
                kso_ref[0, g, k, :, HEAD_DIM:KS_FLAG] = onehot
                kwo_ref[0, g, k, :, :HEAD_DIM] = kw_ref[rows, cols].astype(BF16)
                vso_ref[0, g, k, :HEAD_DIM, :] = vs_t[cols].astype(BF16)
                vwo_ref[0, g, k, :HEAD_DIM, :] = vw_t[cols].astype(BF16)

    @pl.when(step == pl.num_programs(1) - 1)
    def _():
        ones = jnp.ones(kso_ref.shape[:4] + (1,), BF16)
        kso_ref[:, :, :, :, :KS_FLAG] = jnp.zeros(kso_ref.shape[:4] + (KS_FLAG,), BF16)
        kwo_ref[:, :, :, :, :KW_FLAG] = jnp.zeros(kwo_ref.shape[:4] + (KW_FLAG,), BF16)
        kso_ref[:, :, :, :, KS_FLAG:KS_FLAG + 1] = ones
        kwo_ref[:, :, :, :, KW_FLAG:KW_FLAG + 1] = ones
        vso_ref[:, :, :, :HEAD_DIM, :] = jnp.zeros(vso_ref.shape[:3] + (HEAD_DIM, QT), BF16)
        vwo_ref[:, :, :, :HEAD_DIM, :] = jnp.zeros(vwo_ref.shape[:3] + (HEAD_DIM, QT), BF16)


def _kv_tiles(proj, bsz, L):
    nt = L // QT
    g = KV_HEADS
    kt = PAD_TILES
    tm = kt * QT
    steps = L // tm
    in_spec = lambda col: pl.BlockSpec((tm, KV_WIDTH),
                                       lambda b, i: (b * steps + jnp.minimum(i, steps - 1), col // KV_WIDTH))
    out_spec = lambda a, c: pl.BlockSpec((1, g, kt, a, c), lambda b, i: (b, 0, i, 0, 0))
    shape = lambda a, c: jax.ShapeDtypeStruct((bsz, g, nt + kt, a, c), BF16)
    return pl.pallas_call(
        _kv_tiles_kernel,
        grid=(bsz, steps + 1),
        in_specs=[in_spec(COL_KS), in_spec(COL_VS), in_spec(COL_KW), in_spec(COL_VW)],
        out_specs=[out_spec(QT, KS_COLS), out_spec(V_ROWS, QT), out_spec(QT, KW_COLS), out_spec(V_ROWS, QT)],
        out_shape=[shape(QT, KS_COLS), shape(V_ROWS, QT), shape(QT, KW_COLS), shape(V_ROWS, QT)],
        compiler_params=_cparams("parallel", "arbitrary"),
        name="nsa_kv_tiles",
    )(proj, proj, proj, proj)


def _rel_bucket(dist):
    dist = jnp.maximum(dist, 0)
    max_exact = REL_BUCKETS // 2
    large = max_exact + (jnp.log(jnp.maximum(dist, 1).astype(F32) / max_exact)
                         / math.log(REL_MAX_DIST / max_exact) * (REL_BUCKETS - max_exact)).astype(jnp.int32)
    large = jnp.minimum(large, REL_BUCKETS - 1)
    return jnp.where(dist < max_exact, dist, large)


def _toeplitz_rows(base, nrows, step):
    w = base.shape[1]
    x = jnp.broadcast_to(base, (nrows, w))
    k_ix = lax.broadcasted_iota(jnp.int32, (nrows, w), 0)
    bit = 1
    while bit < nrows:
        x = jnp.where((k_ix & bit) != 0, pltpu.roll(x, (bit * step) % w, axis=1), x)
        bit *= 2
    return x


def _bias_tables_kernel(bw_ref, bc_ref, far_ref, wb_ref, nb_ref, strip_ref, *, ncp):
    slab = WINDOW + QT
    y = _toeplitz_rows(bw_ref[0], QT, 1)
    for kh in range(slab // QT):
        tile = y[:, WINDOW - kh * QT: WINDOW - kh * QT + QT]
        wb_ref[0, kh * QT:(kh + 1) * QT, :] = tile
        if kh >= slab // QT - 2:
            row = kh - (slab // QT - 2)
            nb_ref[0, row * QT:(row + 1) * QT, :] = tile - far_ref[0]
    band = 2 * CMP_STRIDE
    z = _toeplitz_rows(bc_ref[0], band, CMP_STRIDE)[:, 2 * QT: 3 * QT]
    strip_ref[0, :ncp - CMP_STRIDE, :] = jnp.broadcast_to(far_ref[0], (ncp - CMP_STRIDE, QT))
    strip_ref[0, ncp - CMP_STRIDE: ncp + CMP_STRIDE, :] = z
    strip_ref[0, ncp + CMP_STRIDE:, :] = jnp.full((ncp - CMP_STRIDE, QT), NEG, F32)


def _nsa_tables(rel_table, L):
    ncp = L // CMP_STRIDE
    slab = WINDOW + QT
    dist = jnp.arange(TABLE_W)
    rel_table = rel_table * LOG2E
    per_dist = rel_table[_rel_bucket(dist)].T
    base_w = jnp.where(dist < WINDOW, per_dist, NEG)
    shift = CMP_BLOCK - 1
    shifted = jnp.concatenate([jnp.full((NSA_HEADS, shift), NEG, F32), per_dist[:, :TABLE_W - shift]], axis=1)
    base_c = jnp.where(dist < TABLE_W // 2, shifted, NEG)
    far = jnp.broadcast_to(rel_table[REL_BUCKETS - 1][:, None, None], (NSA_HEADS, 1, QT))
    wb, nb, strip = pl.pallas_call(
        functools.partial(_bias_tables_kernel, ncp=ncp),
        grid=(NSA_HEADS,),
        in_specs=[pl.BlockSpec((1, 1, TABLE_W), lambda h: (h, 0, 0)),
                  pl.BlockSpec((1, 1, TABLE_W), lambda h: (h, 0, 0)),
                  pl.BlockSpec((1, 1, QT), lambda h: (h, 0, 0))],
        out_specs=[pl.BlockSpec((1, slab, QT), lambda h: (h, 0, 0)),
                   pl.BlockSpec((1, 2 * QT, QT), lambda h: (h, 0, 0)),
                   pl.BlockSpec((1, 2 * ncp, QT), lambda h: (h, 0, 0))],
        out_shape=[jax.ShapeDtypeStruct((NSA_HEADS, slab, QT), F32),
                   jax.ShapeDtypeStruct((NSA_HEADS, 2 * QT, QT), F32),
                   jax.ShapeDtypeStruct((NSA_HEADS, 2 * ncp, QT), F32)],
        compiler_params=_cparams("parallel"),
        name="nsa_bias_tables",
    )(base_w.reshape(NSA_HEADS, 1, TABLE_W), base_c.reshape(NSA_HEADS, 1, TABLE_W), far)
    g, r = KV_HEADS, Q_PER_KV
    return wb.reshape(g, r, slab, QT), nb.reshape(g, r, 2 * QT, QT), strip.reshape(g, r, 2 * ncp, QT)


def _overlap_t(L):
    n_cmp = (L - CMP_BLOCK) // CMP_STRIDE + 1
    n_slc = L // SLC_BLOCK
    cmp_idx = np.arange(n_cmp)[:, None] * CMP_STRIDE + np.arange(CMP_BLOCK)[None, :]
    overlap = ((cmp_idx[:, :, None] // SLC_BLOCK) == np.arange(n_slc)[None, None, :]).sum(1) / CMP_BLOCK
    out = np.zeros((N_SLC_PAD, L // CMP_STRIDE), np.float32)
    out[:n_slc, :n_cmp] = overlap.T
    return jnp.asarray(out, dtype=BF16)


def _split3(x):
    hi = x.astype(BF16)
    r1 = x - hi.astype(F32)
    mid = r1.astype(BF16)
    lo = (r1 - mid.astype(F32)).astype(BF16)
    return hi, mid, lo


def _nsa_kernel(q_ref, gl_ref, kc_ref, vcT_ref, strip_ref, ovT_ref, ks_ref, vsT_ref, kw_ref, vwT_ref, wb_ref,
                nb_ref, o_ref, gate_ref, *, n_slc):
    i = pl.program_id(2)
    R = Q_PER_KV
    RQ = R * QT
    RW = R * HEAD_DIM
    groups = range(kc_ref.shape[1])
    first_group = pl.program_id(1) * len(groups)
    pad_tile = ks_ref.shape[2] - PAD_TILES
    t_pos = lax.broadcasted_iota(jnp.int32, (1, QT), 1) + i * QT

    def flag_rows(n):
        return jnp.where(lax.broadcasted_iota(jnp.int32, (n, QT), 0) == 0, NEG, 0.0).astype(BF16)

    q_heads = []
    for gg in groups:
        qT = (q_ref[:, gg * RW:(gg + 1) * RW].T * (0.125 * LOG2E)).astype(BF16)
        q_heads.append([qT[r * HEAD_DIM:(r + 1) * HEAD_DIM] for r in range(R)])

    slab = WINDOW + QT
    n_win = slab // QT
    win_idx = [jnp.where(i + jj >= n_win - 1, i + jj - (n_win - 1), pad_tile) for jj in range(n_win)]

    def window_scores(gg):
        kw = jnp.concatenate([kw_ref[0, gg, j] for j in win_idx], axis=0)
        qwin = _lane_cat([jnp.concatenate([q_heads[gg][r], flag_rows(KW_COLS - HEAD_DIM)], axis=0)
                          for r in range(R)])
        wb = _lane_cat([wb_ref[gg, r] for r in range(R)])
        return jnp.dot(kw, qwin, preferred_element_type=F32) + wb

    def window_branch(gg, sw):
        m_w = jnp.max(sw, axis=0, keepdims=True)
        e_wb = jnp.exp2(sw - m_w).astype(BF16)
        acc = None
        for jj, j in enumerate(win_idx):
            term = jnp.dot(vwT_ref[0, gg, j], e_wb[jj * QT:(jj + 1) * QT], preferred_element_type=F32)
            acc = term if acc is None else acc + term
        return acc[:HEAD_DIM] * (1.0 / acc[V_ONES:V_ONES + 1])

    sw_all = [window_scores(gg) for gg in groups]

    ncp = kc_ref.shape[2]
    strip_row = pl.multiple_of(ncp - i * (QT // CMP_STRIDE), QT // CMP_STRIDE)
    ov = ovT_ref[...]
    s_ix = lax.broadcasted_iota(jnp.int32, (N_SLC_PAD, QT), 0)
    cur = t_pos >> int(math.log2(SLC_BLOCK))
    forced = (s_ix == 0) | (s_ix == cur) | (s_ix == cur - 1)

    def compressed_scores(gg):
        sc = jnp.dot(kc_ref[0, gg], _lane_cat(q_heads[gg]), preferred_element_type=F32)
        return sc + _lane_cat([strip_ref[gg, r, pl.ds(strip_row, ncp), :] for r in range(R)])

    def compressed_branch(gg, sm):
        m_c = jnp.max(sm, axis=0, keepdims=True)
        e_c = jnp.exp2(sm - m_c)
        l_c = jnp.sum(e_c, axis=0, keepdims=True)
        inv_c = jnp.where(m_c > 0.5 * NEG, 1.0 / l_c, 0.0)
        o_cmp = jnp.dot(vcT_ref[0, gg], e_c.astype(BF16), preferred_element_type=F32) * inv_c
        p_c = e_c * inv_c
        p_sum = p_c[:, 0:QT]
        for r in range(1, R):
            p_sum = p_sum + p_c[:, r * QT:(r + 1) * QT]
        imp = None
        for part in _split3(p_sum):
            term = jnp.dot(ov, part, preferred_element_type=F32)
            imp = term if imp is None else imp + term
        score = jnp.where(forced, FORCED_SCORE, jnp.where(s_ix <= cur, imp, -1.0))
        return o_cmp, jnp.where(s_ix < n_slc, score, -2.0)

    sm_all = [compressed_scores(gg) for gg in groups]
    o_win = [window_branch(gg, sw_all[gg]) for gg in groups]
    cmp_out = [compressed_branch(gg, sm_all[gg]) for gg in groups]
    o_cmp = [c[0] for c in cmp_out]
    scores = [c[1] for c in cmp_out]

    SUB = 8
    n_sub = N_SLC_PAD // SUB
    score_rows = [[score[v * SUB:(v + 1) * SUB] for v in range(n_sub)] for score in scores]
    sub_ix = lax.broadcasted_iota(jnp.int32, (SUB, QT), 0)

    vregs_per_seg = RANK_SEG // SUB
    n_seg = -(-n_slc // RANK_SEG)

    def rank_levels(levels, ranks):
        out = []
        for gg in groups:
            rk = list(ranks[gg])
            for k in levels:
                for seg in range(k + 1):
                    rows = range((k + 1) * vregs_per_seg) if seg == k else range(k * vregs_per_seg,
                                                                                 (k + 1) * vregs_per_seg)
                    for sp in range(seg * RANK_SEG, min((seg + 1) * RANK_SEG, n_slc)):
                        row = scores[gg][sp:sp + 1, :]
                        for v in rows:
                            blk = score_rows[gg][v]
                            if v * SUB > sp:
                                beats = row >= blk
                            elif v * SUB + SUB - 1 <= sp:
                                beats = row > blk
                            else:
                                beats = (row > blk) | ((row == blk) & (sub_ix > sp - v * SUB))
                            rk[v] = rk[v] + jnp.where(beats, 1.0, 0.0)
            out.append(tuple(rk))
        return tuple(out)

    ranks = tuple(tuple(jnp.zeros((SUB, QT), F32) for _ in range(n_sub)) for _ in groups)
    last_block = (i + 1) * (QT // SLC_BLOCK) - 1
    for levels in [(0, 1)] + [(k,) for k in range(2, n_seg)]:
        first = max(levels) * RANK_SEG if max(levels) > 1 else SLC_TOPK
        ranks = lax.cond(last_block >= first, functools.partial(rank_levels, levels), lambda rk: rk, ranks)

    qsel = []
    for gg in groups:
        sel_neg = jnp.where(jnp.concatenate(ranks[gg], axis=0) < float(SLC_TOPK), 0.0, UNSELECTED).astype(BF16)
        qsel.append(_lane_cat([jnp.concatenate([q_heads[gg][r], sel_neg, flag_rows(KS_COLS - KS_FLAG)], axis=0)
                               for r in range(R)]))

    n_far = jnp.maximum(i - 1, 0)

    def far_tiles(j):
        return [jnp.where(j + n < n_far, j + n, pad_tile) for n in range(FAR_TILES)]

    def tile_scores(gg, tiles):
        keys = jnp.concatenate([ks_ref[0, gg, t] for t in tiles], axis=0)
        return jnp.dot(keys, qsel[gg], preferred_element_type=F32)

    def online_step(gg, carry, s, tiles):
        m, acc = carry
        m_new = jnp.maximum(m, jnp.max(s, axis=0, keepdims=True))
        p = jnp.exp2(s - m_new).astype(BF16)
        acc = jnp.exp2(m - m_new) * acc
        for n, t in enumerate(tiles):
            acc = acc + jnp.dot(vsT_ref[0, gg, t], p[n * QT:(n + 1) * QT], preferred_element_type=F32)
        return m_new, acc

    def far_trip(u, carries):
        tiles = far_tiles(FAR_TILES * u)
        s_all = [tile_scores(gg, tiles) for gg in groups]
        return tuple(online_step(gg, carries[gg], s_all[gg], tiles) for gg in groups)

    init = (jnp.full((1, RQ), NEG, F32), jnp.zeros((V_ROWS, RQ), F32))
    carries = lax.fori_loop(0, (n_far + FAR_TILES - 1) // FAR_TILES, far_trip, tuple(init for _ in groups))
    near = [jnp.where(i >= 1, i - 1, pad_tile), i]

    gate_ref[...] = jax.nn.sigmoid(gl_ref[...].T)
    s_near = [tile_scores(gg, near) + _lane_cat([nb_ref[gg, r] for r in range(R)]) for gg in groups]
    for gg in groups:
        _, acc_s = online_step(gg, carries[gg], s_near[gg], near)
        o_slc = acc_s[:HEAD_DIM] * (1.0 / acc_s[V_ONES:V_ONES + 1])
        head0 = (first_group + gg) * R
        gates = [_lane_cat([gate_ref[pl.ds((head0 + r) * 3 + br, 1), :] for r in range(R)]) for br in range(3)]
        total = gates[0] * o_cmp[gg] + gates[1] * o_slc + gates[2] * o_win[gg]
        o_ref[:, gg * RW:(gg + 1) * RW] = jnp.concatenate(
            [total[:, r * QT:(r + 1) * QT] for r in range(R)], axis=0).T.astype(o_ref.dtype)


def _nsa_mixer(proj, bsz, L, pe_ck, w_ck1, w_ck2, pe_cv, w_cv1, w_cv2, rel_table):
    g, r = KV_HEADS, Q_PER_KV
    ng = NSA_GROUPS
    ni = L // QT
    nt = L // QT + PAD_TILES
    ncp = L // CMP_STRIDE
    n_slc = L // SLC_BLOCK
    slab = WINDOW + QT
    k_cmp = _compress(proj, COL_KC, bsz, L, pe_ck, w_ck1, w_ck2, transpose_out=False)
    v_cmp_t = _compress(proj, COL_VC, bsz, L, pe_cv, w_cv1, w_cv2, transpose_out=True)
    ks_t, vs_t, kw_t, vw_t = _kv_tiles(proj, bsz, L)
    wb, nb, strip = _nsa_tables(rel_table, L)
    qw = ng * r * HEAD_DIM
    once = pl.Buffered(1)
    whole = lambda *shape: pl.BlockSpec((1, ng) + shape, lambda b, h, i: (b, h) + (0,) * len(shape), once)
    per_group = lambda *shape: pl.BlockSpec((ng,) + shape, lambda b, h, i: (h,) + (0,) * len(shape), once)
    return pl.pallas_call(
        functools.partial(_nsa_kernel, n_slc=n_slc),
        grid=(bsz, g // ng, ni),
        in_specs=[
            pl.BlockSpec((QT, qw), lambda b, h, i: (b * ni + i, COL_Q // qw + h)),
            pl.BlockSpec((QT, LANES), lambda b, h, i: (b * ni + i, COL_GATE // LANES)),
            whole(ncp, HEAD_DIM),
            whole(HEAD_DIM, ncp),
            per_group(r, 2 * ncp, QT),
            pl.BlockSpec((N_SLC_PAD, ncp), lambda b, h, i: (0, 0), once),
            whole(nt, QT, KS_COLS),
            whole(nt, V_ROWS, QT),
            whole(nt, QT, KW_COLS),
            whole(nt, V_ROWS, QT),
            per_group(r, slab, QT),
            per_group(r, 2 * QT, QT),
        ],
        out_specs=pl.BlockSpec((QT, qw), lambda b, h, i: (b * ni + i, h)),
        out_shape=jax.ShapeDtypeStruct((bsz * L, NSA_WIDTH), BF16),
        scratch_shapes=[pltpu.VMEM((LANES, QT), F32)],
        compiler_params=_cparams("parallel", "parallel", "arbitrary"),
        name="nsa_attention",
    )(proj, proj, k_cmp, v_cmp_t, strip, _overlap_t(L), ks_t, vs_t, kw_t, vw_t, wb, nb)


def _out_proj_kernel(x_ref, y_ref, wg_ref, bg_ref, yn_ref, w_ref, o_ref):
    z = jax.nn.gelu(y_ref[...])
    gate = jnp.dot(z.astype(BF16), wg_ref[...], preferred_element_type=F32) + bg_ref[...]
    ys = (z * jax.nn.sigmoid(gate)).astype(BF16)
    half = ys.shape[1]
    acc = jnp.dot(ys, w_ref[:half, :], preferred_element_type=F32)
    acc = acc + jnp.dot(yn_ref[...], w_ref[half:, :], preferred_element_type=F32)
    o_ref[...] = x_ref[...] + acc


def _out_proj(x2, y_s5, w_glu, b_glu, y_nsa, w, tm=512):
    t, d = x2.shape
    once = pl.Buffered(1)
    return pl.pallas_call(
        _out_proj_kernel,
        grid=(t // tm,),
        in_specs=[pl.BlockSpec((tm, d), lambda i: (i, 0)),
                  pl.BlockSpec((tm, SSM_WIDTH), lambda i: (i, 0)),
                  pl.BlockSpec((SSM_WIDTH, SSM_WIDTH), lambda i: (0, 0), once),
                  pl.BlockSpec((1, SSM_WIDTH), lambda i: (0, 0)),
                  pl.BlockSpec((tm, NSA_WIDTH), lambda i: (i, 0)),
                  pl.BlockSpec((SSM_WIDTH + NSA_WIDTH, d), lambda i: (0, 0), once)],
        out_specs=pl.BlockSpec((tm, d), lambda i: (i, 0)),
        out_shape=jax.ShapeDtypeStruct((t, d), F32),
        compiler_params=_cparams("parallel"),
        name="out_proj",
    )(x2, y_s5, w_glu, b_glu, y_nsa, w)


def _mlp_kernel(x_ref, n2_ref, wu_ref, wd_ref, nf_ref, o_ref, h_ref, acc_ref):
    f = pl.program_id(1)

    @pl.when(f == 0)
    def _():
        h_ref[...] = _rms(x_ref[...], n2_ref[...]).astype(BF16)
        acc_ref[...] = jnp.zeros_like(acc_ref)

    a = jnp.dot(h_ref[...], wu_ref[...], preferred_element_type=F32)
    a = jnp.square(jnp.maximum(a, 0.0))
    acc_ref[...] += jnp.dot(a.astype(BF16), wd_ref[...], preferred_element_type=F32)

    @pl.when(f == pl.num_programs(1) - 1)
    def _():
        o_ref[...] = _rms(x_ref[...] + acc_ref[...], nf_ref[...])


def _mlp(x2, n2, wu, wd, nf, tm=512, tf=1024):
    t, d = x2.shape
    ff = wu.shape[1]
    return pl.pallas_call(
        _mlp_kernel,
        grid=(t // tm, ff // tf),
        in_specs=[pl.BlockSpec((tm, d), lambda i, f: (i, 0)),
                  pl.BlockSpec((1, d), lambda i, f: (0, 0)),
                  pl.BlockSpec((d, tf), lambda i, f: (0, f)),
                  pl.BlockSpec((tf, d), lambda i, f: (f, 0)),
                  pl.BlockSpec((1, d), lambda i, f: (0, 0))],
        out_specs=pl.BlockSpec((tm, d), lambda i, f: (i, 0)),
        out_shape=jax.ShapeDtypeStruct((t, d), F32),
        scratch_shapes=[pltpu.VMEM((tm, d), BF16), pltpu.VMEM((tm, d), F32)],
        compiler_params=_cparams("parallel", "arbitrary"),
        name="mlp_final_norm",
    )(x2, n2, wu, wd, nf)


def kernel(x, norm1_w, w_in, ssm_a_re, ssm_a_im, ssm_log_dt, ssm_b_re, ssm_b_im, ssm_c_re, ssm_c_im, ssm_d,
           w_glu, b_glu, pe_ck, w_ck1, w_ck2, pe_cv, w_cv1, w_cv2, w_out, norm2_w, w_up, w_down, rel_table,
           norm_f_w):
    bsz, L, d = x.shape
    assert w_in.shape[0] == 1, "the closing rmsnorm is fused into the (single) layer's MLP kernel"
    x2 = x.reshape(bsz * L, d)
    w_in_p = jnp.pad(w_in[0].astype(BF16), ((0, 0), (0, D_IN_PAD - D_IN)))
    proj = _norm_matmul(x2, norm1_w[0].reshape(1, d), w_in_p)
    ops = _s5_operators(ssm_a_re[0], ssm_a_im[0], ssm_log_dt[0], ssm_b_re[0], ssm_b_im[0], ssm_c_re[0], ssm_c_im[0],
                        ssm_d[0])
    y_s5 = _s5_mixer(proj, bsz, L, ops)
    y_nsa = _nsa_mixer(proj, bsz, L, pe_ck[0], w_ck1[0], w_ck2[0], pe_cv[0], w_cv1[0], w_cv2[0], rel_table)
    x2 = _out_proj(x2, y_s5, w_glu[0].astype(BF16), b_glu[0].reshape(1, SSM_WIDTH), y_nsa, w_out[0].astype(BF16))
    x2 = _mlp(x2, norm2_w[0].reshape(1, d), w_up[0].astype(BF16), w_down[0].astype(BF16), norm_f_w.reshape(1, d))
    return x2.reshape(bsz, L, d)
```
